```python
import math
import numpy as np
import jax, jax.numpy as jnp
from jax import lax

D_MODEL = 2048
BATCH = 8
SEQ = 2048
DEPTH = 2
DEC_BATCH = 4
DEC_SEQ = 2048
PAST_LEN = 128

F32 = jnp.float32
HEAD_DIM = 64
EPS = 1e-6
HY_WIDTH = D_MODEL // 4
ATT_WIDTH = D_MODEL // 2
RET_WIDTH = D_MODEL // 4
HY_ORDER = 2
HY_EMB = 33
HY_FFN = 64
HY_TARGET = 1e-2
HY_MIN_DECAY = math.log(1.0 / HY_TARGET) / 1.5
HY_MAX_DECAY = math.log(1.0 / HY_TARGET) / 0.3
ATT_HEADS = ATT_WIDTH // HEAD_DIM
ATT_KV_HEADS = 4
ATT_GROUP = ATT_HEADS // ATT_KV_HEADS
KV_WIDTH = ATT_KV_HEADS * HEAD_DIM
WINDOW = 128
BLOCK = 128
REL_BUCKETS = 32
REL_MAX_DIST = 128
RET_HEADS = RET_WIDTH // HEAD_DIM
RET_CHUNK = 128
ROPE_BASE = 10000.0
N_GROUPS = 4
EXPERTS_PER_GROUP = 4
N_EXPERTS = N_GROUPS * EXPERTS_PER_GROUP
TOP_K_IN_GROUP = 2
D_EXPERT = D_MODEL // 2
HY_IN = 3 * HY_WIDTH
ATT_IN = ATT_WIDTH + 2 * KV_WIDTH
RET_IN = 4 * RET_WIDTH
D_IN = HY_IN + ATT_IN + RET_IN

kernel_name = "hymba_hyena_swa_retention_hmoe_encoder"


def _rmsnorm(x, g):
    xf = x.astype(F32)
    y = xf * lax.rsqrt(jnp.mean(xf * xf, -1, keepdims=True) + EPS)
    return (y * g.astype(F32)).astype(x.dtype)


def _short_conv(u, w, b):
    L = u.shape[1]
    up = jnp.pad(u, ((0, 0), (1, 1), (0, 0)))
    return up[:, :L] * w[0] + up[:, 1:L + 1] * w[1] + up[:, 2:] * w[2] + b


def _hyena_filters(L, w1, b1, w2, b2, w3, freq, decay):
    t = jnp.linspace(0.0, 1.0, L, dtype=F32)[:, None]
    bands = (HY_EMB - 1) // 2
    f = jnp.linspace(1e-4, bands - 1, bands, dtype=F32)[None]
    w = 2.0 * math.pi * jnp.arange(L, dtype=F32)[:, None] / L
    z = jnp.concatenate([t, jnp.cos(f * w), -jnp.sin(f * w)], -1)
    fr = freq.astype(F32)
    h = jnp.sin(fr * (z @ w1.astype(F32) + b1.astype(F32)))
    h = jnp.sin(fr * (h @ w2.astype(F32) + b2.astype(F32)))
    h = (h @ w3.astype(F32)).reshape(L, 2, HY_ORDER, HY_WIDTH)
    h = h * jnp.exp(-t[:, :, None, None] * jnp.abs(decay.astype(F32)))
    h_fwd, h_bwd = h[:, 0], h[:, 1]
    k = jnp.concatenate([h_fwd, jnp.zeros_like(h_fwd[:1]), h_bwd[1:][::-1]], 0)
    k = k / jnp.sum(jnp.abs(k), 0, keepdims=True)
    return jnp.fft.rfft(k, axis=0)


def _fftconv(u, kf, d):
    L = u.shape[1]
    y = jnp.fft.irfft(jnp.fft.rfft(u, n=2 * L, axis=1) * kf, n=2 * L, axis=1)[:, :L]
    return y + u * d.astype(F32)


def _hyena(u, conv_w, conv_b, w1, b1, w2, b2, w3, freq, decay, skip):
    L = u.shape[1]
    u = _short_conv(u, conv_w, conv_b).astype(F32)
    v, x1, x2 = jnp.split(u, 3, -1)
    kf = _hyena_filters(L, w1, b1, w2, b2, w3, freq, decay)
    z = x1 * _fftconv(v, kf[:, 0], skip[0])
    return x2 * _fftconv(z, kf[:, 1], skip[1])


def _t5_bucket(rel):
    nb = REL_BUCKETS // 2
    max_exact = nb // 2
    n = np.abs(rel)
    large = max_exact + (np.log(np.maximum(n, 1) / max_exact) / np.log(REL_MAX_DIST / max_exact)
                         * (nb - max_exact)).astype(np.int32)
    large = np.minimum(large, nb - 1)
    return (rel > 0).astype(np.int32) * nb + np.where(n < max_exact, n, large)


def _window_attention(q, k, v, sink, rel_bias):
    B, L = q.shape[:2]
    nb = L // BLOCK
    qb = q.reshape(B, nb, BLOCK, ATT_KV_HEADS, ATT_GROUP, HEAD_DIM)

    def band(t):
        tp = jnp.pad(t, ((0, 0), (BLOCK, BLOCK), (0, 0), (0, 0)))
        tp = tp.reshape(B, nb + 2, BLOCK, ATT_KV_HEADS, HEAD_DIM)
        return jnp.concatenate([tp[:, :-2], tp[:, 1:-1], tp[:, 2:]], axis=2)

    kb, vb = band(k), band(v)
    s = jnp.einsum('bnqkgd,bnskd->bnkgqs', qb, kb, preferred_element_type=F32) * (HEAD_DIM ** -0.5)
    rel = (np.arange(3 * BLOCK)[None, :] - BLOCK) - np.arange(BLOCK)[:, None]
    bias = rel_bias.astype(F32)[_t5_bucket(rel)]
    bias = jnp.transpose(bias, (2, 0, 1)).reshape(ATT_KV_HEADS, ATT_GROUP, BLOCK, 3 * BLOCK)
    kpos = np.arange(nb)[:, None] * BLOCK + np.arange(3 * BLOCK)[None, :] - BLOCK
    mask = (np.abs(rel)[None] <= WINDOW) & (kpos[:, None, :] >= 0) & (kpos[:, None, :] < L)
    s = jnp.where(mask[None, :, None, None], s + bias, -1e30)
    sink_l = jnp.broadcast_to(sink.astype(F32).reshape(ATT_KV_HEADS, ATT_GROUP, 1, 1), s.shape[:-1] + (1,))
    p = jax.nn.softmax(jnp.concatenate([s, sink_l], -1), axis=-1)[..., :-1]
    o = jnp.einsum('bnkgqs,bnskd->bnqkgd', p.astype(v.dtype), vb)
    return o.reshape(B, L, ATT_WIDTH)


def _rotary(x):
    L = x.shape[1]
    inv = ROPE_BASE ** -jnp.linspace(0.0, 1.0, HEAD_DIM // 2, dtype=F32)
    ang = jnp.arange(L, dtype=F32)[:, None] * inv[None]
    cos, sin = jnp.cos(ang)[None, :, None], jnp.sin(ang)[None, :, None]
    x1, x2 = x[..., :HEAD_DIM // 2], x[..., HEAD_DIM // 2:]
    return jnp.concatenate([x1 * cos - x2 * sin, x2 * cos + x1 * sin], -1)


def _retention_dir(q, k, v, log_gamma, strict):
    B, L, H, d = q.shape
    C = RET_CHUNK
    nc = L // C
    qc, kc, vc = (t.reshape(B, nc, C, H, d) for t in (q, k, v))
    idx = jnp.arange(C, dtype=F32)
    diff = idx[:, None] - idx[None, :]
    allowed = (diff > 0) if strict else (diff >= 0)
    dmat = jnp.where(allowed[None], jnp.exp(jnp.maximum(diff, 0.0)[None] * log_gamma[:, None, None]), 0.0)
    inner = jnp.einsum('bnihd,bnjhd->bnhij', qc, kc) * dmat
    o_inner = jnp.einsum('bnhij,bnjhe->bnihe', inner, vc)
    zeta = jnp.exp((C - 1 - idx)[:, None] * log_gamma[None])
    kv = jnp.einsum('bnjhd,bnjhe->nbhde', kc * zeta[:, :, None], vc)
    g_chunk = jnp.exp(C * log_gamma)[None, :, None, None]

    def step(state, kv_c):
        return g_chunk * state + kv_c, state

    _, prev = lax.scan(step, jnp.zeros_like(kv[0]), kv)
    xi = jnp.exp((idx + 1.0)[:, None] * log_gamma[None])
    o_cross = jnp.einsum('bnihd,nbhde->bnihe', qc * xi[:, :, None], prev)
    return (o_inner + o_cross).reshape(B, L, H, d)


def _retention(q, k, v, g, decay):
    B, L = q.shape[:2]
    sh = (B, L, RET_HEADS, HEAD_DIM)
    q = _rotary(q.astype(F32).reshape(sh))
    k = _rotary(k.astype(F32).reshape(sh)) * (HEAD_DIM ** -0.5)
    v = v.astype(F32).reshape(sh)
    lg = -jnp.exp(decay.astype(F32))
    fwd = _retention_dir(q, k, v, lg[0], False)
    bwd = _retention_dir(q[:, ::-1], k[:, ::-1], v[:, ::-1], lg[1], True)[:, ::-1]
    y = fwd + bwd
    y = y * lax.rsqrt(jnp.mean(y * y, -1, keepdims=True) + EPS)
    return jax.nn.silu(g.astype(F32)) * y.reshape(B, L, RET_WIDTH)


def _hier_moe(x, wg, bg, we, be, w_gate, w_up, w_down):
    B, L, D = x.shape
    t = x.reshape(B * L, D)
    tf = t.astype(F32)
    g_logits = tf @ wg.astype(F32) + bg.astype(F32)
    g_prob = jax.nn.softmax(g_logits, -1)
    _, g_sel = lax.top_k(g_logits, 1)
    g_w = jnp.take_along_axis(g_prob, g_sel, -1)
    e_logits = (tf @ we.astype(F32) + be.astype(F32)).reshape(-1, N_GROUPS, EXPERTS_PER_GROUP)
    e_in = jnp.take_along_axis(e_logits, g_sel[:, :, None], 1)[:, 0]
    top_v, top_i = lax.top_k(e_in, TOP_K_IN_GROUP)
    top_w = jax.nn.softmax(top_v, -1) * g_w
    comb = jnp.sum(jax.nn.one_hot(top_i, EXPERTS_PER_GROUP, dtype=F32) * top_w[..., None], 1)
    comb = comb[:, None, :] * jax.nn.one_hot(g_sel[:, 0], N_GROUPS, dtype=F32)[:, :, None]
    wgt = w_gate.reshape(N_GROUPS, EXPERTS_PER_GROUP, D, D_EXPERT)
    wup = w_up.reshape(N_GROUPS, EXPERTS_PER_GROUP, D, D_EXPERT)
    wdn = w_down.reshape(N_GROUPS, EXPERTS_PER_GROUP, D_EXPERT, D)
    out = jnp.zeros((B * L, D), F32)
    for gi in range(N_GROUPS):
        h = jax.nn.silu(jnp.einsum('td,edf->tef', t, wgt[gi])) * jnp.einsum('td,edf->tef', t, wup[gi])
        h = h * comb[:, gi, :, None].astype(h.dtype)
        out = out + jnp.einsum('tef,efd->td', h, wdn[gi], preferred_element_type=F32)
    return out.reshape(B, L, D).astype(x.dtype)


def _layer(x, p, l):
    B, L, _ = x.shape
    h = _rmsnorm(x, p['ln1_g'][l])
    proj = jnp.einsum('bld,de->ble', h, p['w_in'][l])
    hy = proj[..., :HY_IN]
    att = proj[..., HY_IN:HY_IN + ATT_IN]
    ret = proj[..., HY_IN + ATT_IN:]
    y_h = _hyena(hy, p['hy_conv_w'][l], p['hy_conv_b'][l], p['hy_w1'][l], p['hy_b1'][l],
                 p['hy_w2'][l], p['hy_b2'][l], p['hy_w3'][l], p['hy_freq'][l],
                 p['hy_decay'][l], p['hy_skip'][l])
    q = att[..., :ATT_WIDTH].reshape(B, L, ATT_HEADS, HEAD_DIM)
    k = att[..., ATT_WIDTH:ATT_WIDTH + KV_WIDTH].reshape(B, L, ATT_KV_HEADS, HEAD_DIM)
    v = att[..., ATT_WIDTH + KV_WIDTH:].reshape(B, L, ATT_KV_HEADS, HEAD_DIM)
    y_a = _window_attention(q, k, v, p['attn_sink'][l], p['rel_bias'])
    rq, rk, rv, rg = jnp.split(ret, 4, -1)
    y_r = _retention(rq, rk, rv, rg, p['ret_decay'][l])
    mix = jnp.concatenate([y_h.astype(x.dtype), y_a.astype(x.dtype), y_r.astype(x.dtype)], -1)
    x = x + jnp.einsum('ble,ed->bld', mix, p['w_out'][l])
    x = x + _hier_moe(_rmsnorm(x, p['ln2_g'][l]), p['router_group_w'][l], p['router_group_b'][l],
                      p['router_expert_w'][l], p['router_expert_b'][l], p['moe_w_gate'][l],
                      p['moe_w_up'][l], p['moe_w_down'][l])
    return x


def _trunk(x, p):
    for l in range(DEPTH):
        x = _layer(x, p, l)
    return _rmsnorm(x, p['lnf_g'])


def setup_inputs(seed: int = 0) -> dict:
    key = jax.random.key(seed)
    ks = jax.random.split(key, 32)

    def nrm(k, shape, scale):
        return jax.random.normal(k, shape, F32) * scale

    base_delta = jnp.linspace(HY_MIN_DECAY, HY_MAX_DECAY, HY_WIDTH, dtype=F32)
    ret_base = jnp.log(-jnp.log(1.0 - 2.0 ** (-5.0 - jnp.arange(RET_HEADS, dtype=F32))))
    return {
        "x_prompt": nrm(ks[0], (BATCH, SEQ, D_MODEL), 1.0),
        "x_sample": nrm(ks[1], (DEC_BATCH, DEC_SEQ, D_MODEL), 1.0),
        "ln1_g": 1.0 + nrm(ks[2], (DEPTH, D_MODEL), 0.05),
        "w_in": nrm(ks[3], (DEPTH, D_MODEL, D_IN), D_MODEL ** -0.5),
        "hy_conv_w": nrm(ks[4], (DEPTH, 3, HY_IN), 3 ** -0.5),
        "hy_conv_b": nrm(ks[5], (DEPTH, HY_IN), 0.02),
        "hy_w1": nrm(ks[6], (DEPTH, HY_EMB, HY_FFN), HY_EMB ** -0.5),
        "hy_b1": nrm(ks[7], (DEPTH, HY_FFN), 0.1),
        "hy_w2": nrm(ks[8], (DEPTH, HY_FFN, HY_FFN), HY_FFN ** -0.5),
        "hy_b2": nrm(ks[9], (DEPTH, HY_FFN), 0.1),
        "hy_w3": nrm(ks[10], (DEPTH, HY_FFN, 2 * HY_ORDER * HY_WIDTH), HY_FFN ** -0.5),
        "hy_freq": 1.0 + nrm(ks[11], (DEPTH, HY_FFN), 0.1),
        "hy_decay": base_delta * (1.0 + nrm(ks[12], (DEPTH, 2, HY_ORDER, HY_WIDTH), 0.05)),
        "hy_skip": nrm(ks[13], (DEPTH, HY_ORDER, HY_WIDTH), 1.0),
        "attn_sink": nrm(ks[14], (DEPTH, ATT_HEADS), 0.5),
        "rel_bias": nrm(ks[15], (REL_BUCKETS, ATT_HEADS), 0.5),
        "ret_decay": ret_base + nrm(ks[16], (DEPTH, 2, RET_HEADS), 0.05),
        "w_out": nrm(ks[17], (DEPTH, D_MODEL, D_MODEL), D_MODEL ** -0.5),
        "ln2_g": 1.0 + nrm(ks[18], (DEPTH, D_MODEL), 0.05),
        "router_group_w": nrm(ks[19], (DEPTH, D_MODEL, N_GROUPS), D_MODEL ** -0.5),
        "router_group_b": nrm(ks[20], (DEPTH, N_GROUPS), 0.01),
        "router_expert_w": nrm(ks[21], (DEPTH, D_MODEL, N_EXPERTS), D_MODEL ** -0.5),
        "router_expert_b": nrm(ks[22], (DEPTH, N_EXPERTS), 0.01),
        "moe_w_gate": nrm(ks[23], (DEPTH, N_EXPERTS, D_MODEL, D_EXPERT), D_MODEL ** -0.5),
        "moe_w_up": nrm(ks[24], (DEPTH, N_EXPERTS, D_MODEL, D_EXPERT), D_MODEL ** -0.5),
        "moe_w_down": nrm(ks[25], (DEPTH, N_EXPERTS, D_EXPERT, D_MODEL), D_EXPERT ** -0.5),
        "lnf_g": 1.0 + nrm(ks[26], (D_MODEL,), 0.05),
    }


def reference(x_prompt, x_sample, ln1_g, w_in, hy_conv_w, hy_conv_b, hy_w1, hy_b1, hy_w2, hy_b2,
              hy_w3, hy_freq, hy_decay, hy_skip, attn_sink, rel_bias, ret_decay, w_out, ln2_g,
              router_group_w, router_group_b, router_expert_w, router_expert_b, moe_w_gate,
              moe_w_up, moe_w_down, lnf_g):
    p = dict(ln1_g=ln1_g, w_in=w_in, hy_conv_w=hy_conv_w, hy_conv_b=hy_conv_b, hy_w1=hy_w1,
             hy_b1=hy_b1, hy_w2=hy_w2, hy_b2=hy_b2, hy_w3=hy_w3, hy_freq=hy_freq,
             hy_decay=hy_decay, hy_skip=hy_skip, attn_sink=attn_sink, rel_bias=rel_bias,
             ret_decay=ret_decay, w_out=w_out, ln2_g=ln2_g, router_group_w=router_group_w,
             router_group_b=router_group_b, router_expert_w=router_expert_w,
             router_expert_b=router_expert_b, moe_w_gate=moe_w_gate, moe_w_up=moe_w_up,
             moe_w_down=moe_w_down, lnf_g=lnf_g)
    y_prompt = _trunk(x_prompt, p)
    y_sample = _trunk(x_sample, p)
    return (y_prompt, y_sample)
```

```python
import functools
import math

import numpy as np
import jax
import jax.numpy as jnp
from jax import lax
from jax.experimental import pallas as pl
from jax.experimental.pallas import tpu as pltpu

F32 = jnp.float32
BF16 = jnp.bfloat16
HIGHEST = lax.Precision.HIGHEST

D_MODEL = 2048
DEPTH = 2
HEAD_DIM = 64
EPS = 1e-6
HY_WIDTH = D_MODEL // 4
ATT_WIDTH = D_MODEL // 2
RET_WIDTH = D_MODEL // 4
HY_ORDER = 2
HY_EMB = 33
HY_FFN = 64
ATT_HEADS = ATT_WIDTH // HEAD_DIM
ATT_KV_HEADS = 4
ATT_GROUP = ATT_HEADS // ATT_KV_HEADS
KV_WIDTH = ATT_KV_HEADS * HEAD_DIM
WINDOW = 128
BLOCK = 128
REL_BUCKETS = 32
REL_MAX_DIST = 128
RET_HEADS = RET_WIDTH // HEAD_DIM
RET_CHUNK = 128
ROPE_BASE = 10000.0
N_GROUPS = 4
EXPERTS_PER_GROUP = 4
N_EXPERTS = N_GROUPS * EXPERTS_PER_GROUP
D_EXPERT = D_MODEL // 2
HY_IN = 3 * HY_WIDTH
ATT_IN = ATT_WIDTH + 2 * KV_WIDTH
RET_IN = 4 * RET_WIDTH

LANES = 128
MIB = 1024 * 1024


def _params(semantics, vmem_mib):
    return pltpu.CompilerParams(dimension_semantics=semantics, vmem_limit_bytes=vmem_mib * MIB)


def _dot(a, b):
    return jnp.dot(a, b, preferred_element_type=F32)


def _dot_nt(a, b):
    return lax.dot_general(a, b, (((1,), (1,)), ((), ())), preferred_element_type=F32)


def _norm_proj_kernel(x_ref, g_ref, w_ref, o_ref):
    x = x_ref[...]
    inv = lax.rsqrt(jnp.mean(x * x, axis=-1, keepdims=True) + EPS)
    h = (x * inv * g_ref[...]).astype(BF16)
    o_ref[...] = _dot(h, w_ref[...]).astype(o_ref.dtype)


def _norm_proj(x, g, w, out_dtype, tm=512):
    t, d = x.shape
    n = w.shape[1]
    return pl.pallas_call(
        _norm_proj_kernel,
        grid=(t // tm,),
        in_specs=[
            pl.BlockSpec((tm, d), lambda i: (i, 0)),
            pl.BlockSpec((1, d), lambda i: (0, 0)),
            pl.BlockSpec((d, n), lambda i: (0, 0)),
        ],
        out_specs=pl.BlockSpec((tm, n), lambda i: (i, 0)),
        out_shape=jax.ShapeDtypeStruct((t, n), out_dtype),
        compiler_params=_params(("parallel",), 48),
        name="norm_proj",
    )(x, g.reshape(1, d), w)


def _shortconv_kernel(p_ref, w_ref, b_ref, o_ref):
    x = p_ref[0]
    n = x.shape[0]
    row = lax.broadcasted_iota(jnp.int32, x.shape, 0)
    prev = jnp.where(row == 0, 0.0, pltpu.roll(x, 1, 0))
    nxt = jnp.where(row == n - 1, 0.0, pltpu.roll(x, n - 1, 0))
    o_ref[0] = prev * w_ref[0:1, :] + x * w_ref[1:2, :] + nxt * w_ref[2:3, :] + b_ref[...]


def _shortconv(p3, w, b, tc=256):
    bsz, n, c = p3.shape
    return pl.pallas_call(
        _shortconv_kernel,
        grid=(bsz, c // tc),
        in_specs=[
            pl.BlockSpec((1, n, tc), lambda i, j: (i, 0, j)),
            pl.BlockSpec((3, tc), lambda i, j: (0, j)),
            pl.BlockSpec((1, tc), lambda i, j: (0, j)),
        ],
        out_specs=pl.BlockSpec((1, n, tc), lambda i, j: (i, 0, j)),
        out_shape=jax.ShapeDtypeStruct((bsz, n, c), F32),
        compiler_params=_params(("parallel", "parallel"), 32),
        name="hy_shortconv",
    )(p3, w, b.reshape(1, c))


def _hy_filter_kernel(z_ref, w1_ref, b1_ref, w2_ref, b2_ref, fr_ref, w3f_ref, w3b_ref,
                      decf_ref, decb_ref, hs_ref, hd_ref):
    fr = fr_ref[...]
    h = jnp.sin(fr * (jnp.dot(z_ref[...], w1_ref[...], precision=HIGHEST,
                              preferred_element_type=F32) + b1_ref[...]))
    h = jnp.sin(fr * (jnp.dot(h, w2_ref[...], precision=HIGHEST,
                              preferred_element_type=F32) + b2_ref[...]))
    n = h.shape[0]
    shape = (n, w3f_ref.shape[1])
    row = lax.broadcasted_iota(jnp.int32, shape, 0)
    t = row.astype(F32) * (1.0 / (n - 1))
    hf = jnp.dot(h, w3f_ref[...], precision=HIGHEST, preferred_element_type=F32)
    hf = hf * jnp.exp(-t * jnp.abs(decf_ref[...]))
    hb = jnp.dot(h, w3b_ref[...], precision=HIGHEST, preferred_element_type=F32)
    hb = hb * jnp.exp(-t * jnp.abs(decb_ref[...]))
    hb = jnp.where(row == 0, 0.0, hb)
    norm = (jnp.sum(jnp.abs(hf), axis=0, keepdims=True)
            + jnp.sum(jnp.abs(hb), axis=0, keepdims=True))
    hs_ref[...] = (hf + hb) / norm
    hd_ref[...] = (hb - hf) / norm


def _hy_filter(z, w1, b1, w2, b2, fr, w3, decay, tc=256):
    n = z.shape[0]
    cols = HY_ORDER * HY_WIDTH
    w3f, w3b = w3[:, :cols], w3[:, cols:]
    dec = decay.reshape(2, cols)
    zp = jnp.pad(z, ((0, 0), (0, LANES - HY_EMB)))
    w1p = jnp.pad(w1, ((0, LANES - HY_EMB), (0, 0)))
    full = lambda shape: pl.BlockSpec(shape, lambda j: (0, 0))
    col = lambda rows: pl.BlockSpec((rows, tc), lambda j: (0, j))
    return pl.pallas_call(
        _hy_filter_kernel,
        grid=(cols // tc,),
        in_specs=[full((n, LANES)), full((LANES, HY_FFN)), full((1, HY_FFN)), full((HY_FFN, HY_FFN)),
                  full((1, HY_FFN)), full((1, HY_FFN)), col(HY_FFN), col(HY_FFN), col(1), col(1)],
        out_specs=[col(n), col(n)],
        out_shape=[jax.ShapeDtypeStruct((n, cols), F32)] * 2,
        compiler_params=_params(("parallel",), 48),
        name="hy_filter",
    )(zp, w1p, b1.reshape(1, -1), w2, b2.reshape(1, -1), fr.reshape(1, -1), w3f, w3b,
      dec[0:1], dec[1:2])


def _hy_kf_kernel(ff_ref, hs_ref, hd_ref, k_ref):
    hs = hs_ref[...]
    kre = _dot(ff_ref[0], hs.astype(BF16))
    kq = _dot(ff_ref[1], hd_ref[...].astype(BF16))
    row_t = lax.broadcasted_iota(jnp.int32, hs.shape, 0)
    sgn = (1 - 2 * (row_t & 1)).astype(F32)
    nyq = jnp.sum(hs * sgn, axis=0, keepdims=True)
    row = lax.broadcasted_iota(jnp.int32, kq.shape, 0)
    first = jnp.logical_and(row == 0, pl.program_id(0) == 0)
    k_ref[0] = kre
    k_ref[1] = jnp.where(first, nyq, kq)


def _hy_kf(ff, hs, hd, tf=512, tc=512):
    n, cols = hs.shape
    return pl.pallas_call(
        _hy_kf_kernel,
        grid=(n // tf, cols // tc),
        in_specs=[
            pl.BlockSpec((2, tf, n), lambda i, j: (0, i, 0)),
            pl.BlockSpec((n, tc), lambda i, j: (0, j)),
            pl.BlockSpec((n, tc), lambda i, j: (0, j)),
        ],
        out_specs=pl.BlockSpec((2, tf, tc), lambda i, j: (0, i, j)),
        out_shape=jax.ShapeDtypeStruct((2, n, cols), F32),
        compiler_params=_params(("parallel", "parallel"), 48),
        name="hy_filter_dft",
    )(ff, hs, hd)


def _hy_fwd_kernel(ff_ref, u_ref, k_ref, y_ref):
    u = u_ref[0].astype(BF16)
    a = _dot(ff_ref[0], u)
    bq = _dot(ff_ref[1], u)
    kre = k_ref[0]
    kq = k_ref[1]
    row = lax.broadcasted_iota(jnp.int32, a.shape, 0)
    first = jnp.logical_and(row == 0, pl.program_id(1) == 0)
    y_ref[0, 0] = jnp.where(first, a * kre, a * kre + bq * kq).astype(BF16)
    y_ref[0, 1] = jnp.where(first, bq * kq, a * kq - bq * kre).astype(BF16)


def _hy_fwd(ff, u3, ucol, k, kcol, tf=512):
    bsz, n, _ = u3.shape
    c = HY_WIDTH
    return pl.pallas_call(
        _hy_fwd_kernel,
        grid=(bsz, n // tf),
        in_specs=[
            pl.BlockSpec((2, tf, n), lambda b, i: (0, i, 0)),
            pl.BlockSpec((1, n, c), lambda b, i: (b, 0, ucol)),
            pl.BlockSpec((2, tf, c), lambda b, i: (0, i, kcol)),
        ],
        out_specs=pl.BlockSpec((1, 2, tf, c), lambda b, i: (b, 0, i, 0)),
        out_shape=jax.ShapeDtypeStruct((bsz, 2, n, c), BF16),
        compiler_params=_params(("parallel", "parallel"), 48),
        name="hy_dft_fwd",
    )(ff, u3, k)


def _hy_inv_kernel(fi_ref, y_ref, s_ref, g_ref, d_ref, o_ref):
    y = _dot(fi_ref[0], y_ref[0, 0]) + _dot(fi_ref[1], y_ref[0, 1])
    o_ref[0] = (g_ref[0] * (y + s_ref[0] * d_ref[...])).astype(o_ref.dtype)


def _hy_inv(fi, y, s3, scol, g3, gcol, d, out_dtype, tt=512):
    bsz, _, n, c = y.shape
    return pl.pallas_call(
        _hy_inv_kernel,
        grid=(bsz, n // tt),
        in_specs=[
            pl.BlockSpec((2, tt, n), lambda b, i: (0, i, 0)),
            pl.BlockSpec((1, 2, n, c), lambda b, i: (b, 0, 0, 0)),
            pl.BlockSpec((1, tt, c), lambda b, i: (b, i, scol)),
            pl.BlockSpec((1, tt, c), lambda b, i: (b, i, gcol)),
            pl.BlockSpec((1, c), lambda b, i: (0, 0)),
        ],
        out_specs=pl.BlockSpec((1, tt, c), lambda b, i: (b, i, 0)),
        out_shape=jax.ShapeDtypeStruct((bsz, n, c), out_dtype),
        compiler_params=_params(("parallel", "parallel"), 48),
        name="hy_dft_inv",
    )(fi, y, s3, g3, d.reshape(1, c))


def _dft_matrices(n):
    r = lax.broadcasted_iota(jnp.int32, (n, n), 0)
    c = lax.broadcasted_iota(jnp.int32, (n, n), 1)
    ang = ((r * c) & (2 * n - 1)).astype(F32) * (math.pi / n)
    cos, sin = jnp.cos(ang), jnp.sin(ang)
    sgn_c = (1 - 2 * (c & 1)).astype(F32)
    sgn_r = (1 - 2 * (r & 1)).astype(F32)
    fwd = jnp.stack([cos, jnp.where(r == 0, sgn_c, sin)]).astype(BF16)
    inv0 = jnp.where(c == 0, 1.0, 2.0) * cos * (0.5 / n)
    inv1 = jnp.where(c == 0, sgn_r * (0.5 / n), sin * (-1.0 / n))
    inv = jnp.stack([inv0, inv1]).astype(BF16)
    return fwd, inv


def _hyena_embedding(n):
    t = jnp.linspace(0.0, 1.0, n, dtype=F32)[:, None]
    bands = (HY_EMB - 1) // 2
    f = jnp.linspace(1e-4, bands - 1, bands, dtype=F32)[None]
    w = 2.0 * math.pi * jnp.arange(n, dtype=F32)[:, None] / n
    return jnp.concatenate([t, jnp.cos(f * w), -jnp.sin(f * w)], -1)


def _hyena(p3, dft, emb, conv_w, conv_b, w1, b1, w2, b2, w3, freq, decay, skip):
    ff, fi = dft
    u = _shortconv(p3, conv_w, conv_b)
    hs, hd = _hy_filter(emb, w1, b1, w2, b2, freq, w3, decay)
    k = _hy_kf(ff, hs, hd)
    y = _hy_fwd(ff, u, 0, k, 0)
    z = _hy_inv(fi, y, u, 0, u, 1, skip[0], F32)
    y = _hy_fwd(ff, z, 0, k, 1)
    return _hy_inv(fi, y, z, 0, u, 2, skip[1], BF16)


def _t5_bucket(rel):
    nb = REL_BUCKETS // 2
    max_exact = nb // 2
    n = np.abs(rel)
    large = max_exact + (np.log(np.maximum(n, 1) / max_exact) / np.log(REL_MAX_DIST / max_exact)
                         * (nb - max_exact)).astype(np.int32)
    large = np.minimum(large, nb - 1)
    return (rel > 0).astype(np.int32) * nb + np.where(n < max_exact, n, large)


def _attn_kernel(seq_len, q_ref, kp_ref, kc_ref, kn_ref, vp_ref, vc_ref, vn_ref, bias_ref, sink_ref,
                 o_ref):
    n = pl.program_id(1)
    q = q_ref[0]
    k = jnp.concatenate([kp_ref[0], kc_ref[0], kn_ref[0]], axis=0)
    v = jnp.concatenate([vp_ref[0], vc_ref[0], vn_ref[0]], axis=0)
    rows = ATT_GROUP * BLOCK
    qi = lax.broadcasted_iota(jnp.int32, (rows, 3 * BLOCK), 0) & (BLOCK - 1)
    sj = lax.broadcasted_iota(jnp.int32, (rows, 3 * BLOCK), 1)
    rel = sj - BLOCK - qi
    kpos = n * BLOCK + sj - BLOCK
    valid = (jnp.abs(rel) <= WINDOW) & (kpos >= 0) & (kpos < seq_len)
    outs = []
    for kh in range(ATT_KV_HEADS):
        k_h = k[:, kh * HEAD_DIM:(kh + 1) * HEAD_DIM]
        v_h = v[:, kh * HEAD_DIM:(kh + 1) * HEAD_DIM]
        heads = [kh * ATT_GROUP + g for g in range(ATT_GROUP)]
        q_h = jnp.concatenate([q[:, h * HEAD_DIM:(h + 1) * HEAD_DIM] for h in heads], axis=0)
        sink = jnp.concatenate([jnp.full((BLOCK, 1), sink_ref[h], F32) for h in heads], axis=0)
        s = _dot_nt(q_h, k_h) * (HEAD_DIM ** -0.5)
        s = jnp.where(valid, s + bias_ref[kh], -1e30)
        m = jnp.maximum(jnp.max(s, axis=-1, keepdims=True), sink)
        p = jnp.exp(s - m)
        denom = jnp.sum(p, axis=-1, keepdims=True) + jnp.exp(sink - m)
        o = _dot(p.astype(BF16), v_h) / denom
        outs.extend(o[g * BLOCK:(g + 1) * BLOCK] for g in range(ATT_GROUP))
    o_ref[0] = jnp.concatenate(outs, axis=-1).astype(o_ref.dtype)


def _attention(att3, sink, rel_bias):
    bsz, n, _ = att3.shape
    nb = n // BLOCK
    rel = (np.arange(3 * BLOCK)[None, :] - BLOCK) - np.arange(BLOCK)[:, None]
    bias = rel_bias.astype(F32)[_t5_bucket(rel)]
    bias = jnp.transpose(bias, (2, 0, 1)).reshape(ATT_KV_HEADS, ATT_GROUP * BLOCK, 3 * BLOCK)
    kcol = ATT_WIDTH // KV_WIDTH
    vcol = kcol + 1
    prev = lambda b, i: jnp.maximum(i - 1, 0)
    nxt = lambda b, i: jnp.minimum(i + 1, nb - 1)
    kv_spec = lambda blk, col: pl.BlockSpec((1, BLOCK, KV_WIDTH), lambda b, i: (b, blk(b, i), col))
    cur = lambda b, i: i
    return pl.pallas_call(
        functools.partial(_attn_kernel, n),
        grid=(bsz, nb),
        in_specs=[
            pl.BlockSpec((1, BLOCK, ATT_WIDTH), lambda b, i: (b, i, 0)),
            kv_spec(prev, kcol), kv_spec(cur, kcol), kv_spec(nxt, kcol),
            kv_spec(prev, vcol), kv_spec(cur, vcol), kv_spec(nxt, vcol),
            pl.BlockSpec((ATT_KV_HEADS, ATT_GROUP * BLOCK, 3 * BLOCK), lambda b, i: (0, 0, 0)),
            pl.BlockSpec(memory_space=pltpu.SMEM),
        ],
        out_specs=pl.BlockSpec((1, BLOCK, ATT_WIDTH), lambda b, i: (b, i, 0)),
        out_shape=jax.ShapeDtypeStruct((bsz, n, ATT_WIDTH), BF16),
        compiler_params=_params(("parallel", "parallel"), 32),
        name="window_attention",
    )(att3, att3, att3, att3, att3, att3, att3, bias, sink.astype(F32))


def _rotary_tables(n):
    inv = ROPE_BASE ** -jnp.linspace(0.0, 1.0, HEAD_DIM // 2, dtype=F32)
    ang = jnp.arange(n, dtype=F32)[:, None] * inv[None]
    cos, sin = jnp.cos(ang), jnp.sin(ang)
    cos_t = jnp.tile(jnp.concatenate([cos, cos], -1), (1, RET_HEADS))
    sin_t = jnp.tile(jnp.concatenate([-sin, sin], -1), (1, RET_HEADS))
    return cos_t, sin_t


def _rotate(x, cos_t, sin_t):
    width = x.shape[-1]
    half = HEAD_DIM // 2
    lane = lax.broadcasted_iota(jnp.int32, x.shape, 1)
    partner = jnp.where((lane & (HEAD_DIM - 1)) < half,
                        pltpu.roll(x, width - half, 1), pltpu.roll(x, half, 1))
    return x * cos_t + partner * sin_t


def _lane_head(shape):
    return lax.broadcasted_iota(jnp.int32, shape, 1) // HEAD_DIM


def _head_blocks_to_compact(full):
    head = _lane_head((HEAD_DIM, RET_WIDTH))
    acc = jnp.zeros((HEAD_DIM, RET_WIDTH), F32)
    for h in range(RET_HEADS):
        acc = acc + jnp.where(head == h, full[h * HEAD_DIM:(h + 1) * HEAD_DIM], 0.0)
    return acc


def _compact_to_head_blocks(compact):
    head = _lane_head((HEAD_DIM, RET_WIDTH))
    return jnp.concatenate([jnp.where(head == h, compact, 0.0) for h in range(RET_HEADS)], axis=0)


def _ret_state_kernel(kf_ref, vf_ref, kb_ref, vb_ref, cf_ref, sf_ref, cb_ref, sb_ref, lg_ref,
                      of_ref, ob_ref, stf_ref, stb_ref):
    @pl.when(pl.program_id(1) == 0)
    def _():
        stf_ref[...] = jnp.zeros_like(stf_ref)
        stb_ref[...] = jnp.zeros_like(stb_ref)

    of_ref[0, 0] = stf_ref[...]
    ob_ref[0, 0] = stb_ref[...]
    c = RET_CHUNK
    j = lax.broadcasted_iota(jnp.int32, (c, RET_WIDTH), 0).astype(F32)
    lgf = lg_ref[0:1, :]
    lgb = lg_ref[1:2, :]
    scale = HEAD_DIM ** -0.5

    def chunk_state(k_ref, v_ref, cos_ref, sin_ref, zeta):
        k = _rotate(k_ref[0], cos_ref[...], sin_ref[...]) * scale * zeta
        kv = _dot(k.T.astype(BF16), v_ref[0].astype(BF16))
        return _head_blocks_to_compact(kv)

    kvf = chunk_state(kf_ref, vf_ref, cf_ref, sf_ref, jnp.exp((c - 1.0 - j) * lgf))
    kvb = chunk_state(kb_ref, vb_ref, cb_ref, sb_ref, jnp.exp(j * lgb))
    stf_ref[...] = jnp.exp(c * lgf) * stf_ref[...] + kvf
    stb_ref[...] = jnp.exp(c * lgb) * stb_ref[...] + kvb


def _ret_states(ret3, cos_t, sin_t, lg_lane):
    bsz, n, _ = ret3.shape
    nc = n // RET_CHUNK
    c, w = RET_CHUNK, RET_WIDTH
    fwd = lambda b, i: i
    bwd = lambda b, i: nc - 1 - i
    col = lambda pos, cb: pl.BlockSpec((1, c, w), lambda b, i: (b, pos(b, i), cb))
    tab = lambda pos: pl.BlockSpec((c, w), lambda b, i: (pos(b, i), 0))
    out = lambda pos: pl.BlockSpec((1, 1, HEAD_DIM, w), lambda b, i: (b, pos(b, i), 0, 0))
    return pl.pallas_call(
        _ret_state_kernel,
        grid=(bsz, nc),
        in_specs=[col(fwd, 1), col(fwd, 2), col(bwd, 1), col(bwd, 2),
                  tab(fwd), tab(fwd), tab(bwd), tab(bwd),
                  pl.BlockSpec((2, w), lambda b, i: (0, 0))],
        out_specs=[out(fwd), out(bwd)],
        out_shape=[jax.ShapeDtypeStruct((bsz, nc, HEAD_DIM, w), F32)] * 2,
        scratch_shapes=[pltpu.VMEM((HEAD_DIM, w), F32), pltpu.VMEM((HEAD_DIM, w), F32)],
        compiler_params=_params(("parallel", "arbitrary"), 32),
        name="retention_states",
    )(ret3, ret3, ret3, ret3, cos_t, sin_t, cos_t, sin_t, lg_lane)


def _ret_out_kernel(q_ref, k_ref, v_ref, g_ref, cos_ref, sin_ref, lg_ref, lgs_ref, stf_ref, stb_ref,
                    o_ref):
    c, w = RET_CHUNK, RET_WIDTH
    cos_t, sin_t = cos_ref[...], sin_ref[...]
    q = _rotate(q_ref[0], cos_t, sin_t)
    k = _rotate(k_ref[0], cos_t, sin_t) * (HEAD_DIM ** -0.5)
    v = v_ref[0].astype(BF16)
    head = _lane_head((c, w))
    qs = jnp.concatenate([jnp.where(head == h, q, 0.0) for h in range(RET_HEADS)], axis=0)
    s = _dot_nt(qs.astype(BF16), k.astype(BF16))
    i = lax.broadcasted_iota(jnp.int32, (c, c), 0)
    jj = lax.broadcasted_iota(jnp.int32, (c, c), 1)
    diff = (i - jj).astype(F32)
    decay = jnp.concatenate(
        [jnp.exp(jnp.where(diff >= 0, diff * lgs_ref[0, h], -diff * lgs_ref[1, h]))
         for h in range(RET_HEADS)], axis=0)
    o_all = _dot((s * decay).astype(BF16), v)
    y = jnp.zeros((c, w), F32)
    for h in range(RET_HEADS):
        y = y + jnp.where(head == h, o_all[h * c:(h + 1) * c], 0.0)
    pos = lax.broadcasted_iota(jnp.int32, (c, w), 0).astype(F32)
    xi_f = jnp.exp((pos + 1.0) * lg_ref[0:1, :])
    xi_b = jnp.exp((c - pos) * lg_ref[1:2, :])
    y = y + _dot((q * xi_f).astype(BF16), _compact_to_head_blocks(stf_ref[0, 0]).astype(BF16))
    y = y + _dot((q * xi_b).astype(BF16), _compact_to_head_blocks(stb_ref[0, 0]).astype(BF16))
    r = lax.broadcasted_iota(jnp.int32, (w, w), 0) // HEAD_DIM
    avg = jnp.where(r == _lane_head((w, w)), 1.0 / HEAD_DIM, 0.0)
    ms = jnp.dot(y * y, avg, precision=HIGHEST, preferred_element_type=F32)
    y = y * lax.rsqrt(ms + EPS)
    g = g_ref[0]
    o_ref[0] = (g / (1.0 + jnp.exp(-g)) * y).astype(o_ref.dtype)


def _retention(ret3, decay):
    bsz, n, _ = ret3.shape
    nc = n // RET_CHUNK
    c, w = RET_CHUNK, RET_WIDTH
    lg = -jnp.exp(decay.astype(F32))
    lg_lane = jnp.repeat(lg, HEAD_DIM, axis=1)
    cos_t, sin_t = _rotary_tables(n)
    stf, stb = _ret_states(ret3, cos_t, sin_t, lg_lane)
    col = lambda cb: pl.BlockSpec((1, c, w), lambda b, i: (b, i, cb))
    tab = pl.BlockSpec((c, w), lambda b, i: (i, 0))
    st = pl.BlockSpec((1, 1, HEAD_DIM, w), lambda b, i: (b, i, 0, 0))
    return pl.pallas_call(
        _ret_out_kernel,
        grid=(bsz, nc),
        in_specs=[col(0), col(1), col(2), col(3), tab, tab,
                  pl.BlockSpec((2, w), lambda b, i: (0, 0)),
                  pl.BlockSpec(memory_space=pltpu.SMEM), st, st],
        out_specs=pl.BlockSpec((1, c, w), lambda b, i: (b, i, 0)),
        out_shape=jax.ShapeDtypeStruct((bsz, n, w), BF16),
        compiler_params=_params(("parallel", "parallel"), 32),
        name="retention_out",
    )(ret3, ret3, ret3, ret3, cos_t, sin_t, lg_lane, lg, stf, stb)


def _mix_router_kernel(x_ref, yh_ref, ya_ref, yr_ref, w_ref, g_ref, wr_ref, br_ref,
                       xo_ref, h_ref, comb_ref):
    a, b = HY_WIDTH, HY_WIDTH + ATT_WIDTH
    x = x_ref[...] + _dot(yh_ref[...], w_ref[0:a, :]) + _dot(ya_ref[...], w_ref[a:b, :]) \
        + _dot(yr_ref[...], w_ref[b:, :])
    xo_ref[...] = x
    h = x * lax.rsqrt(jnp.mean(x * x, axis=-1, keepdims=True) + EPS) * g_ref[...]
    h_ref[...] = h.astype(BF16)
    logits = jnp.dot(h, wr_ref[...], precision=HIGHEST, preferred_element_type=F32) + br_ref[...]
    lane = lax.broadcasted_iota(jnp.int32, logits.shape, 1)
    big = jnp.int32(LANES)
    neg = -jnp.inf
    is_group = (lane >= N_EXPERTS) & (lane < N_EXPERTS + N_GROUPS)
    gl = jnp.where(is_group, logits, neg)
    gmax = jnp.max(gl, axis=-1, keepdims=True)
    g_w = 1.0 / jnp.sum(jnp.exp(gl - gmax), axis=-1, keepdims=True)
    g_sel = jnp.min(jnp.where(gl == gmax, lane, big), axis=-1, keepdims=True) - N_EXPERTS
    lo = g_sel * EXPERTS_PER_GROUP
    el = jnp.where((lane >= lo) & (lane < lo + EXPERTS_PER_GROUP), logits, neg)
    v0 = jnp.max(el, axis=-1, keepdims=True)
    i0 = jnp.min(jnp.where(el == v0, lane, big), axis=-1, keepdims=True)
    el = jnp.where(lane == i0, neg, el)
    v1 = jnp.max(el, axis=-1, keepdims=True)
    i1 = jnp.min(jnp.where(el == v1, lane, big), axis=-1, keepdims=True)
    e1 = jnp.exp(v1 - v0)
    w0 = g_w / (1.0 + e1)
    comb_ref[...] = jnp.where(lane == i0, w0, 0.0) + jnp.where(lane == i1, w0 * e1, 0.0)


def _mix_router(x, yh, ya, yr, w_out, g, wr, br, tm=256):
    t, d = x.shape
    row = lambda width: pl.BlockSpec((tm, width), lambda i: (i, 0))
    full = lambda shape: pl.BlockSpec(shape, lambda i: (0, 0))
    return pl.pallas_call(
        _mix_router_kernel,
        grid=(t // tm,),
        in_specs=[row(d), row(HY_WIDTH), row(ATT_WIDTH), row(RET_WIDTH), full((d, d)),
                  full((1, d)), full((d, LANES)), full((1, LANES))],
        out_specs=[row(d), row(d), row(LANES)],
        out_shape=[jax.ShapeDtypeStruct((t, d), F32), jax.ShapeDtypeStruct((t, d), BF16),
                   jax.ShapeDtypeStruct((t, LANES), F32)],
        compiler_params=_params(("parallel",), 48),
        name="mix_router",
    )(x, yh, ya, yr, w_out, g.reshape(1, d), wr, br)


def _moe_dense_kernel(final_norm, h_ref, comb_ref, x_ref, wg_ref, wu_ref, wd_ref, gf_ref, o_ref,
                      acc_ref):
    e = pl.program_id(1)
    f = pl.program_id(2)

    @pl.when((e == 0) & (f == 0))
    def _():
        acc_ref[...] = jnp.zeros_like(acc_ref)

    h = h_ref[...]
    a = _dot(h, wg_ref[0])
    b = _dot(h, wu_ref[0])
    comb = comb_ref[...]
    lane = lax.broadcasted_iota(jnp.int32, comb.shape, 1)
    c = jnp.sum(jnp.where(lane == e, comb, 0.0), axis=-1, keepdims=True)
    act = a / (1.0 + jnp.exp(-a)) * b * c
    acc_ref[...] += _dot(act.astype(BF16), wd_ref[0])

    @pl.when((e == pl.num_programs(1) - 1) & (f == pl.num_programs(2) - 1))
    def _():
        y = x_ref[...] + acc_ref[...]
        if final_norm:
            y = y * lax.rsqrt(jnp.mean(y * y, axis=-1, keepdims=True) + EPS) * gf_ref[...]
        o_ref[...] = y


def _moe_dense(h, comb, x, wg, wu, wd, gf, final_norm, tm=512, tf=512):
    t, d = x.shape
    ne, _, de = wg.shape
    return pl.pallas_call(
        functools.partial(_moe_dense_kernel, final_norm),
        grid=(t // tm, ne, de // tf),
        in_specs=[
            pl.BlockSpec((tm, d), lambda i, e, f: (i, 0)),
            pl.BlockSpec((tm, LANES), lambda i, e, f: (i, 0)),
            pl.BlockSpec((tm, d), lambda i, e, f: (i, 0)),
            pl.BlockSpec((1, d, tf), lambda i, e, f: (e, 0, f)),
            pl.BlockSpec((1, d, tf), lambda i, e, f: (e, 0, f)),
            pl.BlockSpec((1, tf, d), lambda i, e, f: (e, f, 0)),
            pl.BlockSpec((1, d), lambda i, e, f: (0, 0)),
        ],
        out_specs=pl.BlockSpec((tm, d), lambda i, e, f: (i, 0)),
        out_shape=jax.ShapeDtypeStruct((t, d), F32),
        scratch_shapes=[pltpu.VMEM((tm, d), F32)],
        compiler_params=_params(("parallel", "arbitrary", "arbitrary"), 48),
        name="moe_dense",
    )(h, comb, x, wg, wu, wd, gf.reshape(1, d))


def _router_weights(wg, bg, we, be):
    d = wg.shape[0]
    pad = LANES - N_EXPERTS - N_GROUPS
    wr = jnp.concatenate([we, wg, jnp.zeros((d, pad), F32)], axis=1).astype(F32)
    br = jnp.concatenate([be, bg, jnp.zeros((pad,), F32)]).astype(F32).reshape(1, LANES)
    return wr, br


def _layer(x, bsz, p, l, consts, final_g):
    t, d = x.shape
    n = t // bsz
    dft, emb = consts
    w_in = p['w_in'][l].astype(BF16)
    g1 = p['ln1_g'][l]
    hy = _norm_proj(x, g1, w_in[:, :HY_IN], F32).reshape(bsz, n, HY_IN)
    att = _norm_proj(x, g1, w_in[:, HY_IN:HY_IN + ATT_IN], BF16).reshape(bsz, n, ATT_IN)
    ret = _norm_proj(x, g1, w_in[:, HY_IN + ATT_IN:], F32).reshape(bsz, n, RET_IN)
    y_h = _hyena(hy, dft, emb, p['hy_conv_w'][l], p['hy_conv_b'][l], p['hy_w1'][l], p['hy_b1'][l],
                 p['hy_w2'][l], p['hy_b2'][l], p['hy_w3'][l], p['hy_freq'][l], p['hy_decay'][l],
                 p['hy_skip'][l])
    y_a = _attention(att, p['attn_sink'][l], p['rel_bias'])
    y_r = _retention(ret, p['ret_decay'][l])
    wr, br = _router_weights(p['router_group_w'][l], p['router_group_b'][l],
                             p['router_expert_w'][l], p['router_expert_b'][l])
    x, h, comb = _mix_router(x, y_h.reshape(t, -1), y_a.reshape(t, -1), y_r.reshape(t, -1),
                             p['w_out'][l].astype(BF16), p['ln2_g'][l], wr, br)
    return _moe_dense(h, comb, x, p['moe_w_gate'][l].astype(BF16), p['moe_w_up'][l].astype(BF16),
                      p['moe_w_down'][l].astype(BF16), final_g, final_norm=(l == DEPTH - 1))


def kernel(x_prompt, x_sample, ln1_g, w_in, hy_conv_w, hy_conv_b, hy_w1, hy_b1, hy_w2, hy_b2, hy_w3,
           hy_freq, hy_decay, hy_skip, attn_sink, rel_bias, ret_decay, w_out, ln2_g, router_group_w,
           router_group_b, router_expert_w, router_expert_b, moe_w_gate, moe_w_up, moe_w_down, lnf_g):
    p = dict(ln1_g=ln1_g, w_in=w_in, hy_conv_w=hy_conv_w, hy_conv_b=hy_conv_b, hy_w1=hy_w1,
             hy_b1=hy_b1, hy_w2=hy_w2, hy_b2=hy_b2, hy_w3=hy_w3, hy_freq=hy_freq, hy_decay=hy_decay,
             hy_skip=hy_skip, attn_sink=attn_sink, rel_bias=rel_bias, ret_decay=ret_decay, w_out=w_out,
             ln2_g=ln2_g, router_group_w=router_group_w, router_group_b=router_group_b,
             router_expert_w=router_expert_w, router_expert_b=router_expert_b, moe_w_gate=moe_w_gate,
             moe_w_up=moe_w_up, moe_w_down=moe_w_down)
    bp, n, d = x_prompt.shape
    bs = x_sample.shape[0]
    assert x_sample.shape[1:] == (n, d)
    bsz = bp + bs
    x = jnp.concatenate([x_prompt, x_sample], axis=0).reshape(bsz * n, d)
    consts = (_dft_matrices(n), _hyena_embedding(n))
    for l in range(DEPTH):
        x = _layer(x, bsz, p, l, consts, lnf_g)
    y = x.reshape(bsz, n, d)
    return (y[:bp], y[bp:])
```

```python
import functools
import math

import numpy as np
import jax
import jax.numpy as jnp
from jax import lax
from jax.experimental import pallas as pl
from jax.experimental.pallas import tpu as pltpu

F32 = jnp.float32
BF16 = jnp.bfloat16
HIGHEST = lax.Precision.HIGHEST

D_MODEL = 2048
DEPTH = 2
HEAD_DIM = 64
EPS = 1e-6
HY_WIDTH = D_MODEL // 4
ATT_WIDTH = D_MODEL // 2
RET_WIDTH = D_MODEL // 4
HY_ORDER = 2
HY_EMB = 33
HY_FFN = 64
ATT_HEADS = ATT_WIDTH // HEAD_DIM
ATT_KV_HEADS = 4
ATT_GROUP = ATT_HEADS // ATT_KV_HEADS
KV_WIDTH = ATT_KV_HEADS * HEAD_DIM
WINDOW = 128
BLOCK = 128
REL_BUCKETS = 32
REL_MAX_DIST = 128
RET_HEADS = RET_WIDTH // HEAD_DIM
RET_CHUNK = 128
ROPE_BASE = 10000.0
N_GROUPS = 4
EXPERTS_PER_GROUP = 4
N_EXPERTS = N_GROUPS * EXPERTS_PER_GROUP
D_EXPERT = D_MODEL // 2
HY_IN = 3 * HY_WIDTH
ATT_IN = ATT_WIDTH + 2 * KV_WIDTH
RET_IN = 4 * RET_WIDTH

LANES = 128
MIB = 1024 * 1024


def _params(semantics, vmem_mib):
    return pltpu.CompilerParams(dimension_semantics=semantics, vmem_limit_bytes=vmem_mib * MIB)


def _dot(a, b):
    return jnp.dot(a, b, preferred_element_type=F32)


def _dot_nt(a, b):
    return lax.dot_general(a, b, (((1,), (1,)), ((), ())), preferred_element_type=F32)


def _norm_proj_kernel(x_ref, g_ref, w_ref, o_ref):
    x = x_ref[...]
    inv = lax.rsqrt(jnp.mean(x * x, axis=-1, keepdims=True) + EPS)
    h = (x * inv * g_ref[...]).astype(BF16)
    o_ref[...] = _dot(h, w_ref[...]).astype(o_ref.dtype)


def _norm_proj(x, g, w, out_dtype, tm=512):
    t, d = x.shape
    n = w.shape[1]
    return pl.pallas_call(
        _norm_proj_kernel,
        grid=(t // tm,),
        in_specs=[
            pl.BlockSpec((tm, d), lambda i: (i, 0)),
            pl.BlockSpec((1, d), lambda i: (0, 0)),
            pl.BlockSpec((d, n), lambda i: (0, 0)),
        ],
        out_specs=pl.BlockSpec((tm, n), lambda i: (i, 0)),
        out_shape=jax.ShapeDtypeStruct((t, n), out_dtype),
        compiler_params=_params(("parallel",), 48),
        name="norm_proj",
    )(x, g.reshape(1, d), w)


def _shortconv_kernel(p_ref, w_ref, b_ref, o_ref):
    x = p_ref[0]
    n = x.shape[0]
    row = lax.broadcasted_iota(jnp.int32, x.shape, 0)
    prev = jnp.where(row == 0, 0.0, pltpu.roll(x, 1, 0))
    nxt = jnp.where(row == n - 1, 0.0, pltpu.roll(x, n - 1, 0))
    o_ref[0] = prev * w_ref[0:1, :] + x * w_ref[1:2, :] + nxt * w_ref[2:3, :] + b_ref[...]


def _shortconv(p3, w, b, tc=256):
    bsz, n, c = p3.shape
    return pl.pallas_call(
        _shortconv_kernel,
        grid=(bsz, c // tc),
        in_specs=[
            pl.BlockSpec((1, n, tc), lambda i, j: (i, 0, j)),
            pl.BlockSpec((3, tc), lambda i, j: (0, j)),
            pl.BlockSpec((1, tc), lambda i, j: (0, j)),
        ],
        out_specs=pl.BlockSpec((1, n, tc), lambda i, j: (i, 0, j)),
        out_shape=jax.ShapeDtypeStruct((bsz, n, c), F32),
        compiler_params=_params(("parallel", "parallel"), 32),
        name="hy_shortconv",
    )(p3, w, b.reshape(1, c))


def _hy_filter_kernel(z_ref, w1_ref, b1_ref, w2_ref, b2_ref, fr_ref, w3f_ref, w3b_ref,
                      decf_ref, decb_ref, hs_ref, hd_ref):
    fr = fr_ref[...]
    h = jnp.sin(fr * (jnp.dot(z_ref[...], w1_ref[...], precision=HIGHEST,
                              preferred_element_type=F32) + b1_ref[...]))
    h = jnp.sin(fr * (jnp.dot(h, w2_ref[...], precision=HIGHEST,
                              preferred_element_type=F32) + b2_ref[...]))
    n = h.shape[0]
    shape = (n, w3f_ref.shape[1])
    row = lax.broadcasted_iota(jnp.int32, shape, 0)
    t = row.astype(F32) * (1.0 / (n - 1))
    hf = jnp.dot(h, w3f_ref[...], precision=HIGHEST, preferred_element_type=F32)
    hf = hf * jnp.exp(-t * jnp.abs(decf_ref[...]))
    hb = jnp.dot(h, w3b_ref[...], precision=HIGHEST, preferred_element_type=F32)
    hb = hb * jnp.exp(-t * jnp.abs(decb_ref[...]))
    hb = jnp.where(row == 0, 0.0, hb)
    norm = (jnp.sum(jnp.abs(hf), axis=0, keepdims=True)
            + jnp.sum(jnp.abs(hb), axis=0, keepdims=True))
    hs_ref[...] = (hf + hb) / norm
    hd_ref[...] = (hb - hf) / norm


def _hy_filter(z, w1, b1, w2, b2, fr, w3, decay, tc=256):
    n = z.shape[0]
    cols = HY_ORDER * HY_WIDTH
    w3f, w3b = w3[:, :cols], w3[:, cols:]
    dec = decay.reshape(2, cols)
    zp = jnp.pad(z, ((0, 0), (0, LANES - HY_EMB)))
    w1p = jnp.pad(w1, ((0, LANES - HY_EMB), (0, 0)))
    full = lambda shape: pl.BlockSpec(shape, lambda j: (0, 0))
    col = lambda rows: pl.BlockSpec((rows, tc), lambda j: (0, j))
    return pl.pallas_call(
        _hy_filter_kernel,
        grid=(cols // tc,),
        in_specs=[full((n, LANES)), full((LANES, HY_FFN)), full((1, HY_FFN)), full((HY_FFN, HY_FFN)),
                  full((1, HY_FFN)), full((1, HY_FFN)), col(HY_FFN), col(HY_FFN), col(1), col(1)],
        out_specs=[col(n), col(n)],
        out_shape=[jax.ShapeDtypeStruct((n, cols), F32)] * 2,
        compiler_params=_params(("parallel",), 48),
        name="hy_filter",
    )(zp, w1p, b1.reshape(1, -1), w2, b2.reshape(1, -1), fr.reshape(1, -1), w3f, w3b,
      dec[0:1], dec[1:2])


def _hy_kf_kernel(ff_ref, hs_ref, hd_ref, k_ref):
    hs = hs_ref[...]
    kre = _dot(ff_ref[0], hs.astype(BF16))
    kq = _dot(ff_ref[1], hd_ref[...].astype(BF16))
    row_t = lax.broadcasted_iota(jnp.int32, hs.shape, 0)
    sgn = (1 - 2 * (row_t & 1)).astype(F32)
    nyq = jnp.sum(hs * sgn, axis=0, keepdims=True)
    row = lax.broadcasted_iota(jnp.int32, kq.shape, 0)
    first = jnp.logical_and(row == 0, pl.program_id(0) == 0)
    k_ref[0] = kre
    k_ref[1] = jnp.where(first, nyq, kq)


def _hy_kf(ff, hs, hd, tf=512, tc=512):
    n, cols = hs.shape
    return pl.pallas_call(
        _hy_kf_kernel,
        grid=(n // tf, cols // tc),
        in_specs=[
            pl.BlockSpec((2, tf, n), lambda i, j: (0, i, 0)),
            pl.BlockSpec((n, tc), lambda i, j: (0, j)),
            pl.BlockSpec((n, tc), lambda i, j: (0, j)),
        ],
        out_specs=pl.BlockSpec((2, tf, tc), lambda i, j: (0, i, j)),
        out_shape=jax.ShapeDtypeStruct((2, n, cols), F32),
        compiler_params=_params(("parallel", "parallel"), 48),
        name="hy_filter_dft",
    )(ff, hs, hd)


def _hy_fwd_kernel(ff_ref, u_ref, k_ref, y_ref):
    u = u_ref[0].astype(BF16)
    a = _dot(ff_ref[0], u)
    bq = _dot(ff_ref[1], u)
    kre = k_ref[0]
    kq = k_ref[1]
    row = lax.broadcasted_iota(jnp.int32, a.shape, 0)
    first = jnp.logical_and(row == 0, pl.program_id(1) == 0)
    y_ref[0, 0] = jnp.where(first, a * kre, a * kre + bq * kq).astype(BF16)
    y_ref[0, 1] = jnp.where(first, bq * kq, a * kq - bq * kre).astype(BF16)


def _hy_fwd(ff, u3, ucol, k, kcol, tf=512):
    bsz, n, _ = u3.shape
    c = HY_WIDTH
    return pl.pallas_call(
        _hy_fwd_kernel,
        grid=(bsz, n // tf),
        in_specs=[
            pl.BlockSpec((2, tf, n), lambda b, i: (0, i, 0)),
            pl.BlockSpec((1, n, c), lambda b, i: (b, 0, ucol)),
            pl.BlockSpec((2, tf, c), lambda b, i: (0, i, kcol)),
        ],
        out_specs=pl.BlockSpec((1, 2, tf, c), lambda b, i: (b, 0, i, 0)),
        out_shape=jax.ShapeDtypeStruct((bsz, 2, n, c), BF16),
        compiler_params=_params(("parallel", "parallel"), 48),
        name="hy_dft_fwd",
    )(ff, u3, k)


def _hy_inv_kernel(fi_ref, y_ref, s_ref, g_ref, d_ref, o_ref):
    y = _dot(fi_ref[0], y_ref[0, 0]) + _dot(fi_ref[1], y_ref[0, 1])
    o_ref[0] = (g_ref[0] * (y + s_ref[0] * d_ref[...])).astype(o_ref.dtype)


def _hy_inv(fi, y, s3, scol, g3, gcol, d, out_dtype, tt=512):
    bsz, _, n, c = y.shape
    return pl.pallas_call(
        _hy_inv_kernel,
        grid=(bsz, n // tt),
        in_specs=[
            pl.BlockSpec((2, tt, n), lambda b, i: (0, i, 0)),
            pl.BlockSpec((1, 2, n, c), lambda b, i: (b, 0, 0, 0)),
            pl.BlockSpec((1, tt, c), lambda b, i: (b, i, scol)),
            pl.BlockSpec((1, tt, c), lambda b, i: (b, i, gcol)),
            pl.BlockSpec((1, c), lambda b, i: (0, 0)),
        ],
        out_specs=pl.BlockSpec((1, tt, c), lambda b, i: (b, i, 0)),
        out_shape=jax.ShapeDtypeStruct((bsz, n, c), out_dtype),
        compiler_params=_params(("parallel", "parallel"), 48),
        name="hy_dft_inv",
    )(fi, y, s3, g3, d.reshape(1, c))


def _dft_matrices(n):
    r = lax.broadcasted_iota(jnp.int32, (n, n), 0)
    c = lax.broadcasted_iota(jnp.int32, (n, n), 1)
    ang = ((r * c) & (2 * n - 1)).astype(F32) * (math.pi / n)
    cos, sin = jnp.cos(ang), jnp.sin(ang)
    sgn_c = (1 - 2 * (c & 1)).astype(F32)
    sgn_r = (1 - 2 * (r & 1)).astype(F32)
    fwd = jnp.stack([cos, jnp.where(r == 0, sgn_c, sin)]).astype(BF16)
    inv0 = jnp.where(c == 0, 1.0, 2.0) * cos * (0.5 / n)
    inv1 = jnp.where(c == 0, sgn_r * (0.5 / n), sin * (-1.0 / n))
    inv = jnp.stack([inv0, inv1]).astype(BF16)
    return fwd, inv


def _hyena_embedding(n):
    t = jnp.linspace(0.0, 1.0, n, dtype=F32)[:, None]
    bands = (HY_EMB - 1) // 2
    f = jnp.linspace(1e-4, bands - 1, bands, dtype=F32)[None]
    w = 2.0 * math.pi * jnp.arange(n, dtype=F32)[:, None] / n
    return jnp.concatenate([t, jnp.cos(f * w), -jnp.sin(f * w)], -1)


def _hyena(p3, dft, emb, conv_w, conv_b, w1, b1, w2, b2, w3, freq, decay, skip):
    ff, fi = dft
    u = _shortconv(p3, conv_w, conv_b)
    hs, hd = _hy_filter(emb, w1, b1, w2, b2, freq, w3, decay)
    k = _hy_kf(ff, hs, hd)
    y = _hy_fwd(ff, u, 0, k, 0)
    z = _hy_inv(fi, y, u, 0, u, 1, skip[0], F32)
    y = _hy_fwd(ff, z, 0, k, 1)
    return _hy_inv(fi, y, z, 0, u, 2, skip[1], BF16)


def _t5_bucket(rel):
    nb = REL_BUCKETS // 2
    max_exact = nb // 2
    n = np.abs(rel)
    large = max_exact + (np.log(np.maximum(n, 1) / max_exact) / np.log(REL_MAX_DIST / max_exact)
                         * (nb - max_exact)).astype(np.int32)
    large = np.minimum(large, nb - 1)
    return (rel > 0).astype(np.int32) * nb + np.where(n < max_exact, n, large)


def _attn_kernel(seq_len, q_ref, kp_ref, kc_ref, kn_ref, vp_ref, vc_ref, vn_ref, bias_ref, sink_ref,
                 o_ref):
    n = pl.program_id(1)
    q = q_ref[0]
    k = jnp.concatenate([kp_ref[0], kc_ref[0], kn_ref[0]], axis=0)
    v = jnp.concatenate([vp_ref[0], vc_ref[0], vn_ref[0]], axis=0)
    rows = ATT_GROUP * BLOCK
    qi = lax.broadcasted_iota(jnp.int32, (rows, 3 * BLOCK), 0) & (BLOCK - 1)
    sj = lax.broadcasted_iota(jnp.int32, (rows, 3 * BLOCK), 1)
    rel = sj - BLOCK - qi
    kpos = n * BLOCK + sj - BLOCK
    valid = (jnp.abs(rel) <= WINDOW) & (kpos >= 0) & (kpos < seq_len)
    outs = []
    for kh in range(ATT_KV_HEADS):
        k_h = k[:, kh * HEAD_DIM:(kh + 1) * HEAD_DIM]
        v_h = v[:, kh * HEAD_DIM:(kh + 1) * HEAD_DIM]
        heads = [kh * ATT_GROUP + g for g in range(ATT_GROUP)]
        q_h = jnp.concatenate([q[:, h * HEAD_DIM:(h + 1) * HEAD_DIM] for h in heads], axis=0)
        sink = jnp.concatenate([jnp.full((BLOCK, 1), sink_ref[h], F32) for h in heads], axis=0)
        s = _dot_nt(q_h, k_h) * (HEAD_DIM ** -0.5)
        s = jnp.where(valid, s + bias_ref[kh], -1e30)
        m = jnp.maximum(jnp.max(s, axis=-1, keepdims=True), sink)
        p = jnp.exp(s - m)
        denom = jnp.sum(p, axis=-1, keepdims=True) + jnp.exp(sink - m)
        o = _dot(p.astype(BF16), v_h) / denom
        outs.extend(o[g * BLOCK:(g + 1) * BLOCK] for g in range(ATT_GROUP))
    o_ref[0] = jnp.concatenate(outs, axis=-1).astype(o_ref.dtype)


def _attention(att3, sink, rel_bias):
    bsz, n, _ = att3.shape
    nb = n // BLOCK
    rel = (np.arange(3 * BLOCK)[None, :] - BLOCK) - np.arange(BLOCK)[:, None]
    bias = rel_bias.astype(F32)[_t5_bucket(rel)]
    bias = jnp.transpose(bias, (2, 0, 1)).reshape(ATT_KV_HEADS, ATT_GROUP * BLOCK, 3 * BLOCK)
    kcol = ATT_WIDTH // KV_WIDTH
    vcol = kcol + 1
    prev = lambda b, i: jnp.maximum(i - 1, 0)
    nxt = lambda b, i: jnp.minimum(i + 1, nb - 1)
    kv_spec = lambda blk, col: pl.BlockSpec((1, BLOCK, KV_WIDTH), lambda b, i: (b, blk(b, i), col))
    cur = lambda b, i: i
    return pl.pallas_call(
        functools.partial(_attn_kernel, n),
        grid=(bsz, nb),
        in_specs=[
            pl.BlockSpec((1, BLOCK, ATT_WIDTH), lambda b, i: (b, i, 0)),
            kv_spec(prev, kcol), kv_spec(cur, kcol), kv_spec(nxt, kcol),
            kv_spec(prev, vcol), kv_spec(cur, vcol), kv_spec(nxt, vcol),
            pl.BlockSpec((ATT_KV_HEADS, ATT_GROUP * BLOCK, 3 * BLOCK), lambda b, i: (0, 0, 0)),
            pl.BlockSpec(memory_space=pltpu.SMEM),
        ],
        out_specs=pl.BlockSpec((1, BLOCK, ATT_WIDTH), lambda b, i: (b, i, 0)),
        out_shape=jax.ShapeDtypeStruct((bsz, n, ATT_WIDTH), BF16),
        compiler_params=_params(("parallel", "parallel"), 32),
        name="window_attention",
    )(att3, att3, att3, att3, att3, att3, att3, bias, sink.astype(F32))


def _rotary_tables(n):
    inv = ROPE_BASE ** -jnp.linspace(0.0, 1.0, HEAD_DIM // 2, dtype=F32)
    ang = jnp.arange(n, dtype=F32)[:, None] * inv[None]
    cos, sin = jnp.cos(ang), jnp.sin(ang)
    cos_t = jnp.tile(jnp.concatenate([cos, cos], -1), (1, RET_HEADS))
    sin_t = jnp.tile(jnp.concatenate([-sin, sin], -1), (1, RET_HEADS))
    return cos_t, sin_t


def _rotate(x, cos_t, sin_t):
    width = x.shape[-1]
    half = HEAD_DIM // 2
    lane = lax.broadcasted_iota(jnp.int32, x.shape, 1)
    partner = jnp.where((lane & (HEAD_DIM - 1)) < half,
                        pltpu.roll(x, width - half, 1), pltpu.roll(x, half, 1))
    return x * cos_t + partner * sin_t


def _lane_head(shape):
    return lax.broadcasted_iota(jnp.int32, shape, 1) // HEAD_DIM


def _head_blocks_to_compact(full):
    head = _lane_head((HEAD_DIM, RET_WIDTH))
    acc = jnp.zeros((HEAD_DIM, RET_WIDTH), F32)
    for h in range(RET_HEADS):
        acc = acc + jnp.where(head == h, full[h * HEAD_DIM:(h + 1) * HEAD_DIM], 0.0)
    return acc


def _compact_to_head_blocks(compact):
    head = _lane_head((HEAD_DIM, RET_WIDTH))
    return jnp.concatenate([jnp.where(head == h, compact, 0.0) for h in range(RET_HEADS)], axis=0)


def _ret_state_kernel(kf_ref, vf_ref, kb_ref, vb_ref, cf_ref, sf_ref, cb_ref, sb_ref, lg_ref,
                      of_ref, ob_ref, stf_ref, stb_ref):
    @pl.when(pl.program_id(1) == 0)
    def _():
        stf_ref[...] = jnp.zeros_like(stf_ref)
        stb_ref[...] = jnp.zeros_like(stb_ref)

    of_ref[0, 0] = stf_ref[...]
    ob_ref[0, 0] = stb_ref[...]
    c = RET_CHUNK
    j = lax.broadcasted_iota(jnp.int32, (c, RET_WIDTH), 0).astype(F32)
    lgf = lg_ref[0:1, :]
    lgb = lg_ref[1:2, :]
    scale = HEAD_DIM ** -0.5

    def chunk_state(k_ref, v_ref, cos_ref, sin_ref, zeta):
        k = _rotate(k_ref[0], cos_ref[...], sin_ref[...]) * scale * zeta
        kv = _dot(k.T.astype(BF16), v_ref[0].astype(BF16))
        return _head_blocks_to_compact(kv)

    kvf = chunk_state(kf_ref, vf_ref, cf_ref, sf_ref, jnp.exp((c - 1.0 - j) * lgf))
    kvb = chunk_state(kb_ref, vb_ref, cb_ref, sb_ref, jnp.exp(j * lgb))
    stf_ref[...] = jnp.exp(c * lgf) * stf_ref[...] + kvf
    stb_ref[...] = jnp.exp(c * lgb) * stb_ref[...] + kvb


def _ret_states(ret3, cos_t, sin_t, lg_lane):
    bsz, n, _ = ret3.shape
    nc = n // RET_CHUNK
    c, w = RET_CHUNK, RET_WIDTH
    fwd = lambda b, i: i
    bwd = lambda b, i: nc - 1 - i
    col = lambda pos, cb: pl.BlockSpec((1, c, w), lambda b, i: (b, pos(b, i), cb))
    tab = lambda pos: pl.BlockSpec((c, w), lambda b, i: (pos(b, i), 0))
    out = lambda pos: pl.BlockSpec((1, 1, HEAD_DIM, w), lambda b, i: (b, pos(b, i), 0, 0))
    return pl.pallas_call(
        _ret_state_kernel,
        grid=(bsz, nc),
        in_specs=[col(fwd, 1), col(fwd, 2), col(bwd, 1), col(bwd, 2),
                  tab(fwd), tab(fwd), tab(bwd), tab(bwd),
                  pl.BlockSpec((2, w), lambda b, i: (0, 0))],
        out_specs=[out(fwd), out(bwd)],
        out_shape=[jax.ShapeDtypeStruct((bsz, nc, HEAD_DIM, w), F32)] * 2,
        scratch_shapes=[pltpu.VMEM((HEAD_DIM, w), F32), pltpu.VMEM((HEAD_DIM, w), F32)],
        compiler_params=_params(("parallel", "arbitrary"), 32),
        name="retention_states",
    )(ret3, ret3, ret3, ret3, cos_t, sin_t, cos_t, sin_t, lg_lane)


def _ret_out_kernel(q_ref, k_ref, v_ref, g_ref, cos_ref, sin_ref, lg_ref, lgs_ref, stf_ref, stb_ref,
                    o_ref):
    c, w = RET_CHUNK, RET_WIDTH
    cos_t, sin_t = cos_ref[...], sin_ref[...]
    q = _rotate(q_ref[0], cos_t, sin_t)
    k = _rotate(k_ref[0], cos_t, sin_t) * (HEAD_DIM ** -0.5)
    v = v_ref[0].astype(BF16)
    head = _lane_head((c, w))
    qs = jnp.concatenate([jnp.where(head == h, q, 0.0) for h in range(RET_HEADS)], axis=0)
    s = _dot_nt(qs.astype(BF16), k.astype(BF16))
    i = lax.broadcasted_iota(jnp.int32, (c, c), 0)
    jj = lax.broadcasted_iota(jnp.int32, (c, c), 1)
    diff = (i - jj).astype(F32)
    decay = jnp.concatenate(
        [jnp.exp(jnp.where(diff >= 0, diff * lgs_ref[0, h], -diff * lgs_ref[1, h]))
         for h in range(RET_HEADS)], axis=0)
    o_all = _dot((s * decay).astype(BF16), v)
    y = jnp.zeros((c, w), F32)
    for h in range(RET_HEADS):
        y = y + jnp.where(head == h, o_all[h * c:(h + 1) * c], 0.0)
    pos = lax.broadcasted_iota(jnp.int32, (c, w), 0).astype(F32)
    xi_f = jnp.exp((pos + 1.0) * lg_ref[0:1, :])
    xi_b = jnp.exp((c - pos) * lg_ref[1:2, :])
    y = y + _dot((q * xi_f).astype(BF16), _compact_to_head_blocks(stf_ref[0, 0]).astype(BF16))
    y = y + _dot((q * xi_b).astype(BF16), _compact_to_head_blocks(stb_ref[0, 0]).astype(BF16))
    r = lax.broadcasted_iota(jnp.int32, (w, w), 0) // HEAD_DIM
    avg = jnp.where(r == _lane_head((w, w)), 1.0 / HEAD_DIM, 0.0)
    ms = jnp.dot(y * y, avg, precision=HIGHEST, preferred_element_type=F32)
    y = y * lax.rsqrt(ms + EPS)
    g = g_ref[0]
    o_ref[0] = (g / (1.0 + jnp.exp(-g)) * y).astype(o_ref.dtype)


def _retention(ret3, decay):
    bsz, n, _ = ret3.shape
    nc = n // RET_CHUNK
    c, w = RET_CHUNK, RET_WIDTH
    lg = -jnp.exp(decay.astype(F32))
    lg_lane = jnp.repeat(lg, HEAD_DIM, axis=1)
    cos_t, sin_t = _rotary_tables(n)
    stf, stb = _ret_states(ret3, cos_t, sin_t, lg_lane)
    col = lambda cb: pl.BlockSpec((1, c, w), lambda b, i: (b, i, cb))
    tab = pl.BlockSpec((c, w), lambda b, i: (i, 0))
    st = pl.BlockSpec((1, 1, HEAD_DIM, w), lambda b, i: (b, i, 0, 0))
    return pl.pallas_call(
        _ret_out_kernel,
        grid=(bsz, nc),
        in_specs=[col(0), col(1), col(2), col(3), tab, tab,
                  pl.BlockSpec((2, w), lambda b, i: (0, 0)),
                  pl.BlockSpec(memory_space=pltpu.SMEM), st, st],
        out_specs=pl.BlockSpec((1, c, w), lambda b, i: (b, i, 0)),
        out_shape=jax.ShapeDtypeStruct((bsz, n, w), BF16),
        compiler_params=_params(("parallel", "parallel"), 32),
        name="retention_out",
    )(ret3, ret3, ret3, ret3, cos_t, sin_t, lg_lane, lg, stf, stb)


ROUTE_I0, ROUTE_I1, ROUTE_W0, ROUTE_W1 = 0, 1, 2, 3


def _mix_router_kernel(x_ref, yh_ref, ya_ref, yr_ref, w_ref, g_ref, wr_ref, br_ref,
                       xo_ref, hp_ref, route_ref):
    a, b = HY_WIDTH, HY_WIDTH + ATT_WIDTH
    x = x_ref[...] + _dot(yh_ref[...], w_ref[0:a, :]) + _dot(ya_ref[...], w_ref[a:b, :]) \
        + _dot(yr_ref[...], w_ref[b:, :])
    xo_ref[...] = x
    h = x * lax.rsqrt(jnp.mean(x * x, axis=-1, keepdims=True) + EPS) * g_ref[...]
    h_hi = h.astype(BF16)
    bits = lax.bitcast_convert_type(h_hi.astype(F32), jnp.uint32)
    half = bits.shape[1] // 2
    hp_ref[...] = bits[:, :half] | (bits[:, half:] >> 16)
    h_lo = (h - h_hi.astype(F32)).astype(BF16)
    wr = wr_ref[...]
    wr_hi = wr.astype(BF16)
    wr_lo = (wr - wr_hi.astype(F32)).astype(BF16)
    logits = _dot(h_hi, wr_hi) + (_dot(h_lo, wr_hi) + _dot(h_hi, wr_lo)) + br_ref[...]
    lane = lax.broadcasted_iota(jnp.int32, logits.shape, 1)
    big = jnp.int32(LANES)
    neg = -jnp.inf
    is_group = (lane >= N_EXPERTS) & (lane < N_EXPERTS + N_GROUPS)
    gl = jnp.where(is_group, logits, neg)
    gmax = jnp.max(gl, axis=-1, keepdims=True)
    g_w = 1.0 / jnp.sum(jnp.exp(gl - gmax), axis=-1, keepdims=True)
    g_sel = jnp.min(jnp.where(gl == gmax, lane, big), axis=-1, keepdims=True) - N_EXPERTS
    lo = g_sel * EXPERTS_PER_GROUP
    el = jnp.where((lane >= lo) & (lane < lo + EXPERTS_PER_GROUP), logits, neg)
    v0 = jnp.max(el, axis=-1, keepdims=True)
    i0 = jnp.min(jnp.where(el == v0, lane, big), axis=-1, keepdims=True)
    el = jnp.where(lane == i0, neg, el)
    v1 = jnp.max(el, axis=-1, keepdims=True)
    i1 = jnp.min(jnp.where(el == v1, lane, big), axis=-1, keepdims=True)
    e1 = jnp.exp(v1 - v0)
    w0 = g_w / (1.0 + e1)
    route_ref[...] = (jnp.where(lane == ROUTE_I0, i0.astype(F32), 0.0)
                      + jnp.where(lane == ROUTE_I1, i1.astype(F32), 0.0)
                      + jnp.where(lane == ROUTE_W0, w0, 0.0)
                      + jnp.where(lane == ROUTE_W1, w0 * e1, 0.0))


def _mix_router(x, yh, ya, yr, w_out, g, wr, br, tm=512):
    t, d = x.shape
    row = lambda width: pl.BlockSpec((tm, width), lambda i: (i, 0))
    full = lambda shape: pl.BlockSpec(shape, lambda i: (0, 0))
    return pl.pallas_call(
        _mix_router_kernel,
        grid=(t // tm,),
        in_specs=[row(d), row(HY_WIDTH), row(ATT_WIDTH), row(RET_WIDTH),
                  pl.BlockSpec((d, d), lambda i: (0, 0), pipeline_mode=pl.Buffered(1)),
                  full((1, d)), full((d, LANES)), full((1, LANES))],
        out_specs=[row(d), row(d // 2), row(LANES)],
        out_shape=[jax.ShapeDtypeStruct((t, d), F32), jax.ShapeDtypeStruct((t, d // 2), jnp.uint32),
                   jax.ShapeDtypeStruct((t, LANES), F32)],
        compiler_params=_params(("parallel",), 48),
        name="mix_router",
    )(x, yh, ya, yr, w_out, g.reshape(1, d), wr, br)


MOE_TM = 512
MOE_GATHER = 512
MOE_TC = 256


def _dispatch_plan(route, tm):
    t = route.shape[0]
    ids = route[:, ROUTE_I0:ROUTE_I1 + 1].astype(jnp.int32).reshape(-1)
    n_tiles = 2 * t // tm + N_EXPERTS
    onehot = (ids[:, None] == jnp.arange(N_EXPERTS, dtype=jnp.int32)[None]).astype(jnp.int32)
    csum = jnp.cumsum(onehot, axis=0)
    rank = jnp.sum((csum - onehot) * onehot, axis=1)
    counts = csum[-1]
    tiles = (counts + tm - 1) // tm
    tile_end = jnp.cumsum(tiles)
    offs = (tile_end - tiles) * tm
    slot = offs[ids] + rank
    src = jnp.zeros((n_tiles * tm,), jnp.int32).at[slot].set(
        jnp.arange(2 * t, dtype=jnp.int32) // 2, unique_indices=True)
    tile_expert = jnp.minimum(
        jnp.searchsorted(tile_end, jnp.arange(n_tiles, dtype=jnp.int32), side='right'),
        N_EXPERTS - 1).astype(jnp.int32)
    return src, slot, tile_expert, tile_end[-1:].astype(jnp.int32)


def _moe_gather_kernel(idx_ref, h_ref, o_ref, sem):
    rows = idx_ref.shape[-1]
    base = pl.program_id(0) * rows

    def issue(r, carry):
        pltpu.make_async_copy(h_ref.at[pl.ds(idx_ref[0, 0, r], 1)],
                              o_ref.at[pl.ds(base + r, 1)], sem).start()
        return carry

    lax.fori_loop(0, rows, issue, 0, unroll=8)
    pltpu.make_async_copy(h_ref.at[pl.ds(0, rows)], o_ref.at[pl.ds(base, rows)], sem).wait()


def _moe_gather(hp, src, rows=MOE_GATHER):
    n_slots = src.shape[0]
    steps = n_slots // rows
    return pl.pallas_call(
        _moe_gather_kernel,
        grid=(steps,),
        in_specs=[pl.BlockSpec((1, 1, rows), lambda i: (i, 0, 0), memory_space=pltpu.SMEM),
                  pl.BlockSpec(memory_space=pl.ANY)],
        out_specs=pl.BlockSpec(memory_space=pl.ANY),
        out_shape=jax.ShapeDtypeStruct((n_slots, hp.shape[1]), hp.dtype),
        scratch_shapes=[pltpu.SemaphoreType.DMA(())],
        compiler_params=_params(("arbitrary",), 16),
        name="moe_gather",
    )(src.reshape(steps, 1, rows), hp)


def _moe_gemm_kernel(te_ref, nv_ref, xs_ref, wg_ref, wu_ref, wd_ref, y_ref):
    @pl.when(pl.program_id(0) < nv_ref[0])
    def _():
        xp = xs_ref[...]
        x_a = lax.bitcast_convert_type(xp & jnp.uint32(0xFFFF0000), F32).astype(BF16)
        x_b = lax.bitcast_convert_type(xp << 16, F32).astype(BF16)
        half = xp.shape[1]
        a = _dot(x_a, wg_ref[0, :half, :]) + _dot(x_b, wg_ref[0, half:, :])
        b = _dot(x_a, wu_ref[0, :half, :]) + _dot(x_b, wu_ref[0, half:, :])
        act = a / (1.0 + jnp.exp(-a)) * b
        y_ref[...] = _dot(act.astype(BF16), wd_ref[0])

    @pl.when(pl.program_id(0) >= nv_ref[0])
    def _():
        y_ref[...] = jnp.zeros_like(y_ref)


def _moe_gemm(xs, tile_expert, n_valid, wg, wu, wd, tm=MOE_TM):
    n_slots, half = xs.shape
    ne, d, de = wg.shape
    tile = lambda i, te, nv: jnp.minimum(i, nv[0] - 1)
    return pl.pallas_call(
        _moe_gemm_kernel,
        grid_spec=pltpu.PrefetchScalarGridSpec(
            num_scalar_prefetch=2,
            grid=(n_slots // tm,),
            in_specs=[
                pl.BlockSpec((tm, half), lambda i, te, nv: (tile(i, te, nv), 0)),
                pl.BlockSpec((1, d, de), lambda i, te, nv: (te[tile(i, te, nv)], 0, 0)),
                pl.BlockSpec((1, d, de), lambda i, te, nv: (te[tile(i, te, nv)], 0, 0)),
                pl.BlockSpec((1, de, d), lambda i, te, nv: (te[tile(i, te, nv)], 0, 0)),
            ],
            out_specs=pl.BlockSpec((tm, d), lambda i, te, nv: (i, 0)),
        ),
        out_shape=jax.ShapeDtypeStruct((n_slots, d), F32),
        compiler_params=_params(("arbitrary",), 56),
        name="moe_gemm",
    )(tile_expert, n_valid, xs, wg, wu, wd)


def _moe_combine_kernel(final_norm, idx_ref, idx_next_ref, x_ref, route_ref, gf_ref, y_ref, o_ref,
                        buf_ref, sem_ref):
    i = pl.program_id(0)
    n = pl.num_programs(0)
    tokens = x_ref.shape[0]

    def start_rows(idx, slot):
        def issue(r, carry):
            for k in range(2):
                pltpu.make_async_copy(y_ref.at[pl.ds(idx[0, 0, 2 * r + k], 1)],
                                      buf_ref.at[slot, k, pl.ds(r, 1)], sem_ref.at[slot]).start()
            return carry
        lax.fori_loop(0, tokens, issue, 0, unroll=4)

    @pl.when(i == 0)
    def _():
        start_rows(idx_ref, 0)

    @pl.when(i + 1 < n)
    def _():
        start_rows(idx_next_ref, (i + 1) % 2)

    slot = i % 2
    for k in range(2):
        pltpu.make_async_copy(y_ref.at[pl.ds(0, tokens)], buf_ref.at[slot, k], sem_ref.at[slot]).wait()
    route = route_ref[...]
    y = x_ref[...] + route[:, ROUTE_W0:ROUTE_W0 + 1] * buf_ref[slot, 0] \
        + route[:, ROUTE_W1:ROUTE_W1 + 1] * buf_ref[slot, 1]
    if final_norm:
        y = y * lax.rsqrt(jnp.mean(y * y, axis=-1, keepdims=True) + EPS) * gf_ref[...]
    o_ref[...] = y


def _moe_combine(x, route, slot, y, gf, final_norm, tc=MOE_TC):
    t, d = x.shape
    steps = t // tc
    idx = slot.reshape(steps, 1, 2 * tc)
    row = lambda width: pl.BlockSpec((tc, width), lambda i: (i, 0))
    smem = lambda pos: pl.BlockSpec((1, 1, 2 * tc), lambda i: (pos(i), 0, 0), memory_space=pltpu.SMEM)
    return pl.pallas_call(
        functools.partial(_moe_combine_kernel, final_norm),
        grid=(steps,),
        in_specs=[smem(lambda i: i), smem(lambda i: jnp.minimum(i + 1, steps - 1)),
                  row(d), row(LANES), pl.BlockSpec((1, d), lambda i: (0, 0)),
                  pl.BlockSpec(memory_space=pl.ANY)],
        out_specs=row(d),
        out_shape=jax.ShapeDtypeStruct((t, d), F32),
        scratch_shapes=[pltpu.VMEM((2, 2, tc, d), F32), pltpu.SemaphoreType.DMA((2,))],
        compiler_params=_params(("arbitrary",), 32),
        name="moe_combine",
    )(idx, idx, x, route, gf.reshape(1, d), y)


def _moe(hp, route, x, wg, wu, wd, gf, final_norm):
    src, slot, tile_expert, n_valid = _dispatch_plan(route, MOE_TM)
    xs = _moe_gather(hp, src)
    y = _moe_gemm(xs, tile_expert, n_valid, wg, wu, wd)
    return _moe_combine(x, route, slot, y, gf, final_norm)


def _router_weights(wg, bg, we, be):
    d = wg.shape[0]
    pad = LANES - N_EXPERTS - N_GROUPS
    wr = jnp.concatenate([we, wg, jnp.zeros((d, pad), F32)], axis=1).astype(F32)
    br = jnp.concatenate([be, bg, jnp.zeros((pad,), F32)]).astype(F32).reshape(1, LANES)
    return wr, br


def _layer(x, bsz, p, l, consts, final_g):
    t, d = x.shape
    n = t // bsz
    dft, emb = consts
    w_in = p['w_in'][l].astype(BF16)
    g1 = p['ln1_g'][l]
    hy = _norm_proj(x, g1, w_in[:, :HY_IN], F32).reshape(bsz, n, HY_IN)
    att = _norm_proj(x, g1, w_in[:, HY_IN:HY_IN + ATT_IN], BF16).reshape(bsz, n, ATT_IN)
    ret = _norm_proj(x, g1, w_in[:, HY_IN + ATT_IN:], F32).reshape(bsz, n, RET_IN)
    y_h = _hyena(hy, dft, emb, p['hy_conv_w'][l], p['hy_conv_b'][l], p['hy_w1'][l], p['hy_b1'][l],
                 p['hy_w2'][l], p['hy_b2'][l], p['hy_w3'][l], p['hy_freq'][l], p['hy_decay'][l],
                 p['hy_skip'][l])
    y_a = _attention(att, p['attn_sink'][l], p['rel_bias'])
    y_r = _retention(ret, p['ret_decay'][l])
    wr, br = _router_weights(p['router_group_w'][l], p['router_group_b'][l],
                             p['router_expert_w'][l], p['router_expert_b'][l])
    x, hp, route = _mix_router(x, y_h.reshape(t, -1), y_a.reshape(t, -1), y_r.reshape(t, -1),
                               p['w_out'][l].astype(BF16), p['ln2_g'][l], wr, br)
    return _moe(hp, route, x, p['moe_w_gate'][l].astype(BF16), p['moe_w_up'][l].astype(BF16),
                p['moe_w_down'][l].astype(BF16), final_g, final_norm=(l == DEPTH - 1))


def kernel(x_prompt, x_sample, ln1_g, w_in, hy_conv_w, hy_conv_b, hy_w1, hy_b1, hy_w2, hy_b2, hy_w3,
           hy_freq, hy_decay, hy_skip, attn_sink, rel_bias, ret_decay, w_out, ln2_g, router_group_w,
           router_group_b, router_expert_w, router_expert_b, moe_w_gate, moe_w_up, moe_w_down, lnf_g):
    p = dict(ln1_g=ln1_g, w_in=w_in, hy_conv_w=hy_conv_w, hy_conv_b=hy_conv_b, hy_w1=hy_w1,
             hy_b1=hy_b1, hy_w2=hy_w2, hy_b2=hy_b2, hy_w3=hy_w3, hy_freq=hy_freq, hy_decay=hy_decay,
             hy_skip=hy_skip, attn_sink=attn_sink, rel_bias=rel_bias, ret_decay=ret_decay, w_out=w_out,
             ln2_g=ln2_g, router_group_w=router_group_w, router_group_b=router_group_b,
             router_expert_w=router_expert_w, router_expert_b=router_expert_b, moe_w_gate=moe_w_gate,
             moe_w_up=moe_w_up, moe_w_down=moe_w_down)
    bp, n, d = x_prompt.shape
    bs = x_sample.shape[0]
    assert x_sample.shape[1:] == (n, d)
    bsz = bp + bs
    x = jnp.concatenate([x_prompt, x_sample], axis=0).reshape(bsz * n, d)
    consts = (_dft_matrices(n), _hyena_embedding(n))
    for l in range(DEPTH):
        x = _layer(x, bsz, p, l, consts, lnf_g)
    y = x.reshape(bsz, n, d)
    return (y[:bp], y[bp:])
```

```python
import functools
import math

import numpy as np
import jax
import jax.numpy as jnp
from jax import lax
from jax.experimental import pallas as pl
from jax.experimental.pallas import tpu as pltpu

F32 = jnp.float32
BF16 = jnp.bfloat16
HIGHEST = lax.Precision.HIGHEST

D_MODEL = 2048
DEPTH = 2
HEAD_DIM = 64
EPS = 1e-6
HY_WIDTH = D_MODEL // 4
ATT_WIDTH = D_MODEL // 2
RET_WIDTH = D_MODEL // 4
HY_ORDER = 2
HY_EMB = 33
HY_FFN = 64
ATT_HEADS = ATT_WIDTH // HEAD_DIM
ATT_KV_HEADS = 4
ATT_GROUP = ATT_HEADS // ATT_KV_HEADS
KV_WIDTH = ATT_KV_HEADS * HEAD_DIM
WINDOW = 128
BLOCK = 128
REL_BUCKETS = 32
REL_MAX_DIST = 128
RET_HEADS = RET_WIDTH // HEAD_DIM
RET_CHUNK = 128
ROPE_BASE = 10000.0
N_GROUPS = 4
EXPERTS_PER_GROUP = 4
N_EXPERTS = N_GROUPS * EXPERTS_PER_GROUP
D_EXPERT = D_MODEL // 2
HY_IN = 3 * HY_WIDTH
ATT_IN = ATT_WIDTH + 2 * KV_WIDTH
RET_IN = 4 * RET_WIDTH

LANES = 128
MIB = 1024 * 1024


def _params(semantics, vmem_mib):
    return pltpu.CompilerParams(dimension_semantics=semantics, vmem_limit_bytes=vmem_mib * MIB)


def _dot(a, b):
    return jnp.dot(a, b, preferred_element_type=F32)


def _dot_nt(a, b):
    return lax.dot_general(a, b, (((1,), (1,)), ((), ())), preferred_element_type=F32)


def _norm_proj_kernel(x_ref, g_ref, w_ref, o_ref):
    x = x_ref[...]
    inv = lax.rsqrt(jnp.mean(x * x, axis=-1, keepdims=True) + EPS)
    h = (x * inv * g_ref[...]).astype(BF16)
    o_ref[...] = _dot(h, w_ref[...]).astype(o_ref.dtype)


def _norm_proj(x, g, w, out_dtype, tm=512):
    t, d = x.shape
    n = w.shape[1]
    return pl.pallas_call(
        _norm_proj_kernel,
        grid=(t // tm,),
        in_specs=[
            pl.BlockSpec((tm, d), lambda i: (i, 0)),
            pl.BlockSpec((1, d), lambda i: (0, 0)),
            pl.BlockSpec((d, n), lambda i: (0, 0)),
        ],
        out_specs=pl.BlockSpec((tm, n), lambda i: (i, 0)),
        out_shape=jax.ShapeDtypeStruct((t, n), out_dtype),
        compiler_params=_params(("parallel",), 48),
        name="norm_proj",
    )(x, g.reshape(1, d), w)


def _norm_proj_stack_kernel(tiles_a, xa_ref, xb_ref, g_ref, w_ref, o_ref, x_ref):
    x = jnp.where(pl.program_id(0) < tiles_a, xa_ref[...], xb_ref[...])
    x_ref[...] = x
    inv = lax.rsqrt(jnp.mean(x * x, axis=-1, keepdims=True) + EPS)
    h = (x * inv * g_ref[...]).astype(BF16)
    o_ref[...] = _dot(h, w_ref[...]).astype(o_ref.dtype)


def _norm_proj_stack(xa, xb, g, w, out_dtype, tm=512):
    ta, d = xa.shape
    tb = xb.shape[0]
    n = w.shape[1]
    tiles_a = ta // tm
    return pl.pallas_call(
        functools.partial(_norm_proj_stack_kernel, tiles_a),
        grid=((ta + tb) // tm,),
        in_specs=[
            pl.BlockSpec((tm, d), lambda i: (jnp.minimum(i, tiles_a - 1), 0)),
            pl.BlockSpec((tm, d), lambda i: (jnp.maximum(i - tiles_a, 0), 0)),
            pl.BlockSpec((1, d), lambda i: (0, 0)),
            pl.BlockSpec((d, n), lambda i: (0, 0)),
        ],
        out_specs=[pl.BlockSpec((tm, n), lambda i: (i, 0)), pl.BlockSpec((tm, d), lambda i: (i, 0))],
        out_shape=[jax.ShapeDtypeStruct((ta + tb, n), out_dtype),
                   jax.ShapeDtypeStruct((ta + tb, d), xa.dtype)],
        compiler_params=_params(("parallel",), 56),
        name="norm_proj_stack",
    )(xa, xb, g.reshape(1, d), w)


def _shortconv_kernel(p_ref, w_ref, b_ref, o_ref):
    x = p_ref[0]
    n = x.shape[0]
    row = lax.broadcasted_iota(jnp.int32, x.shape, 0)
    prev = jnp.where(row == 0, 0.0, pltpu.roll(x, 1, 0))
    nxt = jnp.where(row == n - 1, 0.0, pltpu.roll(x, n - 1, 0))
    o_ref[0] = prev * w_ref[0:1, :] + x * w_ref[1:2, :] + nxt * w_ref[2:3, :] + b_ref[...]


def _shortconv(p3, w, b, tc=256):
    bsz, n, c = p3.shape
    return pl.pallas_call(
        _shortconv_kernel,
        grid=(bsz, c // tc),
        in_specs=[
            pl.BlockSpec((1, n, tc), lambda i, j: (i, 0, j)),
            pl.BlockSpec((3, tc), lambda i, j: (0, j)),
            pl.BlockSpec((1, tc), lambda i, j: (0, j)),
        ],
        out_specs=pl.BlockSpec((1, n, tc), lambda i, j: (i, 0, j)),
        out_shape=jax.ShapeDtypeStruct((bsz, n, c), F32),
        compiler_params=_params(("parallel", "parallel"), 32),
        name="hy_shortconv",
    )(p3, w, b.reshape(1, c))


def _hy_filter_kernel(z_ref, w1_ref, b1_ref, w2_ref, b2_ref, fr_ref, w3f_ref, w3b_ref,
                      decf_ref, decb_ref, hs_ref, hd_ref):
    fr = fr_ref[...]
    h = jnp.sin(fr * (jnp.dot(z_ref[...], w1_ref[...], precision=HIGHEST,
                              preferred_element_type=F32) + b1_ref[...]))
    h = jnp.sin(fr * (jnp.dot(h, w2_ref[...], precision=HIGHEST,
                              preferred_element_type=F32) + b2_ref[...]))
    n = h.shape[0]
    shape = (n, w3f_ref.shape[1])
    row = lax.broadcasted_iota(jnp.int32, shape, 0)
    t = row.astype(F32) * (1.0 / (n - 1))
    hf = jnp.dot(h, w3f_ref[...], precision=HIGHEST, preferred_element_type=F32)
    hf = hf * jnp.exp(-t * jnp.abs(decf_ref[...]))
    hb = jnp.dot(h, w3b_ref[...], precision=HIGHEST, preferred_element_type=F32)
    hb = hb * jnp.exp(-t * jnp.abs(decb_ref[...]))
    hb = jnp.where(row == 0, 0.0, hb)
    norm = (jnp.sum(jnp.abs(hf), axis=0, keepdims=True)
            + jnp.sum(jnp.abs(hb), axis=0, keepdims=True))
    hs_ref[...] = (hf + hb) / norm
    hd_ref[...] = (hb - hf) / norm


def _hy_filter(z, w1, b1, w2, b2, fr, w3, decay, tc=256):
    n = z.shape[0]
    cols = HY_ORDER * HY_WIDTH
    w3f, w3b = w3[:, :cols], w3[:, cols:]
    dec = decay.reshape(2, cols)
    zp = jnp.pad(z, ((0, 0), (0, LANES - HY_EMB)))
    w1p = jnp.pad(w1, ((0, LANES - HY_EMB), (0, 0)))
    full = lambda shape: pl.BlockSpec(shape, lambda j: (0, 0))
    col = lambda rows: pl.BlockSpec((rows, tc), lambda j: (0, j))
    return pl.pallas_call(
        _hy_filter_kernel,
        grid=(cols // tc,),
        in_specs=[full((n, LANES)), full((LANES, HY_FFN)), full((1, HY_FFN)), full((HY_FFN, HY_FFN)),
                  full((1, HY_FFN)), full((1, HY_FFN)), col(HY_FFN), col(HY_FFN), col(1), col(1)],
        out_specs=[col(n), col(n)],
        out_shape=[jax.ShapeDtypeStruct((n, cols), F32)] * 2,
        compiler_params=_params(("parallel",), 48),
        name="hy_filter",
    )(zp, w1p, b1.reshape(1, -1), w2, b2.reshape(1, -1), fr.reshape(1, -1), w3f, w3b,
      dec[0:1], dec[1:2])


def _hy_kf_kernel(ff_ref, hs_ref, hd_ref, k_ref):
    hs = hs_ref[...]
    kre = _dot(ff_ref[0], hs.astype(BF16))
    kq = _dot(ff_ref[1], hd_ref[...].astype(BF16))
    row_t = lax.broadcasted_iota(jnp.int32, hs.shape, 0)
    sgn = (1 - 2 * (row_t & 1)).astype(F32)
    nyq = jnp.sum(hs * sgn, axis=0, keepdims=True)
    row = lax.broadcasted_iota(jnp.int32, kq.shape, 0)
    first = jnp.logical_and(row == 0, pl.program_id(0) == 0)
    k_ref[0] = kre
    k_ref[1] = jnp.where(first, nyq, kq)


def _hy_kf(ff, hs, hd, tf=512, tc=512):
    n, cols = hs.shape
    return pl.pallas_call(
        _hy_kf_kernel,
        grid=(n // tf, cols // tc),
        in_specs=[
            pl.BlockSpec((2, tf, n), lambda i, j: (0, i, 0)),
            pl.BlockSpec((n, tc), lambda i, j: (0, j)),
            pl.BlockSpec((n, tc), lambda i, j: (0, j)),
        ],
        out_specs=pl.BlockSpec((2, tf, tc), lambda i, j: (0, i, j)),
        out_shape=jax.ShapeDtypeStruct((2, n, cols), F32),
        compiler_params=_params(("parallel", "parallel"), 48),
        name="hy_filter_dft",
    )(ff, hs, hd)


def _hy_fwd_kernel(ff_ref, u_ref, k_ref, y_ref):
    u = u_ref[0].astype(BF16)
    a = _dot(ff_ref[0], u)
    bq = _dot(ff_ref[1], u)
    kre = k_ref[0]
    kq = k_ref[1]
    row = lax.broadcasted_iota(jnp.int32, a.shape, 0)
    first = jnp.logical_and(row == 0, pl.program_id(1) == 0)
    y_ref[0, 0] = jnp.where(first, a * kre, a * kre + bq * kq).astype(BF16)
    y_ref[0, 1] = jnp.where(first, bq * kq, a * kq - bq * kre).astype(BF16)


def _hy_fwd(ff, u3, ucol, k, kcol, tf=512):
    bsz, n, _ = u3.shape
    c = HY_WIDTH
    return pl.pallas_call(
        _hy_fwd_kernel,
        grid=(bsz, n // tf),
        in_specs=[
            pl.BlockSpec((2, tf, n), lambda b, i: (0, i, 0)),
            pl.BlockSpec((1, n, c), lambda b, i: (b, 0, ucol)),
            pl.BlockSpec((2, tf, c), lambda b, i: (0, i, kcol)),
        ],
        out_specs=pl.BlockSpec((1, 2, tf, c), lambda b, i: (b, 0, i, 0)),
        out_shape=jax.ShapeDtypeStruct((bsz, 2, n, c), BF16),
        compiler_params=_params(("parallel", "parallel"), 48),
        name="hy_dft_fwd",
    )(ff, u3, k)


def _hy_inv_kernel(fi_ref, y_ref, s_ref, g_ref, d_ref, o_ref):
    y = _dot(fi_ref[0], y_ref[0, 0]) + _dot(fi_ref[1], y_ref[0, 1])
    o_ref[0] = (g_ref[0] * (y + s_ref[0] * d_ref[...])).astype(o_ref.dtype)


def _hy_inv(fi, y, s3, scol, g3, gcol, d, out_dtype, tt=512):
    bsz, _, n, c = y.shape
    return pl.pallas_call(
        _hy_inv_kernel,
        grid=(bsz, n // tt),
        in_specs=[
            pl.BlockSpec((2, tt, n), lambda b, i: (0, i, 0)),
            pl.BlockSpec((1, 2, n, c), lambda b, i: (b, 0, 0, 0)),
            pl.BlockSpec((1, tt, c), lambda b, i: (b, i, scol)),
            pl.BlockSpec((1, tt, c), lambda b, i: (b, i, gcol)),
            pl.BlockSpec((1, c), lambda b, i: (0, 0)),
        ],
        out_specs=pl.BlockSpec((1, tt, c), lambda b, i: (b, i, 0)),
        out_shape=jax.ShapeDtypeStruct((bsz, n, c), out_dtype),
        compiler_params=_params(("parallel", "parallel"), 48),
        name="hy_dft_inv",
    )(fi, y, s3, g3, d.reshape(1, c))


def _dft_matrices(n):
    r = lax.broadcasted_iota(jnp.int32, (n, n), 0)
    c = lax.broadcasted_iota(jnp.int32, (n, n), 1)
    ang = ((r * c) & (2 * n - 1)).astype(F32) * (math.pi / n)
    cos, sin = jnp.cos(ang), jnp.sin(ang)
    sgn_c = (1 - 2 * (c & 1)).astype(F32)
    sgn_r = (1 - 2 * (r & 1)).astype(F32)
    fwd = jnp.stack([cos, jnp.where(r == 0, sgn_c, sin)]).astype(BF16)
    inv0 = jnp.where(c == 0, 1.0, 2.0) * cos * (0.5 / n)
    inv1 = jnp.where(c == 0, sgn_r * (0.5 / n), sin * (-1.0 / n))
    inv = jnp.stack([inv0, inv1]).astype(BF16)
    return fwd, inv


def _hyena_embedding(n):
    t = jnp.linspace(0.0, 1.0, n, dtype=F32)[:, None]
    bands = (HY_EMB - 1) // 2
    f = jnp.linspace(1e-4, bands - 1, bands, dtype=F32)[None]
    w = 2.0 * math.pi * jnp.arange(n, dtype=F32)[:, None] / n
    return jnp.concatenate([t, jnp.cos(f * w), -jnp.sin(f * w)], -1)


def _hyena(p3, dft, emb, conv_w, conv_b, w1, b1, w2, b2, w3, freq, decay, skip):
    ff, fi = dft
    u = _shortconv(p3, conv_w, conv_b)
    hs, hd = _hy_filter(emb, w1, b1, w2, b2, freq, w3, decay)
    k = _hy_kf(ff, hs, hd)
    y = _hy_fwd(ff, u, 0, k, 0)
    z = _hy_inv(fi, y, u, 0, u, 1, skip[0], F32)
    y = _hy_fwd(ff, z, 0, k, 1)
    return _hy_inv(fi, y, z, 0, u, 2, skip[1], BF16)


def _t5_bucket(rel):
    nb = REL_BUCKETS // 2
    max_exact = nb // 2
    n = np.abs(rel)
    large = max_exact + (np.log(np.maximum(n, 1) / max_exact) / np.log(REL_MAX_DIST / max_exact)
                         * (nb - max_exact)).astype(np.int32)
    large = np.minimum(large, nb - 1)
    return (rel > 0).astype(np.int32) * nb + np.where(n < max_exact, n, large)


def _attn_kernel(q_ref, kp_ref, kc_ref, kn_ref, vp_ref, vc_ref, vn_ref, bias_ref, sink_ref, o_ref):
    n = pl.program_id(1)
    q = q_ref[0] * (HEAD_DIM ** -0.5)
    k = jnp.concatenate([kp_ref[0], kc_ref[0], kn_ref[0]], axis=0)
    vt = jnp.concatenate([vp_ref[0], vc_ref[0], vn_ref[0]], axis=1)
    edge_prev = jnp.where(n == 0, -1e30, 0.0).astype(F32)
    edge_next = jnp.where(n == pl.num_programs(1) - 1, -1e30, 0.0).astype(F32)
    ones = jnp.ones((HEAD_DIM, 3 * BLOCK), BF16)
    for kh in range(ATT_KV_HEADS):
        k_h = k[:, kh * HEAD_DIM:(kh + 1) * HEAD_DIM]
        vt_h = jnp.concatenate([vt[kh * HEAD_DIM:(kh + 1) * HEAD_DIM], ones], axis=0)
        q_h = jnp.concatenate([q[:, (kh * ATT_GROUP + g) * HEAD_DIM:(kh * ATT_GROUP + g + 1) * HEAD_DIM]
                               for g in range(ATT_GROUP)], axis=0)
        s = _dot_nt(k_h, q_h) + bias_ref[kh]
        s = jnp.concatenate([s[:BLOCK] + edge_prev, s[BLOCK:2 * BLOCK], s[2 * BLOCK:] + edge_next],
                            axis=0)
        sink = sink_ref[kh]
        m = jnp.maximum(jnp.max(s, axis=0, keepdims=True), sink)
        p = jnp.exp(s - m).astype(BF16)
        o = _dot(vt_h, p)
        o = o[:HEAD_DIM] / (o[HEAD_DIM:HEAD_DIM + 1] + jnp.exp(sink - m))
        for g in range(0, ATT_GROUP, 2):
            pair = jnp.concatenate([o[:, g * BLOCK:(g + 1) * BLOCK],
                                    o[:, (g + 1) * BLOCK:(g + 2) * BLOCK]], axis=0)
            c0 = (kh * ATT_GROUP + g) * HEAD_DIM
            o_ref[0, :, c0:c0 + 2 * HEAD_DIM] = pair.T.astype(o_ref.dtype)


def _attention_bias(rel_bias):
    rel = (np.arange(3 * BLOCK)[None, :] - BLOCK) - np.arange(BLOCK)[:, None]
    bucket = jnp.asarray(_t5_bucket(rel).reshape(-1), jnp.int32)
    onehot = (bucket[:, None] == jnp.arange(REL_BUCKETS, dtype=jnp.int32)[None]).astype(F32)
    bias = jnp.dot(onehot, rel_bias.astype(F32), precision=HIGHEST)
    bias = bias.reshape(BLOCK, 3 * BLOCK, ATT_KV_HEADS, ATT_GROUP)
    bias = jnp.where(jnp.asarray(np.abs(rel) <= WINDOW)[:, :, None, None], bias, -1e30)
    return jnp.transpose(bias, (2, 1, 3, 0)).reshape(ATT_KV_HEADS, 3 * BLOCK, ATT_GROUP * BLOCK)


def _attention(att3, sink, rel_bias):
    bsz, n, _ = att3.shape
    nb = n // BLOCK
    bias = _attention_bias(rel_bias)
    sink_row = jnp.repeat(sink.astype(F32).reshape(ATT_KV_HEADS, 1, ATT_GROUP), BLOCK, axis=2)
    vt = jnp.swapaxes(att3[:, :, ATT_WIDTH + KV_WIDTH:], 1, 2)
    kcol = ATT_WIDTH // KV_WIDTH
    prev = lambda i: jnp.maximum(i - 1, 0)
    nxt = lambda i: jnp.minimum(i + 1, nb - 1)
    cur = lambda i: i
    k_spec = lambda blk: pl.BlockSpec((1, BLOCK, KV_WIDTH), lambda b, i: (b, blk(i), kcol))
    v_spec = lambda blk: pl.BlockSpec((1, KV_WIDTH, BLOCK), lambda b, i: (b, 0, blk(i)))
    return pl.pallas_call(
        _attn_kernel,
        grid=(bsz, nb),
        in_specs=[
            pl.BlockSpec((1, BLOCK, ATT_WIDTH), lambda b, i: (b, i, 0)),
            k_spec(prev), k_spec(cur), k_spec(nxt), v_spec(prev), v_spec(cur), v_spec(nxt),
            pl.BlockSpec((ATT_KV_HEADS, 3 * BLOCK, ATT_GROUP * BLOCK), lambda b, i: (0, 0, 0)),
            pl.BlockSpec((ATT_KV_HEADS, 1, ATT_GROUP * BLOCK), lambda b, i: (0, 0, 0)),
        ],
        out_specs=pl.BlockSpec((1, BLOCK, ATT_WIDTH), lambda b, i: (b, i, 0)),
        out_shape=jax.ShapeDtypeStruct((bsz, n, ATT_WIDTH), BF16),
        compiler_params=_params(("parallel", "parallel"), 32),
        name="window_attention",
    )(att3, att3, att3, att3, vt, vt, vt, bias, sink_row)


def _rotary_tables(n):
    inv = ROPE_BASE ** -jnp.linspace(0.0, 1.0, HEAD_DIM // 2, dtype=F32)
    ang = jnp.arange(n, dtype=F32)[:, None] * inv[None]
    cos, sin = jnp.cos(ang), jnp.sin(ang)
    cos_t = jnp.tile(jnp.concatenate([cos, cos], -1), (1, RET_HEADS))
    sin_t = jnp.tile(jnp.concatenate([-sin, sin], -1), (1, RET_HEADS))
    return cos_t, sin_t


def _rotate(x, cos_t, sin_t):
    width = x.shape[-1]
    half = HEAD_DIM // 2
    lane = lax.broadcasted_iota(jnp.int32, x.shape, 1)
    partner = jnp.where((lane & (HEAD_DIM - 1)) < half,
                        pltpu.roll(x, width - half, 1), pltpu.roll(x, half, 1))
    return x * cos_t + partner * sin_t


def _lane_head(shape):
    return lax.broadcasted_iota(jnp.int32, shape, 1) // HEAD_DIM


def _head_blocks_to_compact(full):
    head = _lane_head((HEAD_DIM, RET_WIDTH))
    acc = jnp.zeros((HEAD_DIM, RET_WIDTH), F32)
    for h in range(RET_HEADS):
        acc = acc + jnp.where(head == h, full[h * HEAD_DIM:(h + 1) * HEAD_DIM], 0.0)
    return acc


def _compact_to_head_blocks(compact):
    head = _lane_head((HEAD_DIM, RET_WIDTH))
    return jnp.concatenate([jnp.where(head == h, compact, 0.0) for h in range(RET_HEADS)], axis=0)


def _ret_state_kernel(kf_ref, vf_ref, kb_ref, vb_ref, cf_ref, sf_ref, cb_ref, sb_ref, lg_ref,
                      of_ref, ob_ref, stf_ref, stb_ref):
    @pl.when(pl.program_id(1) == 0)
    def _():
        stf_ref[...] = jnp.zeros_like(stf_ref)
        stb_ref[...] = jnp.zeros_like(stb_ref)

    of_ref[0, 0] = stf_ref[...]
    ob_ref[0, 0] = stb_ref[...]
    c = RET_CHUNK
    j = lax.broadcasted_iota(jnp.int32, (c, RET_WIDTH), 0).astype(F32)
    lgf = lg_ref[0:1, :]
    lgb = lg_ref[1:2, :]
    scale = HEAD_DIM ** -0.5

    def chunk_state(k_ref, v_ref, cos_ref, sin_ref, zeta):
        k = _rotate(k_ref[0], cos_ref[...], sin_ref[...]) * scale * zeta
        kv = _dot(k.T.astype(BF16), v_ref[0].astype(BF16))
        return _head_blocks_to_compact(kv)

    kvf = chunk_state(kf_ref, vf_ref, cf_ref, sf_ref, jnp.exp((c - 1.0 - j) * lgf))
    kvb = chunk_state(kb_ref, vb_ref, cb_ref, sb_ref, jnp.exp(j * lgb))
    stf_ref[...] = jnp.exp(c * lgf) * stf_ref[...] + kvf
    stb_ref[...] = jnp.exp(c * lgb) * stb_ref[...] + kvb


def _ret_states(ret3, cos_t, sin_t, lg_lane):
    bsz, n, _ = ret3.shape
    nc = n // RET_CHUNK
    c, w = RET_CHUNK, RET_WIDTH
    fwd = lambda b, i: i
    bwd = lambda b, i: nc - 1 - i
    col = lambda pos, cb: pl.BlockSpec((1, c, w), lambda b, i: (b, pos(b, i), cb))
    tab = lambda pos: pl.BlockSpec((c, w), lambda b, i: (pos(b, i), 0))
    out = lambda pos: pl.BlockSpec((1, 1, HEAD_DIM, w), lambda b, i: (b, pos(b, i), 0, 0))
    return pl.pallas_call(
        _ret_state_kernel,
        grid=(bsz, nc),
        in_specs=[col(fwd, 1), col(fwd, 2), col(bwd, 1), col(bwd, 2),
                  tab(fwd), tab(fwd), tab(bwd), tab(bwd),
                  pl.BlockSpec((2, w), lambda b, i: (0, 0))],
        out_specs=[out(fwd), out(bwd)],
        out_shape=[jax.ShapeDtypeStruct((bsz, nc, HEAD_DIM, w), F32)] * 2,
        scratch_shapes=[pltpu.VMEM((HEAD_DIM, w), F32), pltpu.VMEM((HEAD_DIM, w), F32)],
        compiler_params=_params(("parallel", "arbitrary"), 32),
        name="retention_states",
    )(ret3, ret3, ret3, ret3, cos_t, sin_t, cos_t, sin_t, lg_lane)


def _ret_out_kernel(q_ref, k_ref, v_ref, g_ref, cos_ref, sin_ref, lg_ref, lgs_ref, stf_ref, stb_ref,
                    o_ref):
    c, w = RET_CHUNK, RET_WIDTH
    cos_t, sin_t = cos_ref[...], sin_ref[...]
    q = _rotate(q_ref[0], cos_t, sin_t)
    k = _rotate(k_ref[0], cos_t, sin_t) * (HEAD_DIM ** -0.5)
    v = v_ref[0].astype(BF16)
    head = _lane_head((c, w))
    qs = jnp.concatenate([jnp.where(head == h, q, 0.0) for h in range(RET_HEADS)], axis=0)
    s = _dot_nt(qs.astype(BF16), k.astype(BF16))
    i = lax.broadcasted_iota(jnp.int32, (c, c), 0)
    jj = lax.broadcasted_iota(jnp.int32, (c, c), 1)
    diff = (i - jj).astype(F32)
    decay = jnp.concatenate(
        [jnp.exp(jnp.where(diff >= 0, diff * lgs_ref[0, h], -diff * lgs_ref[1, h]))
         for h in range(RET_HEADS)], axis=0)
    o_all = _dot((s * decay).astype(BF16), v)
    y = jnp.zeros((c, w), F32)
    for h in range(RET_HEADS):
        y = y + jnp.where(head == h, o_all[h * c:(h + 1) * c], 0.0)
    pos = lax.broadcasted_iota(jnp.int32, (c, w), 0).astype(F32)
    xi_f = jnp.exp((pos + 1.0) * lg_ref[0:1, :])
    xi_b = jnp.exp((c - pos) * lg_ref[1:2, :])
    y = y + _dot((q * xi_f).astype(BF16), _compact_to_head_blocks(stf_ref[0, 0]).astype(BF16))
    y = y + _dot((q * xi_b).astype(BF16), _compact_to_head_blocks(stb_ref[0, 0]).astype(BF16))
    r = lax.broadcasted_iota(jnp.int32, (w, w), 0) // HEAD_DIM
    avg = jnp.where(r == _lane_head((w, w)), 1.0 / HEAD_DIM, 0.0)
    ms = jnp.dot(y * y, avg, precision=HIGHEST, preferred_element_type=F32)
    y = y * lax.rsqrt(ms + EPS)
    g = g_ref[0]
    o_ref[0] = (g / (1.0 + jnp.exp(-g)) * y).astype(o_ref.dtype)


def _retention(ret3, decay):
    bsz, n, _ = ret3.shape
    nc = n // RET_CHUNK
    c, w = RET_CHUNK, RET_WIDTH
    lg = -jnp.exp(decay.astype(F32))
    lg_lane = jnp.repeat(lg, HEAD_DIM, axis=1)
    cos_t, sin_t = _rotary_tables(n)
    stf, stb = _ret_states(ret3, cos_t, sin_t, lg_lane)
    col = lambda cb: pl.BlockSpec((1, c, w), lambda b, i: (b, i, cb))
    tab = pl.BlockSpec((c, w), lambda b, i: (i, 0))
    st = pl.BlockSpec((1, 1, HEAD_DIM, w), lambda b, i: (b, i, 0, 0))
    return pl.pallas_call(
        _ret_out_kernel,
        grid=(bsz, nc),
        in_specs=[col(0), col(1), col(2), col(3), tab, tab,
                  pl.BlockSpec((2, w), lambda b, i: (0, 0)),
                  pl.BlockSpec(memory_space=pltpu.SMEM), st, st],
        out_specs=pl.BlockSpec((1, c, w), lambda b, i: (b, i, 0)),
        out_shape=jax.ShapeDtypeStruct((bsz, n, w), BF16),
        compiler_params=_params(("parallel", "parallel"), 32),
        name="retention_out",
    )(ret3, ret3, ret3, ret3, cos_t, sin_t, lg_lane, lg, stf, stb)


ROUTE_I0, ROUTE_I1, ROUTE_W0, ROUTE_W1 = 0, 1, 2, 3


def _mix_router_kernel(x_ref, yh_ref, ya_ref, yr_ref, w_ref, g_ref, wr_ref, br_ref,
                       xo_ref, hp_ref, route_ref):
    a, b = HY_WIDTH, HY_WIDTH + ATT_WIDTH
    x = x_ref[...] + _dot(yh_ref[...], w_ref[0:a, :]) + _dot(ya_ref[...], w_ref[a:b, :]) \
        + _dot(yr_ref[...], w_ref[b:, :])
    xo_ref[...] = x
    h = x * lax.rsqrt(jnp.mean(x * x, axis=-1, keepdims=True) + EPS) * g_ref[...]
    h_hi = h.astype(BF16)
    bits = lax.bitcast_convert_type(h_hi.astype(F32), jnp.uint32)
    half = bits.shape[1] // 2
    hp_ref[...] = bits[:, :half] | (bits[:, half:] >> 16)
    h_lo = (h - h_hi.astype(F32)).astype(BF16)
    wr = wr_ref[...]
    wr_hi = wr.astype(BF16)
    wr_lo = (wr - wr_hi.astype(F32)).astype(BF16)
    logits = _dot(h_hi, wr_hi) + (_dot(h_lo, wr_hi) + _dot(h_hi, wr_lo)) + br_ref[...]
    lane = lax.broadcasted_iota(jnp.int32, logits.shape, 1)
    big = jnp.int32(LANES)
    neg = -jnp.inf
    is_group = (lane >= N_EXPERTS) & (lane < N_EXPERTS + N_GROUPS)
    gl = jnp.where(is_group, logits, neg)
    gmax = jnp.max(gl, axis=-1, keepdims=True)
    g_w = 1.0 / jnp.sum(jnp.exp(gl - gmax), axis=-1, keepdims=True)
    g_sel = jnp.min(jnp.where(gl == gmax, lane, big), axis=-1, keepdims=True) - N_EXPERTS
    lo = g_sel * EXPERTS_PER_GROUP
    el = jnp.where((lane >= lo) & (lane < lo + EXPERTS_PER_GROUP), logits, neg)
    v0 = jnp.max(el, axis=-1, keepdims=True)
    i0 = jnp.min(jnp.where(el == v0, lane, big), axis=-1, keepdims=True)
    el = jnp.where(lane == i0, neg, el)
    v1 = jnp.max(el, axis=-1, keepdims=True)
    i1 = jnp.min(jnp.where(el == v1, lane, big), axis=-1, keepdims=True)
    e1 = jnp.exp(v1 - v0)
    w0 = g_w / (1.0 + e1)
    route_ref[...] = (jnp.where(lane == ROUTE_I0, i0.astype(F32), 0.0)
                      + jnp.where(lane == ROUTE_I1, i1.astype(F32), 0.0)
                      + jnp.where(lane == ROUTE_W0, w0, 0.0)
                      + jnp.where(lane == ROUTE_W1, w0 * e1, 0.0))


def _mix_router(x, yh, ya, yr, w_out, g, wr, br, tm=512):
    t, d = x.shape
    row = lambda width: pl.BlockSpec((tm, width), lambda i: (i, 0))
    full = lambda shape: pl.BlockSpec(shape, lambda i: (0, 0))
    return pl.pallas_call(
        _mix_router_kernel,
        grid=(t // tm,),
        in_specs=[row(d), row(HY_WIDTH), row(ATT_WIDTH), row(RET_WIDTH),
                  pl.BlockSpec((d, d), lambda i: (0, 0), pipeline_mode=pl.Buffered(1)),
                  full((1, d)), full((d, LANES)), full((1, LANES))],
        out_specs=[row(d), row(d // 2), row(LANES)],
        out_shape=[jax.ShapeDtypeStruct((t, d), F32), jax.ShapeDtypeStruct((t, d // 2), jnp.uint32),
                   jax.ShapeDtypeStruct((t, LANES), F32)],
        compiler_params=_params(("parallel",), 48),
        name="mix_router",
    )(x, yh, ya, yr, w_out, g.reshape(1, d), wr, br)


MOE_TM = 512
MOE_TC = 256


def _dispatch_plan(route, tm):
    t = route.shape[0]
    ids = route[:, ROUTE_I0:ROUTE_I1 + 1].astype(jnp.int32).reshape(-1)
    n_tiles = 2 * t // tm + N_EXPERTS
    onehot = (ids[:, None] == jnp.arange(N_EXPERTS, dtype=jnp.int32)[None]).astype(jnp.int32)
    csum = jnp.cumsum(onehot, axis=0)
    rank = jnp.sum((csum - onehot) * onehot, axis=1)
    counts = csum[-1]
    tiles = (counts + tm - 1) // tm
    tile_end = jnp.cumsum(tiles)
    offs = (tile_end - tiles) * tm
    slot = offs[ids] + rank
    src = jnp.zeros((n_tiles * tm,), jnp.int32).at[slot].set(
        jnp.arange(2 * t, dtype=jnp.int32) // 2, unique_indices=True)
    tile_expert = jnp.minimum(
        jnp.searchsorted(tile_end, jnp.arange(n_tiles, dtype=jnp.int32), side='right'),
        N_EXPERTS - 1).astype(jnp.int32)
    return src, slot, tile_expert, tile_end[-1:].astype(jnp.int32)


def _moe_gemm_kernel(te_ref, nv_ref, idx_ref, idx_next_ref, h_ref, wg_ref, wu_ref, wd_ref, y_ref,
                     xs_ref, sem_ref):
    i = pl.program_id(0)
    rows = xs_ref.shape[1]

    def start_rows(idx, slot):
        def issue(r, carry):
            pltpu.make_async_copy(h_ref.at[pl.ds(idx[0, 0, r], 1)],
                                  xs_ref.at[slot, pl.ds(r, 1)], sem_ref.at[slot]).start()
            return carry
        lax.fori_loop(0, rows, issue, 0, unroll=8)

    @pl.when(i == 0)
    def _():
        start_rows(idx_ref, 0)

    @pl.when(i + 1 < nv_ref[0])
    def _():
        start_rows(idx_next_ref, (i + 1) % 2)

    @pl.when(i < nv_ref[0])
    def _():
        slot = i % 2
        pltpu.make_async_copy(h_ref.at[pl.ds(0, rows)], xs_ref.at[slot], sem_ref.at[slot]).wait()
        xp = xs_ref[slot]
        x_a = lax.bitcast_convert_type(xp & jnp.uint32(0xFFFF0000), F32).astype(BF16)
        x_b = lax.bitcast_convert_type(xp << 16, F32).astype(BF16)
        half = xp.shape[1]
        a = _dot(x_a, wg_ref[0, :half, :]) + _dot(x_b, wg_ref[0, half:, :])
        b = _dot(x_a, wu_ref[0, :half, :]) + _dot(x_b, wu_ref[0, half:, :])
        act = a / (1.0 + jnp.exp(-a)) * b
        y_ref[...] = _dot(act.astype(BF16), wd_ref[0])

    @pl.when(pl.program_id(0) >= nv_ref[0])
    def _():
        y_ref[...] = jnp.zeros_like(y_ref)


def _moe_gemm(hp, src, tile_expert, n_valid, wg, wu, wd, tm=MOE_TM):
    n_slots = src.shape[0]
    half = hp.shape[1]
    ne, d, de = wg.shape
    n_tiles = n_slots // tm
    tile = lambda i, te, nv: jnp.minimum(i, nv[0] - 1)
    smem = lambda pos: pl.BlockSpec((1, 1, tm), lambda i, te, nv: (pos(i, te, nv), 0, 0),
                                    memory_space=pltpu.SMEM)
    return pl.pallas_call(
        _moe_gemm_kernel,
        grid_spec=pltpu.PrefetchScalarGridSpec(
            num_scalar_prefetch=2,
            grid=(n_tiles,),
            in_specs=[
                smem(tile),
                smem(lambda i, te, nv: jnp.minimum(i + 1, nv[0] - 1)),
                pl.BlockSpec(memory_space=pl.ANY),
                pl.BlockSpec((1, d, de), lambda i, te, nv: (te[tile(i, te, nv)], 0, 0)),
                pl.BlockSpec((1, d, de), lambda i, te, nv: (te[tile(i, te, nv)], 0, 0)),
                pl.BlockSpec((1, de, d), lambda i, te, nv: (te[tile(i, te, nv)], 0, 0)),
            ],
            out_specs=pl.BlockSpec((tm, d), lambda i, te, nv: (i, 0)),
            scratch_shapes=[pltpu.VMEM((2, tm, half), hp.dtype), pltpu.SemaphoreType.DMA((2,))],
        ),
        out_shape=jax.ShapeDtypeStruct((n_slots, d), F32),
        compiler_params=_params(("arbitrary",), 56),
        name="moe_gemm",
    )(tile_expert, n_valid, src.reshape(n_tiles, 1, tm), src.reshape(n_tiles, 1, tm), hp, wg, wu, wd)


def _moe_combine_kernel(final_norm, steps_a, idx_ref, idx_next_ref, x_ref, route_ref, gf_ref, y_ref,
                        *rest):
    o_refs, (buf_ref, sem_ref) = rest[:-2], rest[-2:]
    i = pl.program_id(0)
    n = pl.num_programs(0)
    tokens = x_ref.shape[0]

    def start_rows(idx, slot):
        def issue(r, carry):
            for k in range(2):
                pltpu.make_async_copy(y_ref.at[pl.ds(idx[0, 0, 2 * r + k], 1)],
                                      buf_ref.at[slot, k, pl.ds(r, 1)], sem_ref.at[slot]).start()
            return carry
        lax.fori_loop(0, tokens, issue, 0, unroll=4)

    @pl.when(i == 0)
    def _():
        start_rows(idx_ref, 0)

    @pl.when(i + 1 < n)
    def _():
        start_rows(idx_next_ref, (i + 1) % 2)

    slot = i % 2
    for k in range(2):
        pltpu.make_async_copy(y_ref.at[pl.ds(0, tokens)], buf_ref.at[slot, k], sem_ref.at[slot]).wait()
    route = route_ref[...]
    y = x_ref[...] + route[:, ROUTE_W0:ROUTE_W0 + 1] * buf_ref[slot, 0] \
        + route[:, ROUTE_W1:ROUTE_W1 + 1] * buf_ref[slot, 1]
    if final_norm:
        y = y * lax.rsqrt(jnp.mean(y * y, axis=-1, keepdims=True) + EPS) * gf_ref[...]
    if len(o_refs) == 1:
        o_refs[0][...] = y
    else:
        @pl.when(i < steps_a)
        def _():
            o_refs[0][...] = y

        @pl.when(i >= steps_a)
        def _():
            o_refs[1][...] = y


def _moe_combine(x, route, slot, y, gf, final_norm, tokens_a=None, tc=MOE_TC):
    t, d = x.shape
    steps = t // tc
    idx = slot.reshape(steps, 1, 2 * tc)
    row = lambda width: pl.BlockSpec((tc, width), lambda i: (i, 0))
    smem = lambda pos: pl.BlockSpec((1, 1, 2 * tc), lambda i: (pos(i), 0, 0), memory_space=pltpu.SMEM)
    if tokens_a is None:
        steps_a = steps
        out_specs = [row(d)]
        out_shape = [jax.ShapeDtypeStruct((t, d), F32)]
    else:
        steps_a = tokens_a // tc
        out_specs = [pl.BlockSpec((tc, d), lambda i: (jnp.minimum(i, steps_a - 1), 0)),
                     pl.BlockSpec((tc, d), lambda i: (jnp.maximum(i - steps_a, 0), 0))]
        out_shape = [jax.ShapeDtypeStruct((tokens_a, d), F32),
                     jax.ShapeDtypeStruct((t - tokens_a, d), F32)]
    return pl.pallas_call(
        functools.partial(_moe_combine_kernel, final_norm, steps_a),
        grid=(steps,),
        in_specs=[smem(lambda i: i), smem(lambda i: jnp.minimum(i + 1, steps - 1)),
                  row(d), row(LANES), pl.BlockSpec((1, d), lambda i: (0, 0)),
                  pl.BlockSpec(memory_space=pl.ANY)],
        out_specs=out_specs,
        out_shape=out_shape,
        scratch_shapes=[pltpu.VMEM((2, 2, tc, d), F32), pltpu.SemaphoreType.DMA((2,))],
        compiler_params=_params(("arbitrary",), 40),
        name="moe_combine",
    )(idx, idx, x, route, gf.reshape(1, d), y)


def _moe(hp, route, x, wg, wu, wd, gf, final_norm, tokens_a=None):
    src, slot, tile_expert, n_valid = _dispatch_plan(route, MOE_TM)
    y = _moe_gemm(hp, src, tile_expert, n_valid, wg, wu, wd)
    return _moe_combine(x, route, slot, y, gf, final_norm, tokens_a)


def _router_weights(wg, bg, we, be):
    d = wg.shape[0]
    pad = LANES - N_EXPERTS - N_GROUPS
    wr = jnp.concatenate([we, wg, jnp.zeros((d, pad), F32)], axis=1).astype(F32)
    br = jnp.concatenate([be, bg, jnp.zeros((pad,), F32)]).astype(F32).reshape(1, LANES)
    return wr, br


def _layer(xs, bsz, p, l, consts, final_g, tokens_a):
    dft, emb = consts
    w_in = p['w_in'][l].astype(BF16)
    g1 = p['ln1_g'][l]
    if len(xs) == 2:
        hy, x = _norm_proj_stack(xs[0], xs[1], g1, w_in[:, :HY_IN], F32)
    else:
        x, = xs
        hy = _norm_proj(x, g1, w_in[:, :HY_IN], F32)
    t, d = x.shape
    n = t // bsz
    hy = hy.reshape(bsz, n, HY_IN)
    att = _norm_proj(x, g1, w_in[:, HY_IN:HY_IN + ATT_IN], BF16).reshape(bsz, n, ATT_IN)
    ret = _norm_proj(x, g1, w_in[:, HY_IN + ATT_IN:], F32).reshape(bsz, n, RET_IN)
    y_h = _hyena(hy, dft, emb, p['hy_conv_w'][l], p['hy_conv_b'][l], p['hy_w1'][l], p['hy_b1'][l],
                 p['hy_w2'][l], p['hy_b2'][l], p['hy_w3'][l], p['hy_freq'][l], p['hy_decay'][l],
                 p['hy_skip'][l])
    y_a = _attention(att, p['attn_sink'][l], p['rel_bias'])
    y_r = _retention(ret, p['ret_decay'][l])
    wr, br = _router_weights(p['router_group_w'][l], p['router_group_b'][l],
                             p['router_expert_w'][l], p['router_expert_b'][l])
    x, hp, route = _mix_router(x, y_h.reshape(t, -1), y_a.reshape(t, -1), y_r.reshape(t, -1),
                               p['w_out'][l].astype(BF16), p['ln2_g'][l], wr, br)
    last = l == DEPTH - 1
    return _moe(hp, route, x, p['moe_w_gate'][l].astype(BF16), p['moe_w_up'][l].astype(BF16),
                p['moe_w_down'][l].astype(BF16), final_g, final_norm=last,
                tokens_a=tokens_a if last else None)


def kernel(x_prompt, x_sample, ln1_g, w_in, hy_conv_w, hy_conv_b, hy_w1, hy_b1, hy_w2, hy_b2, hy_w3,
           hy_freq, hy_decay, hy_skip, attn_sink, rel_bias, ret_decay, w_out, ln2_g, router_group_w,
           router_group_b, router_expert_w, router_expert_b, moe_w_gate, moe_w_up, moe_w_down, lnf_g):
    p = dict(ln1_g=ln1_g, w_in=w_in, hy_conv_w=hy_conv_w, hy_conv_b=hy_conv_b, hy_w1=hy_w1,
             hy_b1=hy_b1, hy_w2=hy_w2, hy_b2=hy_b2, hy_w3=hy_w3, hy_freq=hy_freq, hy_decay=hy_decay,
             hy_skip=hy_skip, attn_sink=attn_sink, rel_bias=rel_bias, ret_decay=ret_decay, w_out=w_out,
             ln2_g=ln2_g, router_group_w=router_group_w, router_group_b=router_group_b,
             router_expert_w=router_expert_w, router_expert_b=router_expert_b, moe_w_gate=moe_w_gate,
             moe_w_up=moe_w_up, moe_w_down=moe_w_down)
    bp, n, d = x_prompt.shape
    bs = x_sample.shape[0]
    assert x_sample.shape[1:] == (n, d)
    bsz = bp + bs
    consts = (_dft_matrices(n), _hyena_embedding(n))
    xs = (x_prompt.reshape(bp * n, d), x_sample.reshape(bs * n, d))
    for l in range(DEPTH):
        xs = _layer(xs, bsz, p, l, consts, lnf_g, bp * n)
    y_prompt, y_sample = xs
    return (y_prompt.reshape(bp, n, d), y_sample.reshape(bs, n, d))
```

```python
import functools
import math

import numpy as np
import jax
import jax.numpy as jnp
from jax import lax
from jax.experimental import pallas as pl
from jax.experimental.pallas import tpu as pltpu

F32 = jnp.float32
BF16 = jnp.bfloat16
HIGHEST = lax.Precision.HIGHEST

D_MODEL = 2048
DEPTH = 2
HEAD_DIM = 64
EPS = 1e-6
HY_WIDTH = D_MODEL // 4
ATT_WIDTH = D_MODEL // 2
RET_WIDTH = D_MODEL // 4
HY_ORDER = 2
HY_EMB = 33
HY_FFN = 64
ATT_HEADS = ATT_WIDTH // HEAD_DIM
ATT_KV_HEADS = 4
ATT_GROUP = ATT_HEADS // ATT_KV_HEADS
KV_WIDTH = ATT_KV_HEADS * HEAD_DIM
WINDOW = 128
BLOCK = 128
REL_BUCKETS = 32
REL_MAX_DIST = 128
RET_HEADS = RET_WIDTH // HEAD_DIM
RET_CHUNK = 128
ROPE_BASE = 10000.0
N_GROUPS = 4
EXPERTS_PER_GROUP = 4
N_EXPERTS = N_GROUPS * EXPERTS_PER_GROUP
D_EXPERT = D_MODEL // 2
HY_IN = 3 * HY_WIDTH
ATT_IN = ATT_WIDTH + 2 * KV_WIDTH
RET_IN = 4 * RET_WIDTH

LANES = 128
SUBLANES = 8
MIB = 1024 * 1024


def _params(semantics, vmem_mib):
    return pltpu.CompilerParams(dimension_semantics=semantics, vmem_limit_bytes=vmem_mib * MIB)


def _dot(a, b):
    return jnp.dot(a, b, preferred_element_type=F32)


def _dot_nt(a, b):
    return lax.dot_general(a, b, (((1,), (1,)), ((), ())), preferred_element_type=F32)


def _norm_proj_kernel(x_ref, g_ref, w_ref, o_ref):
    x = x_ref[...]
    inv = lax.rsqrt(jnp.mean(x * x, axis=-1, keepdims=True) + EPS)
    h = (x * inv * g_ref[...]).astype(BF16)
    o_ref[...] = _dot(h, w_ref[...]).astype(o_ref.dtype)


def _norm_proj(x, g, w, out_dtype, tm=512):
    t, d = x.shape
    n = w.shape[1]
    return pl.pallas_call(
        _norm_proj_kernel,
        grid=(t // tm,),
        in_specs=[
            pl.BlockSpec((tm, d), lambda i: (i, 0)),
            pl.BlockSpec((1, d), lambda i: (0, 0)),
            pl.BlockSpec((d, n), lambda i: (0, 0)),
        ],
        out_specs=pl.BlockSpec((tm, n), lambda i: (i, 0)),
        out_shape=jax.ShapeDtypeStruct((t, n), out_dtype),
        compiler_params=_params(("parallel",), 48),
        name="norm_proj",
    )(x, g.reshape(1, d), w)


def _norm_proj_stack_kernel(tiles_a, xa_ref, xb_ref, g_ref, w_ref, o_ref, x_ref):
    x = jnp.where(pl.program_id(0) < tiles_a, xa_ref[...], xb_ref[...])
    x_ref[...] = x
    inv = lax.rsqrt(jnp.mean(x * x, axis=-1, keepdims=True) + EPS)
    h = (x * inv * g_ref[...]).astype(BF16)
    o_ref[...] = _dot(h, w_ref[...]).astype(o_ref.dtype)


def _norm_proj_stack(xa, xb, g, w, out_dtype, tm=512):
    ta, d = xa.shape
    tb = xb.shape[0]
    n = w.shape[1]
    tiles_a = ta // tm
    return pl.pallas_call(
        functools.partial(_norm_proj_stack_kernel, tiles_a),
        grid=((ta + tb) // tm,),
        in_specs=[
            pl.BlockSpec((tm, d), lambda i: (jnp.minimum(i, tiles_a - 1), 0)),
            pl.BlockSpec((tm, d), lambda i: (jnp.maximum(i - tiles_a, 0), 0)),
            pl.BlockSpec((1, d), lambda i: (0, 0)),
            pl.BlockSpec((d, n), lambda i: (0, 0)),
        ],
        out_specs=[pl.BlockSpec((tm, n), lambda i: (i, 0)), pl.BlockSpec((tm, d), lambda i: (i, 0))],
        out_shape=[jax.ShapeDtypeStruct((ta + tb, n), out_dtype),
                   jax.ShapeDtypeStruct((ta + tb, d), xa.dtype)],
        compiler_params=_params(("parallel",), 56),
        name="norm_proj_stack",
    )(xa, xb, g.reshape(1, d), w)


def _shortconv_kernel(p_ref, w_ref, b_ref, o_ref):
    x = p_ref[0]
    n = x.shape[0]
    row = lax.broadcasted_iota(jnp.int32, x.shape, 0)
    prev = jnp.where(row == 0, 0.0, pltpu.roll(x, 1, 0))
    nxt = jnp.where(row == n - 1, 0.0, pltpu.roll(x, n - 1, 0))
    o_ref[0] = prev * w_ref[0:1, :] + x * w_ref[1:2, :] + nxt * w_ref[2:3, :] + b_ref[...]


def _shortconv(p3, w, b, tc=256):
    bsz, n, c = p3.shape
    return pl.pallas_call(
        _shortconv_kernel,
        grid=(bsz, c // tc),
        in_specs=[
            pl.BlockSpec((1, n, tc), lambda i, j: (i, 0, j)),
            pl.BlockSpec((3, tc), lambda i, j: (0, j)),
            pl.BlockSpec((1, tc), lambda i, j: (0, j)),
        ],
        out_specs=pl.BlockSpec((1, n, tc), lambda i, j: (i, 0, j)),
        out_shape=jax.ShapeDtypeStruct((bsz, n, c), F32),
        compiler_params=_params(("parallel", "parallel"), 32),
        name="hy_shortconv",
    )(p3, w, b.reshape(1, c))


def _hy_filter_kernel(z_ref, w1_ref, b1_ref, w2_ref, b2_ref, fr_ref, w3f_ref, w3b_ref,
                      decf_ref, decb_ref, hs_ref, hd_ref):
    fr = fr_ref[...]
    h = jnp.sin(fr * (jnp.dot(z_ref[...], w1_ref[...], precision=HIGHEST,
                              preferred_element_type=F32) + b1_ref[...]))
    h = jnp.sin(fr * (jnp.dot(h, w2_ref[...], precision=HIGHEST,
                              preferred_element_type=F32) + b2_ref[...]))
    n = h.shape[0]
    shape = (n, w3f_ref.shape[1])
    row = lax.broadcasted_iota(jnp.int32, shape, 0)
    t = row.astype(F32) * (1.0 / (n - 1))
    hf = jnp.dot(h, w3f_ref[...], precision=HIGHEST, preferred_element_type=F32)
    hf = hf * jnp.exp(-t * jnp.abs(decf_ref[...]))
    hb = jnp.dot(h, w3b_ref[...], precision=HIGHEST, preferred_element_type=F32)
    hb = hb * jnp.exp(-t * jnp.abs(decb_ref[...]))
    hb = jnp.where(row == 0, 0.0, hb)
    norm = (jnp.sum(jnp.abs(hf), axis=0, keepdims=True)
            + jnp.sum(jnp.abs(hb), axis=0, keepdims=True))
    hs_ref[...] = (hf + hb) / norm
    hd_ref[...] = (hb - hf) / norm


def _hy_filter(z, w1, b1, w2, b2, fr, w3, decay, tc=256):
    n = z.shape[0]
    cols = HY_ORDER * HY_WIDTH
    w3f, w3b = w3[:, :cols], w3[:, cols:]
    dec = decay.reshape(2, cols)
    zp = jnp.pad(z, ((0, 0), (0, LANES - HY_EMB)))
    w1p = jnp.pad(w1, ((0, LANES - HY_EMB), (0, 0)))
    full = lambda shape: pl.BlockSpec(shape, lambda j: (0, 0))
    col = lambda rows: pl.BlockSpec((rows, tc), lambda j: (0, j))
    return pl.pallas_call(
        _hy_filter_kernel,
        grid=(cols // tc,),
        in_specs=[full((n, LANES)), full((LANES, HY_FFN)), full((1, HY_FFN)), full((HY_FFN, HY_FFN)),
                  full((1, HY_FFN)), full((1, HY_FFN)), col(HY_FFN), col(HY_FFN), col(1), col(1)],
        out_specs=[col(n), col(n)],
        out_shape=[jax.ShapeDtypeStruct((n, cols), F32)] * 2,
        compiler_params=_params(("parallel",), 48),
        name="hy_filter",
    )(zp, w1p, b1.reshape(1, -1), w2, b2.reshape(1, -1), fr.reshape(1, -1), w3f, w3b,
      dec[0:1], dec[1:2])


def _hy_kf_kernel(ff_ref, hs_ref, hd_ref, k_ref):
    hs = hs_ref[...]
    kre = _dot(ff_ref[0], hs.astype(BF16))
    kq = _dot(ff_ref[1], hd_ref[...].astype(BF16))
    row_t = lax.broadcasted_iota(jnp.int32, hs.shape, 0)
    sgn = (1 - 2 * (row_t & 1)).astype(F32)
    nyq = jnp.sum(hs * sgn, axis=0, keepdims=True)
    row = lax.broadcasted_iota(jnp.int32, kq.shape, 0)
    first = jnp.logical_and(row == 0, pl.program_id(0) == 0)
    k_ref[0] = kre
    k_ref[1] = jnp.where(first, nyq, kq)


def _hy_kf(ff, hs, hd, tf=512, tc=512):
    n, cols = hs.shape
    return pl.pallas_call(
        _hy_kf_kernel,
        grid=(n // tf, cols // tc),
        in_specs=[
            pl.BlockSpec((2, tf, n), lambda i, j: (0, i, 0)),
            pl.BlockSpec((n, tc), lambda i, j: (0, j)),
            pl.BlockSpec((n, tc), lambda i, j: (0, j)),
        ],
        out_specs=pl.BlockSpec((2, tf, tc), lambda i, j: (0, i, j)),
        out_shape=jax.ShapeDtypeStruct((2, n, cols), F32),
        compiler_params=_params(("parallel", "parallel"), 48),
        name="hy_filter_dft",
    )(ff, hs, hd)


def _hy_fwd_kernel(ff_ref, u_ref, k_ref, y_ref):
    u = u_ref[0].astype(BF16)
    a = _dot(ff_ref[0], u)
    bq = _dot(ff_ref[1], u)
    kre = k_ref[0]
    kq = k_ref[1]
    row = lax.broadcasted_iota(jnp.int32, a.shape, 0)
    first = jnp.logical_and(row == 0, pl.program_id(1) == 0)
    y_ref[0, 0] = jnp.where(first, a * kre, a * kre + bq * kq).astype(BF16)
    y_ref[0, 1] = jnp.where(first, bq * kq, a * kq - bq * kre).astype(BF16)


def _hy_fwd(ff, u3, ucol, k, kcol, tf=512):
    bsz, n, _ = u3.shape
    c = HY_WIDTH
    return pl.pallas_call(
        _hy_fwd_kernel,
        grid=(bsz, n // tf),
        in_specs=[
            pl.BlockSpec((2, tf, n), lambda b, i: (0, i, 0)),
            pl.BlockSpec((1, n, c), lambda b, i: (b, 0, ucol)),
            pl.BlockSpec((2, tf, c), lambda b, i: (0, i, kcol)),
        ],
        out_specs=pl.BlockSpec((1, 2, tf, c), lambda b, i: (b, 0, i, 0)),
        out_shape=jax.ShapeDtypeStruct((bsz, 2, n, c), BF16),
        compiler_params=_params(("parallel", "parallel"), 48),
        name="hy_dft_fwd",
    )(ff, u3, k)


def _hy_inv_kernel(fi_ref, y_ref, s_ref, g_ref, d_ref, o_ref):
    y = _dot(fi_ref[0], y_ref[0, 0]) + _dot(fi_ref[1], y_ref[0, 1])
    o_ref[0] = (g_ref[0] * (y + s_ref[0] * d_ref[...])).astype(o_ref.dtype)


def _hy_inv(fi, y, s3, scol, g3, gcol, d, out_dtype, tt=512):
    bsz, _, n, c = y.shape
    return pl.pallas_call(
        _hy_inv_kernel,
        grid=(bsz, n // tt),
        in_specs=[
            pl.BlockSpec((2, tt, n), lambda b, i: (0, i, 0)),
            pl.BlockSpec((1, 2, n, c), lambda b, i: (b, 0, 0, 0)),
            pl.BlockSpec((1, tt, c), lambda b, i: (b, i, scol)),
            pl.BlockSpec((1, tt, c), lambda b, i: (b, i, gcol)),
            pl.BlockSpec((1, c), lambda b, i: (0, 0)),
        ],
        out_specs=pl.BlockSpec((1, tt, c), lambda b, i: (b, i, 0)),
        out_shape=jax.ShapeDtypeStruct((bsz, n, c), out_dtype),
        compiler_params=_params(("parallel", "parallel"), 48),
        name="hy_dft_inv",
    )(fi, y, s3, g3, d.reshape(1, c))


def _dft_matrices(n):
    r = lax.broadcasted_iota(jnp.int32, (n, n), 0)
    c = lax.broadcasted_iota(jnp.int32, (n, n), 1)
    ang = ((r * c) & (2 * n - 1)).astype(F32) * (math.pi / n)
    cos, sin = jnp.cos(ang), jnp.sin(ang)
    sgn_c = (1 - 2 * (c & 1)).astype(F32)
    sgn_r = (1 - 2 * (r & 1)).astype(F32)
    fwd = jnp.stack([cos, jnp.where(r == 0, sgn_c, sin)]).astype(BF16)
    inv0 = jnp.where(c == 0, 1.0, 2.0) * cos * (0.5 / n)
    inv1 = jnp.where(c == 0, sgn_r * (0.5 / n), sin * (-1.0 / n))
    inv = jnp.stack([inv0, inv1]).astype(BF16)
    return fwd, inv


def _hyena_embedding(n):
    t = jnp.linspace(0.0, 1.0, n, dtype=F32)[:, None]
    bands = (HY_EMB - 1) // 2
    f = jnp.linspace(1e-4, bands - 1, bands, dtype=F32)[None]
    w = 2.0 * math.pi * jnp.arange(n, dtype=F32)[:, None] / n
    return jnp.concatenate([t, jnp.cos(f * w), -jnp.sin(f * w)], -1)


def _hyena(p3, dft, emb, conv_w, conv_b, w1, b1, w2, b2, w3, freq, decay, skip):
    ff, fi = dft
    u = _shortconv(p3, conv_w, conv_b)
    hs, hd = _hy_filter(emb, w1, b1, w2, b2, freq, w3, decay)
    k = _hy_kf(ff, hs, hd)
    y = _hy_fwd(ff, u, 0, k, 0)
    z = _hy_inv(fi, y, u, 0, u, 1, skip[0], F32)
    y = _hy_fwd(ff, z, 0, k, 1)
    return _hy_inv(fi, y, z, 0, u, 2, skip[1], BF16)


def _t5_bucket(rel):
    nb = REL_BUCKETS // 2
    max_exact = nb // 2
    n = np.abs(rel)
    large = max_exact + (np.log(np.maximum(n, 1) / max_exact) / np.log(REL_MAX_DIST / max_exact)
                         * (nb - max_exact)).astype(np.int32)
    large = np.minimum(large, nb - 1)
    return (rel > 0).astype(np.int32) * nb + np.where(n < max_exact, n, large)


def _attn_kernel(q_ref, kp_ref, kc_ref, kn_ref, vp_ref, vc_ref, vn_ref, bias_ref, sink_ref, o_ref):
    n = pl.program_id(1)
    q = q_ref[0] * (HEAD_DIM ** -0.5)
    k = jnp.concatenate([kp_ref[0], kc_ref[0], kn_ref[0]], axis=0)
    vt = jnp.concatenate([vp_ref[0], vc_ref[0], vn_ref[0]], axis=1)
    edge_prev = jnp.where(n == 0, -1e30, 0.0).astype(F32)
    edge_next = jnp.where(n == pl.num_programs(1) - 1, -1e30, 0.0).astype(F32)
    ones = jnp.ones((HEAD_DIM, 3 * BLOCK), BF16)
    for kh in range(ATT_KV_HEADS):
        k_h = k[:, kh * HEAD_DIM:(kh + 1) * HEAD_DIM]
        vt_h = jnp.concatenate([vt[kh * HEAD_DIM:(kh + 1) * HEAD_DIM], ones], axis=0)
        q_h = jnp.concatenate([q[:, (kh * ATT_GROUP + g) * HEAD_DIM:(kh * ATT_GROUP + g + 1) * HEAD_DIM]
                               for g in range(ATT_GROUP)], axis=0)
        s = _dot_nt(k_h, q_h) + bias_ref[kh]
        s = jnp.concatenate([s[:BLOCK] + edge_prev, s[BLOCK:2 * BLOCK], s[2 * BLOCK:] + edge_next],
                            axis=0)
        sink = sink_ref[kh]
        m = jnp.maximum(jnp.max(s, axis=0, keepdims=True), sink)
        p = jnp.exp(s - m).astype(BF16)
        o = _dot(vt_h, p)
        o = o[:HEAD_DIM] / (o[HEAD_DIM:HEAD_DIM + 1] + jnp.exp(sink - m))
        for g in range(0, ATT_GROUP, 2):
            pair = jnp.concatenate([o[:, g * BLOCK:(g + 1) * BLOCK],
                                    o[:, (g + 1) * BLOCK:(g + 2) * BLOCK]], axis=0)
            c0 = (kh * ATT_GROUP + g) * HEAD_DIM
            o_ref[0, :, c0:c0 + 2 * HEAD_DIM] = pair.T.astype(o_ref.dtype)


def _attention_bias(rel_bias):
    rel = (np.arange(3 * BLOCK)[None, :] - BLOCK) - np.arange(BLOCK)[:, None]
    bucket = jnp.asarray(_t5_bucket(rel).reshape(-1), jnp.int32)
    onehot = (bucket[:, None] == jnp.arange(REL_BUCKETS, dtype=jnp.int32)[None]).astype(F32)
    bias = jnp.dot(onehot, rel_bias.astype(F32), precision=HIGHEST)
    bias = bias.reshape(BLOCK, 3 * BLOCK, ATT_KV_HEADS, ATT_GROUP)
    bias = jnp.where(jnp.asarray(np.abs(rel) <= WINDOW)[:, :, None, None], bias, -1e30)
    return jnp.transpose(bias, (2, 1, 3, 0)).reshape(ATT_KV_HEADS, 3 * BLOCK, ATT_GROUP * BLOCK)


def _attention(att3, sink, rel_bias):
    bsz, n, _ = att3.shape
    nb = n // BLOCK
    bias = _attention_bias(rel_bias)
    sink_row = jnp.repeat(sink.astype(F32).reshape(ATT_KV_HEADS, 1, ATT_GROUP), BLOCK, axis=2)
    vt = jnp.swapaxes(att3[:, :, ATT_WIDTH + KV_WIDTH:], 1, 2)
    kcol = ATT_WIDTH // KV_WIDTH
    prev = lambda i: jnp.maximum(i - 1, 0)
    nxt = lambda i: jnp.minimum(i + 1, nb - 1)
    cur = lambda i: i
    k_spec = lambda blk: pl.BlockSpec((1, BLOCK, KV_WIDTH), lambda b, i: (b, blk(i), kcol))
    v_spec = lambda blk: pl.BlockSpec((1, KV_WIDTH, BLOCK), lambda b, i: (b, 0, blk(i)))
    return pl.pallas_call(
        _attn_kernel,
        grid=(bsz, nb),
        in_specs=[
            pl.BlockSpec((1, BLOCK, ATT_WIDTH), lambda b, i: (b, i, 0)),
            k_spec(prev), k_spec(cur), k_spec(nxt), v_spec(prev), v_spec(cur), v_spec(nxt),
            pl.BlockSpec((ATT_KV_HEADS, 3 * BLOCK, ATT_GROUP * BLOCK), lambda b, i: (0, 0, 0)),
            pl.BlockSpec((ATT_KV_HEADS, 1, ATT_GROUP * BLOCK), lambda b, i: (0, 0, 0)),
        ],
        out_specs=pl.BlockSpec((1, BLOCK, ATT_WIDTH), lambda b, i: (b, i, 0)),
        out_shape=jax.ShapeDtypeStruct((bsz, n, ATT_WIDTH), BF16),
        compiler_params=_params(("parallel", "parallel"), 32),
        name="window_attention",
    )(att3, att3, att3, att3, vt, vt, vt, bias, sink_row)


def _rotary_tables(n):
    inv = ROPE_BASE ** -jnp.linspace(0.0, 1.0, HEAD_DIM // 2, dtype=F32)
    ang = jnp.arange(n, dtype=F32)[:, None] * inv[None]
    cos, sin = jnp.cos(ang), jnp.sin(ang)
    cos_t = jnp.tile(jnp.concatenate([cos, cos], -1), (1, RET_HEADS))
    sin_t = jnp.tile(jnp.concatenate([-sin, sin], -1), (1, RET_HEADS))
    return cos_t, sin_t


def _rotate(x, cos_t, sin_t):
    width = x.shape[-1]
    half = HEAD_DIM // 2
    lane = lax.broadcasted_iota(jnp.int32, x.shape, 1)
    partner = jnp.where((lane & (HEAD_DIM - 1)) < half,
                        pltpu.roll(x, width - half, 1), pltpu.roll(x, half, 1))
    return x * cos_t + partner * sin_t


def _lane_head(shape):
    return lax.broadcasted_iota(jnp.int32, shape, 1) // HEAD_DIM


def _head_blocks_to_compact(full):
    head = _lane_head((HEAD_DIM, RET_WIDTH))
    acc = jnp.zeros((HEAD_DIM, RET_WIDTH), F32)
    for h in range(RET_HEADS):
        acc = acc + jnp.where(head == h, full[h * HEAD_DIM:(h + 1) * HEAD_DIM], 0.0)
    return acc


def _compact_to_head_blocks(compact):
    head = _lane_head((HEAD_DIM, RET_WIDTH))
    return jnp.concatenate([jnp.where(head == h, compact, 0.0) for h in range(RET_HEADS)], axis=0)


def _ret_state_kernel(kf_ref, vf_ref, kb_ref, vb_ref, cf_ref, sf_ref, cb_ref, sb_ref, lg_ref,
                      of_ref, ob_ref, stf_ref, stb_ref):
    @pl.when(pl.program_id(1) == 0)
    def _():
        stf_ref[...] = jnp.zeros_like(stf_ref)
        stb_ref[...] = jnp.zeros_like(stb_ref)

    of_ref[0, 0] = stf_ref[...]
    ob_ref[0, 0] = stb_ref[...]
    c = RET_CHUNK
    j = lax.broadcasted_iota(jnp.int32, (c, RET_WIDTH), 0).astype(F32)
    lgf = lg_ref[0:1, :]
    lgb = lg_ref[1:2, :]
    scale = HEAD_DIM ** -0.5

    def chunk_state(k_ref, v_ref, cos_ref, sin_ref, zeta):
        k = _rotate(k_ref[0], cos_ref[...], sin_ref[...]) * scale * zeta
        kv = _dot(k.T.astype(BF16), v_ref[0].astype(BF16))
        return _head_blocks_to_compact(kv)

    kvf = chunk_state(kf_ref, vf_ref, cf_ref, sf_ref, jnp.exp((c - 1.0 - j) * lgf))
    kvb = chunk_state(kb_ref, vb_ref, cb_ref, sb_ref, jnp.exp(j * lgb))
    stf_ref[...] = jnp.exp(c * lgf) * stf_ref[...] + kvf
    stb_ref[...] = jnp.exp(c * lgb) * stb_ref[...] + kvb


def _ret_states(ret3, cos_t, sin_t, lg_lane):
    bsz, n, _ = ret3.shape
    nc = n // RET_CHUNK
    c, w = RET_CHUNK, RET_WIDTH
    fwd = lambda b, i: i
    bwd = lambda b, i: nc - 1 - i
    col = lambda pos, cb: pl.BlockSpec((1, c, w), lambda b, i: (b, pos(b, i), cb))
    tab = lambda pos: pl.BlockSpec((c, w), lambda b, i: (pos(b, i), 0))
    out = lambda pos: pl.BlockSpec((1, 1, HEAD_DIM, w), lambda b, i: (b, pos(b, i), 0, 0))
    return pl.pallas_call(
        _ret_state_kernel,
        grid=(bsz, nc),
        in_specs=[col(fwd, 1), col(fwd, 2), col(bwd, 1), col(bwd, 2),
                  tab(fwd), tab(fwd), tab(bwd), tab(bwd),
                  pl.BlockSpec((2, w), lambda b, i: (0, 0))],
        out_specs=[out(fwd), out(bwd)],
        out_shape=[jax.ShapeDtypeStruct((bsz, nc, HEAD_DIM, w), F32)] * 2,
        scratch_shapes=[pltpu.VMEM((HEAD_DIM, w), F32), pltpu.VMEM((HEAD_DIM, w), F32)],
        compiler_params=_params(("parallel", "arbitrary"), 32),
        name="retention_states",
    )(ret3, ret3, ret3, ret3, cos_t, sin_t, cos_t, sin_t, lg_lane)


def _ret_out_kernel(q_ref, k_ref, v_ref, g_ref, cos_ref, sin_ref, lg_ref, lgs_ref, stf_ref, stb_ref,
                    o_ref):
    c, w = RET_CHUNK, RET_WIDTH
    cos_t, sin_t = cos_ref[...], sin_ref[...]
    q = _rotate(q_ref[0], cos_t, sin_t)
    k = _rotate(k_ref[0], cos_t, sin_t) * (HEAD_DIM ** -0.5)
    v = v_ref[0].astype(BF16)
    head = _lane_head((c, w))
    qs = jnp.concatenate([jnp.where(head == h, q, 0.0) for h in range(RET_HEADS)], axis=0)
    s = _dot_nt(qs.astype(BF16), k.astype(BF16))
    i = lax.broadcasted_iota(jnp.int32, (c, c), 0)
    jj = lax.broadcasted_iota(jnp.int32, (c, c), 1)
    diff = (i - jj).astype(F32)
    decay = jnp.concatenate(
        [jnp.exp(jnp.where(diff >= 0, diff * lgs_ref[0, h], -diff * lgs_ref[1, h]))
         for h in range(RET_HEADS)], axis=0)
    o_all = _dot((s * decay).astype(BF16), v)
    y = jnp.zeros((c, w), F32)
    for h in range(RET_HEADS):
        y = y + jnp.where(head == h, o_all[h * c:(h + 1) * c], 0.0)
    pos = lax.broadcasted_iota(jnp.int32, (c, w), 0).astype(F32)
    xi_f = jnp.exp((pos + 1.0) * lg_ref[0:1, :])
    xi_b = jnp.exp((c - pos) * lg_ref[1:2, :])
    y = y + _dot((q * xi_f).astype(BF16), _compact_to_head_blocks(stf_ref[0, 0]).astype(BF16))
    y = y + _dot((q * xi_b).astype(BF16), _compact_to_head_blocks(stb_ref[0, 0]).astype(BF16))
    r = lax.broadcasted_iota(jnp.int32, (w, w), 0) // HEAD_DIM
    avg = jnp.where(r == _lane_head((w, w)), 1.0 / HEAD_DIM, 0.0)
    ms = jnp.dot(y * y, avg, precision=HIGHEST, preferred_element_type=F32)
    y = y * lax.rsqrt(ms + EPS)
    g = g_ref[0]
    o_ref[0] = (g / (1.0 + jnp.exp(-g)) * y).astype(o_ref.dtype)


def _retention(ret3, decay):
    bsz, n, _ = ret3.shape
    nc = n // RET_CHUNK
    c, w = RET_CHUNK, RET_WIDTH
    lg = -jnp.exp(decay.astype(F32))
    lg_lane = jnp.repeat(lg, HEAD_DIM, axis=1)
    cos_t, sin_t = _rotary_tables(n)
    stf, stb = _ret_states(ret3, cos_t, sin_t, lg_lane)
    col = lambda cb: pl.BlockSpec((1, c, w), lambda b, i: (b, i, cb))
    tab = pl.BlockSpec((c, w), lambda b, i: (i, 0))
    st = pl.BlockSpec((1, 1, HEAD_DIM, w), lambda b, i: (b, i, 0, 0))
    return pl.pallas_call(
        _ret_out_kernel,
        grid=(bsz, nc),
        in_specs=[col(0), col(1), col(2), col(3), tab, tab,
                  pl.BlockSpec((2, w), lambda b, i: (0, 0)),
                  pl.BlockSpec(memory_space=pltpu.SMEM), st, st],
        out_specs=pl.BlockSpec((1, c, w), lambda b, i: (b, i, 0)),
        out_shape=jax.ShapeDtypeStruct((bsz, n, w), BF16),
        compiler_params=_params(("parallel", "parallel"), 32),
        name="retention_out",
    )(ret3, ret3, ret3, ret3, cos_t, sin_t, lg_lane, lg, stf, stb)


ROUTE_I0, ROUTE_I1, ROUTE_W0, ROUTE_W1, ROUTE_R0, ROUTE_R1 = 0, 1, 2, 3, 4, 5


def _mix_router_kernel(x_ref, yh_ref, ya_ref, yr_ref, w_ref, g_ref, wr_ref, br_ref,
                       xo_ref, hp_ref, route_ref, counts_ref, carry_ref):
    @pl.when(pl.program_id(0) == 0)
    def _():
        carry_ref[...] = jnp.zeros_like(carry_ref)

    a, b = HY_WIDTH, HY_WIDTH + ATT_WIDTH
    x = x_ref[...] + _dot(yh_ref[...], w_ref[0:a, :]) + _dot(ya_ref[...], w_ref[a:b, :]) \
        + _dot(yr_ref[...], w_ref[b:, :])
    xo_ref[...] = x
    h = x * lax.rsqrt(jnp.mean(x * x, axis=-1, keepdims=True) + EPS) * g_ref[...]
    h_hi = h.astype(BF16)
    bits = lax.bitcast_convert_type(h_hi.astype(F32), jnp.uint32)
    half = bits.shape[1] // 2
    hp_ref[...] = bits[:, :half] | (bits[:, half:] >> 16)
    h_lo = (h - h_hi.astype(F32)).astype(BF16)
    wr = wr_ref[...]
    wr_hi = wr.astype(BF16)
    wr_lo = (wr - wr_hi.astype(F32)).astype(BF16)
    logits = _dot(h_hi, wr_hi) + (_dot(h_lo, wr_hi) + _dot(h_hi, wr_lo)) + br_ref[...]
    lane = lax.broadcasted_iota(jnp.int32, logits.shape, 1)
    big = jnp.int32(LANES)
    neg = -jnp.inf
    is_group = (lane >= N_EXPERTS) & (lane < N_EXPERTS + N_GROUPS)
    gl = jnp.where(is_group, logits, neg)
    gmax = jnp.max(gl, axis=-1, keepdims=True)
    g_w = 1.0 / jnp.sum(jnp.exp(gl - gmax), axis=-1, keepdims=True)
    g_sel = jnp.min(jnp.where(gl == gmax, lane, big), axis=-1, keepdims=True) - N_EXPERTS
    lo = g_sel * EXPERTS_PER_GROUP
    el = jnp.where((lane >= lo) & (lane < lo + EXPERTS_PER_GROUP), logits, neg)
    v0 = jnp.max(el, axis=-1, keepdims=True)
    i0 = jnp.min(jnp.where(el == v0, lane, big), axis=-1, keepdims=True)
    el = jnp.where(lane == i0, neg, el)
    v1 = jnp.max(el, axis=-1, keepdims=True)
    i1 = jnp.min(jnp.where(el == v1, lane, big), axis=-1, keepdims=True)
    e1 = jnp.exp(v1 - v0)
    w0 = g_w / (1.0 + e1)
    oh0 = jnp.where(lane == i0, 1.0, 0.0)
    oh1 = jnp.where(lane == i1, 1.0, 0.0)
    tm = logits.shape[0]
    earlier = jnp.where(lax.broadcasted_iota(jnp.int32, (tm, tm), 1)
                        < lax.broadcasted_iota(jnp.int32, (tm, tm), 0), 1.0, 0.0).astype(BF16)
    carry = carry_ref[...]
    tot0 = jnp.sum(oh0, axis=0, keepdims=True)
    before0 = _dot(earlier, oh0.astype(BF16)) + carry
    before1 = _dot(earlier, oh1.astype(BF16)) + (carry + tot0)
    r0 = jnp.sum(oh0 * before0, axis=-1, keepdims=True)
    r1 = jnp.sum(oh1 * before1, axis=-1, keepdims=True)
    carry = carry + tot0 + jnp.sum(oh1, axis=0, keepdims=True)
    carry_ref[...] = carry
    counts_ref[...] = jnp.broadcast_to(carry, counts_ref.shape)
    route_ref[...] = (jnp.where(lane == ROUTE_I0, i0.astype(F32), 0.0)
                      + jnp.where(lane == ROUTE_I1, i1.astype(F32), 0.0)
                      + jnp.where(lane == ROUTE_W0, w0, 0.0)
                      + jnp.where(lane == ROUTE_W1, w0 * e1, 0.0)
                      + jnp.where(lane == ROUTE_R0, r0, 0.0)
                      + jnp.where(lane == ROUTE_R1, r1, 0.0))


def _mix_router(x, yh, ya, yr, w_out, g, wr, br, tm=512):
    t, d = x.shape
    row = lambda width: pl.BlockSpec((tm, width), lambda i: (i, 0))
    full = lambda shape: pl.BlockSpec(shape, lambda i: (0, 0))
    return pl.pallas_call(
        _mix_router_kernel,
        grid=(t // tm,),
        in_specs=[row(d), row(HY_WIDTH), row(ATT_WIDTH), row(RET_WIDTH),
                  pl.BlockSpec((d, d), lambda i: (0, 0), pipeline_mode=pl.Buffered(1)),
                  full((1, d)), full((d, LANES)), full((1, LANES))],
        out_specs=[row(d), row(d // 2), row(LANES), full((SUBLANES, LANES))],
        out_shape=[jax.ShapeDtypeStruct((t, d), F32), jax.ShapeDtypeStruct((t, d // 2), jnp.uint32),
                   jax.ShapeDtypeStruct((t, LANES), F32), jax.ShapeDtypeStruct((SUBLANES, LANES), F32)],
        scratch_shapes=[pltpu.VMEM((1, LANES), F32)],
        compiler_params=_params(("arbitrary",), 48),
        name="mix_router",
    )(x, yh, ya, yr, w_out, g.reshape(1, d), wr, br)


MOE_TM = 512
MOE_TC = 256


def _dispatch_plan(route, counts, tm):
    experts = jnp.arange(N_EXPERTS, dtype=jnp.int32)
    ids = route[:, ROUTE_I0:ROUTE_I1 + 1].astype(jnp.int32)
    rank = route[:, ROUTE_R0:ROUTE_R1 + 1].astype(jnp.int32)
    n_tiles = 2 * route.shape[0] // tm + N_EXPERTS
    tiles = (counts[0, :N_EXPERTS].astype(jnp.int32) + tm - 1) // tm
    tile_end = jnp.cumsum(tiles)
    offs = (tile_end - tiles) * tm
    slot = jnp.sum(jnp.where(ids[..., None] == experts, offs, 0), axis=-1) + rank
    tile_expert = jnp.minimum(
        jnp.sum(jnp.arange(n_tiles, dtype=jnp.int32)[:, None] >= tile_end[None, :], axis=1),
        N_EXPERTS - 1).astype(jnp.int32)
    return slot, n_tiles, tile_expert, tile_end[-1:].astype(jnp.int32)


def _moe_dispatch_kernel(idx_ref, hp_ref, xs_in_ref, xs_ref, sem):
    del xs_in_ref
    tokens = hp_ref.shape[0]

    def issue(r, carry):
        for k in range(2):
            pltpu.make_async_copy(hp_ref.at[pl.ds(r, 1)],
                                  xs_ref.at[pl.ds(idx_ref[0, 0, 2 * r + k], 1)], sem).start()
        return carry

    lax.fori_loop(0, tokens, issue, 0, unroll=4)
    for k in range(2):
        pltpu.make_async_copy(hp_ref, xs_ref.at[pl.ds(0, tokens)], sem).wait()


def _moe_dispatch(hp, slot, n_slots, tc=MOE_TC):
    t, half = hp.shape
    steps = t // tc
    return pl.pallas_call(
        _moe_dispatch_kernel,
        grid=(steps,),
        in_specs=[pl.BlockSpec((1, 1, 2 * tc), lambda i: (i, 0, 0), memory_space=pltpu.SMEM),
                  pl.BlockSpec((tc, half), lambda i: (i, 0)),
                  pl.BlockSpec(memory_space=pl.ANY)],
        out_specs=pl.BlockSpec(memory_space=pl.ANY),
        out_shape=jax.ShapeDtypeStruct((n_slots, half), hp.dtype),
        input_output_aliases={2: 0},
        scratch_shapes=[pltpu.SemaphoreType.DMA(())],
        compiler_params=_params(("arbitrary",), 16),
        name="moe_dispatch",
    )(slot.reshape(steps, 1, 2 * tc), hp, jnp.zeros((n_slots, half), hp.dtype))


MOE_CAST_ROWS = 256


def _moe_gemm_kernel(te_ref, nv_ref, xs_ref, wg_ref, wu_ref, wd_ref, y_ref, wgb_ref, wub_ref, wdb_ref):
    i = pl.program_id(0)
    last = nv_ref[0] - 1
    expert = te_ref[jnp.minimum(i, last)]
    fresh = jnp.logical_or(i == 0, expert != te_ref[jnp.maximum(jnp.minimum(i, last) - 1, 0)])

    @pl.when(jnp.logical_and(i <= last, fresh))
    def _():
        for src_ref, dst_ref in ((wg_ref, wgb_ref), (wu_ref, wub_ref), (wd_ref, wdb_ref)):
            def cast(j, carry):
                rows = pl.ds(pl.multiple_of(j * MOE_CAST_ROWS, MOE_CAST_ROWS), MOE_CAST_ROWS)
                dst_ref[rows, :] = src_ref[0, 0, rows, :].astype(BF16)
                return carry
            lax.fori_loop(0, dst_ref.shape[0] // MOE_CAST_ROWS, cast, 0)

    @pl.when(i <= last)
    def _():
        xp = xs_ref[...]
        x_a = lax.bitcast_convert_type(xp & jnp.uint32(0xFFFF0000), F32).astype(BF16)
        x_b = lax.bitcast_convert_type(xp << 16, F32).astype(BF16)
        half = xp.shape[1]
        a = _dot(x_a, wgb_ref[:half, :]) + _dot(x_b, wgb_ref[half:, :])
        b = _dot(x_a, wub_ref[:half, :]) + _dot(x_b, wub_ref[half:, :])
        act = a / (1.0 + jnp.exp(-a)) * b
        y_ref[...] = _dot(act.astype(BF16), wdb_ref[...])

    @pl.when(i > last)
    def _():
        y_ref[...] = jnp.zeros_like(y_ref)


def _moe_gemm(xs, tile_expert, n_valid, layer, wg, wu, wd, tm=MOE_TM):
    n_slots, half = xs.shape
    _, ne, d, de = wg.shape
    tile = lambda i, te, nv: jnp.minimum(i, nv[0] - 1)
    wspec = lambda r, c: pl.BlockSpec((1, 1, r, c), lambda i, te, nv: (layer, te[tile(i, te, nv)], 0, 0),
                                      pipeline_mode=pl.Buffered(1))
    return pl.pallas_call(
        _moe_gemm_kernel,
        grid_spec=pltpu.PrefetchScalarGridSpec(
            num_scalar_prefetch=2,
            grid=(n_slots // tm,),
            in_specs=[pl.BlockSpec((tm, half), lambda i, te, nv: (tile(i, te, nv), 0)),
                      wspec(d, de), wspec(d, de), wspec(de, d)],
            out_specs=pl.BlockSpec((tm, d), lambda i, te, nv: (i, 0)),
            scratch_shapes=[pltpu.VMEM((d, de), BF16), pltpu.VMEM((d, de), BF16),
                            pltpu.VMEM((de, d), BF16)],
        ),
        out_shape=jax.ShapeDtypeStruct((n_slots, d), F32),
        compiler_params=_params(("arbitrary",), 58),
        name="moe_gemm",
    )(tile_expert, n_valid, xs, wg, wu, wd)


def _moe_combine_kernel(final_norm, steps_a, idx_ref, idx_next_ref, x_ref, route_ref, gf_ref, y_ref,
                        *rest):
    o_refs, (buf_ref, sem_ref) = rest[:-2], rest[-2:]
    i = pl.program_id(0)
    n = pl.num_programs(0)
    tokens = x_ref.shape[0]

    def start_rows(idx, slot):
        def issue(r, carry):
            for k in range(2):
                pltpu.make_async_copy(y_ref.at[pl.ds(idx[0, 0, 2 * r + k], 1)],
                                      buf_ref.at[slot, k, pl.ds(r, 1)], sem_ref.at[slot]).start()
            return carry
        lax.fori_loop(0, tokens, issue, 0, unroll=4)

    @pl.when(i == 0)
    def _():
        start_rows(idx_ref, 0)

    @pl.when(i + 1 < n)
    def _():
        start_rows(idx_next_ref, (i + 1) % 2)

    slot = i % 2
    for k in range(2):
        pltpu.make_async_copy(y_ref.at[pl.ds(0, tokens)], buf_ref.at[slot, k], sem_ref.at[slot]).wait()
    route = route_ref[...]
    y = x_ref[...] + route[:, ROUTE_W0:ROUTE_W0 + 1] * buf_ref[slot, 0] \
        + route[:, ROUTE_W1:ROUTE_W1 + 1] * buf_ref[slot, 1]
    if final_norm:
        y = y * lax.rsqrt(jnp.mean(y * y, axis=-1, keepdims=True) + EPS) * gf_ref[...]
    if len(o_refs) == 1:
        o_refs[0][...] = y
    else:
        @pl.when(i < steps_a)
        def _():
            o_refs[0][...] = y

        @pl.when(i >= steps_a)
        def _():
            o_refs[1][...] = y


def _moe_combine(x, route, slot, y, gf, final_norm, tokens_a=None, tc=MOE_TC):
    t, d = x.shape
    steps = t // tc
    idx = slot.reshape(steps, 1, 2 * tc)
    row = lambda width: pl.BlockSpec((tc, width), lambda i: (i, 0))
    smem = lambda pos: pl.BlockSpec((1, 1, 2 * tc), lambda i: (pos(i), 0, 0), memory_space=pltpu.SMEM)
    if tokens_a is None:
        steps_a = steps
        out_specs = [row(d)]
        out_shape = [jax.ShapeDtypeStruct((t, d), F32)]
    else:
        steps_a = tokens_a // tc
        out_specs = [pl.BlockSpec((tc, d), lambda i: (jnp.minimum(i, steps_a - 1), 0)),
                     pl.BlockSpec((tc, d), lambda i: (jnp.maximum(i - steps_a, 0), 0))]
        out_shape = [jax.ShapeDtypeStruct((tokens_a, d), F32),
                     jax.ShapeDtypeStruct((t - tokens_a, d), F32)]
    return pl.pallas_call(
        functools.partial(_moe_combine_kernel, final_norm, steps_a),
        grid=(steps,),
        in_specs=[smem(lambda i: i), smem(lambda i: jnp.minimum(i + 1, steps - 1)),
                  row(d), row(LANES), pl.BlockSpec((1, d), lambda i: (0, 0)),
                  pl.BlockSpec(memory_space=pl.ANY)],
        out_specs=out_specs,
        out_shape=out_shape,
        scratch_shapes=[pltpu.VMEM((2, 2, tc, d), F32), pltpu.SemaphoreType.DMA((2,))],
        compiler_params=_params(("arbitrary",), 40),
        name="moe_combine",
    )(idx, idx, x, route, gf.reshape(1, d), y)


def _moe(hp, route, counts, x, layer, wg, wu, wd, gf, final_norm, tokens_a=None):
    slot, n_tiles, tile_expert, n_valid = _dispatch_plan(route, counts, MOE_TM)
    xs = _moe_dispatch(hp, slot, n_tiles * MOE_TM)
    y = _moe_gemm(xs, tile_expert, n_valid, layer, wg, wu, wd)
    return _moe_combine(x, route, slot, y, gf, final_norm, tokens_a)


def _router_weights(wg, bg, we, be):
    d = wg.shape[0]
    pad = LANES - N_EXPERTS - N_GROUPS
    wr = jnp.concatenate([we, wg, jnp.zeros((d, pad), F32)], axis=1).astype(F32)
    br = jnp.concatenate([be, bg, jnp.zeros((pad,), F32)]).astype(F32).reshape(1, LANES)
    return wr, br


def _layer(xs, bsz, p, l, consts, final_g, tokens_a):
    dft, emb = consts
    w_in = p['w_in'][l].astype(BF16)
    g1 = p['ln1_g'][l]
    if len(xs) == 2:
        hy, x = _norm_proj_stack(xs[0], xs[1], g1, w_in[:, :HY_IN], F32)
    else:
        x, = xs
        hy = _norm_proj(x, g1, w_in[:, :HY_IN], F32)
    t, d = x.shape
    n = t // bsz
    hy = hy.reshape(bsz, n, HY_IN)
    att = _norm_proj(x, g1, w_in[:, HY_IN:HY_IN + ATT_IN], BF16).reshape(bsz, n, ATT_IN)
    ret = _norm_proj(x, g1, w_in[:, HY_IN + ATT_IN:], F32).reshape(bsz, n, RET_IN)
    y_h = _hyena(hy, dft, emb, p['hy_conv_w'][l], p['hy_conv_b'][l], p['hy_w1'][l], p['hy_b1'][l],
                 p['hy_w2'][l], p['hy_b2'][l], p['hy_w3'][l], p['hy_freq'][l], p['hy_decay'][l],
                 p['hy_skip'][l])
    y_a = _attention(att, p['attn_sink'][l], p['rel_bias'])
    y_r = _retention(ret, p['ret_decay'][l])
    wr, br = _router_weights(p['router_group_w'][l], p['router_group_b'][l],
                             p['router_expert_w'][l], p['router_expert_b'][l])
    x, hp, route, counts = _mix_router(x, y_h.reshape(t, -1), y_a.reshape(t, -1), y_r.reshape(t, -1),
                                       p['w_out'][l].astype(BF16), p['ln2_g'][l], wr, br)
    last = l == DEPTH - 1
    return _moe(hp, route, counts, x, l, p['moe_w_gate'], p['moe_w_up'], p['moe_w_down'], final_g,
                final_norm=last, tokens_a=tokens_a if last else None)


def kernel(x_prompt, x_sample, ln1_g, w_in, hy_conv_w, hy_conv_b, hy_w1, hy_b1, hy_w2, hy_b2, hy_w3,
           hy_freq, hy_decay, hy_skip, attn_sink, rel_bias, ret_decay, w_out, ln2_g, router_group_w,
           router_group_b, router_expert_w, router_expert_b, moe_w_gate, moe_w_up, moe_w_down, lnf_g):
    p = dict(ln1_g=ln1_g, w_in=w_in, hy_conv_w=hy_conv_w, hy_conv_b=hy_conv_b, hy_w1=hy_w1,
             hy_b1=hy_b1, hy_w2=hy_w2, hy_b2=hy_b2, hy_w3=hy_w3, hy_freq=hy_freq, hy_decay=hy_decay,
             hy_skip=hy_skip, attn_sink=attn_sink, rel_bias=rel_bias, ret_decay=ret_decay, w_out=w_out,
             ln2_g=ln2_g, router_group_w=router_group_w, router_group_b=router_group_b,
             router_expert_w=router_expert_w, router_expert_b=router_expert_b, moe_w_gate=moe_w_gate,
             moe_w_up=moe_w_up, moe_w_down=moe_w_down)
    bp, n, d = x_prompt.shape
    bs = x_sample.shape[0]
    assert x_sample.shape[1:] == (n, d)
    bsz = bp + bs
    consts = (_dft_matrices(n), _hyena_embedding(n))
    xs = (x_prompt.reshape(bp * n, d), x_sample.reshape(bs * n, d))
    for l in range(DEPTH):
        xs = _layer(xs, bsz, p, l, consts, lnf_g, bp * n)
    y_prompt, y_sample = xs
    return (y_prompt.reshape(bp, n, d), y_sample.reshape(bs, n, d))
```

```python
import functools
import math

import numpy as np
import jax
import jax.numpy as jnp
from jax import lax
from jax.experimental import pallas as pl
from jax.experimental.pallas import tpu as pltpu

F32 = jnp.float32
BF16 = jnp.bfloat16
HIGHEST = lax.Precision.HIGHEST

D_MODEL = 2048
DEPTH = 2
HEAD_DIM = 64
EPS = 1e-6
HY_WIDTH = D_MODEL // 4
ATT_WIDTH = D_MODEL // 2
RET_WIDTH = D_MODEL // 4
HY_ORDER = 2
HY_EMB = 33
HY_FFN = 64
ATT_HEADS = ATT_WIDTH // HEAD_DIM
ATT_KV_HEADS = 4
ATT_GROUP = ATT_HEADS // ATT_KV_HEADS
KV_WIDTH = ATT_KV_HEADS * HEAD_DIM
WINDOW = 128
BLOCK = 128
REL_BUCKETS = 32
REL_MAX_DIST = 128
RET_HEADS = RET_WIDTH // HEAD_DIM
RET_CHUNK = 128
ROPE_BASE = 10000.0
N_GROUPS = 4
EXPERTS_PER_GROUP = 4
N_EXPERTS = N_GROUPS * EXPERTS_PER_GROUP
D_EXPERT = D_MODEL // 2
HY_IN = 3 * HY_WIDTH
ATT_IN = ATT_WIDTH + 2 * KV_WIDTH
RET_IN = 4 * RET_WIDTH

LANES = 128
SUBLANES = 8
MIB = 1024 * 1024


def _params(semantics, vmem_mib):
    return pltpu.CompilerParams(dimension_semantics=semantics, vmem_limit_bytes=vmem_mib * MIB)


def _dot(a, b):
    return jnp.dot(a, b, preferred_element_type=F32)


def _dot_nt(a, b):
    return lax.dot_general(a, b, (((1,), (1,)), ((), ())), preferred_element_type=F32)


def _norm_proj_kernel(x_ref, g_ref, w_ref, o_ref):
    x = x_ref[...]
    inv = lax.rsqrt(jnp.mean(x * x, axis=-1, keepdims=True) + EPS)
    h = (x * inv * g_ref[...]).astype(BF16)
    o_ref[...] = _dot(h, w_ref[...]).astype(o_ref.dtype)


def _norm_proj(x, g, w, out_dtype, tm=512):
    t, d = x.shape
    n = w.shape[1]
    return pl.pallas_call(
        _norm_proj_kernel,
        grid=(t // tm,),
        in_specs=[
            pl.BlockSpec((tm, d), lambda i: (i, 0)),
            pl.BlockSpec((1, d), lambda i: (0, 0)),
            pl.BlockSpec((d, n), lambda i: (0, 0)),
        ],
        out_specs=pl.BlockSpec((tm, n), lambda i: (i, 0)),
        out_shape=jax.ShapeDtypeStruct((t, n), out_dtype),
        compiler_params=_params(("parallel",), 48),
        name="norm_proj",
    )(x, g.reshape(1, d), w)


def _norm_proj_stack_kernel(tiles_a, xa_ref, xb_ref, g_ref, w_ref, o_ref, x_ref):
    x = jnp.where(pl.program_id(0) < tiles_a, xa_ref[...], xb_ref[...])
    x_ref[...] = x
    inv = lax.rsqrt(jnp.mean(x * x, axis=-1, keepdims=True) + EPS)
    h = (x * inv * g_ref[...]).astype(BF16)
    o_ref[...] = _dot(h, w_ref[...]).astype(o_ref.dtype)


def _norm_proj_stack(xa, xb, g, w, out_dtype, tm=512):
    ta, d = xa.shape
    tb = xb.shape[0]
    n = w.shape[1]
    tiles_a = ta // tm
    return pl.pallas_call(
        functools.partial(_norm_proj_stack_kernel, tiles_a),
        grid=((ta + tb) // tm,),
        in_specs=[
            pl.BlockSpec((tm, d), lambda i: (jnp.minimum(i, tiles_a - 1), 0)),
            pl.BlockSpec((tm, d), lambda i: (jnp.maximum(i - tiles_a, 0), 0)),
            pl.BlockSpec((1, d), lambda i: (0, 0)),
            pl.BlockSpec((d, n), lambda i: (0, 0)),
        ],
        out_specs=[pl.BlockSpec((tm, n), lambda i: (i, 0)), pl.BlockSpec((tm, d), lambda i: (i, 0))],
        out_shape=[jax.ShapeDtypeStruct((ta + tb, n), out_dtype),
                   jax.ShapeDtypeStruct((ta + tb, d), xa.dtype)],
        compiler_params=_params(("parallel",), 56),
        name="norm_proj_stack",
    )(xa, xb, g.reshape(1, d), w)


def _shortconv_kernel(p_ref, w_ref, b_ref, o_ref):
    x = p_ref[0]
    n = x.shape[0]
    row = lax.broadcasted_iota(jnp.int32, x.shape, 0)
    prev = jnp.where(row == 0, 0.0, pltpu.roll(x, 1, 0))
    nxt = jnp.where(row == n - 1, 0.0, pltpu.roll(x, n - 1, 0))
    o_ref[0] = prev * w_ref[0:1, :] + x * w_ref[1:2, :] + nxt * w_ref[2:3, :] + b_ref[...]


def _shortconv(p3, w, b, tc=256):
    bsz, n, c = p3.shape
    return pl.pallas_call(
        _shortconv_kernel,
        grid=(bsz, c // tc),
        in_specs=[
            pl.BlockSpec((1, n, tc), lambda i, j: (i, 0, j)),
            pl.BlockSpec((3, tc), lambda i, j: (0, j)),
            pl.BlockSpec((1, tc), lambda i, j: (0, j)),
        ],
        out_specs=pl.BlockSpec((1, n, tc), lambda i, j: (i, 0, j)),
        out_shape=jax.ShapeDtypeStruct((bsz, n, c), F32),
        compiler_params=_params(("parallel", "parallel"), 32),
        name="hy_shortconv",
    )(p3, w, b.reshape(1, c))


def _hy_filter_kernel(z_ref, w1_ref, b1_ref, w2_ref, b2_ref, fr_ref, w3f_ref, w3b_ref,
                      decf_ref, decb_ref, hs_ref, hd_ref):
    fr = fr_ref[...]
    h = jnp.sin(fr * (jnp.dot(z_ref[...], w1_ref[...], precision=HIGHEST,
                              preferred_element_type=F32) + b1_ref[...]))
    h = jnp.sin(fr * (jnp.dot(h, w2_ref[...], precision=HIGHEST,
                              preferred_element_type=F32) + b2_ref[...]))
    n = h.shape[0]
    shape = (n, w3f_ref.shape[1])
    row = lax.broadcasted_iota(jnp.int32, shape, 0)
    t = row.astype(F32) * (1.0 / (n - 1))
    hf = jnp.dot(h, w3f_ref[...], precision=HIGHEST, preferred_element_type=F32)
    hf = hf * jnp.exp(-t * jnp.abs(decf_ref[...]))
    hb = jnp.dot(h, w3b_ref[...], precision=HIGHEST, preferred_element_type=F32)
    hb = hb * jnp.exp(-t * jnp.abs(decb_ref[...]))
    hb = jnp.where(row == 0, 0.0, hb)
    norm = (jnp.sum(jnp.abs(hf), axis=0, keepdims=True)
            + jnp.sum(jnp.abs(hb), axis=0, keepdims=True))
    hs_ref[...] = (hf + hb) / norm
    hd_ref[...] = (hb - hf) / norm


def _hy_filter(z, w1, b1, w2, b2, fr, w3, decay, tc=256):
    n = z.shape[0]
    cols = HY_ORDER * HY_WIDTH
    w3f, w3b = w3[:, :cols], w3[:, cols:]
    dec = decay.reshape(2, cols)
    zp = jnp.pad(z, ((0, 0), (0, LANES - HY_EMB)))
    w1p = jnp.pad(w1, ((0, LANES - HY_EMB), (0, 0)))
    full = lambda shape: pl.BlockSpec(shape, lambda j: (0, 0))
    col = lambda rows: pl.BlockSpec((rows, tc), lambda j: (0, j))
    return pl.pallas_call(
        _hy_filter_kernel,
        grid=(cols // tc,),
        in_specs=[full((n, LANES)), full((LANES, HY_FFN)), full((1, HY_FFN)), full((HY_FFN, HY_FFN)),
                  full((1, HY_FFN)), full((1, HY_FFN)), col(HY_FFN), col(HY_FFN), col(1), col(1)],
        out_specs=[col(n), col(n)],
        out_shape=[jax.ShapeDtypeStruct((n, cols), F32)] * 2,
        compiler_params=_params(("parallel",), 48),
        name="hy_filter",
    )(zp, w1p, b1.reshape(1, -1), w2, b2.reshape(1, -1), fr.reshape(1, -1), w3f, w3b,
      dec[0:1], dec[1:2])


def _hy_kf_kernel(ff_ref, hs_ref, hd_ref, k_ref):
    hs = hs_ref[...]
    kre = _dot(ff_ref[0], hs.astype(BF16))
    kq = _dot(ff_ref[1], hd_ref[...].astype(BF16))
    row_t = lax.broadcasted_iota(jnp.int32, hs.shape, 0)
    sgn = (1 - 2 * (row_t & 1)).astype(F32)
    nyq = jnp.sum(hs * sgn, axis=0, keepdims=True)
    row = lax.broadcasted_iota(jnp.int32, kq.shape, 0)
    first = jnp.logical_and(row == 0, pl.program_id(0) == 0)
    k_ref[0] = kre
    k_ref[1] = jnp.where(first, nyq, kq)


def _hy_kf(ff, hs, hd, tf=512, tc=512):
    n, cols = hs.shape
    return pl.pallas_call(
        _hy_kf_kernel,
        grid=(n // tf, cols // tc),
        in_specs=[
            pl.BlockSpec((2, tf, n), lambda i, j: (0, i, 0)),
            pl.BlockSpec((n, tc), lambda i, j: (0, j)),
            pl.BlockSpec((n, tc), lambda i, j: (0, j)),
        ],
        out_specs=pl.BlockSpec((2, tf, tc), lambda i, j: (0, i, j)),
        out_shape=jax.ShapeDtypeStruct((2, n, cols), F32),
        compiler_params=_params(("parallel", "parallel"), 48),
        name="hy_filter_dft",
    )(ff, hs, hd)


HY_FREQ_TILE = 512


def _hy_fwd_kernel(ff_ref, u_ref, k_ref, y_ref):
    u = u_ref[0].astype(BF16)
    tf = HY_FREQ_TILE
    for i in range(ff_ref.shape[1] // tf):
        rows = slice(i * tf, (i + 1) * tf)
        a = _dot(ff_ref[0, rows, :], u)
        bq = _dot(ff_ref[1, rows, :], u)
        kre = k_ref[0, rows, :]
        kq = k_ref[1, rows, :]
        re = a * kre + bq * kq
        im = a * kq - bq * kre
        if i == 0:
            first = lax.broadcasted_iota(jnp.int32, a.shape, 0) == 0
            re = jnp.where(first, a * kre, re)
            im = jnp.where(first, bq * kq, im)
        y_ref[0, 0, rows, :] = re.astype(BF16)
        y_ref[0, 1, rows, :] = im.astype(BF16)


def _hy_fwd(ff, u3, ucol, k, kcol):
    bsz, n, _ = u3.shape
    c = HY_WIDTH
    return pl.pallas_call(
        _hy_fwd_kernel,
        grid=(bsz,),
        in_specs=[
            pl.BlockSpec((2, n, n), lambda b: (0, 0, 0), pipeline_mode=pl.Buffered(1)),
            pl.BlockSpec((1, n, c), lambda b: (b, 0, ucol)),
            pl.BlockSpec((2, n, c), lambda b: (0, 0, kcol), pipeline_mode=pl.Buffered(1)),
        ],
        out_specs=pl.BlockSpec((1, 2, n, c), lambda b: (b, 0, 0, 0)),
        out_shape=jax.ShapeDtypeStruct((bsz, 2, n, c), BF16),
        compiler_params=_params(("parallel",), 52),
        name="hy_dft_fwd",
    )(ff, u3, k)


def _hy_inv_kernel(fi_ref, y_ref, s_ref, g_ref, d_ref, o_ref):
    y = _dot(fi_ref[0], y_ref[0, 0]) + _dot(fi_ref[1], y_ref[0, 1])
    o_ref[0] = (g_ref[0] * (y + s_ref[0] * d_ref[...])).astype(o_ref.dtype)


def _hy_inv(fi, y, s3, scol, g3, gcol, d, out_dtype, tt=1024):
    bsz, _, n, c = y.shape
    return pl.pallas_call(
        _hy_inv_kernel,
        grid=(bsz, n // tt),
        in_specs=[
            pl.BlockSpec((2, tt, n), lambda b, i: (0, i, 0)),
            pl.BlockSpec((1, 2, n, c), lambda b, i: (b, 0, 0, 0)),
            pl.BlockSpec((1, tt, c), lambda b, i: (b, i, scol)),
            pl.BlockSpec((1, tt, c), lambda b, i: (b, i, gcol)),
            pl.BlockSpec((1, c), lambda b, i: (0, 0)),
        ],
        out_specs=pl.BlockSpec((1, tt, c), lambda b, i: (b, i, 0)),
        out_shape=jax.ShapeDtypeStruct((bsz, n, c), out_dtype),
        compiler_params=_params(("parallel", "parallel"), 48),
        name="hy_dft_inv",
    )(fi, y, s3, g3, d.reshape(1, c))


def _dft_matrices(n):
    r = lax.broadcasted_iota(jnp.int32, (n, n), 0)
    c = lax.broadcasted_iota(jnp.int32, (n, n), 1)
    ang = ((r * c) & (2 * n - 1)).astype(F32) * (math.pi / n)
    cos, sin = jnp.cos(ang), jnp.sin(ang)
    sgn_c = (1 - 2 * (c & 1)).astype(F32)
    sgn_r = (1 - 2 * (r & 1)).astype(F32)
    fwd = jnp.stack([cos, jnp.where(r == 0, sgn_c, sin)]).astype(BF16)
    inv0 = jnp.where(c == 0, 1.0, 2.0) * cos * (0.5 / n)
    inv1 = jnp.where(c == 0, sgn_r * (0.5 / n), sin * (-1.0 / n))
    inv = jnp.stack([inv0, inv1]).astype(BF16)
    return fwd, inv


def _hyena_embedding(n):
    t = jnp.linspace(0.0, 1.0, n, dtype=F32)[:, None]
    bands = (HY_EMB - 1) // 2
    f = jnp.linspace(1e-4, bands - 1, bands, dtype=F32)[None]
    w = 2.0 * math.pi * jnp.arange(n, dtype=F32)[:, None] / n
    return jnp.concatenate([t, jnp.cos(f * w), -jnp.sin(f * w)], -1)


def _hyena(p3, dft, emb, conv_w, conv_b, w1, b1, w2, b2, w3, freq, decay, skip):
    ff, fi = dft
    u = _shortconv(p3, conv_w, conv_b)
    hs, hd = _hy_filter(emb, w1, b1, w2, b2, freq, w3, decay)
    k = _hy_kf(ff, hs, hd)
    y = _hy_fwd(ff, u, 0, k, 0)
    z = _hy_inv(fi, y, u, 0, u, 1, skip[0], F32)
    y = _hy_fwd(ff, z, 0, k, 1)
    return _hy_inv(fi, y, z, 0, u, 2, skip[1], BF16)


def _t5_bucket(rel):
    nb = REL_BUCKETS // 2
    max_exact = nb // 2
    n = np.abs(rel)
    large = max_exact + (np.log(np.maximum(n, 1) / max_exact) / np.log(REL_MAX_DIST / max_exact)
                         * (nb - max_exact)).astype(np.int32)
    large = np.minimum(large, nb - 1)
    return (rel > 0).astype(np.int32) * nb + np.where(n < max_exact, n, large)


def _attn_kernel(q_ref, kp_ref, kc_ref, kn_ref, vp_ref, vc_ref, vn_ref, bias_ref, sink_ref, o_ref):
    n = pl.program_id(1)
    q = q_ref[0] * (HEAD_DIM ** -0.5)
    k = jnp.concatenate([kp_ref[0], kc_ref[0], kn_ref[0]], axis=0)
    vt = jnp.concatenate([vp_ref[0], vc_ref[0], vn_ref[0]], axis=1)
    edge_prev = jnp.where(n == 0, -1e30, 0.0).astype(F32)
    edge_next = jnp.where(n == pl.num_programs(1) - 1, -1e30, 0.0).astype(F32)
    ones = jnp.ones((HEAD_DIM, 3 * BLOCK), BF16)
    for kh in range(ATT_KV_HEADS):
        k_h = k[:, kh * HEAD_DIM:(kh + 1) * HEAD_DIM]
        vt_h = jnp.concatenate([vt[kh * HEAD_DIM:(kh + 1) * HEAD_DIM], ones], axis=0)
        q_h = jnp.concatenate([q[:, (kh * ATT_GROUP + g) * HEAD_DIM:(kh * ATT_GROUP + g + 1) * HEAD_DIM]
                               for g in range(ATT_GROUP)], axis=0)
        s = _dot_nt(k_h, q_h) + bias_ref[kh]
        s = jnp.concatenate([s[:BLOCK] + edge_prev, s[BLOCK:2 * BLOCK], s[2 * BLOCK:] + edge_next],
                            axis=0)
        sink = sink_ref[kh]
        m = jnp.maximum(jnp.max(s, axis=0, keepdims=True), sink)
        p = jnp.exp(s - m).astype(BF16)
        o = _dot(vt_h, p)
        o = o[:HEAD_DIM] / (o[HEAD_DIM:HEAD_DIM + 1] + jnp.exp(sink - m))
        for g in range(0, ATT_GROUP, 2):
            pair = jnp.concatenate([o[:, g * BLOCK:(g + 1) * BLOCK],
                                    o[:, (g + 1) * BLOCK:(g + 2) * BLOCK]], axis=0)
            c0 = (kh * ATT_GROUP + g) * HEAD_DIM
            o_ref[0, :, c0:c0 + 2 * HEAD_DIM] = pair.T.astype(o_ref.dtype)


def _attention_bias(rel_bias):
    rel = (np.arange(3 * BLOCK)[None, :] - BLOCK) - np.arange(BLOCK)[:, None]
    bucket = jnp.asarray(_t5_bucket(rel).reshape(-1), jnp.int32)
    onehot = (bucket[:, None] == jnp.arange(REL_BUCKETS, dtype=jnp.int32)[None]).astype(F32)
    bias = jnp.dot(onehot, rel_bias.astype(F32), precision=HIGHEST)
    bias = bias.reshape(BLOCK, 3 * BLOCK, ATT_KV_HEADS, ATT_GROUP)
    bias = jnp.where(jnp.asarray(np.abs(rel) <= WINDOW)[:, :, None, None], bias, -1e30)
    return jnp.transpose(bias, (2, 1, 3, 0)).reshape(ATT_KV_HEADS, 3 * BLOCK, ATT_GROUP * BLOCK)


def _attention(att3, sink, rel_bias):
    bsz, n, _ = att3.shape
    nb = n // BLOCK
    bias = _attention_bias(rel_bias)
    sink_row = jnp.repeat(sink.astype(F32).reshape(ATT_KV_HEADS, 1, ATT_GROUP), BLOCK, axis=2)
    vt = jnp.swapaxes(att3[:, :, ATT_WIDTH + KV_WIDTH:], 1, 2)
    kcol = ATT_WIDTH // KV_WIDTH
    prev = lambda i: jnp.maximum(i - 1, 0)
    nxt = lambda i: jnp.minimum(i + 1, nb - 1)
    cur = lambda i: i
    k_spec = lambda blk: pl.BlockSpec((1, BLOCK, KV_WIDTH), lambda b, i: (b, blk(i), kcol))
    v_spec = lambda blk: pl.BlockSpec((1, KV_WIDTH, BLOCK), lambda b, i: (b, 0, blk(i)))
    return pl.pallas_call(
        _attn_kernel,
        grid=(bsz, nb),
        in_specs=[
            pl.BlockSpec((1, BLOCK, ATT_WIDTH), lambda b, i: (b, i, 0)),
            k_spec(prev), k_spec(cur), k_spec(nxt), v_spec(prev), v_spec(cur), v_spec(nxt),
            pl.BlockSpec((ATT_KV_HEADS, 3 * BLOCK, ATT_GROUP * BLOCK), lambda b, i: (0, 0, 0)),
            pl.BlockSpec((ATT_KV_HEADS, 1, ATT_GROUP * BLOCK), lambda b, i: (0, 0, 0)),
        ],
        out_specs=pl.BlockSpec((1, BLOCK, ATT_WIDTH), lambda b, i: (b, i, 0)),
        out_shape=jax.ShapeDtypeStruct((bsz, n, ATT_WIDTH), BF16),
        compiler_params=_params(("parallel", "parallel"), 32),
        name="window_attention",
    )(att3, att3, att3, att3, vt, vt, vt, bias, sink_row)


def _rotary_tables(n):
    inv = ROPE_BASE ** -jnp.linspace(0.0, 1.0, HEAD_DIM // 2, dtype=F32)
    ang = jnp.arange(n, dtype=F32)[:, None] * inv[None]
    cos, sin = jnp.cos(ang), jnp.sin(ang)
    cos_t = jnp.tile(jnp.concatenate([cos, cos], -1), (1, RET_HEADS))
    sin_t = jnp.tile(jnp.concatenate([-sin, sin], -1), (1, RET_HEADS))
    return cos_t, sin_t


def _rotate(x, cos_t, sin_t):
    width = x.shape[-1]
    half = HEAD_DIM // 2
    lane = lax.broadcasted_iota(jnp.int32, x.shape, 1)
    partner = jnp.where((lane & (HEAD_DIM - 1)) < half,
                        pltpu.roll(x, width - half, 1), pltpu.roll(x, half, 1))
    return x * cos_t + partner * sin_t


RET_PAIR = 2 * HEAD_DIM
RET_PAIRS = RET_WIDTH // RET_PAIR


def _same_head(shape):
    r = lax.broadcasted_iota(jnp.int32, shape, 0) // HEAD_DIM
    c = lax.broadcasted_iota(jnp.int32, shape, 1) // HEAD_DIM
    return r == c


def _ret_state_kernel(kf_ref, vf_ref, kb_ref, vb_ref, cf_ref, sf_ref, cb_ref, sb_ref, lg_ref,
                      of_ref, ob_ref, stf_ref, stb_ref):
    @pl.when(pl.program_id(1) == 0)
    def _():
        stf_ref[...] = jnp.zeros_like(stf_ref)
        stb_ref[...] = jnp.zeros_like(stb_ref)

    of_ref[0, 0] = stf_ref[...].astype(of_ref.dtype)
    ob_ref[0, 0] = stb_ref[...].astype(ob_ref.dtype)
    c = RET_CHUNK
    j = lax.broadcasted_iota(jnp.int32, (c, RET_WIDTH), 0).astype(F32)
    lgf = lg_ref[0:1, :]
    lgb = lg_ref[1:2, :]
    scale = HEAD_DIM ** -0.5
    same = _same_head((RET_PAIR, RET_PAIR))

    def update(st_ref, k_ref, v_ref, cos_ref, sin_ref, zeta, lg):
        k = (_rotate(k_ref[0], cos_ref[...], sin_ref[...]) * scale * zeta).astype(BF16)
        v = v_ref[0].astype(BF16)
        grow = jnp.exp(c * lg)
        for p in range(RET_PAIRS):
            lanes = slice(p * RET_PAIR, (p + 1) * RET_PAIR)
            kv = lax.dot_general(k[:, lanes], v[:, lanes], (((0,), (0,)), ((), ())),
                                 preferred_element_type=F32)
            st_ref[p] = grow[:, lanes] * st_ref[p] + jnp.where(same, kv, 0.0)

    update(stf_ref, kf_ref, vf_ref, cf_ref, sf_ref, jnp.exp((c - 1.0 - j) * lgf), lgf)
    update(stb_ref, kb_ref, vb_ref, cb_ref, sb_ref, jnp.exp(j * lgb), lgb)


def _ret_states(ret3, cos_t, sin_t, lg_lane):
    bsz, n, _ = ret3.shape
    nc = n // RET_CHUNK
    c, w = RET_CHUNK, RET_WIDTH
    fwd = lambda b, i: i
    bwd = lambda b, i: nc - 1 - i
    col = lambda pos, cb: pl.BlockSpec((1, c, w), lambda b, i: (b, pos(b, i), cb))
    tab = lambda pos: pl.BlockSpec((c, w), lambda b, i: (pos(b, i), 0))
    st_shape = (RET_PAIRS, RET_PAIR, RET_PAIR)
    out = lambda pos: pl.BlockSpec((1, 1) + st_shape, lambda b, i: (b, pos(b, i), 0, 0, 0))
    return pl.pallas_call(
        _ret_state_kernel,
        grid=(bsz, nc),
        in_specs=[col(fwd, 1), col(fwd, 2), col(bwd, 1), col(bwd, 2),
                  tab(fwd), tab(fwd), tab(bwd), tab(bwd),
                  pl.BlockSpec((2, w), lambda b, i: (0, 0))],
        out_specs=[out(fwd), out(bwd)],
        out_shape=[jax.ShapeDtypeStruct((bsz, nc) + st_shape, BF16)] * 2,
        scratch_shapes=[pltpu.VMEM(st_shape, F32), pltpu.VMEM(st_shape, F32)],
        compiler_params=_params(("parallel", "arbitrary"), 32),
        name="retention_states",
    )(ret3, ret3, ret3, ret3, cos_t, sin_t, cos_t, sin_t, lg_lane)


def _ret_out_kernel(q_ref, k_ref, v_ref, g_ref, cos_ref, sin_ref, lg_ref, lgs_ref, stf_ref, stb_ref,
                    o_ref):
    c, w = RET_CHUNK, RET_WIDTH
    cos_t, sin_t = cos_ref[...], sin_ref[...]
    q = _rotate(q_ref[0], cos_t, sin_t)
    k = (_rotate(k_ref[0], cos_t, sin_t) * (HEAD_DIM ** -0.5)).astype(BF16)
    v = v_ref[0].astype(BF16)
    g = g_ref[0]
    pos = lax.broadcasted_iota(jnp.int32, (c, w), 0).astype(F32)
    q_f = (q * jnp.exp((pos + 1.0) * lg_ref[0:1, :])).astype(BF16)
    q_b = (q * jnp.exp((c - pos) * lg_ref[1:2, :])).astype(BF16)
    diff = (lax.broadcasted_iota(jnp.int32, (c, c), 0)
            - lax.broadcasted_iota(jnp.int32, (c, c), 1)).astype(F32)
    first = lax.broadcasted_iota(jnp.int32, (c, RET_PAIR), 1) < HEAD_DIM
    avg = jnp.where(_same_head((RET_PAIR, RET_PAIR)), 1.0 / HEAD_DIM, 0.0).astype(BF16)
    for p in range(RET_PAIRS):
        lanes = slice(p * RET_PAIR, (p + 1) * RET_PAIR)
        q_p = q[:, lanes]
        qs = jnp.concatenate([jnp.where(first, q_p, 0.0), jnp.where(first, 0.0, q_p)], axis=0)
        s = _dot_nt(qs.astype(BF16), k[:, lanes])
        decay = jnp.concatenate(
            [jnp.exp(jnp.where(diff >= 0, diff * lgs_ref[0, h], -diff * lgs_ref[1, h]))
             for h in (2 * p, 2 * p + 1)], axis=0)
        o2 = _dot((s * decay).astype(BF16), v[:, lanes])
        y = jnp.where(first, o2[:c], o2[c:])
        y = y + _dot(q_f[:, lanes], stf_ref[0, 0, p]) + _dot(q_b[:, lanes], stb_ref[0, 0, p])
        sq = y * y
        sq_hi = sq.astype(BF16)
        sq_lo = (sq - sq_hi.astype(F32)).astype(BF16)
        ms = _dot(sq_hi, avg) + _dot(sq_lo, avg)
        y = y * lax.rsqrt(ms + EPS)
        g_p = g[:, lanes]
        o_ref[0, :, lanes] = (g_p / (1.0 + jnp.exp(-g_p)) * y).astype(o_ref.dtype)


def _retention(ret3, decay):
    bsz, n, _ = ret3.shape
    nc = n // RET_CHUNK
    c, w = RET_CHUNK, RET_WIDTH
    lg = -jnp.exp(decay.astype(F32))
    lg_lane = jnp.repeat(lg, HEAD_DIM, axis=1)
    cos_t, sin_t = _rotary_tables(n)
    stf, stb = _ret_states(ret3, cos_t, sin_t, lg_lane)
    col = lambda cb: pl.BlockSpec((1, c, w), lambda b, i: (b, i, cb))
    tab = pl.BlockSpec((c, w), lambda b, i: (i, 0))
    st = pl.BlockSpec((1, 1, RET_PAIRS, RET_PAIR, RET_PAIR), lambda b, i: (b, i, 0, 0, 0))
    return pl.pallas_call(
        _ret_out_kernel,
        grid=(bsz, nc),
        in_specs=[col(0), col(1), col(2), col(3), tab, tab,
                  pl.BlockSpec((2, w), lambda b, i: (0, 0)),
                  pl.BlockSpec(memory_space=pltpu.SMEM), st, st],
        out_specs=pl.BlockSpec((1, c, w), lambda b, i: (b, i, 0)),
        out_shape=jax.ShapeDtypeStruct((bsz, n, w), BF16),
        compiler_params=_params(("parallel", "parallel"), 32),
        name="retention_out",
    )(ret3, ret3, ret3, ret3, cos_t, sin_t, lg_lane, lg, stf, stb)


ROUTE_I0, ROUTE_I1, ROUTE_W0, ROUTE_W1, ROUTE_R0, ROUTE_R1 = 0, 1, 2, 3, 4, 5


def _mix_router_kernel(x_ref, yh_ref, ya_ref, yr_ref, w_ref, g_ref, wr_ref, br_ref,
                       xo_ref, hp_ref, route_ref, counts_ref, carry_ref):
    @pl.when(pl.program_id(0) == 0)
    def _():
        carry_ref[...] = jnp.zeros_like(carry_ref)

    a, b = HY_WIDTH, HY_WIDTH + ATT_WIDTH
    x = x_ref[...] + _dot(yh_ref[...], w_ref[0:a, :]) + _dot(ya_ref[...], w_ref[a:b, :]) \
        + _dot(yr_ref[...], w_ref[b:, :])
    xo_ref[...] = x
    h = x * lax.rsqrt(jnp.mean(x * x, axis=-1, keepdims=True) + EPS) * g_ref[...]
    h_hi = h.astype(BF16)
    bits = lax.bitcast_convert_type(h_hi.astype(F32), jnp.uint32)
    half = bits.shape[1] // 2
    hp_ref[...] = bits[:, :half] | (bits[:, half:] >> 16)
    h_lo = (h - h_hi.astype(F32)).astype(BF16)
    wr = wr_ref[...]
    wr_hi = wr.astype(BF16)
    wr_lo = (wr - wr_hi.astype(F32)).astype(BF16)
    logits = _dot(h_hi, wr_hi) + (_dot(h_lo, wr_hi) + _dot(h_hi, wr_lo)) + br_ref[...]
    lane = lax.broadcasted_iota(jnp.int32, logits.shape, 1)
    big = jnp.int32(LANES)
    neg = -jnp.inf
    is_group = (lane >= N_EXPERTS) & (lane < N_EXPERTS + N_GROUPS)
    gl = jnp.where(is_group, logits, neg)
    gmax = jnp.max(gl, axis=-1, keepdims=True)
    g_w = 1.0 / jnp.sum(jnp.exp(gl - gmax), axis=-1, keepdims=True)
    g_sel = jnp.min(jnp.where(gl == gmax, lane, big), axis=-1, keepdims=True) - N_EXPERTS
    lo = g_sel * EXPERTS_PER_GROUP
    el = jnp.where((lane >= lo) & (lane < lo + EXPERTS_PER_GROUP), logits, neg)
    v0 = jnp.max(el, axis=-1, keepdims=True)
    i0 = jnp.min(jnp.where(el == v0, lane, big), axis=-1, keepdims=True)
    el = jnp.where(lane == i0, neg, el)
    v1 = jnp.max(el, axis=-1, keepdims=True)
    i1 = jnp.min(jnp.where(el == v1, lane, big), axis=-1, keepdims=True)
    e1 = jnp.exp(v1 - v0)
    w0 = g_w / (1.0 + e1)
    oh0 = jnp.where(lane == i0, 1.0, 0.0)
    oh1 = jnp.where(lane == i1, 1.0, 0.0)
    tm = logits.shape[0]
    earlier = jnp.where(lax.broadcasted_iota(jnp.int32, (tm, tm), 1)
                        < lax.broadcasted_iota(jnp.int32, (tm, tm), 0), 1.0, 0.0).astype(BF16)
    carry = carry_ref[...]
    tot0 = jnp.sum(oh0, axis=0, keepdims=True)
    before0 = _dot(earlier, oh0.astype(BF16)) + carry
    before1 = _dot(earlier, oh1.astype(BF16)) + (carry + tot0)
    r0 = jnp.sum(oh0 * before0, axis=-1, keepdims=True)
    r1 = jnp.sum(oh1 * before1, axis=-1, keepdims=True)
    carry = carry + tot0 + jnp.sum(oh1, axis=0, keepdims=True)
    carry_ref[...] = carry
    counts_ref[...] = jnp.broadcast_to(carry, counts_ref.shape)
    route_ref[...] = (jnp.where(lane == ROUTE_I0, i0.astype(F32), 0.0)
                      + jnp.where(lane == ROUTE_I1, i1.astype(F32), 0.0)
                      + jnp.where(lane == ROUTE_W0, w0, 0.0)
                      + jnp.where(lane == ROUTE_W1, w0 * e1, 0.0)
                      + jnp.where(lane == ROUTE_R0, r0, 0.0)
                      + jnp.where(lane == ROUTE_R1, r1, 0.0))


def _mix_router(x, yh, ya, yr, w_out, g, wr, br, tm=512):
    t, d = x.shape
    row = lambda width: pl.BlockSpec((tm, width), lambda i: (i, 0))
    full = lambda shape: pl.BlockSpec(shape, lambda i: (0, 0))
    return pl.pallas_call(
        _mix_router_kernel,
        grid=(t // tm,),
        in_specs=[row(d), row(HY_WIDTH), row(ATT_WIDTH), row(RET_WIDTH),
                  pl.BlockSpec((d, d), lambda i: (0, 0), pipeline_mode=pl.Buffered(1)),
                  full((1, d)), full((d, LANES)), full((1, LANES))],
        out_specs=[row(d), row(d // 2), row(LANES), full((SUBLANES, LANES))],
        out_shape=[jax.ShapeDtypeStruct((t, d), F32), jax.ShapeDtypeStruct((t, d // 2), jnp.uint32),
                   jax.ShapeDtypeStruct((t, LANES), F32), jax.ShapeDtypeStruct((SUBLANES, LANES), F32)],
        scratch_shapes=[pltpu.VMEM((1, LANES), F32)],
        compiler_params=_params(("arbitrary",), 48),
        name="mix_router",
    )(x, yh, ya, yr, w_out, g.reshape(1, d), wr, br)


MOE_TM = 512
MOE_TC = 256


def _dispatch_plan(route, counts, tm):
    experts = jnp.arange(N_EXPERTS, dtype=jnp.int32)
    ids = route[:, ROUTE_I0:ROUTE_I1 + 1].astype(jnp.int32)
    rank = route[:, ROUTE_R0:ROUTE_R1 + 1].astype(jnp.int32)
    n_tiles = 2 * route.shape[0] // tm + N_EXPERTS
    tiles = (counts[0, :N_EXPERTS].astype(jnp.int32) + tm - 1) // tm
    tile_end = jnp.cumsum(tiles)
    offs = (tile_end - tiles) * tm
    slot = jnp.sum(jnp.where(ids[..., None] == experts, offs, 0), axis=-1) + rank
    tile_expert = jnp.minimum(
        jnp.sum(jnp.arange(n_tiles, dtype=jnp.int32)[:, None] >= tile_end[None, :], axis=1),
        N_EXPERTS - 1).astype(jnp.int32)
    later = (experts[None, :] > experts[:, None]) & (tiles[None, :] > 0)
    following = jnp.min(jnp.where(later, experts[None, :], N_EXPERTS), axis=1)
    following = jnp.where(following == N_EXPERTS, -1, following).astype(jnp.int32)
    return slot, n_tiles, tile_expert, tile_end[-1:].astype(jnp.int32), following[tile_expert]


def _moe_dispatch_kernel(idx_ref, hp_ref, xs_in_ref, xs_ref, sem):
    del xs_in_ref
    tokens = hp_ref.shape[0]

    def issue(r, carry):
        for k in range(2):
            pltpu.make_async_copy(hp_ref.at[pl.ds(r, 1)],
                                  xs_ref.at[pl.ds(idx_ref[0, 0, 2 * r + k], 1)], sem).start()
        return carry

    lax.fori_loop(0, tokens, issue, 0, unroll=4)
    for k in range(2):
        pltpu.make_async_copy(hp_ref, xs_ref.at[pl.ds(0, tokens)], sem).wait()


def _moe_dispatch(hp, slot, n_slots, tc=MOE_TC):
    t, half = hp.shape
    steps = t // tc
    return pl.pallas_call(
        _moe_dispatch_kernel,
        grid=(steps,),
        in_specs=[pl.BlockSpec((1, 1, 2 * tc), lambda i: (i, 0, 0), memory_space=pltpu.SMEM),
                  pl.BlockSpec((tc, half), lambda i: (i, 0)),
                  pl.BlockSpec(memory_space=pl.ANY)],
        out_specs=pl.BlockSpec(memory_space=pl.ANY),
        out_shape=jax.ShapeDtypeStruct((n_slots, half), hp.dtype),
        input_output_aliases={2: 0},
        scratch_shapes=[pltpu.SemaphoreType.DMA(())],
        compiler_params=_params(("arbitrary",), 16),
        name="moe_dispatch",
    )(slot.reshape(steps, 1, 2 * tc), hp, jnp.zeros((n_slots, half), hp.dtype))


MOE_CAST_ROWS = 256


def _moe_gemm_kernel(layer, te_ref, nv_ref, nxt_ref, xs_ref, wg_ref, wu_ref, wd_ref, y_ref,
                     stage_g, stage_u, stage_d, wgb_ref, wub_ref, wdb_ref, sem_ref):
    i = pl.program_id(0)
    last = nv_ref[0] - 1
    tile = jnp.minimum(i, last)
    expert = te_ref[tile]
    fresh = jnp.logical_or(i == 0, expert != te_ref[jnp.maximum(tile - 1, 0)])
    pairs = ((wg_ref, stage_g, wgb_ref), (wu_ref, stage_u, wub_ref), (wd_ref, stage_d, wdb_ref))

    def copies(e):
        return [pltpu.make_async_copy(w_ref.at[layer, e], stage, sem_ref.at[n])
                for n, (w_ref, stage, _) in enumerate(pairs)]

    @pl.when(jnp.logical_and(i <= last, fresh))
    def _():
        @pl.when(i == 0)
        def _():
            for cp in copies(expert):
                cp.start()

        for cp in copies(expert):
            cp.wait()
        for _, stage, dst_ref in pairs:
            def cast(j, carry):
                rows = pl.ds(pl.multiple_of(j * MOE_CAST_ROWS, MOE_CAST_ROWS), MOE_CAST_ROWS)
                dst_ref[rows, :] = stage[rows, :].astype(BF16)
                return carry
            lax.fori_loop(0, dst_ref.shape[0] // MOE_CAST_ROWS, cast, 0)

        @pl.when(nxt_ref[tile] >= 0)
        def _():
            for cp in copies(nxt_ref[tile]):
                cp.start()

    @pl.when(i <= last)
    def _():
        xp = xs_ref[...]
        x_a = lax.bitcast_convert_type(xp & jnp.uint32(0xFFFF0000), F32).astype(BF16)
        x_b = lax.bitcast_convert_type(xp << 16, F32).astype(BF16)
        half = xp.shape[1]
        a = _dot(x_a, wgb_ref[:half, :]) + _dot(x_b, wgb_ref[half:, :])
        b = _dot(x_a, wub_ref[:half, :]) + _dot(x_b, wub_ref[half:, :])
        act = a / (1.0 + jnp.exp(-a)) * b
        y_ref[...] = _dot(act.astype(BF16), wdb_ref[...])

    @pl.when(i > last)
    def _():
        y_ref[...] = jnp.zeros_like(y_ref)


def _moe_gemm(xs, tile_expert, n_valid, next_expert, layer, wg, wu, wd, tm=MOE_TM):
    n_slots, half = xs.shape
    _, ne, d, de = wg.shape
    tile = lambda i, te, nv, nx: jnp.minimum(i, nv[0] - 1)
    hbm = pl.BlockSpec(memory_space=pl.ANY)
    return pl.pallas_call(
        functools.partial(_moe_gemm_kernel, layer),
        grid_spec=pltpu.PrefetchScalarGridSpec(
            num_scalar_prefetch=3,
            grid=(n_slots // tm,),
            in_specs=[pl.BlockSpec((tm, half), lambda i, te, nv, nx: (tile(i, te, nv, nx), 0)),
                      hbm, hbm, hbm],
            out_specs=pl.BlockSpec((tm, d), lambda i, te, nv, nx: (i, 0)),
            scratch_shapes=[pltpu.VMEM((d, de), F32), pltpu.VMEM((d, de), F32), pltpu.VMEM((de, d), F32),
                            pltpu.VMEM((d, de), BF16), pltpu.VMEM((d, de), BF16),
                            pltpu.VMEM((de, d), BF16), pltpu.SemaphoreType.DMA((3,))],
        ),
        out_shape=jax.ShapeDtypeStruct((n_slots, d), F32),
        compiler_params=_params(("arbitrary",), 58),
        name="moe_gemm",
    )(tile_expert, n_valid, next_expert, xs, wg, wu, wd)


def _moe_combine_kernel(final_norm, steps_a, idx_ref, idx_next_ref, x_ref, route_ref, gf_ref, y_ref,
                        *rest):
    o_refs, (buf_ref, sem_ref) = rest[:-2], rest[-2:]
    i = pl.program_id(0)
    n = pl.num_programs(0)
    tokens = x_ref.shape[0]

    def start_rows(idx, slot):
        def issue(r, carry):
            for k in range(2):
                pltpu.make_async_copy(y_ref.at[pl.ds(idx[0, 0, 2 * r + k], 1)],
                                      buf_ref.at[slot, k, pl.ds(r, 1)], sem_ref.at[slot]).start()
            return carry
        lax.fori_loop(0, tokens, issue, 0, unroll=4)

    @pl.when(i == 0)
    def _():
        start_rows(idx_ref, 0)

    @pl.when(i + 1 < n)
    def _():
        start_rows(idx_next_ref, (i + 1) % 2)

    slot = i % 2
    for k in range(2):
        pltpu.make_async_copy(y_ref.at[pl.ds(0, tokens)], buf_ref.at[slot, k], sem_ref.at[slot]).wait()
    route = route_ref[...]
    y = x_ref[...] + route[:, ROUTE_W0:ROUTE_W0 + 1] * buf_ref[slot, 0] \
        + route[:, ROUTE_W1:ROUTE_W1 + 1] * buf_ref[slot, 1]
    if final_norm:
        y = y * lax.rsqrt(jnp.mean(y * y, axis=-1, keepdims=True) + EPS) * gf_ref[...]
    if len(o_refs) == 1:
        o_refs[0][...] = y
    else:
        @pl.when(i < steps_a)
        def _():
            o_refs[0][...] = y

        @pl.when(i >= steps_a)
        def _():
            o_refs[1][...] = y


def _moe_combine(x, route, slot, y, gf, final_norm, tokens_a=None, tc=MOE_TC):
    t, d = x.shape
    steps = t // tc
    idx = slot.reshape(steps, 1, 2 * tc)
    row = lambda width: pl.BlockSpec((tc, width), lambda i: (i, 0))
    smem = lambda pos: pl.BlockSpec((1, 1, 2 * tc), lambda i: (pos(i), 0, 0), memory_space=pltpu.SMEM)
    if tokens_a is None:
        steps_a = steps
        out_specs = [row(d)]
        out_shape = [jax.ShapeDtypeStruct((t, d), F32)]
    else:
        steps_a = tokens_a // tc
        out_specs = [pl.BlockSpec((tc, d), lambda i: (jnp.minimum(i, steps_a - 1), 0)),
                     pl.BlockSpec((tc, d), lambda i: (jnp.maximum(i - steps_a, 0), 0))]
        out_shape = [jax.ShapeDtypeStruct((tokens_a, d), F32),
                     jax.ShapeDtypeStruct((t - tokens_a, d), F32)]
    return pl.pallas_call(
        functools.partial(_moe_combine_kernel, final_norm, steps_a),
        grid=(steps,),
        in_specs=[smem(lambda i: i), smem(lambda i: jnp.minimum(i + 1, steps - 1)),
                  row(d), row(LANES), pl.BlockSpec((1, d), lambda i: (0, 0)),
                  pl.BlockSpec(memory_space=pl.ANY)],
        out_specs=out_specs,
        out_shape=out_shape,
        scratch_shapes=[pltpu.VMEM((2, 2, tc, d), F32), pltpu.SemaphoreType.DMA((2,))],
        compiler_params=_params(("arbitrary",), 40),
        name="moe_combine",
    )(idx, idx, x, route, gf.reshape(1, d), y)


def _moe(hp, route, counts, x, layer, wg, wu, wd, gf, final_norm, tokens_a=None):
    slot, n_tiles, tile_expert, n_valid, next_expert = _dispatch_plan(route, counts, MOE_TM)
    xs = _moe_dispatch(hp, slot, n_tiles * MOE_TM)
    y = _moe_gemm(xs, tile_expert, n_valid, next_expert, layer, wg, wu, wd)
    return _moe_combine(x, route, slot, y, gf, final_norm, tokens_a)


def _router_weights(wg, bg, we, be):
    d = wg.shape[0]
    pad = LANES - N_EXPERTS - N_GROUPS
    wr = jnp.concatenate([we, wg, jnp.zeros((d, pad), F32)], axis=1).astype(F32)
    br = jnp.concatenate([be, bg, jnp.zeros((pad,), F32)]).astype(F32).reshape(1, LANES)
    return wr, br


def _layer(xs, bsz, p, l, consts, final_g, tokens_a):
    dft, emb = consts
    w_in = p['w_in'][l].astype(BF16)
    g1 = p['ln1_g'][l]
    if len(xs) == 2:
        hy, x = _norm_proj_stack(xs[0], xs[1], g1, w_in[:, :HY_IN], F32)
    else:
        x, = xs
        hy = _norm_proj(x, g1, w_in[:, :HY_IN], F32)
    t, d = x.shape
    n = t // bsz
    hy = hy.reshape(bsz, n, HY_IN)
    att = _norm_proj(x, g1, w_in[:, HY_IN:HY_IN + ATT_IN], BF16).reshape(bsz, n, ATT_IN)
    ret = _norm_proj(x, g1, w_in[:, HY_IN + ATT_IN:], F32).reshape(bsz, n, RET_IN)
    y_h = _hyena(hy, dft, emb, p['hy_conv_w'][l], p['hy_conv_b'][l], p['hy_w1'][l], p['hy_b1'][l],
                 p['hy_w2'][l], p['hy_b2'][l], p['hy_w3'][l], p['hy_freq'][l], p['hy_decay'][l],
                 p['hy_skip'][l])
    y_a = _attention(att, p['attn_sink'][l], p['rel_bias'])
    y_r = _retention(ret, p['ret_decay'][l])
    wr, br = _router_weights(p['router_group_w'][l], p['router_group_b'][l],
                             p['router_expert_w'][l], p['router_expert_b'][l])
    x, hp, route, counts = _mix_router(x, y_h.reshape(t, -1), y_a.reshape(t, -1), y_r.reshape(t, -1),
                                       p['w_out'][l].astype(BF16), p['ln2_g'][l], wr, br)
    last = l == DEPTH - 1
    return _moe(hp, route, counts, x, l, p['moe_w_gate'], p['moe_w_up'], p['moe_w_down'], final_g,
                final_norm=last, tokens_a=tokens_a if last else None)


def kernel(x_prompt, x_sample, ln1_g, w_in, hy_conv_w, hy_conv_b, hy_w1, hy_b1, hy_w2, hy_b2, hy_w3,
           hy_freq, hy_decay, hy_skip, attn_sink, rel_bias, ret_decay, w_out, ln2_g, router_group_w,
           router_group_b, router_expert_w, router_expert_b, moe_w_gate, moe_w_up, moe_w_down, lnf_g):
    p = dict(ln1_g=ln1_g, w_in=w_in, hy_conv_w=hy_conv_w, hy_conv_b=hy_conv_b, hy_w1=hy_w1,
             hy_b1=hy_b1, hy_w2=hy_w2, hy_b2=hy_b2, hy_w3=hy_w3, hy_freq=hy_freq, hy_decay=hy_decay,
             hy_skip=hy_skip, attn_sink=attn_sink, rel_bias=rel_bias, ret_decay=ret_decay, w_out=w_out,
             ln2_g=ln2_g, router_group_w=router_group_w, router_group_b=router_group_b,
             router_expert_w=router_expert_w, router_expert_b=router_expert_b, moe_w_gate=moe_w_gate,
             moe_w_up=moe_w_up, moe_w_down=moe_w_down)
    bp, n, d = x_prompt.shape
    bs = x_sample.shape[0]
    assert x_sample.shape[1:] == (n, d)
    bsz = bp + bs
    consts = (_dft_matrices(n), _hyena_embedding(n))
    xs = (x_prompt.reshape(bp * n, d), x_sample.reshape(bs * n, d))
    for l in range(DEPTH):
        xs = _layer(xs, bsz, p, l, consts, lnf_g, bp * n)
    y_prompt, y_sample = xs
    return (y_prompt.reshape(bp, n, d), y_sample.reshape(bs, n, d))
```

```python
import functools
import math

import numpy as np
import jax
import jax.numpy as jnp
from jax import lax
from jax.experimental import pallas as pl
from jax.experimental.pallas import tpu as pltpu

F32 = jnp.float32
BF16 = jnp.bfloat16
HIGHEST = lax.Precision.HIGHEST

D_MODEL = 2048
DEPTH = 2
HEAD_DIM = 64
EPS = 1e-6
HY_WIDTH = D_MODEL // 4
ATT_WIDTH = D_MODEL // 2
RET_WIDTH = D_MODEL // 4
HY_ORDER = 2
HY_EMB = 33
HY_FFN = 64
ATT_HEADS = ATT_WIDTH // HEAD_DIM
ATT_KV_HEADS = 4
ATT_GROUP = ATT_HEADS // ATT_KV_HEADS
KV_WIDTH = ATT_KV_HEADS * HEAD_DIM
WINDOW = 128
BLOCK = 128
REL_BUCKETS = 32
REL_MAX_DIST = 128
RET_HEADS = RET_WIDTH // HEAD_DIM
RET_CHUNK = 128
ROPE_BASE = 10000.0
N_GROUPS = 4
EXPERTS_PER_GROUP = 4
N_EXPERTS = N_GROUPS * EXPERTS_PER_GROUP
D_EXPERT = D_MODEL // 2
HY_IN = 3 * HY_WIDTH
ATT_IN = ATT_WIDTH + 2 * KV_WIDTH
RET_IN = 4 * RET_WIDTH

LANES = 128
SUBLANES = 8
MIB = 1024 * 1024


def _params(semantics, vmem_mib):
    return pltpu.CompilerParams(dimension_semantics=semantics, vmem_limit_bytes=vmem_mib * MIB)


def _dot(a, b):
    return jnp.dot(a, b, preferred_element_type=F32)


def _dot_nt(a, b):
    return lax.dot_general(a, b, (((1,), (1,)), ((), ())), preferred_element_type=F32)


def _norm_proj_kernel(x_ref, g_ref, w_ref, o_ref):
    x = x_ref[...]
    inv = lax.rsqrt(jnp.mean(x * x, axis=-1, keepdims=True) + EPS)
    h = (x * inv * g_ref[...]).astype(BF16)
    o_ref[...] = _dot(h, w_ref[...]).astype(o_ref.dtype)


def _norm_proj(x, g, w, out_dtype, tm=512):
    t, d = x.shape
    n = w.shape[1]
    return pl.pallas_call(
        _norm_proj_kernel,
        grid=(t // tm,),
        in_specs=[
            pl.BlockSpec((tm, d), lambda i: (i, 0)),
            pl.BlockSpec((1, d), lambda i: (0, 0)),
            pl.BlockSpec((d, n), lambda i: (0, 0)),
        ],
        out_specs=pl.BlockSpec((tm, n), lambda i: (i, 0)),
        out_shape=jax.ShapeDtypeStruct((t, n), out_dtype),
        compiler_params=_params(("parallel",), 48),
        name="norm_proj",
    )(x, g.reshape(1, d), w)


def _norm_proj_stack_kernel(tiles_a, xa_ref, xb_ref, g_ref, w_ref, o_ref, x_ref):
    x = jnp.where(pl.program_id(0) < tiles_a, xa_ref[...], xb_ref[...])
    x_ref[...] = x
    inv = lax.rsqrt(jnp.mean(x * x, axis=-1, keepdims=True) + EPS)
    h = (x * inv * g_ref[...]).astype(BF16)
    o_ref[...] = _dot(h, w_ref[...]).astype(o_ref.dtype)


def _norm_proj_stack(xa, xb, g, w, out_dtype, tm=512):
    ta, d = xa.shape
    tb = xb.shape[0]
    n = w.shape[1]
    tiles_a = ta // tm
    return pl.pallas_call(
        functools.partial(_norm_proj_stack_kernel, tiles_a),
        grid=((ta + tb) // tm,),
        in_specs=[
            pl.BlockSpec((tm, d), lambda i: (jnp.minimum(i, tiles_a - 1), 0)),
            pl.BlockSpec((tm, d), lambda i: (jnp.maximum(i - tiles_a, 0), 0)),
            pl.BlockSpec((1, d), lambda i: (0, 0)),
            pl.BlockSpec((d, n), lambda i: (0, 0)),
        ],
        out_specs=[pl.BlockSpec((tm, n), lambda i: (i, 0)), pl.BlockSpec((tm, d), lambda i: (i, 0))],
        out_shape=[jax.ShapeDtypeStruct((ta + tb, n), out_dtype),
                   jax.ShapeDtypeStruct((ta + tb, d), xa.dtype)],
        compiler_params=_params(("parallel",), 56),
        name="norm_proj_stack",
    )(xa, xb, g.reshape(1, d), w)


def _short_conv(x, prev_row, next_row, w_ref, b_ref):
    n = x.shape[0]
    row = lax.broadcasted_iota(jnp.int32, x.shape, 0)
    prev = jnp.where(row == 0, prev_row, pltpu.roll(x, 1, 0))
    nxt = jnp.where(row == n - 1, next_row, pltpu.roll(x, n - 1, 0))
    return prev * w_ref[0:1, :] + x * w_ref[1:2, :] + nxt * w_ref[2:3, :] + b_ref[...]


def _hy_filter_kernel(z_ref, w1_ref, b1_ref, w2_ref, b2_ref, fr_ref, w3f_ref, w3b_ref,
                      decf_ref, decb_ref, hs_ref, hd_ref):
    fr = fr_ref[...]
    h = jnp.sin(fr * (jnp.dot(z_ref[...], w1_ref[...], precision=HIGHEST,
                              preferred_element_type=F32) + b1_ref[...]))
    h = jnp.sin(fr * (jnp.dot(h, w2_ref[...], precision=HIGHEST,
                              preferred_element_type=F32) + b2_ref[...]))
    n = h.shape[0]
    shape = (n, w3f_ref.shape[1])
    row = lax.broadcasted_iota(jnp.int32, shape, 0)
    t = row.astype(F32) * (1.0 / (n - 1))
    hf = jnp.dot(h, w3f_ref[...], precision=HIGHEST, preferred_element_type=F32)
    hf = hf * jnp.exp(-t * jnp.abs(decf_ref[...]))
    hb = jnp.dot(h, w3b_ref[...], precision=HIGHEST, preferred_element_type=F32)
    hb = hb * jnp.exp(-t * jnp.abs(decb_ref[...]))
    hb = jnp.where(row == 0, 0.0, hb)
    norm = (jnp.sum(jnp.abs(hf), axis=0, keepdims=True)
            + jnp.sum(jnp.abs(hb), axis=0, keepdims=True))
    hs_ref[...] = (hf + hb) / norm
    hd_ref[...] = (hb - hf) / norm


def _hy_filter(z, w1, b1, w2, b2, fr, w3, decay, tc=256):
    n = z.shape[0]
    cols = HY_ORDER * HY_WIDTH
    w3f, w3b = w3[:, :cols], w3[:, cols:]
    dec = decay.reshape(2, cols)
    zp = jnp.pad(z, ((0, 0), (0, LANES - HY_EMB)))
    w1p = jnp.pad(w1, ((0, LANES - HY_EMB), (0, 0)))
    full = lambda shape: pl.BlockSpec(shape, lambda j: (0, 0))
    col = lambda rows: pl.BlockSpec((rows, tc), lambda j: (0, j))
    return pl.pallas_call(
        _hy_filter_kernel,
        grid=(cols // tc,),
        in_specs=[full((n, LANES)), full((LANES, HY_FFN)), full((1, HY_FFN)), full((HY_FFN, HY_FFN)),
                  full((1, HY_FFN)), full((1, HY_FFN)), col(HY_FFN), col(HY_FFN), col(1), col(1)],
        out_specs=[col(n), col(n)],
        out_shape=[jax.ShapeDtypeStruct((n, cols), F32)] * 2,
        compiler_params=_params(("parallel",), 48),
        name="hy_filter",
    )(zp, w1p, b1.reshape(1, -1), w2, b2.reshape(1, -1), fr.reshape(1, -1), w3f, w3b,
      dec[0:1], dec[1:2])


def _hy_kf_kernel(ff_ref, hs_ref, hd_ref, k_ref):
    hs = hs_ref[...]
    kre = _dot(ff_ref[0], hs.astype(BF16))
    kq = _dot(ff_ref[1], hd_ref[...].astype(BF16))
    row_t = lax.broadcasted_iota(jnp.int32, hs.shape, 0)
    sgn = (1 - 2 * (row_t & 1)).astype(F32)
    nyq = jnp.sum(hs * sgn, axis=0, keepdims=True)
    row = lax.broadcasted_iota(jnp.int32, kq.shape, 0)
    first = jnp.logical_and(row == 0, pl.program_id(0) == 0)
    k_ref[0] = kre
    k_ref[1] = jnp.where(first, nyq, kq)


def _hy_kf(ff, hs, hd, tf=512, tc=512):
    n, cols = hs.shape
    return pl.pallas_call(
        _hy_kf_kernel,
        grid=(n // tf, cols // tc),
        in_specs=[
            pl.BlockSpec((2, tf, n), lambda i, j: (0, i, 0)),
            pl.BlockSpec((n, tc), lambda i, j: (0, j)),
            pl.BlockSpec((n, tc), lambda i, j: (0, j)),
        ],
        out_specs=pl.BlockSpec((2, tf, tc), lambda i, j: (0, i, j)),
        out_shape=jax.ShapeDtypeStruct((2, n, cols), F32),
        compiler_params=_params(("parallel", "parallel"), 48),
        name="hy_filter_dft",
    )(ff, hs, hd)


HY_FREQ_TILE = 512


def _hy_fwd_kernel(conv, ff_ref, u_ref, k_ref, *rest):
    if conv:
        w_ref, b_ref, y_ref = rest
        u = _short_conv(u_ref[0], 0.0, 0.0, w_ref, b_ref).astype(BF16)
    else:
        y_ref, = rest
        u = u_ref[0].astype(BF16)
    tf = HY_FREQ_TILE
    for i in range(ff_ref.shape[1] // tf):
        rows = slice(i * tf, (i + 1) * tf)
        a = _dot(ff_ref[0, rows, :], u)
        bq = _dot(ff_ref[1, rows, :], u)
        kre = k_ref[0, rows, :]
        kq = k_ref[1, rows, :]
        re = a * kre + bq * kq
        im = a * kq - bq * kre
        if i == 0:
            first = lax.broadcasted_iota(jnp.int32, a.shape, 0) == 0
            re = jnp.where(first, a * kre, re)
            im = jnp.where(first, bq * kq, im)
        y_ref[0, 0, rows, :] = re.astype(BF16)
        y_ref[0, 1, rows, :] = im.astype(BF16)


def _hy_fwd(ff, u3, ucol, k, kcol, conv=None):
    bsz, n, _ = u3.shape
    c = HY_WIDTH
    in_specs = [
        pl.BlockSpec((2, n, n), lambda b: (0, 0, 0), pipeline_mode=pl.Buffered(1)),
        pl.BlockSpec((1, n, c), lambda b: (b, 0, ucol)),
        pl.BlockSpec((2, n, c), lambda b: (0, 0, kcol), pipeline_mode=pl.Buffered(1)),
    ]
    args = [ff, u3, k]
    if conv is not None:
        in_specs += [pl.BlockSpec((3, c), lambda b: (0, ucol)), pl.BlockSpec((1, c), lambda b: (0, ucol))]
        args += list(conv)
    return pl.pallas_call(
        functools.partial(_hy_fwd_kernel, conv is not None),
        grid=(bsz,),
        in_specs=in_specs,
        out_specs=pl.BlockSpec((1, 2, n, c), lambda b: (b, 0, 0, 0)),
        out_shape=jax.ShapeDtypeStruct((bsz, 2, n, c), BF16),
        compiler_params=_params(("parallel",), 52),
        name="hy_dft_fwd",
    )(*args)


def _hy_inv_kernel(s_conv, fi_ref, y_ref, d_ref, *rest):
    o_ref = rest[-1]
    i = pl.program_id(0)
    first, last = i == 0, i == pl.num_programs(0) - 1

    def conv_tile(x_ref, lo_ref, hi_ref, w_ref, b_ref):
        prev_row = jnp.where(first, 0.0, lo_ref[0, SUBLANES - 1:SUBLANES, :])
        next_row = jnp.where(last, 0.0, hi_ref[0, 0:1, :])
        return _short_conv(x_ref[0], prev_row, next_row, w_ref, b_ref)

    if s_conv:
        skip = conv_tile(*rest[0:5])
        gate = conv_tile(*rest[5:10])
    else:
        skip = rest[0][0]
        gate = conv_tile(*rest[1:6])
    y = _dot(fi_ref[0], y_ref[0, 0]) + _dot(fi_ref[1], y_ref[0, 1])
    o_ref[0] = (gate * (y + skip * d_ref[...])).astype(o_ref.dtype)


def _hy_inv(fi, y, d, raw3, conv, gcol, out_dtype, scol=None, s3=None, tt=1024):
    bsz, _, n, c = y.shape
    conv_w, conv_b = conv
    halo = tt // SUBLANES
    tile = lambda col: pl.BlockSpec((1, tt, c), lambda i, b: (b, i, col))
    lo = lambda col: pl.BlockSpec((1, SUBLANES, c), lambda i, b: (b, jnp.maximum(i * halo - 1, 0), col))
    hi = lambda col: pl.BlockSpec((1, SUBLANES, c),
                                  lambda i, b: (b, jnp.minimum((i + 1) * halo, n // SUBLANES - 1), col))
    wspec = lambda col: pl.BlockSpec((3, c), lambda i, b: (0, col))
    bspec = lambda col: pl.BlockSpec((1, c), lambda i, b: (0, col))
    conv_in = lambda col: ([tile(col), lo(col), hi(col), wspec(col), bspec(col)],
                           [raw3, raw3, raw3, conv_w, conv_b])
    in_specs = [pl.BlockSpec((2, tt, n), lambda i, b: (0, i, 0)),
                pl.BlockSpec((1, 2, n, c), lambda i, b: (b, 0, 0, 0)),
                pl.BlockSpec((1, c), lambda i, b: (0, 0))]
    args = [fi, y, d.reshape(1, c)]
    if s3 is None:
        specs, arrs = conv_in(scol)
    else:
        specs, arrs = [tile(0)], [s3]
    in_specs += specs
    args += arrs
    specs, arrs = conv_in(gcol)
    in_specs += specs
    args += arrs
    return pl.pallas_call(
        functools.partial(_hy_inv_kernel, s3 is None),
        grid=(n // tt, bsz),
        in_specs=in_specs,
        out_specs=pl.BlockSpec((1, tt, c), lambda i, b: (b, i, 0)),
        out_shape=jax.ShapeDtypeStruct((bsz, n, c), out_dtype),
        compiler_params=_params(("parallel", "parallel"), 48),
        name="hy_dft_inv",
    )(*args)


DFT_SPLIT = 64


def _dft_matrices(n):
    r = lax.broadcasted_iota(jnp.int32, (n, n), 0)
    c = lax.broadcasted_iota(jnp.int32, (n, n), 1)
    def table(rows, step):
        rr = lax.broadcasted_iota(jnp.int32, (rows, n), 0) * step
        cc = lax.broadcasted_iota(jnp.int32, (rows, n), 1)
        ang = ((rr * cc) & (2 * n - 1)).astype(F32) * (math.pi / n)
        return jnp.cos(ang), jnp.sin(ang)
    cos_hi, sin_hi = (t[:, None, :] for t in table(n // DFT_SPLIT, DFT_SPLIT))
    cos_lo, sin_lo = (t[None, :, :] for t in table(DFT_SPLIT, 1))
    cos = (cos_hi * cos_lo - sin_hi * sin_lo).reshape(n, n)
    sin = (sin_hi * cos_lo + cos_hi * sin_lo).reshape(n, n)
    sgn_c = (1 - 2 * (c & 1)).astype(F32)
    sgn_r = (1 - 2 * (r & 1)).astype(F32)
    fwd = jnp.stack([cos, jnp.where(r == 0, sgn_c, sin)]).astype(BF16)
    inv0 = jnp.where(c == 0, 1.0, 2.0) * cos * (0.5 / n)
    inv1 = jnp.where(c == 0, sgn_r * (0.5 / n), sin * (-1.0 / n))
    inv = jnp.stack([inv0, inv1]).astype(BF16)
    return fwd, inv


def _hyena_embedding(n):
    t = jnp.linspace(0.0, 1.0, n, dtype=F32)[:, None]
    bands = (HY_EMB - 1) // 2
    f = jnp.linspace(1e-4, bands - 1, bands, dtype=F32)[None]
    w = 2.0 * math.pi * jnp.arange(n, dtype=F32)[:, None] / n
    return jnp.concatenate([t, jnp.cos(f * w), -jnp.sin(f * w)], -1)


def _hyena(p3, dft, emb, conv_w, conv_b, w1, b1, w2, b2, w3, freq, decay, skip):
    ff, fi = dft
    conv = (conv_w, conv_b.reshape(1, -1))
    hs, hd = _hy_filter(emb, w1, b1, w2, b2, freq, w3, decay)
    k = _hy_kf(ff, hs, hd)
    y = _hy_fwd(ff, p3, 0, k, 0, conv)
    z = _hy_inv(fi, y, skip[0], p3, conv, 1, F32, scol=0)
    y = _hy_fwd(ff, z, 0, k, 1)
    return _hy_inv(fi, y, skip[1], p3, conv, 2, BF16, s3=z)


def _t5_bucket(rel):
    nb = REL_BUCKETS // 2
    max_exact = nb // 2
    n = np.abs(rel)
    large = max_exact + (np.log(np.maximum(n, 1) / max_exact) / np.log(REL_MAX_DIST / max_exact)
                         * (nb - max_exact)).astype(np.int32)
    large = np.minimum(large, nb - 1)
    return (rel > 0).astype(np.int32) * nb + np.where(n < max_exact, n, large)


def _attn_kernel(q_ref, kp_ref, kc_ref, kn_ref, vp_ref, vc_ref, vn_ref, bias_ref, sink_ref, o_ref):
    n = pl.program_id(1)
    q = q_ref[0] * (HEAD_DIM ** -0.5)
    k = jnp.concatenate([kp_ref[0], kc_ref[0], kn_ref[0]], axis=0)
    vt = jnp.concatenate([vp_ref[0], vc_ref[0], vn_ref[0]], axis=1)
    edge_prev = jnp.where(n == 0, -1e30, 0.0).astype(F32)
    edge_next = jnp.where(n == pl.num_programs(1) - 1, -1e30, 0.0).astype(F32)
    ones = jnp.ones((HEAD_DIM, 3 * BLOCK), BF16)
    for kh in range(ATT_KV_HEADS):
        k_h = k[:, kh * HEAD_DIM:(kh + 1) * HEAD_DIM]
        vt_h = jnp.concatenate([vt[kh * HEAD_DIM:(kh + 1) * HEAD_DIM], ones], axis=0)
        q_h = jnp.concatenate([q[:, (kh * ATT_GROUP + g) * HEAD_DIM:(kh * ATT_GROUP + g + 1) * HEAD_DIM]
                               for g in range(ATT_GROUP)], axis=0)
        s = _dot_nt(k_h, q_h) + bias_ref[kh]
        s = jnp.concatenate([s[:BLOCK] + edge_prev, s[BLOCK:2 * BLOCK], s[2 * BLOCK:] + edge_next],
                            axis=0)
        sink = sink_ref[kh]
        m = jnp.maximum(jnp.max(s, axis=0, keepdims=True), sink)
        p = jnp.exp(s - m).astype(BF16)
        o = _dot(vt_h, p)
        o = o[:HEAD_DIM] / (o[HEAD_DIM:HEAD_DIM + 1] + jnp.exp(sink - m))
        for g in range(0, ATT_GROUP, 2):
            pair = jnp.concatenate([o[:, g * BLOCK:(g + 1) * BLOCK],
                                    o[:, (g + 1) * BLOCK:(g + 2) * BLOCK]], axis=0)
            c0 = (kh * ATT_GROUP + g) * HEAD_DIM
            o_ref[0, :, c0:c0 + 2 * HEAD_DIM] = pair.T.astype(o_ref.dtype)


def _attention_bias(rel_bias):
    rel = (np.arange(3 * BLOCK)[None, :] - BLOCK) - np.arange(BLOCK)[:, None]
    bucket = jnp.asarray(_t5_bucket(rel).reshape(-1), jnp.int32)
    onehot = (bucket[:, None] == jnp.arange(REL_BUCKETS, dtype=jnp.int32)[None]).astype(F32)
    bias = jnp.dot(onehot, rel_bias.astype(F32), precision=HIGHEST)
    bias = bias.reshape(BLOCK, 3 * BLOCK, ATT_KV_HEADS, ATT_GROUP)
    bias = jnp.where(jnp.asarray(np.abs(rel) <= WINDOW)[:, :, None, None], bias, -1e30)
    return jnp.transpose(bias, (2, 1, 3, 0)).reshape(ATT_KV_HEADS, 3 * BLOCK, ATT_GROUP * BLOCK)


def _attention(att3, sink, rel_bias):
    bsz, n, _ = att3.shape
    nb = n // BLOCK
    bias = _attention_bias(rel_bias)
    sink_row = jnp.repeat(sink.astype(F32).reshape(ATT_KV_HEADS, 1, ATT_GROUP), BLOCK, axis=2)
    vt = jnp.swapaxes(att3[:, :, ATT_WIDTH + KV_WIDTH:], 1, 2)
    kcol = ATT_WIDTH // KV_WIDTH
    prev = lambda i: jnp.maximum(i - 1, 0)
    nxt = lambda i: jnp.minimum(i + 1, nb - 1)
    cur = lambda i: i
    k_spec = lambda blk: pl.BlockSpec((1, BLOCK, KV_WIDTH), lambda b, i: (b, blk(i), kcol))
    v_spec = lambda blk: pl.BlockSpec((1, KV_WIDTH, BLOCK), lambda b, i: (b, 0, blk(i)))
    return pl.pallas_call(
        _attn_kernel,
        grid=(bsz, nb),
        in_specs=[
            pl.BlockSpec((1, BLOCK, ATT_WIDTH), lambda b, i: (b, i, 0)),
            k_spec(prev), k_spec(cur), k_spec(nxt), v_spec(prev), v_spec(cur), v_spec(nxt),
            pl.BlockSpec((ATT_KV_HEADS, 3 * BLOCK, ATT_GROUP * BLOCK), lambda b, i: (0, 0, 0)),
            pl.BlockSpec((ATT_KV_HEADS, 1, ATT_GROUP * BLOCK), lambda b, i: (0, 0, 0)),
        ],
        out_specs=pl.BlockSpec((1, BLOCK, ATT_WIDTH), lambda b, i: (b, i, 0)),
        out_shape=jax.ShapeDtypeStruct((bsz, n, ATT_WIDTH), BF16),
        compiler_params=_params(("parallel", "parallel"), 32),
        name="window_attention",
    )(att3, att3, att3, att3, vt, vt, vt, bias, sink_row)


def _rotary_tables(n):
    inv = ROPE_BASE ** -jnp.linspace(0.0, 1.0, HEAD_DIM // 2, dtype=F32)
    ang = jnp.arange(n, dtype=F32)[:, None] * inv[None]
    cos, sin = jnp.cos(ang), jnp.sin(ang)
    cos_t = jnp.tile(jnp.concatenate([cos, cos], -1), (1, RET_HEADS))
    sin_t = jnp.tile(jnp.concatenate([-sin, sin], -1), (1, RET_HEADS))
    return cos_t, sin_t


def _rotate(x, cos_t, sin_t):
    width = x.shape[-1]
    half = HEAD_DIM // 2
    lane = lax.broadcasted_iota(jnp.int32, x.shape, 1)
    partner = jnp.where((lane & (HEAD_DIM - 1)) < half,
                        pltpu.roll(x, width - half, 1), pltpu.roll(x, half, 1))
    return x * cos_t + partner * sin_t


RET_PAIR = 2 * HEAD_DIM
RET_PAIRS = RET_WIDTH // RET_PAIR


def _same_head(shape):
    r = lax.broadcasted_iota(jnp.int32, shape, 0) // HEAD_DIM
    c = lax.broadcasted_iota(jnp.int32, shape, 1) // HEAD_DIM
    return r == c


def _ret_state_kernel(kf_ref, vf_ref, kb_ref, vb_ref, cf_ref, sf_ref, cb_ref, sb_ref, lg_ref,
                      of_ref, ob_ref, stf_ref, stb_ref):
    @pl.when(pl.program_id(1) == 0)
    def _():
        stf_ref[...] = jnp.zeros_like(stf_ref)
        stb_ref[...] = jnp.zeros_like(stb_ref)

    of_ref[0, 0] = stf_ref[...].astype(of_ref.dtype)
    ob_ref[0, 0] = stb_ref[...].astype(ob_ref.dtype)
    c = RET_CHUNK
    j = lax.broadcasted_iota(jnp.int32, (c, RET_WIDTH), 0).astype(F32)
    lgf = lg_ref[0:1, :]
    lgb = lg_ref[1:2, :]
    scale = HEAD_DIM ** -0.5
    same = _same_head((RET_PAIR, RET_PAIR))

    def update(st_ref, k_ref, v_ref, cos_ref, sin_ref, zeta, lg):
        k = (_rotate(k_ref[0], cos_ref[...], sin_ref[...]) * scale * zeta).astype(BF16)
        v = v_ref[0].astype(BF16)
        grow = jnp.exp(c * lg)
        for p in range(RET_PAIRS):
            lanes = slice(p * RET_PAIR, (p + 1) * RET_PAIR)
            kv = lax.dot_general(k[:, lanes], v[:, lanes], (((0,), (0,)), ((), ())),
                                 preferred_element_type=F32)
            st_ref[p] = grow[:, lanes] * st_ref[p] + jnp.where(same, kv, 0.0)

    update(stf_ref, kf_ref, vf_ref, cf_ref, sf_ref, jnp.exp((c - 1.0 - j) * lgf), lgf)
    update(stb_ref, kb_ref, vb_ref, cb_ref, sb_ref, jnp.exp(j * lgb), lgb)


def _ret_states(ret3, cos_t, sin_t, lg_lane):
    bsz, n, _ = ret3.shape
    nc = n // RET_CHUNK
    c, w = RET_CHUNK, RET_WIDTH
    fwd = lambda b, i: i
    bwd = lambda b, i: nc - 1 - i
    col = lambda pos, cb: pl.BlockSpec((1, c, w), lambda b, i: (b, pos(b, i), cb))
    tab = lambda pos: pl.BlockSpec((c, w), lambda b, i: (pos(b, i), 0))
    st_shape = (RET_PAIRS, RET_PAIR, RET_PAIR)
    out = lambda pos: pl.BlockSpec((1, 1) + st_shape, lambda b, i: (b, pos(b, i), 0, 0, 0))
    return pl.pallas_call(
        _ret_state_kernel,
        grid=(bsz, nc),
        in_specs=[col(fwd, 1), col(fwd, 2), col(bwd, 1), col(bwd, 2),
                  tab(fwd), tab(fwd), tab(bwd), tab(bwd),
                  pl.BlockSpec((2, w), lambda b, i: (0, 0))],
        out_specs=[out(fwd), out(bwd)],
        out_shape=[jax.ShapeDtypeStruct((bsz, nc) + st_shape, BF16)] * 2,
        scratch_shapes=[pltpu.VMEM(st_shape, F32), pltpu.VMEM(st_shape, F32)],
        compiler_params=_params(("parallel", "arbitrary"), 32),
        name="retention_states",
    )(ret3, ret3, ret3, ret3, cos_t, sin_t, cos_t, sin_t, lg_lane)


def _ret_out_kernel(q_ref, k_ref, v_ref, g_ref, cos_ref, sin_ref, lg_ref, lgs_ref, stf_ref, stb_ref,
                    o_ref):
    c, w = RET_CHUNK, RET_WIDTH
    cos_t, sin_t = cos_ref[...], sin_ref[...]
    q = _rotate(q_ref[0], cos_t, sin_t)
    k = (_rotate(k_ref[0], cos_t, sin_t) * (HEAD_DIM ** -0.5)).astype(BF16)
    v = v_ref[0].astype(BF16)
    g = g_ref[0]
    pos = lax.broadcasted_iota(jnp.int32, (c, w), 0).astype(F32)
    q_f = (q * jnp.exp((pos + 1.0) * lg_ref[0:1, :])).astype(BF16)
    q_b = (q * jnp.exp((c - pos) * lg_ref[1:2, :])).astype(BF16)
    diff = (lax.broadcasted_iota(jnp.int32, (c, c), 0)
            - lax.broadcasted_iota(jnp.int32, (c, c), 1)).astype(F32)
    first = lax.broadcasted_iota(jnp.int32, (c, RET_PAIR), 1) < HEAD_DIM
    avg = jnp.where(_same_head((RET_PAIR, RET_PAIR)), 1.0 / HEAD_DIM, 0.0).astype(BF16)
    for p in range(RET_PAIRS):
        lanes = slice(p * RET_PAIR, (p + 1) * RET_PAIR)
        q_p = q[:, lanes]
        qs = jnp.concatenate([jnp.where(first, q_p, 0.0), jnp.where(first, 0.0, q_p)], axis=0)
        s = _dot_nt(qs.astype(BF16), k[:, lanes])
        decay = jnp.concatenate(
            [jnp.exp(jnp.where(diff >= 0, diff * lgs_ref[0, h], -diff * lgs_ref[1, h]))
             for h in (2 * p, 2 * p + 1)], axis=0)
        o2 = _dot((s * decay).astype(BF16), v[:, lanes])
        y = jnp.where(first, o2[:c], o2[c:])
        y = y + _dot(q_f[:, lanes], stf_ref[0, 0, p]) + _dot(q_b[:, lanes], stb_ref[0, 0, p])
        sq = y * y
        sq_hi = sq.astype(BF16)
        sq_lo = (sq - sq_hi.astype(F32)).astype(BF16)
        ms = _dot(sq_hi, avg) + _dot(sq_lo, avg)
        y = y * lax.rsqrt(ms + EPS)
        g_p = g[:, lanes]
        o_ref[0, :, lanes] = (g_p / (1.0 + jnp.exp(-g_p)) * y).astype(o_ref.dtype)


def _retention(ret3, decay):
    bsz, n, _ = ret3.shape
    nc = n // RET_CHUNK
    c, w = RET_CHUNK, RET_WIDTH
    lg = -jnp.exp(decay.astype(F32))
    lg_lane = jnp.repeat(lg, HEAD_DIM, axis=1)
    cos_t, sin_t = _rotary_tables(n)
    stf, stb = _ret_states(ret3, cos_t, sin_t, lg_lane)
    col = lambda cb: pl.BlockSpec((1, c, w), lambda b, i: (b, i, cb))
    tab = pl.BlockSpec((c, w), lambda b, i: (i, 0))
    st = pl.BlockSpec((1, 1, RET_PAIRS, RET_PAIR, RET_PAIR), lambda b, i: (b, i, 0, 0, 0))
    return pl.pallas_call(
        _ret_out_kernel,
        grid=(bsz, nc),
        in_specs=[col(0), col(1), col(2), col(3), tab, tab,
                  pl.BlockSpec((2, w), lambda b, i: (0, 0)),
                  pl.BlockSpec(memory_space=pltpu.SMEM), st, st],
        out_specs=pl.BlockSpec((1, c, w), lambda b, i: (b, i, 0)),
        out_shape=jax.ShapeDtypeStruct((bsz, n, w), BF16),
        compiler_params=_params(("parallel", "parallel"), 32),
        name="retention_out",
    )(ret3, ret3, ret3, ret3, cos_t, sin_t, lg_lane, lg, stf, stb)


ROUTE_I0, ROUTE_I1, ROUTE_W0, ROUTE_W1, ROUTE_R0, ROUTE_R1 = 0, 1, 2, 3, 4, 5


def _mix_router_kernel(x_ref, yh_ref, ya_ref, yr_ref, w_ref, g_ref, wr_ref, br_ref,
                       xo_ref, hp_ref, route_ref, counts_ref, carry_ref):
    @pl.when(pl.program_id(0) == 0)
    def _():
        carry_ref[...] = jnp.zeros_like(carry_ref)

    a, b = HY_WIDTH, HY_WIDTH + ATT_WIDTH
    x = x_ref[...] + _dot(yh_ref[...], w_ref[0:a, :]) + _dot(ya_ref[...], w_ref[a:b, :]) \
        + _dot(yr_ref[...], w_ref[b:, :])
    xo_ref[...] = x
    h = x * lax.rsqrt(jnp.mean(x * x, axis=-1, keepdims=True) + EPS) * g_ref[...]
    h_hi = h.astype(BF16)
    bits = lax.bitcast_convert_type(h_hi.astype(F32), jnp.uint32)
    half = bits.shape[1] // 2
    hp_ref[...] = bits[:, :half] | (bits[:, half:] >> 16)
    h_lo = (h - h_hi.astype(F32)).astype(BF16)
    wr = wr_ref[...]
    wr_hi = wr.astype(BF16)
    wr_lo = (wr - wr_hi.astype(F32)).astype(BF16)
    logits = _dot(h_hi, wr_hi) + (_dot(h_lo, wr_hi) + _dot(h_hi, wr_lo)) + br_ref[...]
    lane = lax.broadcasted_iota(jnp.int32, logits.shape, 1)
    big = jnp.int32(LANES)
    neg = -jnp.inf
    is_group = (lane >= N_EXPERTS) & (lane < N_EXPERTS + N_GROUPS)
    gl = jnp.where(is_group, logits, neg)
    gmax = jnp.max(gl, axis=-1, keepdims=True)
    g_w = 1.0 / jnp.sum(jnp.exp(gl - gmax), axis=-1, keepdims=True)
    g_sel = jnp.min(jnp.where(gl == gmax, lane, big), axis=-1, keepdims=True) - N_EXPERTS
    lo = g_sel * EXPERTS_PER_GROUP
    el = jnp.where((lane >= lo) & (lane < lo + EXPERTS_PER_GROUP), logits, neg)
    v0 = jnp.max(el, axis=-1, keepdims=True)
    i0 = jnp.min(jnp.where(el == v0, lane, big), axis=-1, keepdims=True)
    el = jnp.where(lane == i0, neg, el)
    v1 = jnp.max(el, axis=-1, keepdims=True)
    i1 = jnp.min(jnp.where(el == v1, lane, big), axis=-1, keepdims=True)
    e1 = jnp.exp(v1 - v0)
    w0 = g_w / (1.0 + e1)
    oh0 = jnp.where(lane == i0, 1.0, 0.0)
    oh1 = jnp.where(lane == i1, 1.0, 0.0)
    tm = logits.shape[0]
    earlier = jnp.where(lax.broadcasted_iota(jnp.int32, (tm, tm), 1)
                        < lax.broadcasted_iota(jnp.int32, (tm, tm), 0), 1.0, 0.0).astype(BF16)
    carry = carry_ref[...]
    tot0 = jnp.sum(oh0, axis=0, keepdims=True)
    before0 = _dot(earlier, oh0.astype(BF16)) + carry
    before1 = _dot(earlier, oh1.astype(BF16)) + (carry + tot0)
    r0 = jnp.sum(oh0 * before0, axis=-1, keepdims=True)
    r1 = jnp.sum(oh1 * before1, axis=-1, keepdims=True)
    carry = carry + tot0 + jnp.sum(oh1, axis=0, keepdims=True)
    carry_ref[...] = carry
    counts_ref[...] = jnp.broadcast_to(carry, counts_ref.shape)
    route_ref[...] = (jnp.where(lane == ROUTE_I0, i0.astype(F32), 0.0)
                      + jnp.where(lane == ROUTE_I1, i1.astype(F32), 0.0)
                      + jnp.where(lane == ROUTE_W0, w0, 0.0)
                      + jnp.where(lane == ROUTE_W1, w0 * e1, 0.0)
                      + jnp.where(lane == ROUTE_R0, r0, 0.0)
                      + jnp.where(lane == ROUTE_R1, r1, 0.0))


def _mix_router(x, yh, ya, yr, w_out, g, wr, br, tm=512):
    t, d = x.shape
    row = lambda width: pl.BlockSpec((tm, width), lambda i: (i, 0))
    full = lambda shape: pl.BlockSpec(shape, lambda i: (0, 0))
    return pl.pallas_call(
        _mix_router_kernel,
        grid=(t // tm,),
        in_specs=[row(d), row(HY_WIDTH), row(ATT_WIDTH), row(RET_WIDTH),
                  pl.BlockSpec((d, d), lambda i: (0, 0), pipeline_mode=pl.Buffered(1)),
                  full((1, d)), full((d, LANES)), full((1, LANES))],
        out_specs=[row(d), row(d // 2), row(LANES), full((SUBLANES, LANES))],
        out_shape=[jax.ShapeDtypeStruct((t, d), F32), jax.ShapeDtypeStruct((t, d // 2), jnp.uint32),
                   jax.ShapeDtypeStruct((t, LANES), F32), jax.ShapeDtypeStruct((SUBLANES, LANES), F32)],
        scratch_shapes=[pltpu.VMEM((1, LANES), F32)],
        compiler_params=_params(("arbitrary",), 48),
        name="mix_router",
    )(x, yh, ya, yr, w_out, g.reshape(1, d), wr, br)


MOE_TM = 512
MOE_TC = 256


def _dispatch_plan(route, counts, tm):
    experts = jnp.arange(N_EXPERTS, dtype=jnp.int32)
    ids = route[:, ROUTE_I0:ROUTE_I1 + 1].astype(jnp.int32)
    rank = route[:, ROUTE_R0:ROUTE_R1 + 1].astype(jnp.int32)
    n_tiles = 2 * route.shape[0] // tm + N_EXPERTS
    count = counts[0, :N_EXPERTS].astype(jnp.int32)
    tiles = (count + tm - 1) // tm
    tile_end = jnp.cumsum(tiles)
    offs = (tile_end - tiles) * tm
    slot = jnp.sum(jnp.where(ids[..., None] == experts, offs, 0), axis=-1) + rank
    tile_expert = jnp.minimum(
        jnp.sum(jnp.arange(n_tiles, dtype=jnp.int32)[:, None] >= tile_end[None, :], axis=1),
        N_EXPERTS - 1).astype(jnp.int32)
    later = (experts[None, :] > experts[:, None]) & (tiles[None, :] > 0)
    following = jnp.min(jnp.where(later, experts[None, :], N_EXPERTS), axis=1)
    following = jnp.where(following == N_EXPERTS, -1, following).astype(jnp.int32)
    pad = (offs + count, tiles * tm - count)
    return slot, n_tiles, tile_expert, tile_end[-1:].astype(jnp.int32), following[tile_expert], pad


def _moe_dispatch_kernel(tm, pad_start_ref, pad_rows_ref, nv_ref, idx_ref, hp_ref, xs_ref, zeros_ref,
                         sem, zsem):
    tokens = hp_ref.shape[0]
    n_tiles = xs_ref.shape[0] // tm

    @pl.when(pl.program_id(0) == 0)
    def _():
        zeros_ref[...] = jnp.zeros_like(zeros_ref)
        fills = []
        for e in range(N_EXPERTS):
            rows = pad_rows_ref[e]
            end = pad_start_ref[e] + rows
            size = tm // 2
            while size >= SUBLANES:
                above = rows & ~(2 * size - 1)
                fills.append(((rows & size) != 0, pl.multiple_of(end - above - size, SUBLANES), size))
                size //= 2
            for j in range(SUBLANES - 1):
                fills.append(((rows & (SUBLANES - 1)) > j, pad_start_ref[e] + j, 1))
        for j in range(N_EXPERTS):
            fills.append((nv_ref[0] + j < n_tiles, pl.multiple_of((nv_ref[0] + j) * tm, tm), tm))
        copies = [(cond, pltpu.make_async_copy(zeros_ref.at[pl.ds(0, size)],
                                               xs_ref.at[pl.ds(start, size)], zsem))
                  for cond, start, size in fills]
        for cond, cp in copies:
            pl.when(cond)(cp.start)
        for cond, cp in copies:
            pl.when(cond)(cp.wait)

    def issue(r, carry):
        for k in range(2):
            pltpu.make_async_copy(hp_ref.at[pl.ds(r, 1)],
                                  xs_ref.at[pl.ds(idx_ref[0, 0, 2 * r + k], 1)], sem).start()
        return carry

    lax.fori_loop(0, tokens, issue, 0, unroll=4)
    for k in range(2):
        pltpu.make_async_copy(hp_ref, xs_ref.at[pl.ds(0, tokens)], sem).wait()


def _moe_dispatch(hp, slot, pad_start, pad_rows, n_valid, n_tiles, tm, tc=MOE_TC):
    t, half = hp.shape
    steps = t // tc
    return pl.pallas_call(
        functools.partial(_moe_dispatch_kernel, tm),
        grid_spec=pltpu.PrefetchScalarGridSpec(
            num_scalar_prefetch=3,
            grid=(steps,),
            in_specs=[pl.BlockSpec((1, 1, 2 * tc), lambda i, ps, pr, nv: (i, 0, 0),
                                   memory_space=pltpu.SMEM),
                      pl.BlockSpec((tc, half), lambda i, ps, pr, nv: (i, 0))],
            out_specs=pl.BlockSpec(memory_space=pl.ANY),
            scratch_shapes=[pltpu.VMEM((tm, half), hp.dtype), pltpu.SemaphoreType.DMA(()),
                            pltpu.SemaphoreType.DMA(())],
        ),
        out_shape=jax.ShapeDtypeStruct((n_tiles * tm, half), hp.dtype),
        compiler_params=_params(("arbitrary",), 16),
        name="moe_dispatch",
    )(pad_start, pad_rows, n_valid, slot.reshape(steps, 1, 2 * tc), hp)


MOE_CAST_ROWS = 256


def _moe_gemm_kernel(layer, te_ref, nv_ref, nxt_ref, xs_ref, wg_ref, wu_ref, wd_ref, y_ref,
                     stage_g, stage_u, stage_d, wgb_ref, wub_ref, wdb_ref, sem_ref):
    i = pl.program_id(0)
    last = nv_ref[0] - 1
    tile = jnp.minimum(i, last)
    expert = te_ref[tile]
    fresh = jnp.logical_or(i == 0, expert != te_ref[jnp.maximum(tile - 1, 0)])
    pairs = ((wg_ref, stage_g, wgb_ref), (wu_ref, stage_u, wub_ref), (wd_ref, stage_d, wdb_ref))

    def copies(e):
        return [pltpu.make_async_copy(w_ref.at[layer, e], stage, sem_ref.at[n])
                for n, (w_ref, stage, _) in enumerate(pairs)]

    @pl.when(jnp.logical_and(i <= last, fresh))
    def _():
        @pl.when(i == 0)
        def _():
            for cp in copies(expert):
                cp.start()

        for cp in copies(expert):
            cp.wait()
        for _, stage, dst_ref in pairs:
            def cast(j, carry):
                rows = pl.ds(pl.multiple_of(j * MOE_CAST_ROWS, MOE_CAST_ROWS), MOE_CAST_ROWS)
                dst_ref[rows, :] = stage[rows, :].astype(BF16)
                return carry
            lax.fori_loop(0, dst_ref.shape[0] // MOE_CAST_ROWS, cast, 0)

        @pl.when(nxt_ref[tile] >= 0)
        def _():
            for cp in copies(nxt_ref[tile]):
                cp.start()

    @pl.when(i <= last)
    def _():
        xp = xs_ref[...]
        x_a = lax.bitcast_convert_type(xp & jnp.uint32(0xFFFF0000), F32).astype(BF16)
        x_b = lax.bitcast_convert_type(xp << 16, F32).astype(BF16)
        half = xp.shape[1]
        a = _dot(x_a, wgb_ref[:half, :]) + _dot(x_b, wgb_ref[half:, :])
        b = _dot(x_a, wub_ref[:half, :]) + _dot(x_b, wub_ref[half:, :])
        act = a / (1.0 + jnp.exp(-a)) * b
        y_ref[...] = _dot(act.astype(BF16), wdb_ref[...])

    @pl.when(i > last)
    def _():
        y_ref[...] = jnp.zeros_like(y_ref)


def _moe_gemm(xs, tile_expert, n_valid, next_expert, layer, wg, wu, wd, tm=MOE_TM):
    n_slots, half = xs.shape
    _, ne, d, de = wg.shape
    tile = lambda i, te, nv, nx: jnp.minimum(i, nv[0] - 1)
    hbm = pl.BlockSpec(memory_space=pl.ANY)
    return pl.pallas_call(
        functools.partial(_moe_gemm_kernel, layer),
        grid_spec=pltpu.PrefetchScalarGridSpec(
            num_scalar_prefetch=3,
            grid=(n_slots // tm,),
            in_specs=[pl.BlockSpec((tm, half), lambda i, te, nv, nx: (tile(i, te, nv, nx), 0)),
                      hbm, hbm, hbm],
            out_specs=pl.BlockSpec((tm, d), lambda i, te, nv, nx: (i, 0)),
            scratch_shapes=[pltpu.VMEM((d, de), F32), pltpu.VMEM((d, de), F32), pltpu.VMEM((de, d), F32),
                            pltpu.VMEM((d, de), BF16), pltpu.VMEM((d, de), BF16),
                            pltpu.VMEM((de, d), BF16), pltpu.SemaphoreType.DMA((3,))],
        ),
        out_shape=jax.ShapeDtypeStruct((n_slots, d), F32),
        compiler_params=_params(("arbitrary",), 58),
        name="moe_gemm",
    )(tile_expert, n_valid, next_expert, xs, wg, wu, wd)


def _moe_combine_kernel(final_norm, steps_a, idx_ref, idx_next_ref, x_ref, route_ref, gf_ref, y_ref,
                        *rest):
    o_refs, (buf_ref, sem_ref) = rest[:-2], rest[-2:]
    i = pl.program_id(0)
    n = pl.num_programs(0)
    tokens = x_ref.shape[0]

    def start_rows(idx, slot):
        def issue(r, carry):
            for k in range(2):
                pltpu.make_async_copy(y_ref.at[pl.ds(idx[0, 0, 2 * r + k], 1)],
                                      buf_ref.at[slot, k, pl.ds(r, 1)], sem_ref.at[slot]).start()
            return carry
        lax.fori_loop(0, tokens, issue, 0, unroll=4)

    @pl.when(i == 0)
    def _():
        start_rows(idx_ref, 0)

    @pl.when(i + 1 < n)
    def _():
        start_rows(idx_next_ref, (i + 1) % 2)

    slot = i % 2
    for k in range(2):
        pltpu.make_async_copy(y_ref.at[pl.ds(0, tokens)], buf_ref.at[slot, k], sem_ref.at[slot]).wait()
    route = route_ref[...]
    y = x_ref[...] + route[:, ROUTE_W0:ROUTE_W0 + 1] * buf_ref[slot, 0] \
        + route[:, ROUTE_W1:ROUTE_W1 + 1] * buf_ref[slot, 1]
    if final_norm:
        y = y * lax.rsqrt(jnp.mean(y * y, axis=-1, keepdims=True) + EPS) * gf_ref[...]
    if len(o_refs) == 1:
        o_refs[0][...] = y
    else:
        @pl.when(i < steps_a)
        def _():
            o_refs[0][...] = y

        @pl.when(i >= steps_a)
        def _():
            o_refs[1][...] = y


def _moe_combine(x, route, slot, y, gf, final_norm, tokens_a=None, tc=MOE_TC):
    t, d = x.shape
    steps = t // tc
    idx = slot.reshape(steps, 1, 2 * tc)
    row = lambda width: pl.BlockSpec((tc, width), lambda i: (i, 0))
    smem = lambda pos: pl.BlockSpec((1, 1, 2 * tc), lambda i: (pos(i), 0, 0), memory_space=pltpu.SMEM)
    if tokens_a is None:
        steps_a = steps
        out_specs = [row(d)]
        out_shape = [jax.ShapeDtypeStruct((t, d), F32)]
    else:
        steps_a = tokens_a // tc
        out_specs = [pl.BlockSpec((tc, d), lambda i: (jnp.minimum(i, steps_a - 1), 0)),
                     pl.BlockSpec((tc, d), lambda i: (jnp.maximum(i - steps_a, 0), 0))]
        out_shape = [jax.ShapeDtypeStruct((tokens_a, d), F32),
                     jax.ShapeDtypeStruct((t - tokens_a, d), F32)]
    return pl.pallas_call(
        functools.partial(_moe_combine_kernel, final_norm, steps_a),
        grid=(steps,),
        in_specs=[smem(lambda i: i), smem(lambda i: jnp.minimum(i + 1, steps - 1)),
                  row(d), row(LANES), pl.BlockSpec((1, d), lambda i: (0, 0)),
                  pl.BlockSpec(memory_space=pl.ANY)],
        out_specs=out_specs,
        out_shape=out_shape,
        scratch_shapes=[pltpu.VMEM((2, 2, tc, d), F32), pltpu.SemaphoreType.DMA((2,))],
        compiler_params=_params(("arbitrary",), 40),
        name="moe_combine",
    )(idx, idx, x, route, gf.reshape(1, d), y)


def _moe(hp, route, counts, x, layer, wg, wu, wd, gf, final_norm, tokens_a=None):
    slot, n_tiles, tile_expert, n_valid, next_expert, pad = _dispatch_plan(route, counts, MOE_TM)
    xs = _moe_dispatch(hp, slot, pad[0], pad[1], n_valid, n_tiles, MOE_TM)
    y = _moe_gemm(xs, tile_expert, n_valid, next_expert, layer, wg, wu, wd)
    return _moe_combine(x, route, slot, y, gf, final_norm, tokens_a)


def _router_weights(wg, bg, we, be):
    d = wg.shape[0]
    pad = LANES - N_EXPERTS - N_GROUPS
    wr = jnp.concatenate([we, wg, jnp.zeros((d, pad), F32)], axis=1).astype(F32)
    br = jnp.concatenate([be, bg, jnp.zeros((pad,), F32)]).astype(F32).reshape(1, LANES)
    return wr, br


def _layer(xs, bsz, p, l, consts, final_g, tokens_a):
    dft, emb = consts
    w_in = p['w_in'][l].astype(BF16)
    g1 = p['ln1_g'][l]
    if len(xs) == 2:
        hy, x = _norm_proj_stack(xs[0], xs[1], g1, w_in[:, :HY_IN], F32)
    else:
        x, = xs
        hy = _norm_proj(x, g1, w_in[:, :HY_IN], F32)
    t, d = x.shape
    n = t // bsz
    hy = hy.reshape(bsz, n, HY_IN)
    att = _norm_proj(x, g1, w_in[:, HY_IN:HY_IN + ATT_IN], BF16).reshape(bsz, n, ATT_IN)
    ret = _norm_proj(x, g1, w_in[:, HY_IN + ATT_IN:], F32).reshape(bsz, n, RET_IN)
    y_h = _hyena(hy, dft, emb, p['hy_conv_w'][l], p['hy_conv_b'][l], p['hy_w1'][l], p['hy_b1'][l],
                 p['hy_w2'][l], p['hy_b2'][l], p['hy_w3'][l], p['hy_freq'][l], p['hy_decay'][l],
                 p['hy_skip'][l])
    y_a = _attention(att, p['attn_sink'][l], p['rel_bias'])
    y_r = _retention(ret, p['ret_decay'][l])
    wr, br = _router_weights(p['router_group_w'][l], p['router_group_b'][l],
                             p['router_expert_w'][l], p['router_expert_b'][l])
    x, hp, route, counts = _mix_router(x, y_h.reshape(t, -1), y_a.reshape(t, -1), y_r.reshape(t, -1),
                                       p['w_out'][l].astype(BF16), p['ln2_g'][l], wr, br)
    last = l == DEPTH - 1
    return _moe(hp, route, counts, x, l, p['moe_w_gate'], p['moe_w_up'], p['moe_w_down'], final_g,
                final_norm=last, tokens_a=tokens_a if last else None)


def kernel(x_prompt, x_sample, ln1_g, w_in, hy_conv_w, hy_conv_b, hy_w1, hy_b1, hy_w2, hy_b2, hy_w3,
           hy_freq, hy_decay, hy_skip, attn_sink, rel_bias, ret_decay, w_out, ln2_g, router_group_w,
           router_group_b, router_expert_w, router_expert_b, moe_w_gate, moe_w_up, moe_w_down, lnf_g):
    p = dict(ln1_g=ln1_g, w_in=w_in, hy_conv_w=hy_conv_w, hy_conv_b=hy_conv_b, hy_w1=hy_w1,
             hy_b1=hy_b1, hy_w2=hy_w2, hy_b2=hy_b2, hy_w3=hy_w3, hy_freq=hy_freq, hy_decay=hy_decay,
             hy_skip=hy_skip, attn_sink=attn_sink, rel_bias=rel_bias, ret_decay=ret_decay, w_out=w_out,
             ln2_g=ln2_g, router_group_w=router_group_w, router_group_b=router_group_b,
             router_expert_w=router_expert_w, router_expert_b=router_expert_b, moe_w_gate=moe_w_gate,
             moe_w_up=moe_w_up, moe_w_down=moe_w_down)
    bp, n, d = x_prompt.shape
    bs = x_sample.shape[0]
    assert x_sample.shape[1:] == (n, d)
    bsz = bp + bs
    consts = (_dft_matrices(n), _hyena_embedding(n))
    xs = (x_prompt.reshape(bp * n, d), x_sample.reshape(bs * n, d))
    for l in range(DEPTH):
        xs = _layer(xs, bsz, p, l, consts, lnf_g, bp * n)
    y_prompt, y_sample = xs
    return (y_prompt.reshape(bp, n, d), y_sample.reshape(bs, n, d))
```

```python
import functools
import math

import numpy as np
import jax
import jax.numpy as jnp
from jax import lax
from jax.experimental import pallas as pl
from jax.experimental.pallas import tpu as pltpu

F32 = jnp.float32
BF16 = jnp.bfloat16
HIGHEST = lax.Precision.HIGHEST

D_MODEL = 2048
DEPTH = 2
HEAD_DIM = 64
EPS = 1e-6
HY_WIDTH = D_MODEL // 4
ATT_WIDTH = D_MODEL // 2
RET_WIDTH = D_MODEL // 4
HY_ORDER = 2
HY_EMB = 33
HY_FFN = 64
ATT_HEADS = ATT_WIDTH // HEAD_DIM
ATT_KV_HEADS = 4
ATT_GROUP = ATT_HEADS // ATT_KV_HEADS
KV_WIDTH = ATT_KV_HEADS * HEAD_DIM
WINDOW = 128
BLOCK = 128
REL_BUCKETS = 32
REL_MAX_DIST = 128
RET_HEADS = RET_WIDTH // HEAD_DIM
RET_CHUNK = 128
ROPE_BASE = 10000.0
N_GROUPS = 4
EXPERTS_PER_GROUP = 4
N_EXPERTS = N_GROUPS * EXPERTS_PER_GROUP
D_EXPERT = D_MODEL // 2
HY_IN = 3 * HY_WIDTH
ATT_IN = ATT_WIDTH + 2 * KV_WIDTH
RET_IN = 4 * RET_WIDTH

LANES = 128
SUBLANES = 8
MIB = 1024 * 1024


def _params(semantics, vmem_mib):
    return pltpu.CompilerParams(dimension_semantics=semantics, vmem_limit_bytes=vmem_mib * MIB)


def _dot(a, b):
    return jnp.dot(a, b, preferred_element_type=F32)


def _dot_nt(a, b):
    return lax.dot_general(a, b, (((1,), (1,)), ((), ())), preferred_element_type=F32)


def _norm_proj_kernel(x_ref, g_ref, w_ref, o_ref):
    x = x_ref[...]
    inv = lax.rsqrt(jnp.mean(x * x, axis=-1, keepdims=True) + EPS)
    h = (x * inv * g_ref[...]).astype(BF16)
    o_ref[...] = _dot(h, w_ref[...]).astype(o_ref.dtype)


def _norm_proj(x, g, w, out_dtype, tm=512):
    t, d = x.shape
    n = w.shape[1]
    return pl.pallas_call(
        _norm_proj_kernel,
        grid=(t // tm,),
        in_specs=[
            pl.BlockSpec((tm, d), lambda i: (i, 0)),
            pl.BlockSpec((1, d), lambda i: (0, 0)),
            pl.BlockSpec((d, n), lambda i: (0, 0)),
        ],
        out_specs=pl.BlockSpec((tm, n), lambda i: (i, 0)),
        out_shape=jax.ShapeDtypeStruct((t, n), out_dtype),
        compiler_params=_params(("parallel",), 48),
        name="norm_proj",
    )(x, g.reshape(1, d), w)


def _norm_proj_stack_kernel(tiles_a, xa_ref, xb_ref, g_ref, w_ref, o_ref, x_ref):
    x = jnp.where(pl.program_id(0) < tiles_a, xa_ref[...], xb_ref[...])
    x_ref[...] = x
    inv = lax.rsqrt(jnp.mean(x * x, axis=-1, keepdims=True) + EPS)
    h = (x * inv * g_ref[...]).astype(BF16)
    o_ref[...] = _dot(h, w_ref[...]).astype(o_ref.dtype)


def _norm_proj_stack(xa, xb, g, w, out_dtype, tm=512):
    ta, d = xa.shape
    tb = xb.shape[0]
    n = w.shape[1]
    tiles_a = ta // tm
    return pl.pallas_call(
        functools.partial(_norm_proj_stack_kernel, tiles_a),
        grid=((ta + tb) // tm,),
        in_specs=[
            pl.BlockSpec((tm, d), lambda i: (jnp.minimum(i, tiles_a - 1), 0)),
            pl.BlockSpec((tm, d), lambda i: (jnp.maximum(i - tiles_a, 0), 0)),
            pl.BlockSpec((1, d), lambda i: (0, 0)),
            pl.BlockSpec((d, n), lambda i: (0, 0)),
        ],
        out_specs=[pl.BlockSpec((tm, n), lambda i: (i, 0)), pl.BlockSpec((tm, d), lambda i: (i, 0))],
        out_shape=[jax.ShapeDtypeStruct((ta + tb, n), out_dtype),
                   jax.ShapeDtypeStruct((ta + tb, d), xa.dtype)],
        compiler_params=_params(("parallel",), 56),
        name="norm_proj_stack",
    )(xa, xb, g.reshape(1, d), w)


def _short_conv(x, prev_row, next_row, w_ref, b_ref):
    n = x.shape[0]
    row = lax.broadcasted_iota(jnp.int32, x.shape, 0)
    prev = jnp.where(row == 0, prev_row, pltpu.roll(x, 1, 0))
    nxt = jnp.where(row == n - 1, next_row, pltpu.roll(x, n - 1, 0))
    return prev * w_ref[0:1, :] + x * w_ref[1:2, :] + nxt * w_ref[2:3, :] + b_ref[...]


def _hy_filter_kernel(z_ref, w1_ref, b1_ref, w2_ref, b2_ref, fr_ref, w3f_ref, w3b_ref,
                      decf_ref, decb_ref, hs_ref, hd_ref):
    fr = fr_ref[...]
    h = jnp.sin(fr * (jnp.dot(z_ref[...], w1_ref[...], precision=HIGHEST,
                              preferred_element_type=F32) + b1_ref[...]))
    h = jnp.sin(fr * (jnp.dot(h, w2_ref[...], precision=HIGHEST,
                              preferred_element_type=F32) + b2_ref[...]))
    n = h.shape[0]
    shape = (n, w3f_ref.shape[1])
    row = lax.broadcasted_iota(jnp.int32, shape, 0)
    t = row.astype(F32) * (1.0 / (n - 1))
    hf = jnp.dot(h, w3f_ref[...], precision=HIGHEST, preferred_element_type=F32)
    hf = hf * jnp.exp(-t * jnp.abs(decf_ref[...]))
    hb = jnp.dot(h, w3b_ref[...], precision=HIGHEST, preferred_element_type=F32)
    hb = hb * jnp.exp(-t * jnp.abs(decb_ref[...]))
    hb = jnp.where(row == 0, 0.0, hb)
    norm = (jnp.sum(jnp.abs(hf), axis=0, keepdims=True)
            + jnp.sum(jnp.abs(hb), axis=0, keepdims=True))
    hs_ref[...] = (hf + hb) / norm
    hd_ref[...] = (hb - hf) / norm


def _hy_filter(z, w1, b1, w2, b2, fr, w3, decay, tc=256):
    n = z.shape[0]
    cols = HY_ORDER * HY_WIDTH
    w3f, w3b = w3[:, :cols], w3[:, cols:]
    dec = decay.reshape(2, cols)
    zp = jnp.pad(z, ((0, 0), (0, LANES - HY_EMB)))
    w1p = jnp.pad(w1, ((0, LANES - HY_EMB), (0, 0)))
    full = lambda shape: pl.BlockSpec(shape, lambda j: (0, 0))
    col = lambda rows: pl.BlockSpec((rows, tc), lambda j: (0, j))
    return pl.pallas_call(
        _hy_filter_kernel,
        grid=(cols // tc,),
        in_specs=[full((n, LANES)), full((LANES, HY_FFN)), full((1, HY_FFN)), full((HY_FFN, HY_FFN)),
                  full((1, HY_FFN)), full((1, HY_FFN)), col(HY_FFN), col(HY_FFN), col(1), col(1)],
        out_specs=[col(n), col(n)],
        out_shape=[jax.ShapeDtypeStruct((n, cols), F32)] * 2,
        compiler_params=_params(("parallel",), 48),
        name="hy_filter",
    )(zp, w1p, b1.reshape(1, -1), w2, b2.reshape(1, -1), fr.reshape(1, -1), w3f, w3b,
      dec[0:1], dec[1:2])


def _hy_kf_kernel(ff_ref, hs_ref, hd_ref, k_ref):
    hs = hs_ref[...]
    kre = _dot(ff_ref[0], hs.astype(BF16))
    kq = _dot(ff_ref[1], hd_ref[...].astype(BF16))
    row_t = lax.broadcasted_iota(jnp.int32, hs.shape, 0)
    sgn = (1 - 2 * (row_t & 1)).astype(F32)
    nyq = jnp.sum(hs * sgn, axis=0, keepdims=True)
    row = lax.broadcasted_iota(jnp.int32, kq.shape, 0)
    first = jnp.logical_and(row == 0, pl.program_id(0) == 0)
    k_ref[0] = kre
    k_ref[1] = jnp.where(first, nyq, kq)


def _hy_kf(ff, hs, hd, tf=512, tc=512):
    n, cols = hs.shape
    return pl.pallas_call(
        _hy_kf_kernel,
        grid=(n // tf, cols // tc),
        in_specs=[
            pl.BlockSpec((2, tf, n), lambda i, j: (0, i, 0)),
            pl.BlockSpec((n, tc), lambda i, j: (0, j)),
            pl.BlockSpec((n, tc), lambda i, j: (0, j)),
        ],
        out_specs=pl.BlockSpec((2, tf, tc), lambda i, j: (0, i, j)),
        out_shape=jax.ShapeDtypeStruct((2, n, cols), F32),
        compiler_params=_params(("parallel", "parallel"), 48),
        name="hy_filter_dft",
    )(ff, hs, hd)


HY_FREQ_TILE = 512


def _hy_fwd_kernel(conv, ff_ref, u_ref, k_ref, *rest):
    if conv:
        w_ref, b_ref, y_ref = rest
        u = _short_conv(u_ref[0], 0.0, 0.0, w_ref, b_ref).astype(BF16)
    else:
        y_ref, = rest
        u = u_ref[0].astype(BF16)
    tf = HY_FREQ_TILE
    for i in range(ff_ref.shape[1] // tf):
        rows = slice(i * tf, (i + 1) * tf)
        a = _dot(ff_ref[0, rows, :], u)
        bq = _dot(ff_ref[1, rows, :], u)
        kre = k_ref[0, rows, :]
        kq = k_ref[1, rows, :]
        re = a * kre + bq * kq
        im = a * kq - bq * kre
        if i == 0:
            first = lax.broadcasted_iota(jnp.int32, a.shape, 0) == 0
            re = jnp.where(first, a * kre, re)
            im = jnp.where(first, bq * kq, im)
        y_ref[0, 0, rows, :] = re.astype(BF16)
        y_ref[0, 1, rows, :] = im.astype(BF16)


def _hy_fwd(ff, u3, ucol, k, kcol, conv=None):
    bsz, n, _ = u3.shape
    c = HY_WIDTH
    in_specs = [
        pl.BlockSpec((2, n, n), lambda b: (0, 0, 0), pipeline_mode=pl.Buffered(1)),
        pl.BlockSpec((1, n, c), lambda b: (b, 0, ucol)),
        pl.BlockSpec((2, n, c), lambda b: (0, 0, kcol), pipeline_mode=pl.Buffered(1)),
    ]
    args = [ff, u3, k]
    if conv is not None:
        in_specs += [pl.BlockSpec((3, c), lambda b: (0, ucol)), pl.BlockSpec((1, c), lambda b: (0, ucol))]
        args += list(conv)
    return pl.pallas_call(
        functools.partial(_hy_fwd_kernel, conv is not None),
        grid=(bsz,),
        in_specs=in_specs,
        out_specs=pl.BlockSpec((1, 2, n, c), lambda b: (b, 0, 0, 0)),
        out_shape=jax.ShapeDtypeStruct((bsz, 2, n, c), BF16),
        compiler_params=_params(("parallel",), 52),
        name="hy_dft_fwd",
    )(*args)


def _hy_inv_kernel(s_conv, fi_ref, y_ref, d_ref, *rest):
    o_ref = rest[-1]
    i = pl.program_id(0)
    first, last = i == 0, i == pl.num_programs(0) - 1

    def conv_tile(x_ref, lo_ref, hi_ref, w_ref, b_ref):
        prev_row = jnp.where(first, 0.0, lo_ref[0, SUBLANES - 1:SUBLANES, :])
        next_row = jnp.where(last, 0.0, hi_ref[0, 0:1, :])
        return _short_conv(x_ref[0], prev_row, next_row, w_ref, b_ref)

    if s_conv:
        skip = conv_tile(*rest[0:5])
        gate = conv_tile(*rest[5:10])
    else:
        skip = rest[0][0]
        gate = conv_tile(*rest[1:6])
    y = _dot(fi_ref[0], y_ref[0, 0]) + _dot(fi_ref[1], y_ref[0, 1])
    o_ref[0] = (gate * (y + skip * d_ref[...])).astype(o_ref.dtype)


def _hy_inv(fi, y, d, raw3, conv, gcol, out_dtype, scol=None, s3=None, tt=1024):
    bsz, _, n, c = y.shape
    conv_w, conv_b = conv
    halo = tt // SUBLANES
    tile = lambda col: pl.BlockSpec((1, tt, c), lambda i, b: (b, i, col))
    lo = lambda col: pl.BlockSpec((1, SUBLANES, c), lambda i, b: (b, jnp.maximum(i * halo - 1, 0), col))
    hi = lambda col: pl.BlockSpec((1, SUBLANES, c),
                                  lambda i, b: (b, jnp.minimum((i + 1) * halo, n // SUBLANES - 1), col))
    wspec = lambda col: pl.BlockSpec((3, c), lambda i, b: (0, col))
    bspec = lambda col: pl.BlockSpec((1, c), lambda i, b: (0, col))
    conv_in = lambda col: ([tile(col), lo(col), hi(col), wspec(col), bspec(col)],
                           [raw3, raw3, raw3, conv_w, conv_b])
    in_specs = [pl.BlockSpec((2, tt, n), lambda i, b: (0, i, 0)),
                pl.BlockSpec((1, 2, n, c), lambda i, b: (b, 0, 0, 0)),
                pl.BlockSpec((1, c), lambda i, b: (0, 0))]
    args = [fi, y, d.reshape(1, c)]
    if s3 is None:
        specs, arrs = conv_in(scol)
    else:
        specs, arrs = [tile(0)], [s3]
    in_specs += specs
    args += arrs
    specs, arrs = conv_in(gcol)
    in_specs += specs
    args += arrs
    return pl.pallas_call(
        functools.partial(_hy_inv_kernel, s3 is None),
        grid=(n // tt, bsz),
        in_specs=in_specs,
        out_specs=pl.BlockSpec((1, tt, c), lambda i, b: (b, i, 0)),
        out_shape=jax.ShapeDtypeStruct((bsz, n, c), out_dtype),
        compiler_params=_params(("parallel", "parallel"), 48),
        name="hy_dft_inv",
    )(*args)


DFT_SPLIT = 64


def _dft_matrices(n):
    r = lax.broadcasted_iota(jnp.int32, (n, n), 0)
    c = lax.broadcasted_iota(jnp.int32, (n, n), 1)
    def table(rows, step):
        rr = lax.broadcasted_iota(jnp.int32, (rows, n), 0) * step
        cc = lax.broadcasted_iota(jnp.int32, (rows, n), 1)
        ang = ((rr * cc) & (2 * n - 1)).astype(F32) * (math.pi / n)
        return jnp.cos(ang), jnp.sin(ang)
    cos_hi, sin_hi = (t[:, None, :] for t in table(n // DFT_SPLIT, DFT_SPLIT))
    cos_lo, sin_lo = (t[None, :, :] for t in table(DFT_SPLIT, 1))
    cos = (cos_hi * cos_lo - sin_hi * sin_lo).reshape(n, n)
    sin = (sin_hi * cos_lo + cos_hi * sin_lo).reshape(n, n)
    sgn_c = (1 - 2 * (c & 1)).astype(F32)
    sgn_r = (1 - 2 * (r & 1)).astype(F32)
    fwd = jnp.stack([cos, jnp.where(r == 0, sgn_c, sin)]).astype(BF16)
    inv0 = jnp.where(c == 0, 1.0, 2.0) * cos * (0.5 / n)
    inv1 = jnp.where(c == 0, sgn_r * (0.5 / n), sin * (-1.0 / n))
    inv = jnp.stack([inv0, inv1]).astype(BF16)
    return fwd, inv


def _hyena_embedding(n):
    t = jnp.linspace(0.0, 1.0, n, dtype=F32)[:, None]
    bands = (HY_EMB - 1) // 2
    f = jnp.linspace(1e-4, bands - 1, bands, dtype=F32)[None]
    w = 2.0 * math.pi * jnp.arange(n, dtype=F32)[:, None] / n
    return jnp.concatenate([t, jnp.cos(f * w), -jnp.sin(f * w)], -1)


def _hyena(p3, dft, emb, conv_w, conv_b, w1, b1, w2, b2, w3, freq, decay, skip):
    ff, fi = dft
    conv = (conv_w, conv_b.reshape(1, -1))
    hs, hd = _hy_filter(emb, w1, b1, w2, b2, freq, w3, decay)
    k = _hy_kf(ff, hs, hd)
    y = _hy_fwd(ff, p3, 0, k, 0, conv)
    z = _hy_inv(fi, y, skip[0], p3, conv, 1, F32, scol=0)
    y = _hy_fwd(ff, z, 0, k, 1)
    return _hy_inv(fi, y, skip[1], p3, conv, 2, BF16, s3=z)


def _t5_bucket(rel):
    nb = REL_BUCKETS // 2
    max_exact = nb // 2
    n = np.abs(rel)
    large = max_exact + (np.log(np.maximum(n, 1) / max_exact) / np.log(REL_MAX_DIST / max_exact)
                         * (nb - max_exact)).astype(np.int32)
    large = np.minimum(large, nb - 1)
    return (rel > 0).astype(np.int32) * nb + np.where(n < max_exact, n, large)


def _attn_kernel(q_ref, kp_ref, kc_ref, kn_ref, vp_ref, vc_ref, vn_ref, bias_ref, sink_ref, o_ref):
    n = pl.program_id(1)
    q = q_ref[0] * (HEAD_DIM ** -0.5)
    k = jnp.concatenate([kp_ref[0], kc_ref[0], kn_ref[0]], axis=0)
    vt = jnp.concatenate([vp_ref[0], vc_ref[0], vn_ref[0]], axis=1)
    edge_prev = jnp.where(n == 0, -1e30, 0.0).astype(F32)
    edge_next = jnp.where(n == pl.num_programs(1) - 1, -1e30, 0.0).astype(F32)
    ones = jnp.ones((HEAD_DIM, 3 * BLOCK), BF16)
    for kh in range(ATT_KV_HEADS):
        k_h = k[:, kh * HEAD_DIM:(kh + 1) * HEAD_DIM]
        vt_h = jnp.concatenate([vt[kh * HEAD_DIM:(kh + 1) * HEAD_DIM], ones], axis=0)
        q_h = jnp.concatenate([q[:, (kh * ATT_GROUP + g) * HEAD_DIM:(kh * ATT_GROUP + g + 1) * HEAD_DIM]
                               for g in range(ATT_GROUP)], axis=0)
        s = _dot_nt(k_h, q_h) + bias_ref[kh]
        s = jnp.concatenate([s[:BLOCK] + edge_prev, s[BLOCK:2 * BLOCK], s[2 * BLOCK:] + edge_next],
                            axis=0)
        sink = sink_ref[kh]
        m = jnp.maximum(jnp.max(s, axis=0, keepdims=True), sink)
        p = jnp.exp(s - m).astype(BF16)
        o = _dot(vt_h, p)
        o = o[:HEAD_DIM] / (o[HEAD_DIM:HEAD_DIM + 1] + jnp.exp(sink - m))
        for g in range(0, ATT_GROUP, 2):
            pair = jnp.concatenate([o[:, g * BLOCK:(g + 1) * BLOCK],
                                    o[:, (g + 1) * BLOCK:(g + 2) * BLOCK]], axis=0)
            c0 = (kh * ATT_GROUP + g) * HEAD_DIM
            o_ref[0, :, c0:c0 + 2 * HEAD_DIM] = pair.T.astype(o_ref.dtype)


def _attention_bias(rel_bias):
    rel = (np.arange(3 * BLOCK)[None, :] - BLOCK) - np.arange(BLOCK)[:, None]
    bucket = jnp.asarray(_t5_bucket(rel).reshape(-1), jnp.int32)
    onehot = (bucket[:, None] == jnp.arange(REL_BUCKETS, dtype=jnp.int32)[None]).astype(F32)
    bias = jnp.dot(onehot, rel_bias.astype(F32), precision=HIGHEST)
    bias = bias.reshape(BLOCK, 3 * BLOCK, ATT_KV_HEADS, ATT_GROUP)
    bias = jnp.where(jnp.asarray(np.abs(rel) <= WINDOW)[:, :, None, None], bias, -1e30)
    return jnp.transpose(bias, (2, 1, 3, 0)).reshape(ATT_KV_HEADS, 3 * BLOCK, ATT_GROUP * BLOCK)


def _attention(att3, sink, rel_bias):
    bsz, n, _ = att3.shape
    nb = n // BLOCK
    bias = _attention_bias(rel_bias)
    sink_row = jnp.repeat(sink.astype(F32).reshape(ATT_KV_HEADS, 1, ATT_GROUP), BLOCK, axis=2)
    vt = jnp.swapaxes(att3[:, :, ATT_WIDTH + KV_WIDTH:], 1, 2)
    kcol = ATT_WIDTH // KV_WIDTH
    prev = lambda i: jnp.maximum(i - 1, 0)
    nxt = lambda i: jnp.minimum(i + 1, nb - 1)
    cur = lambda i: i
    k_spec = lambda blk: pl.BlockSpec((1, BLOCK, KV_WIDTH), lambda b, i: (b, blk(i), kcol))
    v_spec = lambda blk: pl.BlockSpec((1, KV_WIDTH, BLOCK), lambda b, i: (b, 0, blk(i)))
    return pl.pallas_call(
        _attn_kernel,
        grid=(bsz, nb),
        in_specs=[
            pl.BlockSpec((1, BLOCK, ATT_WIDTH), lambda b, i: (b, i, 0)),
            k_spec(prev), k_spec(cur), k_spec(nxt), v_spec(prev), v_spec(cur), v_spec(nxt),
            pl.BlockSpec((ATT_KV_HEADS, 3 * BLOCK, ATT_GROUP * BLOCK), lambda b, i: (0, 0, 0)),
            pl.BlockSpec((ATT_KV_HEADS, 1, ATT_GROUP * BLOCK), lambda b, i: (0, 0, 0)),
        ],
        out_specs=pl.BlockSpec((1, BLOCK, ATT_WIDTH), lambda b, i: (b, i, 0)),
        out_shape=jax.ShapeDtypeStruct((bsz, n, ATT_WIDTH), BF16),
        compiler_params=_params(("parallel", "parallel"), 32),
        name="window_attention",
    )(att3, att3, att3, att3, vt, vt, vt, bias, sink_row)


def _rotary_tables(n):
    inv = ROPE_BASE ** -jnp.linspace(0.0, 1.0, HEAD_DIM // 2, dtype=F32)
    ang = jnp.arange(n, dtype=F32)[:, None] * inv[None]
    cos, sin = jnp.cos(ang), jnp.sin(ang)
    cos_t = jnp.tile(jnp.concatenate([cos, cos], -1), (1, RET_HEADS))
    sin_t = jnp.tile(jnp.concatenate([-sin, sin], -1), (1, RET_HEADS))
    return cos_t, sin_t


def _rotate(x, cos_t, sin_t):
    width = x.shape[-1]
    half = HEAD_DIM // 2
    lane = lax.broadcasted_iota(jnp.int32, x.shape, 1)
    partner = jnp.where((lane & (HEAD_DIM - 1)) < half,
                        pltpu.roll(x, width - half, 1), pltpu.roll(x, half, 1))
    return x * cos_t + partner * sin_t


RET_PAIR = 2 * HEAD_DIM
RET_PAIRS = RET_WIDTH // RET_PAIR


def _same_head(shape):
    r = lax.broadcasted_iota(jnp.int32, shape, 0) // HEAD_DIM
    c = lax.broadcasted_iota(jnp.int32, shape, 1) // HEAD_DIM
    return r == c


RET_STATE_CHUNKS = 4
RET_OUT_CHUNKS = 2


def _ret_state_kernel(kf_ref, vf_ref, kb_ref, vb_ref, cf_ref, sf_ref, cb_ref, sb_ref, lg_ref,
                      of_ref, ob_ref, stf_ref, stb_ref):
    @pl.when(pl.program_id(1) == 0)
    def _():
        stf_ref[...] = jnp.zeros_like(stf_ref)
        stb_ref[...] = jnp.zeros_like(stb_ref)

    c = RET_CHUNK
    j = lax.broadcasted_iota(jnp.int32, (c, RET_WIDTH), 0).astype(F32)
    lgf = lg_ref[0:1, :]
    lgb = lg_ref[1:2, :]
    scale = HEAD_DIM ** -0.5
    same = _same_head((RET_PAIR, RET_PAIR))

    def scan(st_ref, o_ref, k_ref, v_ref, cos_ref, sin_ref, zeta, lg, order):
        grow = jnp.exp(c * lg)
        for sub in order:
            rows = slice(sub * c, (sub + 1) * c)
            o_ref[0, sub] = st_ref[...].astype(o_ref.dtype)
            k = (_rotate(k_ref[0, rows, :], cos_ref[rows, :], sin_ref[rows, :]) * scale * zeta).astype(BF16)
            v = v_ref[0, rows, :].astype(BF16)
            for p in range(RET_PAIRS):
                lanes = slice(p * RET_PAIR, (p + 1) * RET_PAIR)
                kv = lax.dot_general(k[:, lanes], v[:, lanes], (((0,), (0,)), ((), ())),
                                     preferred_element_type=F32)
                st_ref[p] = grow[:, lanes] * st_ref[p] + jnp.where(same, kv, 0.0)

    chunks = range(RET_STATE_CHUNKS)
    scan(stf_ref, of_ref, kf_ref, vf_ref, cf_ref, sf_ref, jnp.exp((c - 1.0 - j) * lgf), lgf, chunks)
    scan(stb_ref, ob_ref, kb_ref, vb_ref, cb_ref, sb_ref, jnp.exp(j * lgb), lgb, reversed(chunks))


def _ret_states(ret3, cos_t, sin_t, lg_lane):
    bsz, n, _ = ret3.shape
    nc = n // RET_CHUNK
    rows, w = RET_STATE_CHUNKS * RET_CHUNK, RET_WIDTH
    steps = n // rows
    fwd = lambda b, i: i
    bwd = lambda b, i: steps - 1 - i
    col = lambda pos, cb: pl.BlockSpec((1, rows, w), lambda b, i: (b, pos(b, i), cb))
    tab = lambda pos: pl.BlockSpec((rows, w), lambda b, i: (pos(b, i), 0))
    st_shape = (RET_PAIRS, RET_PAIR, RET_PAIR)
    out = lambda pos: pl.BlockSpec((1, RET_STATE_CHUNKS) + st_shape, lambda b, i: (b, pos(b, i), 0, 0, 0))
    return pl.pallas_call(
        _ret_state_kernel,
        grid=(bsz, steps),
        in_specs=[col(fwd, 1), col(fwd, 2), col(bwd, 1), col(bwd, 2),
                  tab(fwd), tab(fwd), tab(bwd), tab(bwd),
                  pl.BlockSpec((2, w), lambda b, i: (0, 0))],
        out_specs=[out(fwd), out(bwd)],
        out_shape=[jax.ShapeDtypeStruct((bsz, nc) + st_shape, BF16)] * 2,
        scratch_shapes=[pltpu.VMEM(st_shape, F32), pltpu.VMEM(st_shape, F32)],
        compiler_params=_params(("parallel", "arbitrary"), 32),
        name="retention_states",
    )(ret3, ret3, ret3, ret3, cos_t, sin_t, cos_t, sin_t, lg_lane)


def _ret_out_kernel(q_ref, k_ref, v_ref, g_ref, cos_ref, sin_ref, lg_ref, lgs_ref, stf_ref, stb_ref,
                    o_ref):
    c, w = RET_CHUNK, RET_WIDTH
    pos = lax.broadcasted_iota(jnp.int32, (c, w), 0).astype(F32)
    xi_f = jnp.exp((pos + 1.0) * lg_ref[0:1, :])
    xi_b = jnp.exp((c - pos) * lg_ref[1:2, :])
    diff = (lax.broadcasted_iota(jnp.int32, (c, c), 0)
            - lax.broadcasted_iota(jnp.int32, (c, c), 1)).astype(F32)
    first = lax.broadcasted_iota(jnp.int32, (c, RET_PAIR), 1) < HEAD_DIM
    avg = jnp.where(_same_head((RET_PAIR, RET_PAIR)), 1.0 / HEAD_DIM, 0.0).astype(BF16)
    decays = [jnp.concatenate(
        [jnp.exp(jnp.where(diff >= 0, diff * lgs_ref[0, h], -diff * lgs_ref[1, h]))
         for h in (2 * p, 2 * p + 1)], axis=0) for p in range(RET_PAIRS)]
    for sub in range(RET_OUT_CHUNKS):
        rows = slice(sub * c, (sub + 1) * c)
        cos_t, sin_t = cos_ref[rows, :], sin_ref[rows, :]
        q = _rotate(q_ref[0, rows, :], cos_t, sin_t)
        k = (_rotate(k_ref[0, rows, :], cos_t, sin_t) * (HEAD_DIM ** -0.5)).astype(BF16)
        v = v_ref[0, rows, :].astype(BF16)
        g = g_ref[0, rows, :]
        q_f = (q * xi_f).astype(BF16)
        q_b = (q * xi_b).astype(BF16)
        for p in range(RET_PAIRS):
            lanes = slice(p * RET_PAIR, (p + 1) * RET_PAIR)
            q_p = q[:, lanes]
            qs = jnp.concatenate([jnp.where(first, q_p, 0.0), jnp.where(first, 0.0, q_p)], axis=0)
            s = _dot_nt(qs.astype(BF16), k[:, lanes])
            o2 = _dot((s * decays[p]).astype(BF16), v[:, lanes])
            y = jnp.where(first, o2[:c], o2[c:])
            y = y + _dot(q_f[:, lanes], stf_ref[0, sub, p]) + _dot(q_b[:, lanes], stb_ref[0, sub, p])
            sq = y * y
            sq_hi = sq.astype(BF16)
            sq_lo = (sq - sq_hi.astype(F32)).astype(BF16)
            ms = _dot(sq_hi, avg) + _dot(sq_lo, avg)
            y = y * lax.rsqrt(ms + EPS)
            g_p = g[:, lanes]
            o_ref[0, rows, lanes] = (g_p / (1.0 + jnp.exp(-g_p)) * y).astype(o_ref.dtype)


def _retention(ret3, decay):
    bsz, n, _ = ret3.shape
    rows, w = RET_OUT_CHUNKS * RET_CHUNK, RET_WIDTH
    lg = -jnp.exp(decay.astype(F32))
    lg_lane = jnp.repeat(lg, HEAD_DIM, axis=1)
    cos_t, sin_t = _rotary_tables(n)
    stf, stb = _ret_states(ret3, cos_t, sin_t, lg_lane)
    col = lambda cb: pl.BlockSpec((1, rows, w), lambda b, i: (b, i, cb))
    tab = pl.BlockSpec((rows, w), lambda b, i: (i, 0))
    st = pl.BlockSpec((1, RET_OUT_CHUNKS, RET_PAIRS, RET_PAIR, RET_PAIR), lambda b, i: (b, i, 0, 0, 0))
    return pl.pallas_call(
        _ret_out_kernel,
        grid=(bsz, n // rows),
        in_specs=[col(0), col(1), col(2), col(3), tab, tab,
                  pl.BlockSpec((2, w), lambda b, i: (0, 0)),
                  pl.BlockSpec(memory_space=pltpu.SMEM), st, st],
        out_specs=pl.BlockSpec((1, rows, w), lambda b, i: (b, i, 0)),
        out_shape=jax.ShapeDtypeStruct((bsz, n, w), BF16),
        compiler_params=_params(("parallel", "parallel"), 32),
        name="retention_out",
    )(ret3, ret3, ret3, ret3, cos_t, sin_t, lg_lane, lg, stf, stb)


ROUTE_I0, ROUTE_I1, ROUTE_W0, ROUTE_W1, ROUTE_R0, ROUTE_R1 = 0, 1, 2, 3, 4, 5


def _mix_router_kernel(x_ref, yh_ref, ya_ref, yr_ref, w_ref, g_ref, wr_ref, br_ref,
                       xo_ref, hp_ref, route_ref, counts_ref, carry_ref):
    @pl.when(pl.program_id(0) == 0)
    def _():
        carry_ref[...] = jnp.zeros_like(carry_ref)

    a, b = HY_WIDTH, HY_WIDTH + ATT_WIDTH
    x = x_ref[...] + _dot(yh_ref[...], w_ref[0:a, :]) + _dot(ya_ref[...], w_ref[a:b, :]) \
        + _dot(yr_ref[...], w_ref[b:, :])
    xo_ref[...] = x
    h = x * lax.rsqrt(jnp.mean(x * x, axis=-1, keepdims=True) + EPS) * g_ref[...]
    h_hi = h.astype(BF16)
    bits = lax.bitcast_convert_type(h_hi.astype(F32), jnp.uint32)
    half = bits.shape[1] // 2
    hp_ref[...] = bits[:, :half] | (bits[:, half:] >> 16)
    h_lo = (h - h_hi.astype(F32)).astype(BF16)
    wr = wr_ref[...]
    wr_hi = wr.astype(BF16)
    wr_lo = (wr - wr_hi.astype(F32)).astype(BF16)
    logits = _dot(h_hi, wr_hi) + (_dot(h_lo, wr_hi) + _dot(h_hi, wr_lo)) + br_ref[...]
    lane = lax.broadcasted_iota(jnp.int32, logits.shape, 1)
    big = jnp.int32(LANES)
    neg = -jnp.inf
    is_group = (lane >= N_EXPERTS) & (lane < N_EXPERTS + N_GROUPS)
    gl = jnp.where(is_group, logits, neg)
    gmax = jnp.max(gl, axis=-1, keepdims=True)
    g_w = 1.0 / jnp.sum(jnp.exp(gl - gmax), axis=-1, keepdims=True)
    g_sel = jnp.min(jnp.where(gl == gmax, lane, big), axis=-1, keepdims=True) - N_EXPERTS
    lo = g_sel * EXPERTS_PER_GROUP
    el = jnp.where((lane >= lo) & (lane < lo + EXPERTS_PER_GROUP), logits, neg)
    v0 = jnp.max(el, axis=-1, keepdims=True)
    i0 = jnp.min(jnp.where(el == v0, lane, big), axis=-1, keepdims=True)
    el = jnp.where(lane == i0, neg, el)
    v1 = jnp.max(el, axis=-1, keepdims=True)
    i1 = jnp.min(jnp.where(el == v1, lane, big), axis=-1, keepdims=True)
    e1 = jnp.exp(v1 - v0)
    w0 = g_w / (1.0 + e1)
    oh0 = jnp.where(lane == i0, 1.0, 0.0)
    oh1 = jnp.where(lane == i1, 1.0, 0.0)
    tm = logits.shape[0]
    earlier = jnp.where(lax.broadcasted_iota(jnp.int32, (tm, tm), 1)
                        < lax.broadcasted_iota(jnp.int32, (tm, tm), 0), 1.0, 0.0).astype(BF16)
    carry = carry_ref[...]
    tot0 = jnp.sum(oh0, axis=0, keepdims=True)
    before0 = _dot(earlier, oh0.astype(BF16)) + carry
    before1 = _dot(earlier, oh1.astype(BF16)) + (carry + tot0)
    r0 = jnp.sum(oh0 * before0, axis=-1, keepdims=True)
    r1 = jnp.sum(oh1 * before1, axis=-1, keepdims=True)
    carry = carry + tot0 + jnp.sum(oh1, axis=0, keepdims=True)
    carry_ref[...] = carry
    counts_ref[...] = jnp.broadcast_to(carry, counts_ref.shape)
    route_ref[...] = (jnp.where(lane == ROUTE_I0, i0.astype(F32), 0.0)
                      + jnp.where(lane == ROUTE_I1, i1.astype(F32), 0.0)
                      + jnp.where(lane == ROUTE_W0, w0, 0.0)
                      + jnp.where(lane == ROUTE_W1, w0 * e1, 0.0)
                      + jnp.where(lane == ROUTE_R0, r0, 0.0)
                      + jnp.where(lane == ROUTE_R1, r1, 0.0))


def _mix_router(x, yh, ya, yr, w_out, g, wr, br, tm=512):
    t, d = x.shape
    row = lambda width: pl.BlockSpec((tm, width), lambda i: (i, 0))
    full = lambda shape: pl.BlockSpec(shape, lambda i: (0, 0))
    return pl.pallas_call(
        _mix_router_kernel,
        grid=(t // tm,),
        in_specs=[row(d), row(HY_WIDTH), row(ATT_WIDTH), row(RET_WIDTH),
                  pl.BlockSpec((d, d), lambda i: (0, 0), pipeline_mode=pl.Buffered(1)),
                  full((1, d)), full((d, LANES)), full((1, LANES))],
        out_specs=[row(d), row(d // 2), row(LANES), full((SUBLANES, LANES))],
        out_shape=[jax.ShapeDtypeStruct((t, d), F32), jax.ShapeDtypeStruct((t, d // 2), jnp.uint32),
                   jax.ShapeDtypeStruct((t, LANES), F32), jax.ShapeDtypeStruct((SUBLANES, LANES), F32)],
        scratch_shapes=[pltpu.VMEM((1, LANES), F32)],
        compiler_params=_params(("arbitrary",), 48),
        name="mix_router",
    )(x, yh, ya, yr, w_out, g.reshape(1, d), wr, br)


MOE_TM = 512
MOE_TC = 256


def _dispatch_plan(route, counts, tm):
    experts = jnp.arange(N_EXPERTS, dtype=jnp.int32)
    ids = route[:, ROUTE_I0:ROUTE_I1 + 1].astype(jnp.int32)
    rank = route[:, ROUTE_R0:ROUTE_R1 + 1].astype(jnp.int32)
    n_tiles = 2 * route.shape[0] // tm + N_EXPERTS
    count = counts[0, :N_EXPERTS].astype(jnp.int32)
    tiles = (count + tm - 1) // tm
    tile_end = jnp.cumsum(tiles)
    offs = (tile_end - tiles) * tm
    slot = jnp.sum(jnp.where(ids[..., None] == experts, offs, 0), axis=-1) + rank
    tile_expert = jnp.minimum(
        jnp.sum(jnp.arange(n_tiles, dtype=jnp.int32)[:, None] >= tile_end[None, :], axis=1),
        N_EXPERTS - 1).astype(jnp.int32)
    later = (experts[None, :] > experts[:, None]) & (tiles[None, :] > 0)
    following = jnp.min(jnp.where(later, experts[None, :], N_EXPERTS), axis=1)
    following = jnp.where(following == N_EXPERTS, -1, following).astype(jnp.int32)
    pad = (offs + count, tiles * tm - count)
    return slot, n_tiles, tile_expert, tile_end[-1:].astype(jnp.int32), following[tile_expert], pad


def _moe_dispatch_kernel(tm, pad_start_ref, pad_rows_ref, nv_ref, idx_ref, hp_ref, xs_ref, zeros_ref,
                         sem, zsem):
    tokens = hp_ref.shape[0]
    n_tiles = xs_ref.shape[0] // tm

    @pl.when(pl.program_id(0) == 0)
    def _():
        zeros_ref[...] = jnp.zeros_like(zeros_ref)
        fills = []
        for e in range(N_EXPERTS):
            rows = pad_rows_ref[e]
            end = pad_start_ref[e] + rows
            size = tm // 2
            while size >= SUBLANES:
                above = rows & ~(2 * size - 1)
                fills.append(((rows & size) != 0, pl.multiple_of(end - above - size, SUBLANES), size))
                size //= 2
            for j in range(SUBLANES - 1):
                fills.append(((rows & (SUBLANES - 1)) > j, pad_start_ref[e] + j, 1))
        for j in range(N_EXPERTS):
            fills.append((nv_ref[0] + j < n_tiles, pl.multiple_of((nv_ref[0] + j) * tm, tm), tm))
        copies = [(cond, pltpu.make_async_copy(zeros_ref.at[pl.ds(0, size)],
                                               xs_ref.at[pl.ds(start, size)], zsem))
                  for cond, start, size in fills]
        for cond, cp in copies:
            pl.when(cond)(cp.start)
        for cond, cp in copies:
            pl.when(cond)(cp.wait)

    def issue(r, carry):
        for k in range(2):
            pltpu.make_async_copy(hp_ref.at[pl.ds(r, 1)],
                                  xs_ref.at[pl.ds(idx_ref[0, 0, 2 * r + k], 1)], sem).start(priority=k)
        return carry

    lax.fori_loop(0, tokens, issue, 0, unroll=4)
    for k in range(2):
        pltpu.make_async_copy(hp_ref, xs_ref.at[pl.ds(0, tokens)], sem).wait()


def _moe_dispatch(hp, slot, pad_start, pad_rows, n_valid, n_tiles, tm, tc=MOE_TC):
    t, half = hp.shape
    steps = t // tc
    return pl.pallas_call(
        functools.partial(_moe_dispatch_kernel, tm),
        grid_spec=pltpu.PrefetchScalarGridSpec(
            num_scalar_prefetch=3,
            grid=(steps,),
            in_specs=[pl.BlockSpec((1, 1, 2 * tc), lambda i, ps, pr, nv: (i, 0, 0),
                                   memory_space=pltpu.SMEM),
                      pl.BlockSpec((tc, half), lambda i, ps, pr, nv: (i, 0))],
            out_specs=pl.BlockSpec(memory_space=pl.ANY),
            scratch_shapes=[pltpu.VMEM((tm, half), hp.dtype), pltpu.SemaphoreType.DMA(()),
                            pltpu.SemaphoreType.DMA(())],
        ),
        out_shape=jax.ShapeDtypeStruct((n_tiles * tm, half), hp.dtype),
        compiler_params=_params(("arbitrary",), 16),
        name="moe_dispatch",
    )(pad_start, pad_rows, n_valid, slot.reshape(steps, 1, 2 * tc), hp)


MOE_CAST_ROWS = 256


def _moe_gemm_kernel(layer, te_ref, nv_ref, nxt_ref, xs_ref, wg_ref, wu_ref, wd_ref, y_ref,
                     stage_g, stage_u, stage_d, wgb_ref, wub_ref, wdb_ref, sem_ref):
    i = pl.program_id(0)
    last = nv_ref[0] - 1
    tile = jnp.minimum(i, last)
    expert = te_ref[tile]
    fresh = jnp.logical_or(i == 0, expert != te_ref[jnp.maximum(tile - 1, 0)])
    pairs = ((wg_ref, stage_g, wgb_ref), (wu_ref, stage_u, wub_ref), (wd_ref, stage_d, wdb_ref))

    def copies(e):
        return [pltpu.make_async_copy(w_ref.at[layer, e], stage, sem_ref.at[n])
                for n, (w_ref, stage, _) in enumerate(pairs)]

    @pl.when(jnp.logical_and(i <= last, fresh))
    def _():
        @pl.when(i == 0)
        def _():
            for cp in copies(expert):
                cp.start()

        for cp in copies(expert):
            cp.wait()
        for _, stage, dst_ref in pairs:
            def cast(j, carry):
                rows = pl.ds(pl.multiple_of(j * MOE_CAST_ROWS, MOE_CAST_ROWS), MOE_CAST_ROWS)
                dst_ref[rows, :] = stage[rows, :].astype(BF16)
                return carry
            lax.fori_loop(0, dst_ref.shape[0] // MOE_CAST_ROWS, cast, 0)

        @pl.when(nxt_ref[tile] >= 0)
        def _():
            for cp in copies(nxt_ref[tile]):
                cp.start()

    @pl.when(i <= last)
    def _():
        xp = xs_ref[...]
        x_a = lax.bitcast_convert_type(xp & jnp.uint32(0xFFFF0000), F32).astype(BF16)
        x_b = lax.bitcast_convert_type(xp << 16, F32).astype(BF16)
        half = xp.shape[1]
        a = _dot(x_a, wgb_ref[:half, :]) + _dot(x_b, wgb_ref[half:, :])
        b = _dot(x_a, wub_ref[:half, :]) + _dot(x_b, wub_ref[half:, :])
        act = a / (1.0 + jnp.exp(-a)) * b
        y_ref[...] = _dot(act.astype(BF16), wdb_ref[...])

    @pl.when(i > last)
    def _():
        y_ref[...] = jnp.zeros_like(y_ref)


def _moe_gemm(xs, tile_expert, n_valid, next_expert, layer, wg, wu, wd, tm=MOE_TM):
    n_slots, half = xs.shape
    _, ne, d, de = wg.shape
    tile = lambda i, te, nv, nx: jnp.minimum(i, nv[0] - 1)
    hbm = pl.BlockSpec(memory_space=pl.ANY)
    return pl.pallas_call(
        functools.partial(_moe_gemm_kernel, layer),
        grid_spec=pltpu.PrefetchScalarGridSpec(
            num_scalar_prefetch=3,
            grid=(n_slots // tm,),
            in_specs=[pl.BlockSpec((tm, half), lambda i, te, nv, nx: (tile(i, te, nv, nx), 0)),
                      hbm, hbm, hbm],
            out_specs=pl.BlockSpec((tm, d), lambda i, te, nv, nx: (i, 0)),
            scratch_shapes=[pltpu.VMEM((d, de), F32), pltpu.VMEM((d, de), F32), pltpu.VMEM((de, d), F32),
                            pltpu.VMEM((d, de), BF16), pltpu.VMEM((d, de), BF16),
                            pltpu.VMEM((de, d), BF16), pltpu.SemaphoreType.DMA((3,))],
        ),
        out_shape=jax.ShapeDtypeStruct((n_slots, d), F32),
        compiler_params=_params(("arbitrary",), 58),
        name="moe_gemm",
    )(tile_expert, n_valid, next_expert, xs, wg, wu, wd)


def _moe_combine_kernel(final_norm, steps_a, idx_ref, idx_next_ref, x_ref, route_ref, gf_ref, y_ref,
                        *rest):
    o_refs, (buf_ref, sem_ref) = rest[:-2], rest[-2:]
    i = pl.program_id(0)
    n = pl.num_programs(0)
    tokens = x_ref.shape[0]

    def start_rows(idx, slot):
        def issue(r, carry):
            for k in range(2):
                pltpu.make_async_copy(y_ref.at[pl.ds(idx[0, 0, 2 * r + k], 1)],
                                      buf_ref.at[slot, k, pl.ds(r, 1)], sem_ref.at[slot]).start(priority=k)
            return carry
        lax.fori_loop(0, tokens, issue, 0, unroll=4)

    @pl.when(i == 0)
    def _():
        start_rows(idx_ref, 0)

    @pl.when(i + 1 < n)
    def _():
        start_rows(idx_next_ref, (i + 1) % 2)

    slot = i % 2
    for k in range(2):
        pltpu.make_async_copy(y_ref.at[pl.ds(0, tokens)], buf_ref.at[slot, k], sem_ref.at[slot]).wait()
    route = route_ref[...]
    y = x_ref[...] + route[:, ROUTE_W0:ROUTE_W0 + 1] * buf_ref[slot, 0] \
        + route[:, ROUTE_W1:ROUTE_W1 + 1] * buf_ref[slot, 1]
    if final_norm:
        y = y * lax.rsqrt(jnp.mean(y * y, axis=-1, keepdims=True) + EPS) * gf_ref[...]
    if len(o_refs) == 1:
        o_refs[0][...] = y
    else:
        @pl.when(i < steps_a)
        def _():
            o_refs[0][...] = y

        @pl.when(i >= steps_a)
        def _():
            o_refs[1][...] = y


def _moe_combine(x, route, slot, y, gf, final_norm, tokens_a=None, tc=MOE_TC):
    t, d = x.shape
    steps = t // tc
    idx = slot.reshape(steps, 1, 2 * tc)
    row = lambda width: pl.BlockSpec((tc, width), lambda i: (i, 0))
    smem = lambda pos: pl.BlockSpec((1, 1, 2 * tc), lambda i: (pos(i), 0, 0), memory_space=pltpu.SMEM)
    if tokens_a is None:
        steps_a = steps
        out_specs = [row(d)]
        out_shape = [jax.ShapeDtypeStruct((t, d), F32)]
    else:
        steps_a = tokens_a // tc
        out_specs = [pl.BlockSpec((tc, d), lambda i: (jnp.minimum(i, steps_a - 1), 0)),
                     pl.BlockSpec((tc, d), lambda i: (jnp.maximum(i - steps_a, 0), 0))]
        out_shape = [jax.ShapeDtypeStruct((tokens_a, d), F32),
                     jax.ShapeDtypeStruct((t - tokens_a, d), F32)]
    return pl.pallas_call(
        functools.partial(_moe_combine_kernel, final_norm, steps_a),
        grid=(steps,),
        in_specs=[smem(lambda i: i), smem(lambda i: jnp.minimum(i + 1, steps - 1)),
                  row(d), row(LANES), pl.BlockSpec((1, d), lambda i: (0, 0)),
                  pl.BlockSpec(memory_space=pl.ANY)],
        out_specs=out_specs,
        out_shape=out_shape,
        scratch_shapes=[pltpu.VMEM((2, 2, tc, d), F32), pltpu.SemaphoreType.DMA((2,))],
        compiler_params=_params(("arbitrary",), 40),
        name="moe_combine",
    )(idx, idx, x, route, gf.reshape(1, d), y)


def _moe(hp, route, counts, x, layer, wg, wu, wd, gf, final_norm, tokens_a=None):
    slot, n_tiles, tile_expert, n_valid, next_expert, pad = _dispatch_plan(route, counts, MOE_TM)
    xs = _moe_dispatch(hp, slot, pad[0], pad[1], n_valid, n_tiles, MOE_TM)
    y = _moe_gemm(xs, tile_expert, n_valid, next_expert, layer, wg, wu, wd)
    return _moe_combine(x, route, slot, y, gf, final_norm, tokens_a)


def _router_weights(wg, bg, we, be):
    d = wg.shape[0]
    pad = LANES - N_EXPERTS - N_GROUPS
    wr = jnp.concatenate([we, wg, jnp.zeros((d, pad), F32)], axis=1).astype(F32)
    br = jnp.concatenate([be, bg, jnp.zeros((pad,), F32)]).astype(F32).reshape(1, LANES)
    return wr, br


def _layer(xs, bsz, p, l, consts, final_g, tokens_a):
    dft, emb = consts
    w_in = p['w_in'][l].astype(BF16)
    g1 = p['ln1_g'][l]
    if len(xs) == 2:
        hy, x = _norm_proj_stack(xs[0], xs[1], g1, w_in[:, :HY_IN], F32)
    else:
        x, = xs
        hy = _norm_proj(x, g1, w_in[:, :HY_IN], F32)
    t, d = x.shape
    n = t // bsz
    hy = hy.reshape(bsz, n, HY_IN)
    att = _norm_proj(x, g1, w_in[:, HY_IN:HY_IN + ATT_IN], BF16).reshape(bsz, n, ATT_IN)
    ret = _norm_proj(x, g1, w_in[:, HY_IN + ATT_IN:], F32).reshape(bsz, n, RET_IN)
    y_h = _hyena(hy, dft, emb, p['hy_conv_w'][l], p['hy_conv_b'][l], p['hy_w1'][l], p['hy_b1'][l],
                 p['hy_w2'][l], p['hy_b2'][l], p['hy_w3'][l], p['hy_freq'][l], p['hy_decay'][l],
                 p['hy_skip'][l])
    y_a = _attention(att, p['attn_sink'][l], p['rel_bias'])
    y_r = _retention(ret, p['ret_decay'][l])
    wr, br = _router_weights(p['router_group_w'][l], p['router_group_b'][l],
                             p['router_expert_w'][l], p['router_expert_b'][l])
    x, hp, route, counts = _mix_router(x, y_h.reshape(t, -1), y_a.reshape(t, -1), y_r.reshape(t, -1),
                                       p['w_out'][l].astype(BF16), p['ln2_g'][l], wr, br)
    last = l == DEPTH - 1
    return _moe(hp, route, counts, x, l, p['moe_w_gate'], p['moe_w_up'], p['moe_w_down'], final_g,
                final_norm=last, tokens_a=tokens_a if last else None)


def kernel(x_prompt, x_sample, ln1_g, w_in, hy_conv_w, hy_conv_b, hy_w1, hy_b1, hy_w2, hy_b2, hy_w3,
           hy_freq, hy_decay, hy_skip, attn_sink, rel_bias, ret_decay, w_out, ln2_g, router_group_w,
           router_group_b, router_expert_w, router_expert_b, moe_w_gate, moe_w_up, moe_w_down, lnf_g):
    p = dict(ln1_g=ln1_g, w_in=w_in, hy_conv_w=hy_conv_w, hy_conv_b=hy_conv_b, hy_w1=hy_w1,
             hy_b1=hy_b1, hy_w2=hy_w2, hy_b2=hy_b2, hy_w3=hy_w3, hy_freq=hy_freq, hy_decay=hy_decay,
             hy_skip=hy_skip, attn_sink=attn_sink, rel_bias=rel_bias, ret_decay=ret_decay, w_out=w_out,
             ln2_g=ln2_g, router_group_w=router_group_w, router_group_b=router_group_b,
             router_expert_w=router_expert_w, router_expert_b=router_expert_b, moe_w_gate=moe_w_gate,
             moe_w_up=moe_w_up, moe_w_down=moe_w_down)
    bp, n, d = x_prompt.shape
    bs = x_sample.shape[0]
    assert x_sample.shape[1:] == (n, d)
    bsz = bp + bs
    consts = (_dft_matrices(n), _hyena_embedding(n))
    xs = (x_prompt.reshape(bp * n, d), x_sample.reshape(bs * n, d))
    for l in range(DEPTH):
        xs = _layer(xs, bsz, p, l, consts, lnf_g, bp * n)
    y_prompt, y_sample = xs
    return (y_prompt.reshape(bp, n, d), y_sample.reshape(bs, n, d))
```

```python
import functools
import math

import numpy as np
import jax
import jax.numpy as jnp
from jax import lax
from jax.experimental import pallas as pl
from jax.experimental.pallas import tpu as pltpu

F32 = jnp.float32
BF16 = jnp.bfloat16
HIGHEST = lax.Precision.HIGHEST

D_MODEL = 2048
DEPTH = 2
HEAD_DIM = 64
EPS = 1e-6
HY_WIDTH = D_MODEL // 4
ATT_WIDTH = D_MODEL // 2
RET_WIDTH = D_MODEL // 4
HY_ORDER = 2
HY_EMB = 33
HY_FFN = 64
ATT_HEADS = ATT_WIDTH // HEAD_DIM
ATT_KV_HEADS = 4
ATT_GROUP = ATT_HEADS // ATT_KV_HEADS
KV_WIDTH = ATT_KV_HEADS * HEAD_DIM
WINDOW = 128
BLOCK = 128
REL_BUCKETS = 32
REL_MAX_DIST = 128
RET_HEADS = RET_WIDTH // HEAD_DIM
RET_CHUNK = 128
ROPE_BASE = 10000.0
N_GROUPS = 4
EXPERTS_PER_GROUP = 4
N_EXPERTS = N_GROUPS * EXPERTS_PER_GROUP
D_EXPERT = D_MODEL // 2
HY_IN = 3 * HY_WIDTH
ATT_IN = ATT_WIDTH + 2 * KV_WIDTH
RET_IN = 4 * RET_WIDTH

LANES = 128
SUBLANES = 8
MIB = 1024 * 1024


def _params(semantics, vmem_mib):
    return pltpu.CompilerParams(dimension_semantics=semantics, vmem_limit_bytes=vmem_mib * MIB)


def _dot(a, b):
    return jnp.dot(a, b, preferred_element_type=F32)


def _dot_nt(a, b):
    return lax.dot_general(a, b, (((1,), (1,)), ((), ())), preferred_element_type=F32)


def _norm_proj_kernel(x_ref, g_ref, w_ref, o_ref):
    x = x_ref[...]
    inv = lax.rsqrt(jnp.mean(x * x, axis=-1, keepdims=True) + EPS)
    h = (x * inv * g_ref[...]).astype(BF16)
    o_ref[...] = _dot(h, w_ref[...]).astype(o_ref.dtype)


def _norm_proj(x, g, w, out_dtype, tm=1024):
    t, d = x.shape
    n = w.shape[1]
    return pl.pallas_call(
        _norm_proj_kernel,
        grid=(t // tm,),
        in_specs=[
            pl.BlockSpec((tm, d), lambda i: (i, 0)),
            pl.BlockSpec((1, d), lambda i: (0, 0)),
            pl.BlockSpec((d, n), lambda i: (0, 0), pipeline_mode=pl.Buffered(1)),
        ],
        out_specs=pl.BlockSpec((tm, n), lambda i: (i, 0)),
        out_shape=jax.ShapeDtypeStruct((t, n), out_dtype),
        compiler_params=_params(("parallel",), 56),
        name="norm_proj",
    )(x, g.reshape(1, d), w)


def _norm_proj_stack_kernel(tiles_a, xa_ref, xb_ref, g_ref, w_ref, o_ref, x_ref):
    x = jnp.where(pl.program_id(0) < tiles_a, xa_ref[...], xb_ref[...])
    x_ref[...] = x
    inv = lax.rsqrt(jnp.mean(x * x, axis=-1, keepdims=True) + EPS)
    h = (x * inv * g_ref[...]).astype(BF16)
    o_ref[...] = _dot(h, w_ref[...]).astype(o_ref.dtype)


def _norm_proj_stack(xa, xb, g, w, out_dtype, tm=512):
    ta, d = xa.shape
    tb = xb.shape[0]
    n = w.shape[1]
    tiles_a = ta // tm
    return pl.pallas_call(
        functools.partial(_norm_proj_stack_kernel, tiles_a),
        grid=((ta + tb) // tm,),
        in_specs=[
            pl.BlockSpec((tm, d), lambda i: (jnp.minimum(i, tiles_a - 1), 0)),
            pl.BlockSpec((tm, d), lambda i: (jnp.maximum(i - tiles_a, 0), 0)),
            pl.BlockSpec((1, d), lambda i: (0, 0)),
            pl.BlockSpec((d, n), lambda i: (0, 0)),
        ],
        out_specs=[pl.BlockSpec((tm, n), lambda i: (i, 0)), pl.BlockSpec((tm, d), lambda i: (i, 0))],
        out_shape=[jax.ShapeDtypeStruct((ta + tb, n), out_dtype),
                   jax.ShapeDtypeStruct((ta + tb, d), xa.dtype)],
        compiler_params=_params(("parallel",), 56),
        name="norm_proj_stack",
    )(xa, xb, g.reshape(1, d), w)


def _short_conv(x, prev_row, next_row, w_ref, b_ref):
    n = x.shape[0]
    row = lax.broadcasted_iota(jnp.int32, x.shape, 0)
    prev = jnp.where(row == 0, prev_row, pltpu.roll(x, 1, 0))
    nxt = jnp.where(row == n - 1, next_row, pltpu.roll(x, n - 1, 0))
    return prev * w_ref[0:1, :] + x * w_ref[1:2, :] + nxt * w_ref[2:3, :] + b_ref[...]


def _hy_filter_kernel(z_ref, w1_ref, b1_ref, w2_ref, b2_ref, fr_ref, w3f_ref, w3b_ref,
                      decf_ref, decb_ref, hs_ref, hd_ref):
    fr = fr_ref[...]
    h = jnp.sin(fr * (jnp.dot(z_ref[...], w1_ref[...], precision=HIGHEST,
                              preferred_element_type=F32) + b1_ref[...]))
    h = jnp.sin(fr * (jnp.dot(h, w2_ref[...], precision=HIGHEST,
                              preferred_element_type=F32) + b2_ref[...]))
    n = h.shape[0]
    shape = (n, w3f_ref.shape[1])
    row = lax.broadcasted_iota(jnp.int32, shape, 0)
    t = row.astype(F32) * (1.0 / (n - 1))
    hf = jnp.dot(h, w3f_ref[...], precision=HIGHEST, preferred_element_type=F32)
    hf = hf * jnp.exp(-t * jnp.abs(decf_ref[...]))
    hb = jnp.dot(h, w3b_ref[...], precision=HIGHEST, preferred_element_type=F32)
    hb = hb * jnp.exp(-t * jnp.abs(decb_ref[...]))
    hb = jnp.where(row == 0, 0.0, hb)
    norm = (jnp.sum(jnp.abs(hf), axis=0, keepdims=True)
            + jnp.sum(jnp.abs(hb), axis=0, keepdims=True))
    hs_ref[...] = (hf + hb) / norm
    hd_ref[...] = (hb - hf) / norm


def _hy_filter(z, w1, b1, w2, b2, fr, w3, decay, tc=256):
    n = z.shape[0]
    cols = HY_ORDER * HY_WIDTH
    w3f, w3b = w3[:, :cols], w3[:, cols:]
    dec = decay.reshape(2, cols)
    zp = jnp.pad(z, ((0, 0), (0, LANES - HY_EMB)))
    w1p = jnp.pad(w1, ((0, LANES - HY_EMB), (0, 0)))
    full = lambda shape: pl.BlockSpec(shape, lambda j: (0, 0))
    col = lambda rows: pl.BlockSpec((rows, tc), lambda j: (0, j))
    return pl.pallas_call(
        _hy_filter_kernel,
        grid=(cols // tc,),
        in_specs=[full((n, LANES)), full((LANES, HY_FFN)), full((1, HY_FFN)), full((HY_FFN, HY_FFN)),
                  full((1, HY_FFN)), full((1, HY_FFN)), col(HY_FFN), col(HY_FFN), col(1), col(1)],
        out_specs=[col(n), col(n)],
        out_shape=[jax.ShapeDtypeStruct((n, cols), F32)] * 2,
        compiler_params=_params(("parallel",), 48),
        name="hy_filter",
    )(zp, w1p, b1.reshape(1, -1), w2, b2.reshape(1, -1), fr.reshape(1, -1), w3f, w3b,
      dec[0:1], dec[1:2])


def _hy_kf_kernel(ff_ref, hs_ref, hd_ref, k_ref):
    hs = hs_ref[...]
    kre = _dot(ff_ref[0], hs.astype(BF16))
    kq = _dot(ff_ref[1], hd_ref[...].astype(BF16))
    row_t = lax.broadcasted_iota(jnp.int32, hs.shape, 0)
    sgn = (1 - 2 * (row_t & 1)).astype(F32)
    nyq = jnp.sum(hs * sgn, axis=0, keepdims=True)
    row = lax.broadcasted_iota(jnp.int32, kq.shape, 0)
    first = jnp.logical_and(row == 0, pl.program_id(0) == 0)
    k_ref[0] = kre
    k_ref[1] = jnp.where(first, nyq, kq)


def _hy_kf(ff, hs, hd, tf=512, tc=512):
    n, cols = hs.shape
    return pl.pallas_call(
        _hy_kf_kernel,
        grid=(n // tf, cols // tc),
        in_specs=[
            pl.BlockSpec((2, tf, n), lambda i, j: (0, i, 0)),
            pl.BlockSpec((n, tc), lambda i, j: (0, j)),
            pl.BlockSpec((n, tc), lambda i, j: (0, j)),
        ],
        out_specs=pl.BlockSpec((2, tf, tc), lambda i, j: (0, i, j)),
        out_shape=jax.ShapeDtypeStruct((2, n, cols), F32),
        compiler_params=_params(("parallel", "parallel"), 48),
        name="hy_filter_dft",
    )(ff, hs, hd)


HY_FREQ_TILE = 512


def _hy_fwd_kernel(conv, ff_ref, u_ref, k_ref, *rest):
    if conv:
        w_ref, b_ref, y_ref = rest
        u = _short_conv(u_ref[0], 0.0, 0.0, w_ref, b_ref).astype(BF16)
    else:
        y_ref, = rest
        u = u_ref[0].astype(BF16)
    tf = HY_FREQ_TILE
    for i in range(ff_ref.shape[1] // tf):
        rows = slice(i * tf, (i + 1) * tf)
        a = _dot(ff_ref[0, rows, :], u)
        bq = _dot(ff_ref[1, rows, :], u)
        kre = k_ref[0, rows, :]
        kq = k_ref[1, rows, :]
        re = a * kre + bq * kq
        im = a * kq - bq * kre
        if i == 0:
            first = lax.broadcasted_iota(jnp.int32, a.shape, 0) == 0
            re = jnp.where(first, a * kre, re)
            im = jnp.where(first, bq * kq, im)
        y_ref[0, 0, rows, :] = re.astype(BF16)
        y_ref[0, 1, rows, :] = im.astype(BF16)


def _hy_fwd(ff, u3, ucol, k, kcol, conv=None):
    bsz, n, _ = u3.shape
    c = HY_WIDTH
    in_specs = [
        pl.BlockSpec((2, n, n), lambda b: (0, 0, 0), pipeline_mode=pl.Buffered(1)),
        pl.BlockSpec((1, n, c), lambda b: (b, 0, ucol)),
        pl.BlockSpec((2, n, c), lambda b: (0, 0, kcol), pipeline_mode=pl.Buffered(1)),
    ]
    args = [ff, u3, k]
    if conv is not None:
        in_specs += [pl.BlockSpec((3, c), lambda b: (0, ucol)), pl.BlockSpec((1, c), lambda b: (0, ucol))]
        args += list(conv)
    return pl.pallas_call(
        functools.partial(_hy_fwd_kernel, conv is not None),
        grid=(bsz,),
        in_specs=in_specs,
        out_specs=pl.BlockSpec((1, 2, n, c), lambda b: (b, 0, 0, 0)),
        out_shape=jax.ShapeDtypeStruct((bsz, 2, n, c), BF16),
        compiler_params=_params(("parallel",), 52),
        name="hy_dft_fwd",
    )(*args)


def _hy_inv_kernel(s_conv, fi_ref, y_ref, d_ref, *rest):
    o_ref = rest[-1]
    i = pl.program_id(0)
    first, last = i == 0, i == pl.num_programs(0) - 1

    def conv_tile(x_ref, lo_ref, hi_ref, w_ref, b_ref):
        prev_row = jnp.where(first, 0.0, lo_ref[0, SUBLANES - 1:SUBLANES, :])
        next_row = jnp.where(last, 0.0, hi_ref[0, 0:1, :])
        return _short_conv(x_ref[0], prev_row, next_row, w_ref, b_ref)

    if s_conv:
        skip = conv_tile(*rest[0:5])
        gate = conv_tile(*rest[5:10])
    else:
        skip = rest[0][0]
        gate = conv_tile(*rest[1:6])
    y = _dot(fi_ref[0], y_ref[0, 0]) + _dot(fi_ref[1], y_ref[0, 1])
    o_ref[0] = (gate * (y + skip * d_ref[...])).astype(o_ref.dtype)


def _hy_inv(fi, y, d, raw3, conv, gcol, out_dtype, scol=None, s3=None, tt=1024):
    bsz, _, n, c = y.shape
    conv_w, conv_b = conv
    halo = tt // SUBLANES
    tile = lambda col: pl.BlockSpec((1, tt, c), lambda i, b: (b, i, col))
    lo = lambda col: pl.BlockSpec((1, SUBLANES, c), lambda i, b: (b, jnp.maximum(i * halo - 1, 0), col))
    hi = lambda col: pl.BlockSpec((1, SUBLANES, c),
                                  lambda i, b: (b, jnp.minimum((i + 1) * halo, n // SUBLANES - 1), col))
    wspec = lambda col: pl.BlockSpec((3, c), lambda i, b: (0, col))
    bspec = lambda col: pl.BlockSpec((1, c), lambda i, b: (0, col))
    conv_in = lambda col: ([tile(col), lo(col), hi(col), wspec(col), bspec(col)],
                           [raw3, raw3, raw3, conv_w, conv_b])
    in_specs = [pl.BlockSpec((2, tt, n), lambda i, b: (0, i, 0)),
                pl.BlockSpec((1, 2, n, c), lambda i, b: (b, 0, 0, 0)),
                pl.BlockSpec((1, c), lambda i, b: (0, 0))]
    args = [fi, y, d.reshape(1, c)]
    if s3 is None:
        specs, arrs = conv_in(scol)
    else:
        specs, arrs = [tile(0)], [s3]
    in_specs += specs
    args += arrs
    specs, arrs = conv_in(gcol)
    in_specs += specs
    args += arrs
    return pl.pallas_call(
        functools.partial(_hy_inv_kernel, s3 is None),
        grid=(n // tt, bsz),
        in_specs=in_specs,
        out_specs=pl.BlockSpec((1, tt, c), lambda i, b: (b, i, 0)),
        out_shape=jax.ShapeDtypeStruct((bsz, n, c), out_dtype),
        compiler_params=_params(("parallel", "parallel"), 48),
        name="hy_dft_inv",
    )(*args)


DFT_SPLIT = 64


def _dft_matrices(n):
    r = lax.broadcasted_iota(jnp.int32, (n, n), 0)
    c = lax.broadcasted_iota(jnp.int32, (n, n), 1)
    def table(rows, step):
        rr = lax.broadcasted_iota(jnp.int32, (rows, n), 0) * step
        cc = lax.broadcasted_iota(jnp.int32, (rows, n), 1)
        ang = ((rr * cc) & (2 * n - 1)).astype(F32) * (math.pi / n)
        return jnp.cos(ang), jnp.sin(ang)
    cos_hi, sin_hi = (t[:, None, :] for t in table(n // DFT_SPLIT, DFT_SPLIT))
    cos_lo, sin_lo = (t[None, :, :] for t in table(DFT_SPLIT, 1))
    cos = (cos_hi * cos_lo - sin_hi * sin_lo).reshape(n, n)
    sin = (sin_hi * cos_lo + cos_hi * sin_lo).reshape(n, n)
    sgn_c = (1 - 2 * (c & 1)).astype(F32)
    sgn_r = (1 - 2 * (r & 1)).astype(F32)
    fwd = jnp.stack([cos, jnp.where(r == 0, sgn_c, sin)]).astype(BF16)
    inv0 = jnp.where(c == 0, 1.0, 2.0) * cos * (0.5 / n)
    inv1 = jnp.where(c == 0, sgn_r * (0.5 / n), sin * (-1.0 / n))
    inv = jnp.stack([inv0, inv1]).astype(BF16)
    return fwd, inv


def _hyena_embedding(n):
    t = jnp.linspace(0.0, 1.0, n, dtype=F32)[:, None]
    bands = (HY_EMB - 1) // 2
    f = jnp.linspace(1e-4, bands - 1, bands, dtype=F32)[None]
    w = 2.0 * math.pi * jnp.arange(n, dtype=F32)[:, None] / n
    return jnp.concatenate([t, jnp.cos(f * w), -jnp.sin(f * w)], -1)


def _hyena(p3, dft, emb, conv_w, conv_b, w1, b1, w2, b2, w3, freq, decay, skip):
    ff, fi = dft
    conv = (conv_w, conv_b.reshape(1, -1))
    hs, hd = _hy_filter(emb, w1, b1, w2, b2, freq, w3, decay)
    k = _hy_kf(ff, hs, hd)
    y = _hy_fwd(ff, p3, 0, k, 0, conv)
    z = _hy_inv(fi, y, skip[0], p3, conv, 1, F32, scol=0)
    y = _hy_fwd(ff, z, 0, k, 1)
    return _hy_inv(fi, y, skip[1], p3, conv, 2, BF16, s3=z)


def _t5_bucket(rel):
    nb = REL_BUCKETS // 2
    max_exact = nb // 2
    n = np.abs(rel)
    large = max_exact + (np.log(np.maximum(n, 1) / max_exact) / np.log(REL_MAX_DIST / max_exact)
                         * (nb - max_exact)).astype(np.int32)
    large = np.minimum(large, nb - 1)
    return (rel > 0).astype(np.int32) * nb + np.where(n < max_exact, n, large)


def _attn_kernel(q_ref, kp_ref, kc_ref, kn_ref, vp_ref, vc_ref, vn_ref, bias_ref, sink_ref, o_ref):
    n = pl.program_id(1)
    q = q_ref[0] * (HEAD_DIM ** -0.5)
    k = jnp.concatenate([kp_ref[0], kc_ref[0], kn_ref[0]], axis=0)
    vt = jnp.concatenate([vp_ref[0], vc_ref[0], vn_ref[0]], axis=1)
    edge_prev = jnp.where(n == 0, -1e30, 0.0).astype(F32)
    edge_next = jnp.where(n == pl.num_programs(1) - 1, -1e30, 0.0).astype(F32)
    ones = jnp.ones((HEAD_DIM, 3 * BLOCK), BF16)
    for kh in range(ATT_KV_HEADS):
        k_h = k[:, kh * HEAD_DIM:(kh + 1) * HEAD_DIM]
        vt_h = jnp.concatenate([vt[kh * HEAD_DIM:(kh + 1) * HEAD_DIM], ones], axis=0)
        q_h = jnp.concatenate([q[:, (kh * ATT_GROUP + g) * HEAD_DIM:(kh * ATT_GROUP + g + 1) * HEAD_DIM]
                               for g in range(ATT_GROUP)], axis=0)
        s = _dot_nt(k_h, q_h) + bias_ref[kh]
        s = jnp.concatenate([s[:BLOCK] + edge_prev, s[BLOCK:2 * BLOCK], s[2 * BLOCK:] + edge_next],
                            axis=0)
        sink = sink_ref[kh]
        m = jnp.maximum(jnp.max(s, axis=0, keepdims=True), sink)
        p = jnp.exp(s - m).astype(BF16)
        o = _dot(vt_h, p)
        o = o[:HEAD_DIM] / (o[HEAD_DIM:HEAD_DIM + 1] + jnp.exp(sink - m))
        for g in range(0, ATT_GROUP, 2):
            pair = jnp.concatenate([o[:, g * BLOCK:(g + 1) * BLOCK],
                                    o[:, (g + 1) * BLOCK:(g + 2) * BLOCK]], axis=0)
            c0 = (kh * ATT_GROUP + g) * HEAD_DIM
            o_ref[0, :, c0:c0 + 2 * HEAD_DIM] = pair.T.astype(o_ref.dtype)


def _attention_bias(rel_bias):
    rel = (np.arange(3 * BLOCK)[None, :] - BLOCK) - np.arange(BLOCK)[:, None]
    bucket = jnp.asarray(_t5_bucket(rel).reshape(-1), jnp.int32)
    onehot = (bucket[:, None] == jnp.arange(REL_BUCKETS, dtype=jnp.int32)[None]).astype(F32)
    bias = jnp.dot(onehot, rel_bias.astype(F32), precision=HIGHEST)
    bias = bias.reshape(BLOCK, 3 * BLOCK, ATT_KV_HEADS, ATT_GROUP)
    bias = jnp.where(jnp.asarray(np.abs(rel) <= WINDOW)[:, :, None, None], bias, -1e30)
    return jnp.transpose(bias, (2, 1, 3, 0)).reshape(ATT_KV_HEADS, 3 * BLOCK, ATT_GROUP * BLOCK)


def _attention(att3, sink, rel_bias):
    bsz, n, _ = att3.shape
    nb = n // BLOCK
    bias = _attention_bias(rel_bias)
    sink_row = jnp.repeat(sink.astype(F32).reshape(ATT_KV_HEADS, 1, ATT_GROUP), BLOCK, axis=2)
    vt = jnp.swapaxes(att3[:, :, ATT_WIDTH + KV_WIDTH:], 1, 2)
    kcol = ATT_WIDTH // KV_WIDTH
    prev = lambda i: jnp.maximum(i - 1, 0)
    nxt = lambda i: jnp.minimum(i + 1, nb - 1)
    cur = lambda i: i
    k_spec = lambda blk: pl.BlockSpec((1, BLOCK, KV_WIDTH), lambda b, i: (b, blk(i), kcol))
    v_spec = lambda blk: pl.BlockSpec((1, KV_WIDTH, BLOCK), lambda b, i: (b, 0, blk(i)))
    return pl.pallas_call(
        _attn_kernel,
        grid=(bsz, nb),
        in_specs=[
            pl.BlockSpec((1, BLOCK, ATT_WIDTH), lambda b, i: (b, i, 0)),
            k_spec(prev), k_spec(cur), k_spec(nxt), v_spec(prev), v_spec(cur), v_spec(nxt),
            pl.BlockSpec((ATT_KV_HEADS, 3 * BLOCK, ATT_GROUP * BLOCK), lambda b, i: (0, 0, 0)),
            pl.BlockSpec((ATT_KV_HEADS, 1, ATT_GROUP * BLOCK), lambda b, i: (0, 0, 0)),
        ],
        out_specs=pl.BlockSpec((1, BLOCK, ATT_WIDTH), lambda b, i: (b, i, 0)),
        out_shape=jax.ShapeDtypeStruct((bsz, n, ATT_WIDTH), BF16),
        compiler_params=_params(("parallel", "parallel"), 32),
        name="window_attention",
    )(att3, att3, att3, att3, vt, vt, vt, bias, sink_row)


def _rotary_tables(n):
    inv = ROPE_BASE ** -jnp.linspace(0.0, 1.0, HEAD_DIM // 2, dtype=F32)
    ang = jnp.arange(n, dtype=F32)[:, None] * inv[None]
    cos, sin = jnp.cos(ang), jnp.sin(ang)
    cos_t = jnp.tile(jnp.concatenate([cos, cos], -1), (1, RET_HEADS))
    sin_t = jnp.tile(jnp.concatenate([-sin, sin], -1), (1, RET_HEADS))
    return cos_t, sin_t


def _rotate(x, cos_t, sin_t):
    width = x.shape[-1]
    half = HEAD_DIM // 2
    lane = lax.broadcasted_iota(jnp.int32, x.shape, 1)
    partner = jnp.where((lane & (HEAD_DIM - 1)) < half,
                        pltpu.roll(x, width - half, 1), pltpu.roll(x, half, 1))
    return x * cos_t + partner * sin_t


RET_PAIR = 2 * HEAD_DIM
RET_PAIRS = RET_WIDTH // RET_PAIR


def _same_head(shape):
    r = lax.broadcasted_iota(jnp.int32, shape, 0) // HEAD_DIM
    c = lax.broadcasted_iota(jnp.int32, shape, 1) // HEAD_DIM
    return r == c


RET_STATE_CHUNKS = 4
RET_OUT_CHUNKS = 2


def _ret_state_kernel(kf_ref, vf_ref, kb_ref, vb_ref, cf_ref, sf_ref, cb_ref, sb_ref, lg_ref,
                      of_ref, ob_ref, stf_ref, stb_ref):
    @pl.when(pl.program_id(1) == 0)
    def _():
        stf_ref[...] = jnp.zeros_like(stf_ref)
        stb_ref[...] = jnp.zeros_like(stb_ref)

    c = RET_CHUNK
    j = lax.broadcasted_iota(jnp.int32, (c, RET_WIDTH), 0).astype(F32)
    lgf = lg_ref[0:1, :]
    lgb = lg_ref[1:2, :]
    scale = HEAD_DIM ** -0.5
    same = _same_head((RET_PAIR, RET_PAIR))

    def scan(st_ref, o_ref, k_ref, v_ref, cos_ref, sin_ref, zeta, lg, order):
        grow = jnp.exp(c * lg)
        for sub in order:
            rows = slice(sub * c, (sub + 1) * c)
            o_ref[0, sub] = st_ref[...].astype(o_ref.dtype)
            k = (_rotate(k_ref[0, rows, :], cos_ref[rows, :], sin_ref[rows, :]) * scale * zeta).astype(BF16)
            v = v_ref[0, rows, :].astype(BF16)
            for p in range(RET_PAIRS):
                lanes = slice(p * RET_PAIR, (p + 1) * RET_PAIR)
                kv = lax.dot_general(k[:, lanes], v[:, lanes], (((0,), (0,)), ((), ())),
                                     preferred_element_type=F32)
                st_ref[p] = grow[:, lanes] * st_ref[p] + jnp.where(same, kv, 0.0)

    chunks = range(RET_STATE_CHUNKS)
    scan(stf_ref, of_ref, kf_ref, vf_ref, cf_ref, sf_ref, jnp.exp((c - 1.0 - j) * lgf), lgf, chunks)
    scan(stb_ref, ob_ref, kb_ref, vb_ref, cb_ref, sb_ref, jnp.exp(j * lgb), lgb, reversed(chunks))


def _ret_states(ret3, cos_t, sin_t, lg_lane):
    bsz, n, _ = ret3.shape
    nc = n // RET_CHUNK
    rows, w = RET_STATE_CHUNKS * RET_CHUNK, RET_WIDTH
    steps = n // rows
    fwd = lambda b, i: i
    bwd = lambda b, i: steps - 1 - i
    col = lambda pos, cb: pl.BlockSpec((1, rows, w), lambda b, i: (b, pos(b, i), cb))
    tab = lambda pos: pl.BlockSpec((rows, w), lambda b, i: (pos(b, i), 0))
    st_shape = (RET_PAIRS, RET_PAIR, RET_PAIR)
    out = lambda pos: pl.BlockSpec((1, RET_STATE_CHUNKS) + st_shape, lambda b, i: (b, pos(b, i), 0, 0, 0))
    return pl.pallas_call(
        _ret_state_kernel,
        grid=(bsz, steps),
        in_specs=[col(fwd, 1), col(fwd, 2), col(bwd, 1), col(bwd, 2),
                  tab(fwd), tab(fwd), tab(bwd), tab(bwd),
                  pl.BlockSpec((2, w), lambda b, i: (0, 0))],
        out_specs=[out(fwd), out(bwd)],
        out_shape=[jax.ShapeDtypeStruct((bsz, nc) + st_shape, BF16)] * 2,
        scratch_shapes=[pltpu.VMEM(st_shape, F32), pltpu.VMEM(st_shape, F32)],
        compiler_params=_params(("parallel", "arbitrary"), 32),
        name="retention_states",
    )(ret3, ret3, ret3, ret3, cos_t, sin_t, cos_t, sin_t, lg_lane)


def _ret_out_kernel(q_ref, k_ref, v_ref, g_ref, cos_ref, sin_ref, lg_ref, lgs_ref, stf_ref, stb_ref,
                    o_ref):
    c, w = RET_CHUNK, RET_WIDTH
    pos = lax.broadcasted_iota(jnp.int32, (c, w), 0).astype(F32)
    xi_f = jnp.exp((pos + 1.0) * lg_ref[0:1, :])
    xi_b = jnp.exp((c - pos) * lg_ref[1:2, :])
    diff = (lax.broadcasted_iota(jnp.int32, (c, c), 0)
            - lax.broadcasted_iota(jnp.int32, (c, c), 1)).astype(F32)
    first = lax.broadcasted_iota(jnp.int32, (c, RET_PAIR), 1) < HEAD_DIM
    avg = jnp.where(_same_head((RET_PAIR, RET_PAIR)), 1.0 / HEAD_DIM, 0.0).astype(BF16)
    decays = [jnp.concatenate(
        [jnp.exp(jnp.where(diff >= 0, diff * lgs_ref[0, h], -diff * lgs_ref[1, h]))
         for h in (2 * p, 2 * p + 1)], axis=0) for p in range(RET_PAIRS)]
    for sub in range(RET_OUT_CHUNKS):
        rows = slice(sub * c, (sub + 1) * c)
        cos_t, sin_t = cos_ref[rows, :], sin_ref[rows, :]
        q = _rotate(q_ref[0, rows, :], cos_t, sin_t)
        k = (_rotate(k_ref[0, rows, :], cos_t, sin_t) * (HEAD_DIM ** -0.5)).astype(BF16)
        v = v_ref[0, rows, :].astype(BF16)
        g = g_ref[0, rows, :]
        q_f = (q * xi_f).astype(BF16)
        q_b = (q * xi_b).astype(BF16)
        for p in range(RET_PAIRS):
            lanes = slice(p * RET_PAIR, (p + 1) * RET_PAIR)
            q_p = q[:, lanes]
            qs = jnp.concatenate([jnp.where(first, q_p, 0.0), jnp.where(first, 0.0, q_p)], axis=0)
            s = _dot_nt(qs.astype(BF16), k[:, lanes])
            o2 = _dot((s * decays[p]).astype(BF16), v[:, lanes])
            y = jnp.where(first, o2[:c], o2[c:])
            y = y + _dot(q_f[:, lanes], stf_ref[0, sub, p]) + _dot(q_b[:, lanes], stb_ref[0, sub, p])
            sq = y * y
            sq_hi = sq.astype(BF16)
            sq_lo = (sq - sq_hi.astype(F32)).astype(BF16)
            ms = _dot(sq_hi, avg) + _dot(sq_lo, avg)
            y = y * lax.rsqrt(ms + EPS)
            g_p = g[:, lanes]
            o_ref[0, rows, lanes] = (g_p / (1.0 + jnp.exp(-g_p)) * y).astype(o_ref.dtype)


def _retention(ret3, decay):
    bsz, n, _ = ret3.shape
    rows, w = RET_OUT_CHUNKS * RET_CHUNK, RET_WIDTH
    lg = -jnp.exp(decay.astype(F32))
    lg_lane = jnp.repeat(lg, HEAD_DIM, axis=1)
    cos_t, sin_t = _rotary_tables(n)
    stf, stb = _ret_states(ret3, cos_t, sin_t, lg_lane)
    col = lambda cb: pl.BlockSpec((1, rows, w), lambda b, i: (b, i, cb))
    tab = pl.BlockSpec((rows, w), lambda b, i: (i, 0))
    st = pl.BlockSpec((1, RET_OUT_CHUNKS, RET_PAIRS, RET_PAIR, RET_PAIR), lambda b, i: (b, i, 0, 0, 0))
    return pl.pallas_call(
        _ret_out_kernel,
        grid=(bsz, n // rows),
        in_specs=[col(0), col(1), col(2), col(3), tab, tab,
                  pl.BlockSpec((2, w), lambda b, i: (0, 0)),
                  pl.BlockSpec(memory_space=pltpu.SMEM), st, st],
        out_specs=pl.BlockSpec((1, rows, w), lambda b, i: (b, i, 0)),
        out_shape=jax.ShapeDtypeStruct((bsz, n, w), BF16),
        compiler_params=_params(("parallel", "parallel"), 32),
        name="retention_out",
    )(ret3, ret3, ret3, ret3, cos_t, sin_t, lg_lane, lg, stf, stb)


ROUTE_I0, ROUTE_I1, ROUTE_W0, ROUTE_W1, ROUTE_R0, ROUTE_R1 = 0, 1, 2, 3, 4, 5


def _mix_router_kernel(x_ref, yh_ref, ya_ref, yr_ref, w_ref, g_ref, wr_ref, br_ref,
                       xo_ref, hp_ref, route_ref, counts_ref, carry_ref, logits_ref):
    i = pl.program_id(0)

    @pl.when(i == 0)
    def _():
        carry_ref[...] = jnp.zeros_like(carry_ref)
        logits_ref[...] = jnp.zeros_like(logits_ref)

    logits = logits_ref[(i + 1) % 2]
    live = jnp.where(i >= 1, 1.0, 0.0).astype(F32)
    lane = lax.broadcasted_iota(jnp.int32, logits.shape, 1)
    big = jnp.int32(LANES)
    neg = -jnp.inf
    is_group = (lane >= N_EXPERTS) & (lane < N_EXPERTS + N_GROUPS)
    gl = jnp.where(is_group, logits, neg)
    gmax = jnp.max(gl, axis=-1, keepdims=True)
    g_w = 1.0 / jnp.sum(jnp.exp(gl - gmax), axis=-1, keepdims=True)
    g_sel = jnp.min(jnp.where(gl == gmax, lane, big), axis=-1, keepdims=True) - N_EXPERTS
    lo = g_sel * EXPERTS_PER_GROUP
    el = jnp.where((lane >= lo) & (lane < lo + EXPERTS_PER_GROUP), logits, neg)
    v0 = jnp.max(el, axis=-1, keepdims=True)
    i0 = jnp.min(jnp.where(el == v0, lane, big), axis=-1, keepdims=True)
    el = jnp.where(lane == i0, neg, el)
    v1 = jnp.max(el, axis=-1, keepdims=True)
    i1 = jnp.min(jnp.where(el == v1, lane, big), axis=-1, keepdims=True)
    e1 = jnp.exp(v1 - v0)
    w0 = g_w / (1.0 + e1)
    oh0 = jnp.where(lane == i0, 1.0, 0.0)
    oh1 = jnp.where(lane == i1, 1.0, 0.0)
    tm = logits.shape[0]
    earlier = jnp.where(lax.broadcasted_iota(jnp.int32, (tm, tm), 1)
                        < lax.broadcasted_iota(jnp.int32, (tm, tm), 0), 1.0, 0.0).astype(BF16)
    carry = carry_ref[...]
    tot0 = jnp.sum(oh0, axis=0, keepdims=True)
    before0 = _dot(earlier, oh0.astype(BF16)) + carry
    before1 = _dot(earlier, oh1.astype(BF16)) + (carry + tot0)
    r0 = jnp.sum(oh0 * before0, axis=-1, keepdims=True)
    r1 = jnp.sum(oh1 * before1, axis=-1, keepdims=True)
    carry = carry + live * (tot0 + jnp.sum(oh1, axis=0, keepdims=True))
    carry_ref[...] = carry
    counts_ref[...] = jnp.broadcast_to(carry, counts_ref.shape)
    route_ref[...] = (jnp.where(lane == ROUTE_I0, i0.astype(F32), 0.0)
                      + jnp.where(lane == ROUTE_I1, i1.astype(F32), 0.0)
                      + jnp.where(lane == ROUTE_W0, w0, 0.0)
                      + jnp.where(lane == ROUTE_W1, w0 * e1, 0.0)
                      + jnp.where(lane == ROUTE_R0, r0, 0.0)
                      + jnp.where(lane == ROUTE_R1, r1, 0.0))

    mix = jnp.concatenate([yh_ref[...], ya_ref[...], yr_ref[...]], axis=1)
    x = x_ref[...] + _dot(mix, w_ref[...])
    xo_ref[...] = x
    h = x * lax.rsqrt(jnp.mean(x * x, axis=-1, keepdims=True) + EPS) * g_ref[...]
    h_hi = h.astype(BF16)
    bits = lax.bitcast_convert_type(h_hi.astype(F32), jnp.uint32)
    half = bits.shape[1] // 2
    hp_ref[...] = bits[:, :half] | (bits[:, half:] >> 16)
    h_lo = (h - h_hi.astype(F32)).astype(BF16)
    both = _dot(h_hi, wr_ref[...])
    logits_ref[i % 2] = (both[:, :LANES] + (_dot(h_lo, wr_ref[:, :LANES]) + both[:, LANES:])
                         + br_ref[...])


def _mix_router(x, yh, ya, yr, w_out, g, wr, br, tm=512):
    t, d = x.shape
    steps = t // tm
    cur = lambda i: (jnp.minimum(i, steps - 1), 0)
    row = lambda width: pl.BlockSpec((tm, width), cur)
    full = lambda shape: pl.BlockSpec(shape, lambda i: (0, 0))
    return pl.pallas_call(
        _mix_router_kernel,
        grid=(steps + 1,),
        in_specs=[row(d), row(HY_WIDTH), row(ATT_WIDTH), row(RET_WIDTH),
                  pl.BlockSpec((d, d), lambda i: (0, 0), pipeline_mode=pl.Buffered(1)),
                  full((1, d)), full((d, 2 * LANES)), full((1, LANES))],
        out_specs=[row(d), row(d // 2),
                   pl.BlockSpec((tm, LANES), lambda i: (jnp.maximum(i - 1, 0), 0)),
                   full((SUBLANES, LANES))],
        out_shape=[jax.ShapeDtypeStruct((t, d), F32), jax.ShapeDtypeStruct((t, d // 2), jnp.uint32),
                   jax.ShapeDtypeStruct((t, LANES), F32), jax.ShapeDtypeStruct((SUBLANES, LANES), F32)],
        scratch_shapes=[pltpu.VMEM((1, LANES), F32), pltpu.VMEM((2, tm, LANES), F32)],
        compiler_params=_params(("arbitrary",), 48),
        name="mix_router",
    )(x, yh, ya, yr, w_out, g.reshape(1, d), wr, br)


MOE_TM = 512
MOE_TC = 512


def _dispatch_plan(route, counts, tm):
    experts = jnp.arange(N_EXPERTS, dtype=jnp.int32)
    ids = route[:, ROUTE_I0:ROUTE_I1 + 1].astype(jnp.int32)
    rank = route[:, ROUTE_R0:ROUTE_R1 + 1].astype(jnp.int32)
    n_tiles = 2 * route.shape[0] // tm + N_EXPERTS
    count = counts[0, :N_EXPERTS].astype(jnp.int32)
    tiles = (count + tm - 1) // tm
    tile_end = jnp.cumsum(tiles)
    offs = (tile_end - tiles) * tm
    slot = jnp.sum(jnp.where(ids[..., None] == experts, offs, 0), axis=-1) + rank
    tile_expert = jnp.minimum(
        jnp.sum(jnp.arange(n_tiles, dtype=jnp.int32)[:, None] >= tile_end[None, :], axis=1),
        N_EXPERTS - 1).astype(jnp.int32)
    later = (experts[None, :] > experts[:, None]) & (tiles[None, :] > 0)
    following = jnp.min(jnp.where(later, experts[None, :], N_EXPERTS), axis=1)
    following = jnp.where(following == N_EXPERTS, -1, following).astype(jnp.int32)
    pad = (offs + count, tiles * tm - count)
    return slot, n_tiles, tile_expert, tile_end[-1:].astype(jnp.int32), following[tile_expert], pad


def _moe_dispatch_kernel(tm, pad_start_ref, pad_rows_ref, nv_ref, idx_ref, hp_ref, xs_ref, zeros_ref,
                         sem, zsem):
    tokens = hp_ref.shape[0]
    n_tiles = xs_ref.shape[0] // tm

    @pl.when(pl.program_id(0) == 0)
    def _():
        zeros_ref[...] = jnp.zeros_like(zeros_ref)
        fills = []
        for e in range(N_EXPERTS):
            rows = pad_rows_ref[e]
            end = pad_start_ref[e] + rows
            size = tm // 2
            while size >= SUBLANES:
                above = rows & ~(2 * size - 1)
                fills.append(((rows & size) != 0, pl.multiple_of(end - above - size, SUBLANES), size))
                size //= 2
            for j in range(SUBLANES - 1):
                fills.append(((rows & (SUBLANES - 1)) > j, pad_start_ref[e] + j, 1))
        for j in range(N_EXPERTS):
            fills.append((nv_ref[0] + j < n_tiles, pl.multiple_of((nv_ref[0] + j) * tm, tm), tm))
        copies = [(cond, pltpu.make_async_copy(zeros_ref.at[pl.ds(0, size)],
                                               xs_ref.at[pl.ds(start, size)], zsem))
                  for cond, start, size in fills]
        for cond, cp in copies:
            pl.when(cond)(cp.start)
        for cond, cp in copies:
            pl.when(cond)(cp.wait)

    def issue(r, carry):
        for k in range(2):
            pltpu.make_async_copy(hp_ref.at[pl.ds(r, 1)],
                                  xs_ref.at[pl.ds(idx_ref[0, 0, 2 * r + k], 1)], sem).start(priority=k)
        return carry

    lax.fori_loop(0, tokens, issue, 0, unroll=4)
    for k in range(2):
        pltpu.make_async_copy(hp_ref, xs_ref.at[pl.ds(0, tokens)], sem).wait()


def _moe_dispatch(hp, slot, pad_start, pad_rows, n_valid, n_tiles, tm, tc=MOE_TC):
    t, half = hp.shape
    steps = t // tc
    return pl.pallas_call(
        functools.partial(_moe_dispatch_kernel, tm),
        grid_spec=pltpu.PrefetchScalarGridSpec(
            num_scalar_prefetch=3,
            grid=(steps,),
            in_specs=[pl.BlockSpec((1, 1, 2 * tc), lambda i, ps, pr, nv: (i, 0, 0),
                                   memory_space=pltpu.SMEM),
                      pl.BlockSpec((tc, half), lambda i, ps, pr, nv: (i, 0))],
            out_specs=pl.BlockSpec(memory_space=pl.ANY),
            scratch_shapes=[pltpu.VMEM((tm, half), hp.dtype), pltpu.SemaphoreType.DMA(()),
                            pltpu.SemaphoreType.DMA(())],
        ),
        out_shape=jax.ShapeDtypeStruct((n_tiles * tm, half), hp.dtype),
        compiler_params=_params(("arbitrary",), 16),
        name="moe_dispatch",
    )(pad_start, pad_rows, n_valid, slot.reshape(steps, 1, 2 * tc), hp)


MOE_CAST_ROWS = 256


def _moe_gemm_kernel(layer, te_ref, nv_ref, nxt_ref, xs_ref, wg_ref, wu_ref, wd_ref, y_ref,
                     stage_g, stage_u, stage_d, wgb_ref, wub_ref, wdb_ref, sem_ref):
    i = pl.program_id(0)
    last = nv_ref[0] - 1
    tile = jnp.minimum(i, last)
    expert = te_ref[tile]
    fresh = jnp.logical_or(i == 0, expert != te_ref[jnp.maximum(tile - 1, 0)])
    pairs = ((wg_ref, stage_g, wgb_ref), (wu_ref, stage_u, wub_ref), (wd_ref, stage_d, wdb_ref))

    def copies(e):
        return [pltpu.make_async_copy(w_ref.at[layer, e], stage, sem_ref.at[n])
                for n, (w_ref, stage, _) in enumerate(pairs)]

    @pl.when(jnp.logical_and(i <= last, fresh))
    def _():
        @pl.when(i == 0)
        def _():
            for cp in copies(expert):
                cp.start()

        for cp in copies(expert):
            cp.wait()
        for _, stage, dst_ref in pairs:
            def cast(j, carry):
                rows = pl.ds(pl.multiple_of(j * MOE_CAST_ROWS, MOE_CAST_ROWS), MOE_CAST_ROWS)
                dst_ref[rows, :] = stage[rows, :].astype(BF16)
                return carry
            lax.fori_loop(0, dst_ref.shape[0] // MOE_CAST_ROWS, cast, 0)

        @pl.when(nxt_ref[tile] >= 0)
        def _():
            for cp in copies(nxt_ref[tile]):
                cp.start()

    @pl.when(i <= last)
    def _():
        xp = xs_ref[...]
        x_a = lax.bitcast_convert_type(xp & jnp.uint32(0xFFFF0000), F32).astype(BF16)
        x_b = lax.bitcast_convert_type(xp << 16, F32).astype(BF16)
        x = jnp.concatenate([x_a, x_b], axis=1)
        a = _dot(x, wgb_ref[...])
        b = _dot(x, wub_ref[...])
        act = a / (1.0 + jnp.exp(-a)) * b
        y_ref[...] = _dot(act.astype(BF16), wdb_ref[...])

    @pl.when(i > last)
    def _():
        y_ref[...] = jnp.zeros_like(y_ref)


def _moe_gemm(xs, tile_expert, n_valid, next_expert, layer, wg, wu, wd, tm=MOE_TM):
    n_slots, half = xs.shape
    _, ne, d, de = wg.shape
    tile = lambda i, te, nv, nx: jnp.minimum(i, nv[0] - 1)
    hbm = pl.BlockSpec(memory_space=pl.ANY)
    return pl.pallas_call(
        functools.partial(_moe_gemm_kernel, layer),
        grid_spec=pltpu.PrefetchScalarGridSpec(
            num_scalar_prefetch=3,
            grid=(n_slots // tm,),
            in_specs=[pl.BlockSpec((tm, half), lambda i, te, nv, nx: (tile(i, te, nv, nx), 0)),
                      hbm, hbm, hbm],
            out_specs=pl.BlockSpec((tm, d), lambda i, te, nv, nx: (i, 0)),
            scratch_shapes=[pltpu.VMEM((d, de), F32), pltpu.VMEM((d, de), F32), pltpu.VMEM((de, d), F32),
                            pltpu.VMEM((d, de), BF16), pltpu.VMEM((d, de), BF16),
                            pltpu.VMEM((de, d), BF16), pltpu.SemaphoreType.DMA((3,))],
        ),
        out_shape=jax.ShapeDtypeStruct((n_slots, d), F32),
        compiler_params=_params(("arbitrary",), 58),
        name="moe_gemm",
    )(tile_expert, n_valid, next_expert, xs, wg, wu, wd)


def _moe_combine_kernel(final_norm, steps_a, idx_ref, idx_next_ref, x_ref, route_ref, gf_ref, y_ref,
                        *rest):
    o_refs, (buf_ref, sem_ref) = rest[:-2], rest[-2:]
    i = pl.program_id(0)
    n = pl.num_programs(0)
    tokens = x_ref.shape[0]

    def start_rows(idx, slot):
        def issue(r, carry):
            for k in range(2):
                pltpu.make_async_copy(y_ref.at[pl.ds(idx[0, 0, 2 * r + k], 1)],
                                      buf_ref.at[slot, k, pl.ds(r, 1)], sem_ref.at[slot]).start(priority=k)
            return carry
        lax.fori_loop(0, tokens, issue, 0, unroll=4)

    @pl.when(i == 0)
    def _():
        start_rows(idx_ref, 0)

    @pl.when(i + 1 < n)
    def _():
        start_rows(idx_next_ref, (i + 1) % 2)

    slot = i % 2
    for k in range(2):
        pltpu.make_async_copy(y_ref.at[pl.ds(0, tokens)], buf_ref.at[slot, k], sem_ref.at[slot]).wait()
    route = route_ref[...]
    y = x_ref[...] + route[:, ROUTE_W0:ROUTE_W0 + 1] * buf_ref[slot, 0] \
        + route[:, ROUTE_W1:ROUTE_W1 + 1] * buf_ref[slot, 1]
    if final_norm:
        y = y * lax.rsqrt(jnp.mean(y * y, axis=-1, keepdims=True) + EPS) * gf_ref[...]
    if len(o_refs) == 1:
        o_refs[0][...] = y
    else:
        @pl.when(i < steps_a)
        def _():
            o_refs[0][...] = y

        @pl.when(i >= steps_a)
        def _():
            o_refs[1][...] = y


def _moe_combine(x, route, slot, y, gf, final_norm, tokens_a=None, tc=MOE_TC):
    t, d = x.shape
    steps = t // tc
    idx = slot.reshape(steps, 1, 2 * tc)
    row = lambda width: pl.BlockSpec((tc, width), lambda i: (i, 0))
    smem = lambda pos: pl.BlockSpec((1, 1, 2 * tc), lambda i: (pos(i), 0, 0), memory_space=pltpu.SMEM)
    if tokens_a is None:
        steps_a = steps
        out_specs = [row(d)]
        out_shape = [jax.ShapeDtypeStruct((t, d), F32)]
    else:
        steps_a = tokens_a // tc
        out_specs = [pl.BlockSpec((tc, d), lambda i: (jnp.minimum(i, steps_a - 1), 0)),
                     pl.BlockSpec((tc, d), lambda i: (jnp.maximum(i - steps_a, 0), 0))]
        out_shape = [jax.ShapeDtypeStruct((tokens_a, d), F32),
                     jax.ShapeDtypeStruct((t - tokens_a, d), F32)]
    return pl.pallas_call(
        functools.partial(_moe_combine_kernel, final_norm, steps_a),
        grid=(steps,),
        in_specs=[smem(lambda i: i), smem(lambda i: jnp.minimum(i + 1, steps - 1)),
                  row(d), row(LANES), pl.BlockSpec((1, d), lambda i: (0, 0)),
                  pl.BlockSpec(memory_space=pl.ANY)],
        out_specs=out_specs,
        out_shape=out_shape,
        scratch_shapes=[pltpu.VMEM((2, 2, tc, d), F32), pltpu.SemaphoreType.DMA((2,))],
        compiler_params=_params(("arbitrary",), 52),
        name="moe_combine",
    )(idx, idx, x, route, gf.reshape(1, d), y)


def _moe(hp, route, counts, x, layer, wg, wu, wd, gf, final_norm, tokens_a=None):
    slot, n_tiles, tile_expert, n_valid, next_expert, pad = _dispatch_plan(route, counts, MOE_TM)
    xs = _moe_dispatch(hp, slot, pad[0], pad[1], n_valid, n_tiles, MOE_TM)
    y = _moe_gemm(xs, tile_expert, n_valid, next_expert, layer, wg, wu, wd)
    return _moe_combine(x, route, slot, y, gf, final_norm, tokens_a)


def _router_weights(wg, bg, we, be):
    d = wg.shape[0]
    pad = LANES - N_EXPERTS - N_GROUPS
    wr = jnp.concatenate([we, wg, jnp.zeros((d, pad), F32)], axis=1).astype(F32)
    br = jnp.concatenate([be, bg, jnp.zeros((pad,), F32)]).astype(F32).reshape(1, LANES)
    wr_hi = wr.astype(BF16)
    wr_lo = (wr - wr_hi.astype(F32)).astype(BF16)
    return jnp.concatenate([wr_hi, wr_lo], axis=1), br


def _layer(xs, bsz, p, l, consts, final_g, tokens_a):
    dft, emb = consts
    w_in = p['w_in'][l].astype(BF16)
    g1 = p['ln1_g'][l]
    if len(xs) == 2:
        hy, x = _norm_proj_stack(xs[0], xs[1], g1, w_in[:, :HY_IN], F32)
    else:
        x, = xs
        hy = _norm_proj(x, g1, w_in[:, :HY_IN], F32)
    t, d = x.shape
    n = t // bsz
    hy = hy.reshape(bsz, n, HY_IN)
    att = _norm_proj(x, g1, w_in[:, HY_IN:HY_IN + ATT_IN], BF16).reshape(bsz, n, ATT_IN)
    ret = _norm_proj(x, g1, w_in[:, HY_IN + ATT_IN:], F32).reshape(bsz, n, RET_IN)
    y_h = _hyena(hy, dft, emb, p['hy_conv_w'][l], p['hy_conv_b'][l], p['hy_w1'][l], p['hy_b1'][l],
                 p['hy_w2'][l], p['hy_b2'][l], p['hy_w3'][l], p['hy_freq'][l], p['hy_decay'][l],
                 p['hy_skip'][l])
    y_a = _attention(att, p['attn_sink'][l], p['rel_bias'])
    y_r = _retention(ret, p['ret_decay'][l])
    wr, br = _router_weights(p['router_group_w'][l], p['router_group_b'][l],
                             p['router_expert_w'][l], p['router_expert_b'][l])
    x, hp, route, counts = _mix_router(x, y_h.reshape(t, -1), y_a.reshape(t, -1), y_r.reshape(t, -1),
                                       p['w_out'][l].astype(BF16), p['ln2_g'][l], wr, br)
    last = l == DEPTH - 1
    return _moe(hp, route, counts, x, l, p['moe_w_gate'], p['moe_w_up'], p['moe_w_down'], final_g,
                final_norm=last, tokens_a=tokens_a if last else None)


def kernel(x_prompt, x_sample, ln1_g, w_in, hy_conv_w, hy_conv_b, hy_w1, hy_b1, hy_w2, hy_b2, hy_w3,
           hy_freq, hy_decay, hy_skip, attn_sink, rel_bias, ret_decay, w_out, ln2_g, router_group_w,
           router_group_b, router_expert_w, router_expert_b, moe_w_gate, moe_w_up, moe_w_down, lnf_g):
    p = dict(ln1_g=ln1_g, w_in=w_in, hy_conv_w=hy_conv_w, hy_conv_b=hy_conv_b, hy_w1=hy_w1,
             hy_b1=hy_b1, hy_w2=hy_w2, hy_b2=hy_b2, hy_w3=hy_w3, hy_freq=hy_freq, hy_decay=hy_decay,
             hy_skip=hy_skip, attn_sink=attn_sink, rel_bias=rel_bias, ret_decay=ret_decay, w_out=w_out,
             ln2_g=ln2_g, router_group_w=router_group_w, router_group_b=router_group_b,
             router_expert_w=router_expert_w, router_expert_b=router_expert_b, moe_w_gate=moe_w_gate,
             moe_w_up=moe_w_up, moe_w_down=moe_w_down)
    bp, n, d = x_prompt.shape
    bs = x_sample.shape[0]
    assert x_sample.shape[1:] == (n, d)
    bsz = bp + bs
    consts = (_dft_matrices(n), _hyena_embedding(n))
    xs = (x_prompt.reshape(bp * n, d), x_sample.reshape(bs * n, d))
    for l in range(DEPTH):
        xs = _layer(xs, bsz, p, l, consts, lnf_g, bp * n)
    y_prompt, y_sample = xs
    return (y_prompt.reshape(bp, n, d), y_sample.reshape(bs, n, d))
```

```python
import functools
import math

import numpy as np
import jax
import jax.numpy as jnp
from jax import lax
from jax.experimental import pallas as pl
from jax.experimental.pallas import tpu as pltpu

F32 = jnp.float32
BF16 = jnp.bfloat16
HIGHEST = lax.Precision.HIGHEST

D_MODEL = 2048
DEPTH = 2
HEAD_DIM = 64
EPS = 1e-6
HY_WIDTH = D_MODEL // 4
ATT_WIDTH = D_MODEL // 2
RET_WIDTH = D_MODEL // 4
HY_ORDER = 2
HY_EMB = 33
HY_FFN = 64
ATT_HEADS = ATT_WIDTH // HEAD_DIM
ATT_KV_HEADS = 4
ATT_GROUP = ATT_HEADS // ATT_KV_HEADS
KV_WIDTH = ATT_KV_HEADS * HEAD_DIM
WINDOW = 128
BLOCK = 128
REL_BUCKETS = 32
REL_MAX_DIST = 128
RET_HEADS = RET_WIDTH // HEAD_DIM
RET_CHUNK = 128
ROPE_BASE = 10000.0
N_GROUPS = 4
EXPERTS_PER_GROUP = 4
N_EXPERTS = N_GROUPS * EXPERTS_PER_GROUP
D_EXPERT = D_MODEL // 2
HY_IN = 3 * HY_WIDTH
ATT_IN = ATT_WIDTH + 2 * KV_WIDTH
RET_IN = 4 * RET_WIDTH

LANES = 128
SUBLANES = 8
MIB = 1024 * 1024


def _params(semantics, vmem_mib):
    return pltpu.CompilerParams(dimension_semantics=semantics, vmem_limit_bytes=vmem_mib * MIB)


def _dot(a, b):
    return jnp.dot(a, b, preferred_element_type=F32)


def _dot_nt(a, b):
    return lax.dot_general(a, b, (((1,), (1,)), ((), ())), preferred_element_type=F32)


def _norm_proj_kernel(x_ref, g_ref, w_ref, o_ref):
    x = x_ref[...]
    inv = lax.rsqrt(jnp.mean(x * x, axis=-1, keepdims=True) + EPS)
    h = (x * inv * g_ref[...]).astype(BF16)
    o_ref[...] = _dot(h, w_ref[...]).astype(o_ref.dtype)


def _norm_proj(x, g, w, out_dtype, tm=1024):
    t, d = x.shape
    n = w.shape[1]
    return pl.pallas_call(
        _norm_proj_kernel,
        grid=(t // tm,),
        in_specs=[
            pl.BlockSpec((tm, d), lambda i: (i, 0)),
            pl.BlockSpec((1, d), lambda i: (0, 0)),
            pl.BlockSpec((d, n), lambda i: (0, 0), pipeline_mode=pl.Buffered(1)),
        ],
        out_specs=pl.BlockSpec((tm, n), lambda i: (i, 0)),
        out_shape=jax.ShapeDtypeStruct((t, n), out_dtype),
        compiler_params=_params(("parallel",), 56),
        name="norm_proj",
    )(x, g.reshape(1, d), w)


def _norm_proj_stack_kernel(tiles_a, xa_ref, xb_ref, g_ref, w_ref, o_ref, x_ref):
    x = jnp.where(pl.program_id(0) < tiles_a, xa_ref[...], xb_ref[...])
    x_ref[...] = x
    inv = lax.rsqrt(jnp.mean(x * x, axis=-1, keepdims=True) + EPS)
    h = (x * inv * g_ref[...]).astype(BF16)
    o_ref[...] = _dot(h, w_ref[...]).astype(o_ref.dtype)


def _norm_proj_stack(xa, xb, g, w, out_dtype, tm=512):
    ta, d = xa.shape
    tb = xb.shape[0]
    n = w.shape[1]
    tiles_a = ta // tm
    return pl.pallas_call(
        functools.partial(_norm_proj_stack_kernel, tiles_a),
        grid=((ta + tb) // tm,),
        in_specs=[
            pl.BlockSpec((tm, d), lambda i: (jnp.minimum(i, tiles_a - 1), 0)),
            pl.BlockSpec((tm, d), lambda i: (jnp.maximum(i - tiles_a, 0), 0)),
            pl.BlockSpec((1, d), lambda i: (0, 0)),
            pl.BlockSpec((d, n), lambda i: (0, 0)),
        ],
        out_specs=[pl.BlockSpec((tm, n), lambda i: (i, 0)), pl.BlockSpec((tm, d), lambda i: (i, 0))],
        out_shape=[jax.ShapeDtypeStruct((ta + tb, n), out_dtype),
                   jax.ShapeDtypeStruct((ta + tb, d), xa.dtype)],
        compiler_params=_params(("parallel",), 56),
        name="norm_proj_stack",
    )(xa, xb, g.reshape(1, d), w)


def _short_conv_split(x_e, x_o, prev_odd, next_even, w_ref, b_ref):
    n = x_e.shape[0]
    row = lax.broadcasted_iota(jnp.int32, x_e.shape, 0)
    before_e = jnp.where(row == 0, prev_odd, pltpu.roll(x_o, 1, 0))
    after_o = jnp.where(row == n - 1, next_even, pltpu.roll(x_e, n - 1, 0))
    w0, w1, w2, bias = w_ref[0:1, :], w_ref[1:2, :], w_ref[2:3, :], b_ref[...]
    return before_e * w0 + x_e * w1 + x_o * w2 + bias, x_e * w0 + x_o * w1 + after_o * w2 + bias


def _hy_filter_kernel(z_ref, w1_ref, b1_ref, w2_ref, b2_ref, fr_ref, w3f_ref, w3b_ref,
                      decf_ref, decb_ref, hs_ref, hd_ref):
    fr = fr_ref[...]
    h = jnp.sin(fr * (jnp.dot(z_ref[...], w1_ref[...], precision=HIGHEST,
                              preferred_element_type=F32) + b1_ref[...]))
    h = jnp.sin(fr * (jnp.dot(h, w2_ref[...], precision=HIGHEST,
                              preferred_element_type=F32) + b2_ref[...]))
    n = h.shape[0]
    shape = (n, w3f_ref.shape[1])
    row = lax.broadcasted_iota(jnp.int32, shape, 0)
    t = row.astype(F32) * (1.0 / (n - 1))
    hf = jnp.dot(h, w3f_ref[...], precision=HIGHEST, preferred_element_type=F32)
    hf = hf * jnp.exp(-t * jnp.abs(decf_ref[...]))
    hb = jnp.dot(h, w3b_ref[...], precision=HIGHEST, preferred_element_type=F32)
    hb = hb * jnp.exp(-t * jnp.abs(decb_ref[...]))
    hb = jnp.where(row == 0, 0.0, hb)
    norm = (jnp.sum(jnp.abs(hf), axis=0, keepdims=True)
            + jnp.sum(jnp.abs(hb), axis=0, keepdims=True))
    hs_ref[...] = (hf + hb) / norm
    hd_ref[...] = (hb - hf) / norm


def _hy_filter(z, w1, b1, w2, b2, fr, w3, decay, tc=256):
    n = z.shape[0]
    cols = HY_ORDER * HY_WIDTH
    w3f, w3b = w3[:, :cols], w3[:, cols:]
    dec = decay.reshape(2, cols)
    zp = jnp.pad(z, ((0, 0), (0, LANES - HY_EMB)))
    w1p = jnp.pad(w1, ((0, LANES - HY_EMB), (0, 0)))
    full = lambda shape: pl.BlockSpec(shape, lambda j: (0, 0))
    col = lambda rows: pl.BlockSpec((rows, tc), lambda j: (0, j))
    return pl.pallas_call(
        _hy_filter_kernel,
        grid=(cols // tc,),
        in_specs=[full((n, LANES)), full((LANES, HY_FFN)), full((1, HY_FFN)), full((HY_FFN, HY_FFN)),
                  full((1, HY_FFN)), full((1, HY_FFN)), col(HY_FFN), col(HY_FFN), col(1), col(1)],
        out_specs=[col(n), col(n)],
        out_shape=[jax.ShapeDtypeStruct((n, cols), F32)] * 2,
        compiler_params=_params(("parallel",), 48),
        name="hy_filter",
    )(zp, w1p, b1.reshape(1, -1), w2, b2.reshape(1, -1), fr.reshape(1, -1), w3f, w3b,
      dec[0:1], dec[1:2])


def _alternating_sum(x):
    row = lax.broadcasted_iota(jnp.int32, x.shape, 0)
    return jnp.sum(x * (1 - 2 * (row & 1)).astype(F32), axis=0, keepdims=True)


def _half_spectra(fw_ref, rows, x_e, x_o, y_e, y_o):
    ec, oc = _dot(fw_ref[0, rows, :], x_e), _dot(fw_ref[1, rows, :], x_o)
    es, os_ = _dot(fw_ref[2, rows, :], y_e), _dot(fw_ref[3, rows, :], y_o)
    return ec + oc, ec - oc, es + os_, os_ - es


def _time_pairs(x):
    return x.reshape(x.shape[:-2] + (x.shape[-2] // 2, 2 * x.shape[-1]))


def _hy_kf_kernel(fw_ref, hse_ref, hso_ref, hde_ref, hdo_ref, k_ref):
    hs_e, hd_o = hse_ref[...], hdo_ref[...]
    k0, k1, k2, k3 = _half_spectra(fw_ref, slice(None), hs_e.astype(BF16), hso_ref[...].astype(BF16),
                                   hde_ref[...].astype(BF16), hd_o.astype(BF16))
    first = jnp.logical_and(lax.broadcasted_iota(jnp.int32, k0.shape, 0) == 0, pl.program_id(0) == 0)
    k_ref[0] = k0
    k_ref[1] = k1
    k_ref[2] = jnp.where(first, _alternating_sum(hs_e), k2)
    k_ref[3] = jnp.where(first, _alternating_sum(hd_o), k3)


def _hy_kf(fw, hs, hd, tf=256, tc=512):
    n, cols = hs.shape
    h = n // 2
    odd = cols // tc
    even_spec = pl.BlockSpec((h, tc), lambda i, j: (0, j))
    odd_spec = pl.BlockSpec((h, tc), lambda i, j: (0, odd + j))
    hs2, hd2 = _time_pairs(hs), _time_pairs(hd)
    return pl.pallas_call(
        _hy_kf_kernel,
        grid=(h // tf, cols // tc),
        in_specs=[pl.BlockSpec((4, tf, h), lambda i, j: (0, i, 0)),
                  even_spec, odd_spec, even_spec, odd_spec],
        out_specs=pl.BlockSpec((4, tf, tc), lambda i, j: (0, i, j)),
        out_shape=jax.ShapeDtypeStruct((4, h, cols), F32),
        compiler_params=_params(("parallel", "parallel"), 48),
        name="hy_filter_dft",
    )(fw, hs2, hs2, hd2, hd2)


HY_FREQ_TILE = 256


def _hy_fwd_kernel(conv, fw_ref, ue_ref, uo_ref, k_ref, *rest):
    half = fw_ref.shape[1]
    u_e, u_o = ue_ref[0], uo_ref[0]
    if conv:
        w_ref, b_ref, y_ref = rest
        u_e, u_o = _short_conv_split(u_e, u_o, 0.0, 0.0, w_ref, b_ref)
    else:
        y_ref, = rest
    mid_a, mid_b = _alternating_sum(u_e), _alternating_sum(u_o)
    u_e, u_o = u_e.astype(BF16), u_o.astype(BF16)
    tf = HY_FREQ_TILE
    for i in range(half // tf):
        rows = slice(i * tf, (i + 1) * tf)
        p0, p1, p2, p3 = _half_spectra(fw_ref, rows, u_e, u_o, u_e, u_o)
        k0, k1, k2, k3 = (k_ref[j, rows, :] for j in range(4))
        y0, y2 = p0 * k0 + p2 * k2, p0 * k2 - p2 * k0
        y1, y3 = p1 * k1 + p3 * k3, p1 * k3 - p3 * k1
        he, ho = y2 - y3, y2 + y3
        if i == 0:
            first = lax.broadcasted_iota(jnp.int32, p0.shape, 0) == 0
            y0 = jnp.where(first, p0 * k0, y0)
            y1 = jnp.where(first, p1 * k1, y1)
            he = jnp.where(first, mid_a * k2 + mid_b * k3, he)
            ho = jnp.where(first, mid_a * k3 - mid_b * k2, ho)
        y_ref[0, 0, rows, :] = (y0 + y1).astype(BF16)
        y_ref[0, 1, rows, :] = he.astype(BF16)
        y_ref[0, 2, rows, :] = (y0 - y1).astype(BF16)
        y_ref[0, 3, rows, :] = ho.astype(BF16)


def _hy_fwd(fw, u3, ucol, k, kcol, conv=None):
    bsz, n, width = u3.shape
    c = HY_WIDTH
    h = n // 2
    odd = width // c
    u2 = _time_pairs(u3)
    in_specs = [
        pl.BlockSpec((4, h, h), lambda b: (0, 0, 0), pipeline_mode=pl.Buffered(1)),
        pl.BlockSpec((1, h, c), lambda b: (b, 0, ucol)),
        pl.BlockSpec((1, h, c), lambda b: (b, 0, odd + ucol)),
        pl.BlockSpec((4, h, c), lambda b: (0, 0, kcol), pipeline_mode=pl.Buffered(1)),
    ]
    args = [fw, u2, u2, k]
    if conv is not None:
        in_specs += [pl.BlockSpec((3, c), lambda b: (0, ucol)), pl.BlockSpec((1, c), lambda b: (0, ucol))]
        args += list(conv)
    return pl.pallas_call(
        functools.partial(_hy_fwd_kernel, conv is not None),
        grid=(bsz,),
        in_specs=in_specs,
        out_specs=pl.BlockSpec((1, 4, h, c), lambda b: (b, 0, 0, 0)),
        out_shape=jax.ShapeDtypeStruct((bsz, 4, h, c), BF16),
        compiler_params=_params(("parallel",), 52),
        name="hy_dft_fwd",
    )(*args)


def _hy_inv_kernel(s_conv, iv_ref, y_ref, d_ref, *rest):
    o_ref = rest[-1]
    i = pl.program_id(0)
    first, last = i == 0, i == pl.num_programs(0) - 1

    def conv_tile(e_ref, o_ref_, lo_ref, hi_ref, w_ref, b_ref):
        prev_odd = jnp.where(first, 0.0, lo_ref[0, SUBLANES - 1:SUBLANES, :])
        next_even = jnp.where(last, 0.0, hi_ref[0, 0:1, :])
        return _short_conv_split(e_ref[0], o_ref_[0], prev_odd, next_even, w_ref, b_ref)

    if s_conv:
        skip_e, skip_o = conv_tile(*rest[0:6])
        gate_e, gate_o = conv_tile(*rest[6:12])
    else:
        skip_e, skip_o = rest[0][0], rest[1][0]
        gate_e, gate_o = conv_tile(*rest[2:8])
    c = gate_e.shape[1]
    d = d_ref[...]
    y_e = _dot(iv_ref[0], y_ref[0, 0]) + _dot(iv_ref[1], y_ref[0, 1])
    y_o = _dot(iv_ref[2], y_ref[0, 2]) + _dot(iv_ref[3], y_ref[0, 3])
    o_ref[0, :, :c] = (gate_e * (y_e + skip_e * d)).astype(o_ref.dtype)
    o_ref[0, :, c:] = (gate_o * (y_o + skip_o * d)).astype(o_ref.dtype)


def _hy_inv(iv, y, d, raw3, conv, gcol, out_dtype, scol=None, s3=None, th=512):
    bsz, _, h, c = y.shape
    conv_w, conv_b = conv
    halo = th // SUBLANES
    raw2 = _time_pairs(raw3)
    odd = raw3.shape[2] // c

    def pair_specs(arr_odd, col):
        return [pl.BlockSpec((1, th, c), lambda i, b: (b, i, col)),
                pl.BlockSpec((1, th, c), lambda i, b: (b, i, arr_odd + col))]

    def conv_in(col):
        specs = pair_specs(odd, col) + [
            pl.BlockSpec((1, SUBLANES, c), lambda i, b: (b, jnp.maximum(i * halo - 1, 0), odd + col)),
            pl.BlockSpec((1, SUBLANES, c),
                         lambda i, b: (b, jnp.minimum((i + 1) * halo, h // SUBLANES - 1), col)),
            pl.BlockSpec((3, c), lambda i, b: (0, col)), pl.BlockSpec((1, c), lambda i, b: (0, col))]
        return specs, [raw2, raw2, raw2, raw2, conv_w, conv_b]

    in_specs = [pl.BlockSpec((4, th, h), lambda i, b: (0, i, 0)),
                pl.BlockSpec((1, 4, h, c), lambda i, b: (b, 0, 0, 0)),
                pl.BlockSpec((1, c), lambda i, b: (0, 0))]
    args = [iv, y, d.reshape(1, c)]
    if s3 is None:
        specs, arrs = conv_in(scol)
    else:
        s2 = _time_pairs(s3)
        specs, arrs = pair_specs(1, 0), [s2, s2]
    in_specs += specs
    args += arrs
    specs, arrs = conv_in(gcol)
    in_specs += specs
    args += arrs
    out = pl.pallas_call(
        functools.partial(_hy_inv_kernel, s3 is None),
        grid=(h // th, bsz),
        in_specs=in_specs,
        out_specs=pl.BlockSpec((1, th, 2 * c), lambda i, b: (b, i, 0)),
        out_shape=jax.ShapeDtypeStruct((bsz, h, 2 * c), out_dtype),
        compiler_params=_params(("parallel", "parallel"), 48),
        name="hy_dft_inv",
    )(*args)
    return out.reshape(bsz, 2 * h, c)


DFT_SPLIT = 32


def _dft_matrices(n):
    h = n // 2

    def cos_sin(offset):
        def table(rows, step):
            f = lax.broadcasted_iota(jnp.int32, (rows, h), 0) * step
            t = 2 * lax.broadcasted_iota(jnp.int32, (rows, h), 1) + offset
            ang = ((f * t) & (2 * n - 1)).astype(F32) * (math.pi / n)
            return jnp.cos(ang), jnp.sin(ang)
        cos_hi, sin_hi = (x[:, None, :] for x in table(h // DFT_SPLIT, DFT_SPLIT))
        cos_lo, sin_lo = (x[None, :, :] for x in table(DFT_SPLIT, 1))
        return ((cos_hi * cos_lo - sin_hi * sin_lo).reshape(h, h),
                (sin_hi * cos_lo + cos_hi * sin_lo).reshape(h, h))

    cos_e, sin_e = cos_sin(0)
    cos_o, sin_o = cos_sin(1)
    fwd = jnp.stack([cos_e, cos_o, sin_e, sin_o]).astype(BF16)
    m = lax.broadcasted_iota(jnp.int32, (h, h), 0)
    f = lax.broadcasted_iota(jnp.int32, (h, h), 1)
    weight = jnp.where(f == 0, 0.5 / n, 1.0 / n)
    mid = (1 - 2 * (m & 1)).astype(F32) * (1.0 / n)
    inv = jnp.stack([weight * cos_e.T, jnp.where(f == 0, mid, sin_e.T * (-1.0 / n)),
                     weight * cos_o.T, jnp.where(f == 0, -mid, sin_o.T * (-1.0 / n))]).astype(BF16)
    return fwd, inv


def _hyena_embedding(n):
    t = jnp.linspace(0.0, 1.0, n, dtype=F32)[:, None]
    bands = (HY_EMB - 1) // 2
    f = jnp.linspace(1e-4, bands - 1, bands, dtype=F32)[None]
    w = 2.0 * math.pi * jnp.arange(n, dtype=F32)[:, None] / n
    return jnp.concatenate([t, jnp.cos(f * w), -jnp.sin(f * w)], -1)


def _hyena(p3, dft, emb, conv_w, conv_b, w1, b1, w2, b2, w3, freq, decay, skip):
    ff, fi = dft
    conv = (conv_w, conv_b.reshape(1, -1))
    hs, hd = _hy_filter(emb, w1, b1, w2, b2, freq, w3, decay)
    k = _hy_kf(ff, hs, hd)
    y = _hy_fwd(ff, p3, 0, k, 0, conv)
    z = _hy_inv(fi, y, skip[0], p3, conv, 1, F32, scol=0)
    y = _hy_fwd(ff, z, 0, k, 1)
    return _hy_inv(fi, y, skip[1], p3, conv, 2, BF16, s3=z)


def _t5_bucket(rel):
    nb = REL_BUCKETS // 2
    max_exact = nb // 2
    n = np.abs(rel)
    large = max_exact + (np.log(np.maximum(n, 1) / max_exact) / np.log(REL_MAX_DIST / max_exact)
                         * (nb - max_exact)).astype(np.int32)
    large = np.minimum(large, nb - 1)
    return (rel > 0).astype(np.int32) * nb + np.where(n < max_exact, n, large)


def _attn_kernel(q_ref, kp_ref, kc_ref, kn_ref, vp_ref, vc_ref, vn_ref, bias_ref, sink_ref, o_ref):
    n = pl.program_id(1)
    q = q_ref[0] * (HEAD_DIM ** -0.5)
    k = jnp.concatenate([kp_ref[0], kc_ref[0], kn_ref[0]], axis=0)
    vt = jnp.concatenate([vp_ref[0], vc_ref[0], vn_ref[0]], axis=1)
    edge_prev = jnp.where(n == 0, -1e30, 0.0).astype(F32)
    edge_next = jnp.where(n == pl.num_programs(1) - 1, -1e30, 0.0).astype(F32)
    ones = jnp.ones((HEAD_DIM, 3 * BLOCK), BF16)
    for kh in range(ATT_KV_HEADS):
        k_h = k[:, kh * HEAD_DIM:(kh + 1) * HEAD_DIM]
        vt_h = jnp.concatenate([vt[kh * HEAD_DIM:(kh + 1) * HEAD_DIM], ones], axis=0)
        q_h = jnp.concatenate([q[:, (kh * ATT_GROUP + g) * HEAD_DIM:(kh * ATT_GROUP + g + 1) * HEAD_DIM]
                               for g in range(ATT_GROUP)], axis=0)
        s = _dot_nt(k_h, q_h) + bias_ref[kh]
        s = jnp.concatenate([s[:BLOCK] + edge_prev, s[BLOCK:2 * BLOCK], s[2 * BLOCK:] + edge_next],
                            axis=0)
        sink = sink_ref[kh]
        m = jnp.maximum(jnp.max(s, axis=0, keepdims=True), sink)
        p = jnp.exp(s - m).astype(BF16)
        o = _dot(vt_h, p)
        o = o[:HEAD_DIM] / (o[HEAD_DIM:HEAD_DIM + 1] + jnp.exp(sink - m))
        for g in range(0, ATT_GROUP, 2):
            pair = jnp.concatenate([o[:, g * BLOCK:(g + 1) * BLOCK],
                                    o[:, (g + 1) * BLOCK:(g + 2) * BLOCK]], axis=0)
            c0 = (kh * ATT_GROUP + g) * HEAD_DIM
            o_ref[0, :, c0:c0 + 2 * HEAD_DIM] = pair.T.astype(o_ref.dtype)


def _attention_bias(rel_bias):
    rel = (np.arange(3 * BLOCK)[None, :] - BLOCK) - np.arange(BLOCK)[:, None]
    bucket = jnp.asarray(_t5_bucket(rel).reshape(-1), jnp.int32)
    onehot = (bucket[:, None] == jnp.arange(REL_BUCKETS, dtype=jnp.int32)[None]).astype(F32)
    bias = jnp.dot(onehot, rel_bias.astype(F32), precision=HIGHEST)
    bias = bias.reshape(BLOCK, 3 * BLOCK, ATT_KV_HEADS, ATT_GROUP)
    bias = jnp.where(jnp.asarray(np.abs(rel) <= WINDOW)[:, :, None, None], bias, -1e30)
    return jnp.transpose(bias, (2, 1, 3, 0)).reshape(ATT_KV_HEADS, 3 * BLOCK, ATT_GROUP * BLOCK)


def _attention(att3, sink, rel_bias):
    bsz, n, _ = att3.shape
    nb = n // BLOCK
    bias = _attention_bias(rel_bias)
    sink_row = jnp.repeat(sink.astype(F32).reshape(ATT_KV_HEADS, 1, ATT_GROUP), BLOCK, axis=2)
    vt = jnp.swapaxes(att3[:, :, ATT_WIDTH + KV_WIDTH:], 1, 2)
    kcol = ATT_WIDTH // KV_WIDTH
    prev = lambda i: jnp.maximum(i - 1, 0)
    nxt = lambda i: jnp.minimum(i + 1, nb - 1)
    cur = lambda i: i
    k_spec = lambda blk: pl.BlockSpec((1, BLOCK, KV_WIDTH), lambda b, i: (b, blk(i), kcol))
    v_spec = lambda blk: pl.BlockSpec((1, KV_WIDTH, BLOCK), lambda b, i: (b, 0, blk(i)))
    return pl.pallas_call(
        _attn_kernel,
        grid=(bsz, nb),
        in_specs=[
            pl.BlockSpec((1, BLOCK, ATT_WIDTH), lambda b, i: (b, i, 0)),
            k_spec(prev), k_spec(cur), k_spec(nxt), v_spec(prev), v_spec(cur), v_spec(nxt),
            pl.BlockSpec((ATT_KV_HEADS, 3 * BLOCK, ATT_GROUP * BLOCK), lambda b, i: (0, 0, 0)),
            pl.BlockSpec((ATT_KV_HEADS, 1, ATT_GROUP * BLOCK), lambda b, i: (0, 0, 0)),
        ],
        out_specs=pl.BlockSpec((1, BLOCK, ATT_WIDTH), lambda b, i: (b, i, 0)),
        out_shape=jax.ShapeDtypeStruct((bsz, n, ATT_WIDTH), BF16),
        compiler_params=_params(("parallel", "parallel"), 32),
        name="window_attention",
    )(att3, att3, att3, att3, vt, vt, vt, bias, sink_row)


def _rotary_tables(n):
    inv = ROPE_BASE ** -jnp.linspace(0.0, 1.0, HEAD_DIM // 2, dtype=F32)
    ang = jnp.arange(n, dtype=F32)[:, None] * inv[None]
    cos, sin = jnp.cos(ang), jnp.sin(ang)
    cos_t = jnp.tile(jnp.concatenate([cos, cos], -1), (1, RET_HEADS))
    sin_t = jnp.tile(jnp.concatenate([-sin, sin], -1), (1, RET_HEADS))
    return cos_t, sin_t


def _rotate(x, cos_t, sin_t):
    width = x.shape[-1]
    half = HEAD_DIM // 2
    lane = lax.broadcasted_iota(jnp.int32, x.shape, 1)
    partner = jnp.where((lane & (HEAD_DIM - 1)) < half,
                        pltpu.roll(x, width - half, 1), pltpu.roll(x, half, 1))
    return x * cos_t + partner * sin_t


RET_PAIR = 2 * HEAD_DIM
RET_PAIRS = RET_WIDTH // RET_PAIR


def _same_head(shape):
    r = lax.broadcasted_iota(jnp.int32, shape, 0) // HEAD_DIM
    c = lax.broadcasted_iota(jnp.int32, shape, 1) // HEAD_DIM
    return r == c


RET_STATE_CHUNKS = 4
RET_OUT_CHUNKS = 2


def _ret_state_kernel(kf_ref, vf_ref, kb_ref, vb_ref, cf_ref, sf_ref, cb_ref, sb_ref, lg_ref,
                      of_ref, ob_ref, stf_ref, stb_ref):
    @pl.when(pl.program_id(1) == 0)
    def _():
        stf_ref[...] = jnp.zeros_like(stf_ref)
        stb_ref[...] = jnp.zeros_like(stb_ref)

    c = RET_CHUNK
    j = lax.broadcasted_iota(jnp.int32, (c, RET_WIDTH), 0).astype(F32)
    lgf = lg_ref[0:1, :]
    lgb = lg_ref[1:2, :]
    scale = HEAD_DIM ** -0.5
    same = _same_head((RET_PAIR, RET_PAIR))

    def scan(st_ref, o_ref, k_ref, v_ref, cos_ref, sin_ref, zeta, lg, order):
        grow = jnp.exp(c * lg)
        for sub in order:
            rows = slice(sub * c, (sub + 1) * c)
            o_ref[0, sub] = st_ref[...].astype(o_ref.dtype)
            k = (_rotate(k_ref[0, rows, :], cos_ref[rows, :], sin_ref[rows, :]) * scale * zeta).astype(BF16)
            v = v_ref[0, rows, :].astype(BF16)
            for p in range(RET_PAIRS):
                lanes = slice(p * RET_PAIR, (p + 1) * RET_PAIR)
                kv = lax.dot_general(k[:, lanes], v[:, lanes], (((0,), (0,)), ((), ())),
                                     preferred_element_type=F32)
                st_ref[p] = grow[:, lanes] * st_ref[p] + jnp.where(same, kv, 0.0)

    chunks = range(RET_STATE_CHUNKS)
    scan(stf_ref, of_ref, kf_ref, vf_ref, cf_ref, sf_ref, jnp.exp((c - 1.0 - j) * lgf), lgf, chunks)
    scan(stb_ref, ob_ref, kb_ref, vb_ref, cb_ref, sb_ref, jnp.exp(j * lgb), lgb, reversed(chunks))


def _ret_states(ret3, cos_t, sin_t, lg_lane):
    bsz, n, _ = ret3.shape
    nc = n // RET_CHUNK
    rows, w = RET_STATE_CHUNKS * RET_CHUNK, RET_WIDTH
    steps = n // rows
    fwd = lambda b, i: i
    bwd = lambda b, i: steps - 1 - i
    col = lambda pos, cb: pl.BlockSpec((1, rows, w), lambda b, i: (b, pos(b, i), cb))
    tab = lambda pos: pl.BlockSpec((rows, w), lambda b, i: (pos(b, i), 0))
    st_shape = (RET_PAIRS, RET_PAIR, RET_PAIR)
    out = lambda pos: pl.BlockSpec((1, RET_STATE_CHUNKS) + st_shape, lambda b, i: (b, pos(b, i), 0, 0, 0))
    return pl.pallas_call(
        _ret_state_kernel,
        grid=(bsz, steps),
        in_specs=[col(fwd, 1), col(fwd, 2), col(bwd, 1), col(bwd, 2),
                  tab(fwd), tab(fwd), tab(bwd), tab(bwd),
                  pl.BlockSpec((2, w), lambda b, i: (0, 0))],
        out_specs=[out(fwd), out(bwd)],
        out_shape=[jax.ShapeDtypeStruct((bsz, nc) + st_shape, BF16)] * 2,
        scratch_shapes=[pltpu.VMEM(st_shape, F32), pltpu.VMEM(st_shape, F32)],
        compiler_params=_params(("parallel", "arbitrary"), 32),
        name="retention_states",
    )(ret3, ret3, ret3, ret3, cos_t, sin_t, cos_t, sin_t, lg_lane)


def _ret_out_kernel(q_ref, k_ref, v_ref, g_ref, cos_ref, sin_ref, lg_ref, lgs_ref, stf_ref, stb_ref,
                    o_ref):
    c, w = RET_CHUNK, RET_WIDTH
    pos = lax.broadcasted_iota(jnp.int32, (c, w), 0).astype(F32)
    xi_f = jnp.exp((pos + 1.0) * lg_ref[0:1, :])
    xi_b = jnp.exp((c - pos) * lg_ref[1:2, :])
    diff = (lax.broadcasted_iota(jnp.int32, (c, c), 0)
            - lax.broadcasted_iota(jnp.int32, (c, c), 1)).astype(F32)
    first = lax.broadcasted_iota(jnp.int32, (c, RET_PAIR), 1) < HEAD_DIM
    avg = jnp.where(_same_head((RET_PAIR, RET_PAIR)), 1.0 / HEAD_DIM, 0.0).astype(BF16)
    decays = [jnp.concatenate(
        [jnp.exp(jnp.where(diff >= 0, diff * lgs_ref[0, h], -diff * lgs_ref[1, h]))
         for h in (2 * p, 2 * p + 1)], axis=0) for p in range(RET_PAIRS)]
    for sub in range(RET_OUT_CHUNKS):
        rows = slice(sub * c, (sub + 1) * c)
        cos_t, sin_t = cos_ref[rows, :], sin_ref[rows, :]
        q = _rotate(q_ref[0, rows, :], cos_t, sin_t)
        k = (_rotate(k_ref[0, rows, :], cos_t, sin_t) * (HEAD_DIM ** -0.5)).astype(BF16)
        v = v_ref[0, rows, :].astype(BF16)
        g = g_ref[0, rows, :]
        q_f = (q * xi_f).astype(BF16)
        q_b = (q * xi_b).astype(BF16)
        for p in range(RET_PAIRS):
            lanes = slice(p * RET_PAIR, (p + 1) * RET_PAIR)
            q_p = q[:, lanes]
            qs = jnp.concatenate([jnp.where(first, q_p, 0.0), jnp.where(first, 0.0, q_p)], axis=0)
            s = _dot_nt(qs.astype(BF16), k[:, lanes])
            o2 = _dot((s * decays[p]).astype(BF16), v[:, lanes])
            y = jnp.where(first, o2[:c], o2[c:])
            y = y + _dot(q_f[:, lanes], stf_ref[0, sub, p]) + _dot(q_b[:, lanes], stb_ref[0, sub, p])
            sq = y * y
            sq_hi = sq.astype(BF16)
            sq_lo = (sq - sq_hi.astype(F32)).astype(BF16)
            ms = _dot(sq_hi, avg) + _dot(sq_lo, avg)
            y = y * lax.rsqrt(ms + EPS)
            g_p = g[:, lanes]
            o_ref[0, rows, lanes] = (g_p / (1.0 + jnp.exp(-g_p)) * y).astype(o_ref.dtype)


def _retention(ret3, decay):
    bsz, n, _ = ret3.shape
    rows, w = RET_OUT_CHUNKS * RET_CHUNK, RET_WIDTH
    lg = -jnp.exp(decay.astype(F32))
    lg_lane = jnp.repeat(lg, HEAD_DIM, axis=1)
    cos_t, sin_t = _rotary_tables(n)
    stf, stb = _ret_states(ret3, cos_t, sin_t, lg_lane)
    col = lambda cb: pl.BlockSpec((1, rows, w), lambda b, i: (b, i, cb))
    tab = pl.BlockSpec((rows, w), lambda b, i: (i, 0))
    st = pl.BlockSpec((1, RET_OUT_CHUNKS, RET_PAIRS, RET_PAIR, RET_PAIR), lambda b, i: (b, i, 0, 0, 0))
    return pl.pallas_call(
        _ret_out_kernel,
        grid=(bsz, n // rows),
        in_specs=[col(0), col(1), col(2), col(3), tab, tab,
                  pl.BlockSpec((2, w), lambda b, i: (0, 0)),
                  pl.BlockSpec(memory_space=pltpu.SMEM), st, st],
        out_specs=pl.BlockSpec((1, rows, w), lambda b, i: (b, i, 0)),
        out_shape=jax.ShapeDtypeStruct((bsz, n, w), BF16),
        compiler_params=_params(("parallel", "parallel"), 32),
        name="retention_out",
    )(ret3, ret3, ret3, ret3, cos_t, sin_t, lg_lane, lg, stf, stb)


ROUTE_I0, ROUTE_I1, ROUTE_W0, ROUTE_W1, ROUTE_R0, ROUTE_R1 = 0, 1, 2, 3, 4, 5


def _mix_router_kernel(x_ref, yh_ref, ya_ref, yr_ref, w_ref, g_ref, wr_ref, br_ref,
                       xo_ref, hp_ref, route_ref, counts_ref, carry_ref, logits_ref):
    i = pl.program_id(0)

    @pl.when(i == 0)
    def _():
        carry_ref[...] = jnp.zeros_like(carry_ref)
        logits_ref[...] = jnp.zeros_like(logits_ref)

    logits = logits_ref[(i + 1) % 2]
    live = jnp.where(i >= 1, 1.0, 0.0).astype(F32)
    lane = lax.broadcasted_iota(jnp.int32, logits.shape, 1)
    big = jnp.int32(LANES)
    neg = -jnp.inf
    is_group = (lane >= N_EXPERTS) & (lane < N_EXPERTS + N_GROUPS)
    gl = jnp.where(is_group, logits, neg)
    gmax = jnp.max(gl, axis=-1, keepdims=True)
    g_w = 1.0 / jnp.sum(jnp.exp(gl - gmax), axis=-1, keepdims=True)
    g_sel = jnp.min(jnp.where(gl == gmax, lane, big), axis=-1, keepdims=True) - N_EXPERTS
    lo = g_sel * EXPERTS_PER_GROUP
    el = jnp.where((lane >= lo) & (lane < lo + EXPERTS_PER_GROUP), logits, neg)
    v0 = jnp.max(el, axis=-1, keepdims=True)
    i0 = jnp.min(jnp.where(el == v0, lane, big), axis=-1, keepdims=True)
    el = jnp.where(lane == i0, neg, el)
    v1 = jnp.max(el, axis=-1, keepdims=True)
    i1 = jnp.min(jnp.where(el == v1, lane, big), axis=-1, keepdims=True)
    e1 = jnp.exp(v1 - v0)
    w0 = g_w / (1.0 + e1)
    oh0 = jnp.where(lane == i0, 1.0, 0.0)
    oh1 = jnp.where(lane == i1, 1.0, 0.0)
    tm = logits.shape[0]
    earlier = jnp.where(lax.broadcasted_iota(jnp.int32, (tm, tm), 1)
                        < lax.broadcasted_iota(jnp.int32, (tm, tm), 0), 1.0, 0.0).astype(BF16)
    carry = carry_ref[...]
    tot0 = jnp.sum(oh0, axis=0, keepdims=True)
    before0 = _dot(earlier, oh0.astype(BF16)) + carry
    before1 = _dot(earlier, oh1.astype(BF16)) + (carry + tot0)
    r0 = jnp.sum(oh0 * before0, axis=-1, keepdims=True)
    r1 = jnp.sum(oh1 * before1, axis=-1, keepdims=True)
    carry = carry + live * (tot0 + jnp.sum(oh1, axis=0, keepdims=True))
    carry_ref[...] = carry
    counts_ref[...] = jnp.broadcast_to(carry, counts_ref.shape)
    route_ref[...] = (jnp.where(lane == ROUTE_I0, i0.astype(F32), 0.0)
                      + jnp.where(lane == ROUTE_I1, i1.astype(F32), 0.0)
                      + jnp.where(lane == ROUTE_W0, w0, 0.0)
                      + jnp.where(lane == ROUTE_W1, w0 * e1, 0.0)
                      + jnp.where(lane == ROUTE_R0, r0, 0.0)
                      + jnp.where(lane == ROUTE_R1, r1, 0.0))

    mix = jnp.concatenate([yh_ref[...], ya_ref[...], yr_ref[...]], axis=1)
    x = x_ref[...] + _dot(mix, w_ref[...])
    xo_ref[...] = x
    h = x * lax.rsqrt(jnp.mean(x * x, axis=-1, keepdims=True) + EPS) * g_ref[...]
    h_hi = h.astype(BF16)
    bits = lax.bitcast_convert_type(h_hi.astype(F32), jnp.uint32)
    half = bits.shape[1] // 2
    hp_ref[...] = bits[:, :half] | (bits[:, half:] >> 16)
    h_lo = (h - h_hi.astype(F32)).astype(BF16)
    both = _dot(h_hi, wr_ref[...])
    logits_ref[i % 2] = (both[:, :LANES] + (_dot(h_lo, wr_ref[:, :LANES]) + both[:, LANES:])
                         + br_ref[...])


def _mix_router(x, yh, ya, yr, w_out, g, wr, br, tm=512):
    t, d = x.shape
    steps = t // tm
    cur = lambda i: (jnp.minimum(i, steps - 1), 0)
    row = lambda width: pl.BlockSpec((tm, width), cur)
    full = lambda shape: pl.BlockSpec(shape, lambda i: (0, 0))
    return pl.pallas_call(
        _mix_router_kernel,
        grid=(steps + 1,),
        in_specs=[row(d), row(HY_WIDTH), row(ATT_WIDTH), row(RET_WIDTH),
                  pl.BlockSpec((d, d), lambda i: (0, 0), pipeline_mode=pl.Buffered(1)),
                  full((1, d)), full((d, 2 * LANES)), full((1, LANES))],
        out_specs=[row(d), row(d // 2),
                   pl.BlockSpec((tm, LANES), lambda i: (jnp.maximum(i - 1, 0), 0)),
                   full((SUBLANES, LANES))],
        out_shape=[jax.ShapeDtypeStruct((t, d), F32), jax.ShapeDtypeStruct((t, d // 2), jnp.uint32),
                   jax.ShapeDtypeStruct((t, LANES), F32), jax.ShapeDtypeStruct((SUBLANES, LANES), F32)],
        scratch_shapes=[pltpu.VMEM((1, LANES), F32), pltpu.VMEM((2, tm, LANES), F32)],
        compiler_params=_params(("arbitrary",), 48),
        name="mix_router",
    )(x, yh, ya, yr, w_out, g.reshape(1, d), wr, br)


MOE_TM = 512
MOE_TC = 512


def _dispatch_plan(route, counts, tm):
    experts = jnp.arange(N_EXPERTS, dtype=jnp.int32)
    ids = route[:, ROUTE_I0:ROUTE_I1 + 1].astype(jnp.int32)
    rank = route[:, ROUTE_R0:ROUTE_R1 + 1].astype(jnp.int32)
    n_tiles = 2 * route.shape[0] // tm + N_EXPERTS
    count = counts[0, :N_EXPERTS].astype(jnp.int32)
    tiles = (count + tm - 1) // tm
    tile_end = jnp.cumsum(tiles)
    offs = (tile_end - tiles) * tm
    slot = jnp.sum(jnp.where(ids[..., None] == experts, offs, 0), axis=-1) + rank
    tile_expert = jnp.minimum(
        jnp.sum(jnp.arange(n_tiles, dtype=jnp.int32)[:, None] >= tile_end[None, :], axis=1),
        N_EXPERTS - 1).astype(jnp.int32)
    later = (experts[None, :] > experts[:, None]) & (tiles[None, :] > 0)
    following = jnp.min(jnp.where(later, experts[None, :], N_EXPERTS), axis=1)
    following = jnp.where(following == N_EXPERTS, -1, following).astype(jnp.int32)
    pad = (offs + count, tiles * tm - count)
    return slot, n_tiles, tile_expert, tile_end[-1:].astype(jnp.int32), following[tile_expert], pad


def _moe_dispatch_kernel(tm, pad_start_ref, pad_rows_ref, nv_ref, idx_ref, hp_ref, xs_ref, zeros_ref,
                         sem, zsem):
    tokens = hp_ref.shape[0]
    n_tiles = xs_ref.shape[0] // tm

    @pl.when(pl.program_id(0) == 0)
    def _():
        zeros_ref[...] = jnp.zeros_like(zeros_ref)
        fills = []
        for e in range(N_EXPERTS):
            rows = pad_rows_ref[e]
            end = pad_start_ref[e] + rows
            size = tm // 2
            while size >= SUBLANES:
                above = rows & ~(2 * size - 1)
                fills.append(((rows & size) != 0, pl.multiple_of(end - above - size, SUBLANES), size))
                size //= 2
            for j in range(SUBLANES - 1):
                fills.append(((rows & (SUBLANES - 1)) > j, pad_start_ref[e] + j, 1))
        for j in range(N_EXPERTS):
            fills.append((nv_ref[0] + j < n_tiles, pl.multiple_of((nv_ref[0] + j) * tm, tm), tm))
        copies = [(cond, pltpu.make_async_copy(zeros_ref.at[pl.ds(0, size)],
                                               xs_ref.at[pl.ds(start, size)], zsem))
                  for cond, start, size in fills]
        for cond, cp in copies:
            pl.when(cond)(cp.start)
        for cond, cp in copies:
            pl.when(cond)(cp.wait)

    def issue(r, carry):
        for k in range(2):
            pltpu.make_async_copy(hp_ref.at[pl.ds(r, 1)],
                                  xs_ref.at[pl.ds(idx_ref[0, 0, 2 * r + k], 1)], sem).start(priority=k)
        return carry

    lax.fori_loop(0, tokens, issue, 0, unroll=4)
    for k in range(2):
        pltpu.make_async_copy(hp_ref, xs_ref.at[pl.ds(0, tokens)], sem).wait()


def _moe_dispatch(hp, slot, pad_start, pad_rows, n_valid, n_tiles, tm, tc=MOE_TC):
    t, half = hp.shape
    steps = t // tc
    return pl.pallas_call(
        functools.partial(_moe_dispatch_kernel, tm),
        grid_spec=pltpu.PrefetchScalarGridSpec(
            num_scalar_prefetch=3,
            grid=(steps,),
            in_specs=[pl.BlockSpec((1, 1, 2 * tc), lambda i, ps, pr, nv: (i, 0, 0),
                                   memory_space=pltpu.SMEM),
                      pl.BlockSpec((tc, half), lambda i, ps, pr, nv: (i, 0))],
            out_specs=pl.BlockSpec(memory_space=pl.ANY),
            scratch_shapes=[pltpu.VMEM((tm, half), hp.dtype), pltpu.SemaphoreType.DMA(()),
                            pltpu.SemaphoreType.DMA(())],
        ),
        out_shape=jax.ShapeDtypeStruct((n_tiles * tm, half), hp.dtype),
        compiler_params=_params(("arbitrary",), 16),
        name="moe_dispatch",
    )(pad_start, pad_rows, n_valid, slot.reshape(steps, 1, 2 * tc), hp)


MOE_CAST_ROWS = 256


def _moe_gemm_kernel(layer, te_ref, nv_ref, nxt_ref, xs_ref, wg_ref, wu_ref, wd_ref, y_ref,
                     stage_g, stage_u, stage_d, wgb_ref, wub_ref, wdb_ref, sem_ref):
    i = pl.program_id(0)
    last = nv_ref[0] - 1
    tile = jnp.minimum(i, last)
    expert = te_ref[tile]
    fresh = jnp.logical_or(i == 0, expert != te_ref[jnp.maximum(tile - 1, 0)])
    pairs = ((wg_ref, stage_g, wgb_ref), (wu_ref, stage_u, wub_ref), (wd_ref, stage_d, wdb_ref))

    def copies(e):
        return [pltpu.make_async_copy(w_ref.at[layer, e], stage, sem_ref.at[n])
                for n, (w_ref, stage, _) in enumerate(pairs)]

    @pl.when(jnp.logical_and(i <= last, fresh))
    def _():
        @pl.when(i == 0)
        def _():
            for cp in copies(expert):
                cp.start()

        for cp in copies(expert):
            cp.wait()
        for _, stage, dst_ref in pairs:
            def cast(j, carry):
                rows = pl.ds(pl.multiple_of(j * MOE_CAST_ROWS, MOE_CAST_ROWS), MOE_CAST_ROWS)
                dst_ref[rows, :] = stage[rows, :].astype(BF16)
                return carry
            lax.fori_loop(0, dst_ref.shape[0] // MOE_CAST_ROWS, cast, 0)

        @pl.when(nxt_ref[tile] >= 0)
        def _():
            for cp in copies(nxt_ref[tile]):
                cp.start()

    @pl.when(i <= last)
    def _():
        xp = xs_ref[...]
        x_a = lax.bitcast_convert_type(xp & jnp.uint32(0xFFFF0000), F32).astype(BF16)
        x_b = lax.bitcast_convert_type(xp << 16, F32).astype(BF16)
        x = jnp.concatenate([x_a, x_b], axis=1)
        a = _dot(x, wgb_ref[...])
        b = _dot(x, wub_ref[...])
        act = a / (1.0 + jnp.exp(-a)) * b
        y_ref[...] = _dot(act.astype(BF16), wdb_ref[...])

    @pl.when(i > last)
    def _():
        y_ref[...] = jnp.zeros_like(y_ref)


def _moe_gemm(xs, tile_expert, n_valid, next_expert, layer, wg, wu, wd, tm=MOE_TM):
    n_slots, half = xs.shape
    _, ne, d, de = wg.shape
    tile = lambda i, te, nv, nx: jnp.minimum(i, nv[0] - 1)
    hbm = pl.BlockSpec(memory_space=pl.ANY)
    return pl.pallas_call(
        functools.partial(_moe_gemm_kernel, layer),
        grid_spec=pltpu.PrefetchScalarGridSpec(
            num_scalar_prefetch=3,
            grid=(n_slots // tm,),
            in_specs=[pl.BlockSpec((tm, half), lambda i, te, nv, nx: (tile(i, te, nv, nx), 0)),
                      hbm, hbm, hbm],
            out_specs=pl.BlockSpec((tm, d), lambda i, te, nv, nx: (i, 0)),
            scratch_shapes=[pltpu.VMEM((d, de), F32), pltpu.VMEM((d, de), F32), pltpu.VMEM((de, d), F32),
                            pltpu.VMEM((d, de), BF16), pltpu.VMEM((d, de), BF16),
                            pltpu.VMEM((de, d), BF16), pltpu.SemaphoreType.DMA((3,))],
        ),
        out_shape=jax.ShapeDtypeStruct((n_slots, d), F32),
        compiler_params=_params(("arbitrary",), 58),
        name="moe_gemm",
    )(tile_expert, n_valid, next_expert, xs, wg, wu, wd)


def _moe_combine_kernel(final_norm, steps_a, idx_ref, idx_next_ref, x_ref, route_ref, gf_ref, y_ref,
                        *rest):
    o_refs, (buf_ref, sem_ref) = rest[:-2], rest[-2:]
    i = pl.program_id(0)
    n = pl.num_programs(0)
    tokens = x_ref.shape[0]

    def start_rows(idx, slot):
        def issue(r, carry):
            for k in range(2):
                pltpu.make_async_copy(y_ref.at[pl.ds(idx[0, 0, 2 * r + k], 1)],
                                      buf_ref.at[slot, k, pl.ds(r, 1)], sem_ref.at[slot]).start(priority=k)
            return carry
        lax.fori_loop(0, tokens, issue, 0, unroll=4)

    @pl.when(i == 0)
    def _():
        start_rows(idx_ref, 0)

    @pl.when(i + 1 < n)
    def _():
        start_rows(idx_next_ref, (i + 1) % 2)

    slot = i % 2
    for k in range(2):
        pltpu.make_async_copy(y_ref.at[pl.ds(0, tokens)], buf_ref.at[slot, k], sem_ref.at[slot]).wait()
    route = route_ref[...]
    y = x_ref[...] + route[:, ROUTE_W0:ROUTE_W0 + 1] * buf_ref[slot, 0] \
        + route[:, ROUTE_W1:ROUTE_W1 + 1] * buf_ref[slot, 1]
    if final_norm:
        y = y * lax.rsqrt(jnp.mean(y * y, axis=-1, keepdims=True) + EPS) * gf_ref[...]
    if len(o_refs) == 1:
        o_refs[0][...] = y
    else:
        @pl.when(i < steps_a)
        def _():
            o_refs[0][...] = y

        @pl.when(i >= steps_a)
        def _():
            o_refs[1][...] = y


def _moe_combine(x, route, slot, y, gf, final_norm, tokens_a=None, tc=MOE_TC):
    t, d = x.shape
    steps = t // tc
    idx = slot.reshape(steps, 1, 2 * tc)
    row = lambda width: pl.BlockSpec((tc, width), lambda i: (i, 0))
    smem = lambda pos: pl.BlockSpec((1, 1, 2 * tc), lambda i: (pos(i), 0, 0), memory_space=pltpu.SMEM)
    if tokens_a is None:
        steps_a = steps
        out_specs = [row(d)]
        out_shape = [jax.ShapeDtypeStruct((t, d), F32)]
    else:
        steps_a = tokens_a // tc
        out_specs = [pl.BlockSpec((tc, d), lambda i: (jnp.minimum(i, steps_a - 1), 0)),
                     pl.BlockSpec((tc, d), lambda i: (jnp.maximum(i - steps_a, 0), 0))]
        out_shape = [jax.ShapeDtypeStruct((tokens_a, d), F32),
                     jax.ShapeDtypeStruct((t - tokens_a, d), F32)]
    return pl.pallas_call(
        functools.partial(_moe_combine_kernel, final_norm, steps_a),
        grid=(steps,),
        in_specs=[smem(lambda i: i), smem(lambda i: jnp.minimum(i + 1, steps - 1)),
                  row(d), row(LANES), pl.BlockSpec((1, d), lambda i: (0, 0)),
                  pl.BlockSpec(memory_space=pl.ANY)],
        out_specs=out_specs,
        out_shape=out_shape,
        scratch_shapes=[pltpu.VMEM((2, 2, tc, d), F32), pltpu.SemaphoreType.DMA((2,))],
        compiler_params=_params(("arbitrary",), 52),
        name="moe_combine",
    )(idx, idx, x, route, gf.reshape(1, d), y)


def _moe(hp, route, counts, x, layer, wg, wu, wd, gf, final_norm, tokens_a=None):
    slot, n_tiles, tile_expert, n_valid, next_expert, pad = _dispatch_plan(route, counts, MOE_TM)
    xs = _moe_dispatch(hp, slot, pad[0], pad[1], n_valid, n_tiles, MOE_TM)
    y = _moe_gemm(xs, tile_expert, n_valid, next_expert, layer, wg, wu, wd)
    return _moe_combine(x, route, slot, y, gf, final_norm, tokens_a)


def _router_weights(wg, bg, we, be):
    d = wg.shape[0]
    pad = LANES - N_EXPERTS - N_GROUPS
    wr = jnp.concatenate([we, wg, jnp.zeros((d, pad), F32)], axis=1).astype(F32)
    br = jnp.concatenate([be, bg, jnp.zeros((pad,), F32)]).astype(F32).reshape(1, LANES)
    wr_hi = wr.astype(BF16)
    wr_lo = (wr - wr_hi.astype(F32)).astype(BF16)
    return jnp.concatenate([wr_hi, wr_lo], axis=1), br


def _layer(xs, bsz, p, l, consts, final_g, tokens_a):
    dft, emb = consts
    w_in = p['w_in'][l].astype(BF16)
    g1 = p['ln1_g'][l]
    if len(xs) == 2:
        hy, x = _norm_proj_stack(xs[0], xs[1], g1, w_in[:, :HY_IN], F32)
    else:
        x, = xs
        hy = _norm_proj(x, g1, w_in[:, :HY_IN], F32)
    t, d = x.shape
    n = t // bsz
    hy = hy.reshape(bsz, n, HY_IN)
    att = _norm_proj(x, g1, w_in[:, HY_IN:HY_IN + ATT_IN], BF16).reshape(bsz, n, ATT_IN)
    ret = _norm_proj(x, g1, w_in[:, HY_IN + ATT_IN:], F32).reshape(bsz, n, RET_IN)
    y_h = _hyena(hy, dft, emb, p['hy_conv_w'][l], p['hy_conv_b'][l], p['hy_w1'][l], p['hy_b1'][l],
                 p['hy_w2'][l], p['hy_b2'][l], p['hy_w3'][l], p['hy_freq'][l], p['hy_decay'][l],
                 p['hy_skip'][l])
    y_a = _attention(att, p['attn_sink'][l], p['rel_bias'])
    y_r = _retention(ret, p['ret_decay'][l])
    wr, br = _router_weights(p['router_group_w'][l], p['router_group_b'][l],
                             p['router_expert_w'][l], p['router_expert_b'][l])
    x, hp, route, counts = _mix_router(x, y_h.reshape(t, -1), y_a.reshape(t, -1), y_r.reshape(t, -1),
                                       p['w_out'][l].astype(BF16), p['ln2_g'][l], wr, br)
    last = l == DEPTH - 1
    return _moe(hp, route, counts, x, l, p['moe_w_gate'], p['moe_w_up'], p['moe_w_down'], final_g,
                final_norm=last, tokens_a=tokens_a if last else None)


def kernel(x_prompt, x_sample, ln1_g, w_in, hy_conv_w, hy_conv_b, hy_w1, hy_b1, hy_w2, hy_b2, hy_w3,
           hy_freq, hy_decay, hy_skip, attn_sink, rel_bias, ret_decay, w_out, ln2_g, router_group_w,
           router_group_b, router_expert_w, router_expert_b, moe_w_gate, moe_w_up, moe_w_down, lnf_g):
    p = dict(ln1_g=ln1_g, w_in=w_in, hy_conv_w=hy_conv_w, hy_conv_b=hy_conv_b, hy_w1=hy_w1,
             hy_b1=hy_b1, hy_w2=hy_w2, hy_b2=hy_b2, hy_w3=hy_w3, hy_freq=hy_freq, hy_decay=hy_decay,
             hy_skip=hy_skip, attn_sink=attn_sink, rel_bias=rel_bias, ret_decay=ret_decay, w_out=w_out,
             ln2_g=ln2_g, router_group_w=router_group_w, router_group_b=router_group_b,
             router_expert_w=router_expert_w, router_expert_b=router_expert_b, moe_w_gate=moe_w_gate,
             moe_w_up=moe_w_up, moe_w_down=moe_w_down)
    bp, n, d = x_prompt.shape
    bs = x_sample.shape[0]
    assert x_sample.shape[1:] == (n, d)
    bsz = bp + bs
    consts = (_dft_matrices(n), _hyena_embedding(n))
    xs = (x_prompt.reshape(bp * n, d), x_sample.reshape(bs * n, d))
    for l in range(DEPTH):
        xs = _layer(xs, bsz, p, l, consts, lnf_g, bp * n)
    y_prompt, y_sample = xs
    return (y_prompt.reshape(bp, n, d), y_sample.reshape(bs, n, d))
```

```python
import functools
import math

import numpy as np
import jax
import jax.numpy as jnp
from jax import lax
from jax.experimental import pallas as pl
from jax.experimental.pallas import tpu as pltpu

F32 = jnp.float32
BF16 = jnp.bfloat16
HIGHEST = lax.Precision.HIGHEST

D_MODEL = 2048
DEPTH = 2
HEAD_DIM = 64
EPS = 1e-6
HY_WIDTH = D_MODEL // 4
ATT_WIDTH = D_MODEL // 2
RET_WIDTH = D_MODEL // 4
HY_ORDER = 2
HY_EMB = 33
HY_FFN = 64
ATT_HEADS = ATT_WIDTH // HEAD_DIM
ATT_KV_HEADS = 4
ATT_GROUP = ATT_HEADS // ATT_KV_HEADS
KV_WIDTH = ATT_KV_HEADS * HEAD_DIM
WINDOW = 128
BLOCK = 128
REL_BUCKETS = 32
REL_MAX_DIST = 128
RET_HEADS = RET_WIDTH // HEAD_DIM
RET_CHUNK = 128
ROPE_BASE = 10000.0
N_GROUPS = 4
EXPERTS_PER_GROUP = 4
N_EXPERTS = N_GROUPS * EXPERTS_PER_GROUP
D_EXPERT = D_MODEL // 2
HY_IN = 3 * HY_WIDTH
ATT_IN = ATT_WIDTH + 2 * KV_WIDTH
RET_IN = 4 * RET_WIDTH

LANES = 128
SUBLANES = 8
MIB = 1024 * 1024


def _params(semantics, vmem_mib):
    return pltpu.CompilerParams(dimension_semantics=semantics, vmem_limit_bytes=vmem_mib * MIB)


def _dot(a, b):
    return jnp.dot(a, b, preferred_element_type=F32)


def _dot_nt(a, b):
    return lax.dot_general(a, b, (((1,), (1,)), ((), ())), preferred_element_type=F32)


def _norm_proj_kernel(x_ref, g_ref, w_ref, o_ref):
    x = x_ref[...]
    inv = lax.rsqrt(jnp.mean(x * x, axis=-1, keepdims=True) + EPS)
    h = (x * inv * g_ref[...]).astype(BF16)
    o_ref[...] = _dot(h, w_ref[...]).astype(o_ref.dtype)


def _norm_proj(x, g, w, out_dtype, tm=1024):
    t, d = x.shape
    n = w.shape[1]
    return pl.pallas_call(
        _norm_proj_kernel,
        grid=(t // tm,),
        in_specs=[
            pl.BlockSpec((tm, d), lambda i: (i, 0)),
            pl.BlockSpec((1, d), lambda i: (0, 0)),
            pl.BlockSpec((d, n), lambda i: (0, 0), pipeline_mode=pl.Buffered(1)),
        ],
        out_specs=pl.BlockSpec((tm, n), lambda i: (i, 0)),
        out_shape=jax.ShapeDtypeStruct((t, n), out_dtype),
        compiler_params=_params(("parallel",), 56),
        name="norm_proj",
    )(x, g.reshape(1, d), w)


def _norm_proj_stack_kernel(tiles_a, xa_ref, xb_ref, g_ref, w_ref, o_ref, x_ref):
    x = jnp.where(pl.program_id(0) < tiles_a, xa_ref[...], xb_ref[...])
    x_ref[...] = x
    inv = lax.rsqrt(jnp.mean(x * x, axis=-1, keepdims=True) + EPS)
    h = (x * inv * g_ref[...]).astype(BF16)
    o_ref[...] = _dot(h, w_ref[...]).astype(o_ref.dtype)


def _norm_proj_stack(xa, xb, g, w, out_dtype, tm=512):
    ta, d = xa.shape
    tb = xb.shape[0]
    n = w.shape[1]
    tiles_a = ta // tm
    return pl.pallas_call(
        functools.partial(_norm_proj_stack_kernel, tiles_a),
        grid=((ta + tb) // tm,),
        in_specs=[
            pl.BlockSpec((tm, d), lambda i: (jnp.minimum(i, tiles_a - 1), 0)),
            pl.BlockSpec((tm, d), lambda i: (jnp.maximum(i - tiles_a, 0), 0)),
            pl.BlockSpec((1, d), lambda i: (0, 0)),
            pl.BlockSpec((d, n), lambda i: (0, 0)),
        ],
        out_specs=[pl.BlockSpec((tm, n), lambda i: (i, 0)), pl.BlockSpec((tm, d), lambda i: (i, 0))],
        out_shape=[jax.ShapeDtypeStruct((ta + tb, n), out_dtype),
                   jax.ShapeDtypeStruct((ta + tb, d), xa.dtype)],
        compiler_params=_params(("parallel",), 56),
        name="norm_proj_stack",
    )(xa, xb, g.reshape(1, d), w)


def _short_conv(x, prev_row, next_row, w_ref, b_ref):
    n = x.shape[0]
    row = lax.broadcasted_iota(jnp.int32, x.shape, 0)
    prev = jnp.where(row == 0, prev_row, pltpu.roll(x, 1, 0))
    nxt = jnp.where(row == n - 1, next_row, pltpu.roll(x, n - 1, 0))
    return prev * w_ref[0:1, :] + x * w_ref[1:2, :] + nxt * w_ref[2:3, :] + b_ref[...]


def _split_rows(x, slab_ref):
    half = x.shape[0] // 2
    slabs = x.shape[1] // LANES
    for j in range(slabs):
        slab_ref[j] = x[:, j * LANES:(j + 1) * LANES]
    rows = lambda start: jnp.concatenate(
        [slab_ref[j, pl.ds(start, half, stride=2), :] for j in range(slabs)], axis=1)
    return rows(0), rows(1)


def _merge_rows(x_e, x_o, slab_ref):
    half = x_e.shape[0]
    slabs = x_e.shape[1] // LANES
    for j in range(slabs):
        slab_ref[j, pl.ds(0, half, stride=2), :] = x_e[:, j * LANES:(j + 1) * LANES]
        slab_ref[j, pl.ds(1, half, stride=2), :] = x_o[:, j * LANES:(j + 1) * LANES]
    return jnp.concatenate([slab_ref[j] for j in range(slabs)], axis=1)


def _hy_filter_kernel(z_ref, w1_ref, b1_ref, w2_ref, b2_ref, fr_ref, w3f_ref, w3b_ref,
                      decf_ref, decb_ref, hs_ref, hd_ref):
    fr = fr_ref[...]
    h = jnp.sin(fr * (jnp.dot(z_ref[...], w1_ref[...], precision=HIGHEST,
                              preferred_element_type=F32) + b1_ref[...]))
    h = jnp.sin(fr * (jnp.dot(h, w2_ref[...], precision=HIGHEST,
                              preferred_element_type=F32) + b2_ref[...]))
    n = h.shape[0]
    shape = (n, w3f_ref.shape[1])
    row = lax.broadcasted_iota(jnp.int32, shape, 0)
    t = row.astype(F32) * (1.0 / (n - 1))
    hf = jnp.dot(h, w3f_ref[...], precision=HIGHEST, preferred_element_type=F32)
    hf = hf * jnp.exp(-t * jnp.abs(decf_ref[...]))
    hb = jnp.dot(h, w3b_ref[...], precision=HIGHEST, preferred_element_type=F32)
    hb = hb * jnp.exp(-t * jnp.abs(decb_ref[...]))
    hb = jnp.where(row == 0, 0.0, hb)
    norm = (jnp.sum(jnp.abs(hf), axis=0, keepdims=True)
            + jnp.sum(jnp.abs(hb), axis=0, keepdims=True))
    hs_ref[...] = (hf + hb) / norm
    hd_ref[...] = (hb - hf) / norm


def _hy_filter(z, w1, b1, w2, b2, fr, w3, decay, tc=256):
    n = z.shape[0]
    cols = HY_ORDER * HY_WIDTH
    w3f, w3b = w3[:, :cols], w3[:, cols:]
    dec = decay.reshape(2, cols)
    zp = jnp.pad(z, ((0, 0), (0, LANES - HY_EMB)))
    w1p = jnp.pad(w1, ((0, LANES - HY_EMB), (0, 0)))
    full = lambda shape: pl.BlockSpec(shape, lambda j: (0, 0))
    col = lambda rows: pl.BlockSpec((rows, tc), lambda j: (0, j))
    return pl.pallas_call(
        _hy_filter_kernel,
        grid=(cols // tc,),
        in_specs=[full((n, LANES)), full((LANES, HY_FFN)), full((1, HY_FFN)), full((HY_FFN, HY_FFN)),
                  full((1, HY_FFN)), full((1, HY_FFN)), col(HY_FFN), col(HY_FFN), col(1), col(1)],
        out_specs=[col(n), col(n)],
        out_shape=[jax.ShapeDtypeStruct((n, cols), F32)] * 2,
        compiler_params=_params(("parallel",), 48),
        name="hy_filter",
    )(zp, w1p, b1.reshape(1, -1), w2, b2.reshape(1, -1), fr.reshape(1, -1), w3f, w3b,
      dec[0:1], dec[1:2])


def _alternating_sum(x):
    row = lax.broadcasted_iota(jnp.int32, x.shape, 0)
    return jnp.sum(x * (1 - 2 * (row & 1)).astype(F32), axis=0, keepdims=True)


def _half_spectra(fw_ref, rows, x_e, x_o, y_e, y_o):
    ec, oc = _dot(fw_ref[0, rows, :], x_e), _dot(fw_ref[1, rows, :], x_o)
    es, os_ = _dot(fw_ref[2, rows, :], y_e), _dot(fw_ref[3, rows, :], y_o)
    return ec + oc, ec - oc, es + os_, os_ - es


def _hy_kf_kernel(fw_ref, hs_ref, hd_ref, k_ref, slab_ref):
    hs_e, hs_o = _split_rows(hs_ref[...], slab_ref)
    hd_e, hd_o = _split_rows(hd_ref[...], slab_ref)
    k0, k1, k2, k3 = _half_spectra(fw_ref, slice(None), hs_e.astype(BF16), hs_o.astype(BF16),
                                   hd_e.astype(BF16), hd_o.astype(BF16))
    first = jnp.logical_and(lax.broadcasted_iota(jnp.int32, k0.shape, 0) == 0, pl.program_id(0) == 0)
    k_ref[0] = k0
    k_ref[1] = k1
    k_ref[2] = jnp.where(first, _alternating_sum(hs_e), k2)
    k_ref[3] = jnp.where(first, _alternating_sum(hd_o), k3)


def _hy_kf(fw, hs, hd, tf=256, tc=512):
    n, cols = hs.shape
    h = n // 2
    return pl.pallas_call(
        _hy_kf_kernel,
        grid=(h // tf, cols // tc),
        in_specs=[pl.BlockSpec((4, tf, h), lambda i, j: (0, i, 0)),
                  pl.BlockSpec((n, tc), lambda i, j: (0, j)),
                  pl.BlockSpec((n, tc), lambda i, j: (0, j))],
        out_specs=pl.BlockSpec((4, tf, tc), lambda i, j: (0, i, j)),
        out_shape=jax.ShapeDtypeStruct((4, h, cols), F32),
        scratch_shapes=[pltpu.VMEM((tc // LANES, n, LANES), F32)],
        compiler_params=_params(("parallel", "arbitrary"), 48),
        name="hy_filter_dft",
    )(fw, hs, hd)


HY_FREQ_TILE = 256


def _hy_fwd_kernel(conv, fw_ref, u_ref, k_ref, *rest):
    half = fw_ref.shape[1]
    u = u_ref[0]
    if conv:
        w_ref, b_ref, y_ref, slab_ref = rest
        u = _short_conv(u, 0.0, 0.0, w_ref, b_ref)
    else:
        y_ref, slab_ref = rest
    u_e, u_o = _split_rows(u, slab_ref)
    mid_a, mid_b = _alternating_sum(u_e), _alternating_sum(u_o)
    u_e, u_o = u_e.astype(BF16), u_o.astype(BF16)
    tf = HY_FREQ_TILE
    for i in range(half // tf):
        rows = slice(i * tf, (i + 1) * tf)
        p0, p1, p2, p3 = _half_spectra(fw_ref, rows, u_e, u_o, u_e, u_o)
        k0, k1, k2, k3 = (k_ref[j, rows, :] for j in range(4))
        y0, y2 = p0 * k0 + p2 * k2, p0 * k2 - p2 * k0
        y1, y3 = p1 * k1 + p3 * k3, p1 * k3 - p3 * k1
        he, ho = y2 - y3, y2 + y3
        if i == 0:
            first = lax.broadcasted_iota(jnp.int32, p0.shape, 0) == 0
            y0 = jnp.where(first, p0 * k0, y0)
            y1 = jnp.where(first, p1 * k1, y1)
            he = jnp.where(first, mid_a * k2 + mid_b * k3, he)
            ho = jnp.where(first, mid_a * k3 - mid_b * k2, ho)
        y_ref[0, 0, rows, :] = (y0 + y1).astype(BF16)
        y_ref[0, 1, rows, :] = he.astype(BF16)
        y_ref[0, 2, rows, :] = (y0 - y1).astype(BF16)
        y_ref[0, 3, rows, :] = ho.astype(BF16)


def _hy_fwd(fw, u3, ucol, k, kcol, conv=None):
    bsz, n, _ = u3.shape
    c = HY_WIDTH
    h = n // 2
    in_specs = [
        pl.BlockSpec((4, h, h), lambda b: (0, 0, 0), pipeline_mode=pl.Buffered(1)),
        pl.BlockSpec((1, n, c), lambda b: (b, 0, ucol)),
        pl.BlockSpec((4, h, c), lambda b: (0, 0, kcol), pipeline_mode=pl.Buffered(1)),
    ]
    args = [fw, u3, k]
    if conv is not None:
        in_specs += [pl.BlockSpec((3, c), lambda b: (0, ucol)), pl.BlockSpec((1, c), lambda b: (0, ucol))]
        args += list(conv)
    return pl.pallas_call(
        functools.partial(_hy_fwd_kernel, conv is not None),
        grid=(bsz,),
        in_specs=in_specs,
        out_specs=pl.BlockSpec((1, 4, h, c), lambda b: (b, 0, 0, 0)),
        out_shape=jax.ShapeDtypeStruct((bsz, 4, h, c), BF16),
        scratch_shapes=[pltpu.VMEM((c // LANES, n, LANES), F32)],
        compiler_params=_params(("arbitrary",), 52),
        name="hy_dft_fwd",
    )(*args)


def _hy_inv_kernel(s_conv, iv_ref, y_ref, d_ref, *rest):
    o_ref, slab_ref = rest[-2:]
    i = pl.program_id(0)
    first, last = i == 0, i == pl.num_programs(0) - 1

    def conv_tile(x_ref, lo_ref, hi_ref, w_ref, b_ref):
        prev_row = jnp.where(first, 0.0, lo_ref[0, SUBLANES - 1:SUBLANES, :])
        next_row = jnp.where(last, 0.0, hi_ref[0, 0:1, :])
        return _short_conv(x_ref[0], prev_row, next_row, w_ref, b_ref)

    if s_conv:
        skip = conv_tile(*rest[0:5])
        gate = conv_tile(*rest[5:10])
    else:
        skip = rest[0][0]
        gate = conv_tile(*rest[1:6])
    y_e = _dot(iv_ref[0], y_ref[0, 0]) + _dot(iv_ref[1], y_ref[0, 1])
    y_o = _dot(iv_ref[2], y_ref[0, 2]) + _dot(iv_ref[3], y_ref[0, 3])
    y = _merge_rows(y_e, y_o, slab_ref)
    o_ref[0] = (gate * (y + skip * d_ref[...])).astype(o_ref.dtype)


def _hy_inv(iv, y, d, raw3, conv, gcol, out_dtype, scol=None, s3=None, tt=1024):
    bsz, _, h, c = y.shape
    n = 2 * h
    conv_w, conv_b = conv
    halo = tt // SUBLANES
    tile = lambda col: pl.BlockSpec((1, tt, c), lambda i, b: (b, i, col))
    lo = lambda col: pl.BlockSpec((1, SUBLANES, c), lambda i, b: (b, jnp.maximum(i * halo - 1, 0), col))
    hi = lambda col: pl.BlockSpec((1, SUBLANES, c),
                                  lambda i, b: (b, jnp.minimum((i + 1) * halo, n // SUBLANES - 1), col))
    wspec = lambda col: pl.BlockSpec((3, c), lambda i, b: (0, col))
    bspec = lambda col: pl.BlockSpec((1, c), lambda i, b: (0, col))
    conv_in = lambda col: ([tile(col), lo(col), hi(col), wspec(col), bspec(col)],
                           [raw3, raw3, raw3, conv_w, conv_b])
    in_specs = [pl.BlockSpec((4, tt // 2, h), lambda i, b: (0, i, 0)),
                pl.BlockSpec((1, 4, h, c), lambda i, b: (b, 0, 0, 0)),
                pl.BlockSpec((1, c), lambda i, b: (0, 0))]
    args = [iv, y, d.reshape(1, c)]
    if s3 is None:
        specs, arrs = conv_in(scol)
    else:
        specs, arrs = [tile(0)], [s3]
    in_specs += specs
    args += arrs
    specs, arrs = conv_in(gcol)
    in_specs += specs
    args += arrs
    return pl.pallas_call(
        functools.partial(_hy_inv_kernel, s3 is None),
        grid=(n // tt, bsz),
        in_specs=in_specs,
        out_specs=pl.BlockSpec((1, tt, c), lambda i, b: (b, i, 0)),
        out_shape=jax.ShapeDtypeStruct((bsz, n, c), out_dtype),
        scratch_shapes=[pltpu.VMEM((c // LANES, tt, LANES), F32)],
        compiler_params=_params(("arbitrary", "arbitrary"), 48),
        name="hy_dft_inv",
    )(*args)


DFT_SPLIT = 32


def _dft_matrices(n):
    h = n // 2

    def cos_sin(offset):
        def table(rows, step):
            f = lax.broadcasted_iota(jnp.int32, (rows, h), 0) * step
            t = 2 * lax.broadcasted_iota(jnp.int32, (rows, h), 1) + offset
            ang = ((f * t) & (2 * n - 1)).astype(F32) * (math.pi / n)
            return jnp.cos(ang), jnp.sin(ang)
        cos_hi, sin_hi = (x[:, None, :] for x in table(h // DFT_SPLIT, DFT_SPLIT))
        cos_lo, sin_lo = (x[None, :, :] for x in table(DFT_SPLIT, 1))
        return ((cos_hi * cos_lo - sin_hi * sin_lo).reshape(h, h),
                (sin_hi * cos_lo + cos_hi * sin_lo).reshape(h, h))

    cos_e, sin_e = cos_sin(0)
    cos_o, sin_o = cos_sin(1)
    fwd = jnp.stack([cos_e, cos_o, sin_e, sin_o]).astype(BF16)
    m = lax.broadcasted_iota(jnp.int32, (h, h), 0)
    f = lax.broadcasted_iota(jnp.int32, (h, h), 1)
    weight = jnp.where(f == 0, 0.5 / n, 1.0 / n)
    mid = (1 - 2 * (m & 1)).astype(F32) * (1.0 / n)
    inv = jnp.stack([weight * cos_e.T, jnp.where(f == 0, mid, sin_e.T * (-1.0 / n)),
                     weight * cos_o.T, jnp.where(f == 0, -mid, sin_o.T * (-1.0 / n))]).astype(BF16)
    return fwd, inv


def _hyena_embedding(n):
    t = jnp.linspace(0.0, 1.0, n, dtype=F32)[:, None]
    bands = (HY_EMB - 1) // 2
    f = jnp.linspace(1e-4, bands - 1, bands, dtype=F32)[None]
    w = 2.0 * math.pi * jnp.arange(n, dtype=F32)[:, None] / n
    return jnp.concatenate([t, jnp.cos(f * w), -jnp.sin(f * w)], -1)


def _hyena(p3, dft, emb, conv_w, conv_b, w1, b1, w2, b2, w3, freq, decay, skip):
    ff, fi = dft
    conv = (conv_w, conv_b.reshape(1, -1))
    hs, hd = _hy_filter(emb, w1, b1, w2, b2, freq, w3, decay)
    k = _hy_kf(ff, hs, hd)
    y = _hy_fwd(ff, p3, 0, k, 0, conv)
    z = _hy_inv(fi, y, skip[0], p3, conv, 1, F32, scol=0)
    y = _hy_fwd(ff, z, 0, k, 1)
    return _hy_inv(fi, y, skip[1], p3, conv, 2, BF16, s3=z)


def _t5_bucket(rel):
    nb = REL_BUCKETS // 2
    max_exact = nb // 2
    n = np.abs(rel)
    large = max_exact + (np.log(np.maximum(n, 1) / max_exact) / np.log(REL_MAX_DIST / max_exact)
                         * (nb - max_exact)).astype(np.int32)
    large = np.minimum(large, nb - 1)
    return (rel > 0).astype(np.int32) * nb + np.where(n < max_exact, n, large)


def _attn_kernel(q_ref, kp_ref, kc_ref, kn_ref, vp_ref, vc_ref, vn_ref, bias_ref, sink_ref, o_ref):
    n = pl.program_id(1)
    q = q_ref[0] * (HEAD_DIM ** -0.5)
    k = jnp.concatenate([kp_ref[0], kc_ref[0], kn_ref[0]], axis=0)
    vt = jnp.concatenate([vp_ref[0], vc_ref[0], vn_ref[0]], axis=1)
    edge_prev = jnp.where(n == 0, -1e30, 0.0).astype(F32)
    edge_next = jnp.where(n == pl.num_programs(1) - 1, -1e30, 0.0).astype(F32)
    ones = jnp.ones((HEAD_DIM, 3 * BLOCK), BF16)
    for kh in range(ATT_KV_HEADS):
        k_h = k[:, kh * HEAD_DIM:(kh + 1) * HEAD_DIM]
        vt_h = jnp.concatenate([vt[kh * HEAD_DIM:(kh + 1) * HEAD_DIM], ones], axis=0)
        q_h = jnp.concatenate([q[:, (kh * ATT_GROUP + g) * HEAD_DIM:(kh * ATT_GROUP + g + 1) * HEAD_DIM]
                               for g in range(ATT_GROUP)], axis=0)
        s = _dot_nt(k_h, q_h) + bias_ref[kh]
        s = jnp.concatenate([s[:BLOCK] + edge_prev, s[BLOCK:2 * BLOCK], s[2 * BLOCK:] + edge_next],
                            axis=0)
        sink = sink_ref[kh]
        m = jnp.maximum(jnp.max(s, axis=0, keepdims=True), sink)
        p = jnp.exp(s - m).astype(BF16)
        o = _dot(vt_h, p)
        o = o[:HEAD_DIM] / (o[HEAD_DIM:HEAD_DIM + 1] + jnp.exp(sink - m))
        for g in range(0, ATT_GROUP, 2):
            pair = jnp.concatenate([o[:, g * BLOCK:(g + 1) * BLOCK],
                                    o[:, (g + 1) * BLOCK:(g + 2) * BLOCK]], axis=0)
            c0 = (kh * ATT_GROUP + g) * HEAD_DIM
            o_ref[0, :, c0:c0 + 2 * HEAD_DIM] = pair.T.astype(o_ref.dtype)


def _attention_bias(rel_bias):
    rel = (np.arange(3 * BLOCK)[None, :] - BLOCK) - np.arange(BLOCK)[:, None]
    bucket = jnp.asarray(_t5_bucket(rel).reshape(-1), jnp.int32)
    onehot = (bucket[:, None] == jnp.arange(REL_BUCKETS, dtype=jnp.int32)[None]).astype(F32)
    bias = jnp.dot(onehot, rel_bias.astype(F32), precision=HIGHEST)
    bias = bias.reshape(BLOCK, 3 * BLOCK, ATT_KV_HEADS, ATT_GROUP)
    bias = jnp.where(jnp.asarray(np.abs(rel) <= WINDOW)[:, :, None, None], bias, -1e30)
    return jnp.transpose(bias, (2, 1, 3, 0)).reshape(ATT_KV_HEADS, 3 * BLOCK, ATT_GROUP * BLOCK)


def _attention(att3, sink, rel_bias):
    bsz, n, _ = att3.shape
    nb = n // BLOCK
    bias = _attention_bias(rel_bias)
    sink_row = jnp.repeat(sink.astype(F32).reshape(ATT_KV_HEADS, 1, ATT_GROUP), BLOCK, axis=2)
    vt = jnp.swapaxes(att3[:, :, ATT_WIDTH + KV_WIDTH:], 1, 2)
    kcol = ATT_WIDTH // KV_WIDTH
    prev = lambda i: jnp.maximum(i - 1, 0)
    nxt = lambda i: jnp.minimum(i + 1, nb - 1)
    cur = lambda i: i
    k_spec = lambda blk: pl.BlockSpec((1, BLOCK, KV_WIDTH), lambda b, i: (b, blk(i), kcol))
    v_spec = lambda blk: pl.BlockSpec((1, KV_WIDTH, BLOCK), lambda b, i: (b, 0, blk(i)))
    return pl.pallas_call(
        _attn_kernel,
        grid=(bsz, nb),
        in_specs=[
            pl.BlockSpec((1, BLOCK, ATT_WIDTH), lambda b, i: (b, i, 0)),
            k_spec(prev), k_spec(cur), k_spec(nxt), v_spec(prev), v_spec(cur), v_spec(nxt),
            pl.BlockSpec((ATT_KV_HEADS, 3 * BLOCK, ATT_GROUP * BLOCK), lambda b, i: (0, 0, 0)),
            pl.BlockSpec((ATT_KV_HEADS, 1, ATT_GROUP * BLOCK), lambda b, i: (0, 0, 0)),
        ],
        out_specs=pl.BlockSpec((1, BLOCK, ATT_WIDTH), lambda b, i: (b, i, 0)),
        out_shape=jax.ShapeDtypeStruct((bsz, n, ATT_WIDTH), BF16),
        compiler_params=_params(("parallel", "parallel"), 32),
        name="window_attention",
    )(att3, att3, att3, att3, vt, vt, vt, bias, sink_row)


def _rotary_tables(n):
    inv = ROPE_BASE ** -jnp.linspace(0.0, 1.0, HEAD_DIM // 2, dtype=F32)
    ang = jnp.arange(n, dtype=F32)[:, None] * inv[None]
    cos, sin = jnp.cos(ang), jnp.sin(ang)
    cos_t = jnp.tile(jnp.concatenate([cos, cos], -1), (1, RET_HEADS))
    sin_t = jnp.tile(jnp.concatenate([-sin, sin], -1), (1, RET_HEADS))
    return cos_t, sin_t


def _rotate(x, cos_t, sin_t):
    width = x.shape[-1]
    half = HEAD_DIM // 2
    lane = lax.broadcasted_iota(jnp.int32, x.shape, 1)
    partner = jnp.where((lane & (HEAD_DIM - 1)) < half,
                        pltpu.roll(x, width - half, 1), pltpu.roll(x, half, 1))
    return x * cos_t + partner * sin_t


RET_PAIR = 2 * HEAD_DIM
RET_PAIRS = RET_WIDTH // RET_PAIR


def _same_head(shape):
    r = lax.broadcasted_iota(jnp.int32, shape, 0) // HEAD_DIM
    c = lax.broadcasted_iota(jnp.int32, shape, 1) // HEAD_DIM
    return r == c


RET_STATE_CHUNKS = 4
RET_OUT_CHUNKS = 2


def _ret_state_kernel(kf_ref, vf_ref, kb_ref, vb_ref, cf_ref, sf_ref, cb_ref, sb_ref, lg_ref,
                      of_ref, ob_ref, stf_ref, stb_ref):
    @pl.when(pl.program_id(1) == 0)
    def _():
        stf_ref[...] = jnp.zeros_like(stf_ref)
        stb_ref[...] = jnp.zeros_like(stb_ref)

    c = RET_CHUNK
    j = lax.broadcasted_iota(jnp.int32, (c, RET_WIDTH), 0).astype(F32)
    lgf = lg_ref[0:1, :]
    lgb = lg_ref[1:2, :]
    scale = HEAD_DIM ** -0.5
    same = _same_head((RET_PAIR, RET_PAIR))

    def scan(st_ref, o_ref, k_ref, v_ref, cos_ref, sin_ref, zeta, lg, order):
        grow = jnp.exp(c * lg)
        for sub in order:
            rows = slice(sub * c, (sub + 1) * c)
            o_ref[0, sub] = st_ref[...].astype(o_ref.dtype)
            k = (_rotate(k_ref[0, rows, :], cos_ref[rows, :], sin_ref[rows, :]) * scale * zeta).astype(BF16)
            v = v_ref[0, rows, :].astype(BF16)
            for p in range(RET_PAIRS):
                lanes = slice(p * RET_PAIR, (p + 1) * RET_PAIR)
                kv = lax.dot_general(k[:, lanes], v[:, lanes], (((0,), (0,)), ((), ())),
                                     preferred_element_type=F32)
                st_ref[p] = grow[:, lanes] * st_ref[p] + jnp.where(same, kv, 0.0)

    chunks = range(RET_STATE_CHUNKS)
    scan(stf_ref, of_ref, kf_ref, vf_ref, cf_ref, sf_ref, jnp.exp((c - 1.0 - j) * lgf), lgf, chunks)
    scan(stb_ref, ob_ref, kb_ref, vb_ref, cb_ref, sb_ref, jnp.exp(j * lgb), lgb, reversed(chunks))


def _ret_states(ret3, cos_t, sin_t, lg_lane):
    bsz, n, _ = ret3.shape
    nc = n // RET_CHUNK
    rows, w = RET_STATE_CHUNKS * RET_CHUNK, RET_WIDTH
    steps = n // rows
    fwd = lambda b, i: i
    bwd = lambda b, i: steps - 1 - i
    col = lambda pos, cb: pl.BlockSpec((1, rows, w), lambda b, i: (b, pos(b, i), cb))
    tab = lambda pos: pl.BlockSpec((rows, w), lambda b, i: (pos(b, i), 0))
    st_shape = (RET_PAIRS, RET_PAIR, RET_PAIR)
    out = lambda pos: pl.BlockSpec((1, RET_STATE_CHUNKS) + st_shape, lambda b, i: (b, pos(b, i), 0, 0, 0))
    return pl.pallas_call(
        _ret_state_kernel,
        grid=(bsz, steps),
        in_specs=[col(fwd, 1), col(fwd, 2), col(bwd, 1), col(bwd, 2),
                  tab(fwd), tab(fwd), tab(bwd), tab(bwd),
                  pl.BlockSpec((2, w), lambda b, i: (0, 0))],
        out_specs=[out(fwd), out(bwd)],
        out_shape=[jax.ShapeDtypeStruct((bsz, nc) + st_shape, BF16)] * 2,
        scratch_shapes=[pltpu.VMEM(st_shape, F32), pltpu.VMEM(st_shape, F32)],
        compiler_params=_params(("parallel", "arbitrary"), 32),
        name="retention_states",
    )(ret3, ret3, ret3, ret3, cos_t, sin_t, cos_t, sin_t, lg_lane)


def _ret_out_kernel(q_ref, k_ref, v_ref, g_ref, cos_ref, sin_ref, lg_ref, lgs_ref, stf_ref, stb_ref,
                    o_ref):
    c, w = RET_CHUNK, RET_WIDTH
    pos = lax.broadcasted_iota(jnp.int32, (c, w), 0).astype(F32)
    xi_f = jnp.exp((pos + 1.0) * lg_ref[0:1, :])
    xi_b = jnp.exp((c - pos) * lg_ref[1:2, :])
    diff = (lax.broadcasted_iota(jnp.int32, (c, c), 0)
            - lax.broadcasted_iota(jnp.int32, (c, c), 1)).astype(F32)
    first = lax.broadcasted_iota(jnp.int32, (c, RET_PAIR), 1) < HEAD_DIM
    avg = jnp.where(_same_head((RET_PAIR, RET_PAIR)), 1.0 / HEAD_DIM, 0.0).astype(BF16)
    decays = [jnp.concatenate(
        [jnp.exp(jnp.where(diff >= 0, diff * lgs_ref[0, h], -diff * lgs_ref[1, h]))
         for h in (2 * p, 2 * p + 1)], axis=0) for p in range(RET_PAIRS)]
    for sub in range(RET_OUT_CHUNKS):
        rows = slice(sub * c, (sub + 1) * c)
        cos_t, sin_t = cos_ref[rows, :], sin_ref[rows, :]
        q = _rotate(q_ref[0, rows, :], cos_t, sin_t)
        k = (_rotate(k_ref[0, rows, :], cos_t, sin_t) * (HEAD_DIM ** -0.5)).astype(BF16)
        v = v_ref[0, rows, :].astype(BF16)
        g = g_ref[0, rows, :]
        q_f = (q * xi_f).astype(BF16)
        q_b = (q * xi_b).astype(BF16)
        for p in range(RET_PAIRS):
            lanes = slice(p * RET_PAIR, (p + 1) * RET_PAIR)
            q_p = q[:, lanes]
            qs = jnp.concatenate([jnp.where(first, q_p, 0.0), jnp.where(first, 0.0, q_p)], axis=0)
            s = _dot_nt(qs.astype(BF16), k[:, lanes])
            o2 = _dot((s * decays[p]).astype(BF16), v[:, lanes])
            y = jnp.where(first, o2[:c], o2[c:])
            y = y + _dot(q_f[:, lanes], stf_ref[0, sub, p]) + _dot(q_b[:, lanes], stb_ref[0, sub, p])
            sq = y * y
            sq_hi = sq.astype(BF16)
            sq_lo = (sq - sq_hi.astype(F32)).astype(BF16)
            ms = _dot(sq_hi, avg) + _dot(sq_lo, avg)
            y = y * lax.rsqrt(ms + EPS)
            g_p = g[:, lanes]
            o_ref[0, rows, lanes] = (g_p / (1.0 + jnp.exp(-g_p)) * y).astype(o_ref.dtype)


def _retention(ret3, decay):
    bsz, n, _ = ret3.shape
    rows, w = RET_OUT_CHUNKS * RET_CHUNK, RET_WIDTH
    lg = -jnp.exp(decay.astype(F32))
    lg_lane = jnp.repeat(lg, HEAD_DIM, axis=1)
    cos_t, sin_t = _rotary_tables(n)
    stf, stb = _ret_states(ret3, cos_t, sin_t, lg_lane)
    col = lambda cb: pl.BlockSpec((1, rows, w), lambda b, i: (b, i, cb))
    tab = pl.BlockSpec((rows, w), lambda b, i: (i, 0))
    st = pl.BlockSpec((1, RET_OUT_CHUNKS, RET_PAIRS, RET_PAIR, RET_PAIR), lambda b, i: (b, i, 0, 0, 0))
    return pl.pallas_call(
        _ret_out_kernel,
        grid=(bsz, n // rows),
        in_specs=[col(0), col(1), col(2), col(3), tab, tab,
                  pl.BlockSpec((2, w), lambda b, i: (0, 0)),
                  pl.BlockSpec(memory_space=pltpu.SMEM), st, st],
        out_specs=pl.BlockSpec((1, rows, w), lambda b, i: (b, i, 0)),
        out_shape=jax.ShapeDtypeStruct((bsz, n, w), BF16),
        compiler_params=_params(("parallel", "parallel"), 32),
        name="retention_out",
    )(ret3, ret3, ret3, ret3, cos_t, sin_t, lg_lane, lg, stf, stb)


ROUTE_I0, ROUTE_I1, ROUTE_W0, ROUTE_W1, ROUTE_R0, ROUTE_R1 = 0, 1, 2, 3, 4, 5


def _mix_router_kernel(x_ref, yh_ref, ya_ref, yr_ref, w_ref, g_ref, wr_ref, br_ref,
                       xo_ref, hp_ref, route_ref, counts_ref, carry_ref, logits_ref):
    i = pl.program_id(0)

    @pl.when(i == 0)
    def _():
        carry_ref[...] = jnp.zeros_like(carry_ref)
        logits_ref[...] = jnp.zeros_like(logits_ref)

    logits = logits_ref[(i + 1) % 2]
    live = jnp.where(i >= 1, 1.0, 0.0).astype(F32)
    lane = lax.broadcasted_iota(jnp.int32, logits.shape, 1)
    big = jnp.int32(LANES)
    neg = -jnp.inf
    is_group = (lane >= N_EXPERTS) & (lane < N_EXPERTS + N_GROUPS)
    gl = jnp.where(is_group, logits, neg)
    gmax = jnp.max(gl, axis=-1, keepdims=True)
    g_w = 1.0 / jnp.sum(jnp.exp(gl - gmax), axis=-1, keepdims=True)
    g_sel = jnp.min(jnp.where(gl == gmax, lane, big), axis=-1, keepdims=True) - N_EXPERTS
    lo = g_sel * EXPERTS_PER_GROUP
    el = jnp.where((lane >= lo) & (lane < lo + EXPERTS_PER_GROUP), logits, neg)
    v0 = jnp.max(el, axis=-1, keepdims=True)
    i0 = jnp.min(jnp.where(el == v0, lane, big), axis=-1, keepdims=True)
    el = jnp.where(lane == i0, neg, el)
    v1 = jnp.max(el, axis=-1, keepdims=True)
    i1 = jnp.min(jnp.where(el == v1, lane, big), axis=-1, keepdims=True)
    e1 = jnp.exp(v1 - v0)
    w0 = g_w / (1.0 + e1)
    oh0 = jnp.where(lane == i0, 1.0, 0.0)
    oh1 = jnp.where(lane == i1, 1.0, 0.0)
    tm = logits.shape[0]
    earlier = jnp.where(lax.broadcasted_iota(jnp.int32, (tm, tm), 1)
                        < lax.broadcasted_iota(jnp.int32, (tm, tm), 0), 1.0, 0.0).astype(BF16)
    carry = carry_ref[...]
    tot0 = jnp.sum(oh0, axis=0, keepdims=True)
    before0 = _dot(earlier, oh0.astype(BF16)) + carry
    before1 = _dot(earlier, oh1.astype(BF16)) + (carry + tot0)
    r0 = jnp.sum(oh0 * before0, axis=-1, keepdims=True)
    r1 = jnp.sum(oh1 * before1, axis=-1, keepdims=True)
    carry = carry + live * (tot0 + jnp.sum(oh1, axis=0, keepdims=True))
    carry_ref[...] = carry
    counts_ref[...] = jnp.broadcast_to(carry, counts_ref.shape)
    route_ref[...] = (jnp.where(lane == ROUTE_I0, i0.astype(F32), 0.0)
                      + jnp.where(lane == ROUTE_I1, i1.astype(F32), 0.0)
                      + jnp.where(lane == ROUTE_W0, w0, 0.0)
                      + jnp.where(lane == ROUTE_W1, w0 * e1, 0.0)
                      + jnp.where(lane == ROUTE_R0, r0, 0.0)
                      + jnp.where(lane == ROUTE_R1, r1, 0.0))

    mix = jnp.concatenate([yh_ref[...], ya_ref[...], yr_ref[...]], axis=1)
    x = x_ref[...] + _dot(mix, w_ref[...])
    xo_ref[...] = x
    h = x * lax.rsqrt(jnp.mean(x * x, axis=-1, keepdims=True) + EPS) * g_ref[...]
    h_hi = h.astype(BF16)
    bits = lax.bitcast_convert_type(h_hi.astype(F32), jnp.uint32)
    half = bits.shape[1] // 2
    hp_ref[...] = bits[:, :half] | (bits[:, half:] >> 16)
    h_lo = (h - h_hi.astype(F32)).astype(BF16)
    both = _dot(h_hi, wr_ref[...])
    logits_ref[i % 2] = (both[:, :LANES] + (_dot(h_lo, wr_ref[:, :LANES]) + both[:, LANES:])
                         + br_ref[...])


def _mix_router(x, yh, ya, yr, w_out, g, wr, br, tm=512):
    t, d = x.shape
    steps = t // tm
    cur = lambda i: (jnp.minimum(i, steps - 1), 0)
    row = lambda width: pl.BlockSpec((tm, width), cur)
    full = lambda shape: pl.BlockSpec(shape, lambda i: (0, 0))
    return pl.pallas_call(
        _mix_router_kernel,
        grid=(steps + 1,),
        in_specs=[row(d), row(HY_WIDTH), row(ATT_WIDTH), row(RET_WIDTH),
                  pl.BlockSpec((d, d), lambda i: (0, 0), pipeline_mode=pl.Buffered(1)),
                  full((1, d)), full((d, 2 * LANES)), full((1, LANES))],
        out_specs=[row(d), row(d // 2),
                   pl.BlockSpec((tm, LANES), lambda i: (jnp.maximum(i - 1, 0), 0)),
                   full((SUBLANES, LANES))],
        out_shape=[jax.ShapeDtypeStruct((t, d), F32), jax.ShapeDtypeStruct((t, d // 2), jnp.uint32),
                   jax.ShapeDtypeStruct((t, LANES), F32), jax.ShapeDtypeStruct((SUBLANES, LANES), F32)],
        scratch_shapes=[pltpu.VMEM((1, LANES), F32), pltpu.VMEM((2, tm, LANES), F32)],
        compiler_params=_params(("arbitrary",), 48),
        name="mix_router",
    )(x, yh, ya, yr, w_out, g.reshape(1, d), wr, br)


MOE_TM = 512
MOE_TC = 512


def _dispatch_plan(route, counts, tm):
    experts = jnp.arange(N_EXPERTS, dtype=jnp.int32)
    ids = route[:, ROUTE_I0:ROUTE_I1 + 1].astype(jnp.int32)
    rank = route[:, ROUTE_R0:ROUTE_R1 + 1].astype(jnp.int32)
    n_tiles = 2 * route.shape[0] // tm + N_EXPERTS
    count = counts[0, :N_EXPERTS].astype(jnp.int32)
    tiles = (count + tm - 1) // tm
    tile_end = jnp.cumsum(tiles)
    offs = (tile_end - tiles) * tm
    slot = jnp.sum(jnp.where(ids[..., None] == experts, offs, 0), axis=-1) + rank
    tile_expert = jnp.minimum(
        jnp.sum(jnp.arange(n_tiles, dtype=jnp.int32)[:, None] >= tile_end[None, :], axis=1),
        N_EXPERTS - 1).astype(jnp.int32)
    later = (experts[None, :] > experts[:, None]) & (tiles[None, :] > 0)
    following = jnp.min(jnp.where(later, experts[None, :], N_EXPERTS), axis=1)
    following = jnp.where(following == N_EXPERTS, -1, following).astype(jnp.int32)
    pad = (offs + count, tiles * tm - count)
    return slot, n_tiles, tile_expert, tile_end[-1:].astype(jnp.int32), following[tile_expert], pad


def _moe_dispatch_kernel(tm, pad_start_ref, pad_rows_ref, nv_ref, idx_ref, hp_ref, xs_ref, zeros_ref,
                         sem, zsem):
    tokens = hp_ref.shape[0]
    n_tiles = xs_ref.shape[0] // tm

    @pl.when(pl.program_id(0) == 0)
    def _():
        zeros_ref[...] = jnp.zeros_like(zeros_ref)
        fills = []
        for e in range(N_EXPERTS):
            rows = pad_rows_ref[e]
            end = pad_start_ref[e] + rows
            size = tm // 2
            while size >= SUBLANES:
                above = rows & ~(2 * size - 1)
                fills.append(((rows & size) != 0, pl.multiple_of(end - above - size, SUBLANES), size))
                size //= 2
            for j in range(SUBLANES - 1):
                fills.append(((rows & (SUBLANES - 1)) > j, pad_start_ref[e] + j, 1))
        for j in range(N_EXPERTS):
            fills.append((nv_ref[0] + j < n_tiles, pl.multiple_of((nv_ref[0] + j) * tm, tm), tm))
        copies = [(cond, pltpu.make_async_copy(zeros_ref.at[pl.ds(0, size)],
                                               xs_ref.at[pl.ds(start, size)], zsem))
                  for cond, start, size in fills]
        for cond, cp in copies:
            pl.when(cond)(cp.start)
        for cond, cp in copies:
            pl.when(cond)(cp.wait)

    def issue(r, carry):
        for k in range(2):
            pltpu.make_async_copy(hp_ref.at[pl.ds(r, 1)],
                                  xs_ref.at[pl.ds(idx_ref[0, 0, 2 * r + k], 1)], sem).start(priority=k)
        return carry

    lax.fori_loop(0, tokens, issue, 0, unroll=4)
    for k in range(2):
        pltpu.make_async_copy(hp_ref, xs_ref.at[pl.ds(0, tokens)], sem).wait()


def _moe_dispatch(hp, slot, pad_start, pad_rows, n_valid, n_tiles, tm, tc=MOE_TC):
    t, half = hp.shape
    steps = t // tc
    return pl.pallas_call(
        functools.partial(_moe_dispatch_kernel, tm),
        grid_spec=pltpu.PrefetchScalarGridSpec(
            num_scalar_prefetch=3,
            grid=(steps,),
            in_specs=[pl.BlockSpec((1, 1, 2 * tc), lambda i, ps, pr, nv: (i, 0, 0),
                                   memory_space=pltpu.SMEM),
                      pl.BlockSpec((tc, half), lambda i, ps, pr, nv: (i, 0))],
            out_specs=pl.BlockSpec(memory_space=pl.ANY),
            scratch_shapes=[pltpu.VMEM((tm, half), hp.dtype), pltpu.SemaphoreType.DMA(()),
                            pltpu.SemaphoreType.DMA(())],
        ),
        out_shape=jax.ShapeDtypeStruct((n_tiles * tm, half), hp.dtype),
        compiler_params=_params(("arbitrary",), 16),
        name="moe_dispatch",
    )(pad_start, pad_rows, n_valid, slot.reshape(steps, 1, 2 * tc), hp)


MOE_CAST_ROWS = 256


def _moe_gemm_kernel(layer, te_ref, nv_ref, nxt_ref, xs_ref, wg_ref, wu_ref, wd_ref, y_ref,
                     stage_g, stage_u, stage_d, wgb_ref, wub_ref, wdb_ref, sem_ref):
    i = pl.program_id(0)
    last = nv_ref[0] - 1
    tile = jnp.minimum(i, last)
    expert = te_ref[tile]
    fresh = jnp.logical_or(i == 0, expert != te_ref[jnp.maximum(tile - 1, 0)])
    pairs = ((wg_ref, stage_g, wgb_ref), (wu_ref, stage_u, wub_ref), (wd_ref, stage_d, wdb_ref))

    def copies(e):
        return [pltpu.make_async_copy(w_ref.at[layer, e], stage, sem_ref.at[n])
                for n, (w_ref, stage, _) in enumerate(pairs)]

    @pl.when(jnp.logical_and(i <= last, fresh))
    def _():
        @pl.when(i == 0)
        def _():
            for cp in copies(expert):
                cp.start()

        for cp in copies(expert):
            cp.wait()
        for _, stage, dst_ref in pairs:
            def cast(j, carry):
                rows = pl.ds(pl.multiple_of(j * MOE_CAST_ROWS, MOE_CAST_ROWS), MOE_CAST_ROWS)
                dst_ref[rows, :] = stage[rows, :].astype(BF16)
                return carry
            lax.fori_loop(0, dst_ref.shape[0] // MOE_CAST_ROWS, cast, 0)

        @pl.when(nxt_ref[tile] >= 0)
        def _():
            for cp in copies(nxt_ref[tile]):
                cp.start()

    @pl.when(i <= last)
    def _():
        xp = xs_ref[...]
        x_a = lax.bitcast_convert_type(xp & jnp.uint32(0xFFFF0000), F32).astype(BF16)
        x_b = lax.bitcast_convert_type(xp << 16, F32).astype(BF16)
        x = jnp.concatenate([x_a, x_b], axis=1)
        a = _dot(x, wgb_ref[...])
        b = _dot(x, wub_ref[...])
        act = a / (1.0 + jnp.exp(-a)) * b
        y_ref[...] = _dot(act.astype(BF16), wdb_ref[...])

    @pl.when(i > last)
    def _():
        y_ref[...] = jnp.zeros_like(y_ref)


def _moe_gemm(xs, tile_expert, n_valid, next_expert, layer, wg, wu, wd, tm=MOE_TM):
    n_slots, half = xs.shape
    _, ne, d, de = wg.shape
    tile = lambda i, te, nv, nx: jnp.minimum(i, nv[0] - 1)
    hbm = pl.BlockSpec(memory_space=pl.ANY)
    return pl.pallas_call(
        functools.partial(_moe_gemm_kernel, layer),
        grid_spec=pltpu.PrefetchScalarGridSpec(
            num_scalar_prefetch=3,
            grid=(n_slots // tm,),
            in_specs=[pl.BlockSpec((tm, half), lambda i, te, nv, nx: (tile(i, te, nv, nx), 0)),
                      hbm, hbm, hbm],
            out_specs=pl.BlockSpec((tm, d), lambda i, te, nv, nx: (i, 0)),
            scratch_shapes=[pltpu.VMEM((d, de), F32), pltpu.VMEM((d, de), F32), pltpu.VMEM((de, d), F32),
                            pltpu.VMEM((d, de), BF16), pltpu.VMEM((d, de), BF16),
                            pltpu.VMEM((de, d), BF16), pltpu.SemaphoreType.DMA((3,))],
        ),
        out_shape=jax.ShapeDtypeStruct((n_slots, d), F32),
        compiler_params=_params(("arbitrary",), 58),
        name="moe_gemm",
    )(tile_expert, n_valid, next_expert, xs, wg, wu, wd)


def _moe_combine_kernel(final_norm, steps_a, idx_ref, idx_next_ref, x_ref, route_ref, gf_ref, y_ref,
                        *rest):
    o_refs, (buf_ref, sem_ref) = rest[:-2], rest[-2:]
    i = pl.program_id(0)
    n = pl.num_programs(0)
    tokens = x_ref.shape[0]

    def start_rows(idx, slot):
        def issue(r, carry):
            for k in range(2):
                pltpu.make_async_copy(y_ref.at[pl.ds(idx[0, 0, 2 * r + k], 1)],
                                      buf_ref.at[slot, k, pl.ds(r, 1)], sem_ref.at[slot]).start(priority=k)
            return carry
        lax.fori_loop(0, tokens, issue, 0, unroll=4)

    @pl.when(i == 0)
    def _():
        start_rows(idx_ref, 0)

    @pl.when(i + 1 < n)
    def _():
        start_rows(idx_next_ref, (i + 1) % 2)

    slot = i % 2
    for k in range(2):
        pltpu.make_async_copy(y_ref.at[pl.ds(0, tokens)], buf_ref.at[slot, k], sem_ref.at[slot]).wait()
    route = route_ref[...]
    y = x_ref[...] + route[:, ROUTE_W0:ROUTE_W0 + 1] * buf_ref[slot, 0] \
        + route[:, ROUTE_W1:ROUTE_W1 + 1] * buf_ref[slot, 1]
    if final_norm:
        y = y * lax.rsqrt(jnp.mean(y * y, axis=-1, keepdims=True) + EPS) * gf_ref[...]
    if len(o_refs) == 1:
        o_refs[0][...] = y
    else:
        @pl.when(i < steps_a)
        def _():
            o_refs[0][...] = y

        @pl.when(i >= steps_a)
        def _():
            o_refs[1][...] = y


def _moe_combine(x, route, slot, y, gf, final_norm, tokens_a=None, tc=MOE_TC):
    t, d = x.shape
    steps = t // tc
    idx = slot.reshape(steps, 1, 2 * tc)
    row = lambda width: pl.BlockSpec((tc, width), lambda i: (i, 0))
    smem = lambda pos: pl.BlockSpec((1, 1, 2 * tc), lambda i: (pos(i), 0, 0), memory_space=pltpu.SMEM)
    if tokens_a is None:
        steps_a = steps
        out_specs = [row(d)]
        out_shape = [jax.ShapeDtypeStruct((t, d), F32)]
    else:
        steps_a = tokens_a // tc
        out_specs = [pl.BlockSpec((tc, d), lambda i: (jnp.minimum(i, steps_a - 1), 0)),
                     pl.BlockSpec((tc, d), lambda i: (jnp.maximum(i - steps_a, 0), 0))]
        out_shape = [jax.ShapeDtypeStruct((tokens_a, d), F32),
                     jax.ShapeDtypeStruct((t - tokens_a, d), F32)]
    return pl.pallas_call(
        functools.partial(_moe_combine_kernel, final_norm, steps_a),
        grid=(steps,),
        in_specs=[smem(lambda i: i), smem(lambda i: jnp.minimum(i + 1, steps - 1)),
                  row(d), row(LANES), pl.BlockSpec((1, d), lambda i: (0, 0)),
                  pl.BlockSpec(memory_space=pl.ANY)],
        out_specs=out_specs,
        out_shape=out_shape,
        scratch_shapes=[pltpu.VMEM((2, 2, tc, d), F32), pltpu.SemaphoreType.DMA((2,))],
        compiler_params=_params(("arbitrary",), 52),
        name="moe_combine",
    )(idx, idx, x, route, gf.reshape(1, d), y)


def _moe(hp, route, counts, x, layer, wg, wu, wd, gf, final_norm, tokens_a=None):
    slot, n_tiles, tile_expert, n_valid, next_expert, pad = _dispatch_plan(route, counts, MOE_TM)
    xs = _moe_dispatch(hp, slot, pad[0], pad[1], n_valid, n_tiles, MOE_TM)
    y = _moe_gemm(xs, tile_expert, n_valid, next_expert, layer, wg, wu, wd)
    return _moe_combine(x, route, slot, y, gf, final_norm, tokens_a)


def _router_weights(wg, bg, we, be):
    d = wg.shape[0]
    pad = LANES - N_EXPERTS - N_GROUPS
    wr = jnp.concatenate([we, wg, jnp.zeros((d, pad), F32)], axis=1).astype(F32)
    br = jnp.concatenate([be, bg, jnp.zeros((pad,), F32)]).astype(F32).reshape(1, LANES)
    wr_hi = wr.astype(BF16)
    wr_lo = (wr - wr_hi.astype(F32)).astype(BF16)
    return jnp.concatenate([wr_hi, wr_lo], axis=1), br


def _layer(xs, bsz, p, l, consts, final_g, tokens_a):
    dft, emb = consts
    w_in = p['w_in'][l].astype(BF16)
    g1 = p['ln1_g'][l]
    if len(xs) == 2:
        hy, x = _norm_proj_stack(xs[0], xs[1], g1, w_in[:, :HY_IN], F32)
    else:
        x, = xs
        hy = _norm_proj(x, g1, w_in[:, :HY_IN], F32)
    t, d = x.shape
    n = t // bsz
    hy = hy.reshape(bsz, n, HY_IN)
    att = _norm_proj(x, g1, w_in[:, HY_IN:HY_IN + ATT_IN], BF16).reshape(bsz, n, ATT_IN)
    ret = _norm_proj(x, g1, w_in[:, HY_IN + ATT_IN:], F32).reshape(bsz, n, RET_IN)
    y_h = _hyena(hy, dft, emb, p['hy_conv_w'][l], p['hy_conv_b'][l], p['hy_w1'][l], p['hy_b1'][l],
                 p['hy_w2'][l], p['hy_b2'][l], p['hy_w3'][l], p['hy_freq'][l], p['hy_decay'][l],
                 p['hy_skip'][l])
    y_a = _attention(att, p['attn_sink'][l], p['rel_bias'])
    y_r = _retention(ret, p['ret_decay'][l])
    wr, br = _router_weights(p['router_group_w'][l], p['router_group_b'][l],
                             p['router_expert_w'][l], p['router_expert_b'][l])
    x, hp, route, counts = _mix_router(x, y_h.reshape(t, -1), y_a.reshape(t, -1), y_r.reshape(t, -1),
                                       p['w_out'][l].astype(BF16), p['ln2_g'][l], wr, br)
    last = l == DEPTH - 1
    return _moe(hp, route, counts, x, l, p['moe_w_gate'], p['moe_w_up'], p['moe_w_down'], final_g,
                final_norm=last, tokens_a=tokens_a if last else None)


def kernel(x_prompt, x_sample, ln1_g, w_in, hy_conv_w, hy_conv_b, hy_w1, hy_b1, hy_w2, hy_b2, hy_w3,
           hy_freq, hy_decay, hy_skip, attn_sink, rel_bias, ret_decay, w_out, ln2_g, router_group_w,
           router_group_b, router_expert_w, router_expert_b, moe_w_gate, moe_w_up, moe_w_down, lnf_g):
    p = dict(ln1_g=ln1_g, w_in=w_in, hy_conv_w=hy_conv_w, hy_conv_b=hy_conv_b, hy_w1=hy_w1,
             hy_b1=hy_b1, hy_w2=hy_w2, hy_b2=hy_b2, hy_w3=hy_w3, hy_freq=hy_freq, hy_decay=hy_decay,
             hy_skip=hy_skip, attn_sink=attn_sink, rel_bias=rel_bias, ret_decay=ret_decay, w_out=w_out,
             ln2_g=ln2_g, router_group_w=router_group_w, router_group_b=router_group_b,
             router_expert_w=router_expert_w, router_expert_b=router_expert_b, moe_w_gate=moe_w_gate,
             moe_w_up=moe_w_up, moe_w_down=moe_w_down)
    bp, n, d = x_prompt.shape
    bs = x_sample.shape[0]
    assert x_sample.shape[1:] == (n, d)
    bsz = bp + bs
    consts = (_dft_matrices(n), _hyena_embedding(n))
    xs = (x_prompt.reshape(bp * n, d), x_sample.reshape(bs * n, d))
    for l in range(DEPTH):
        xs = _layer(xs, bsz, p, l, consts, lnf_g, bp * n)
    y_prompt, y_sample = xs
    return (y_prompt.reshape(bp, n, d), y_sample.reshape(bs, n, d))
```

```python
import functools
import math

import numpy as np
import jax
import jax.numpy as jnp
from jax import lax
from jax.experimental import pallas as pl
from jax.experimental.pallas import tpu as pltpu

F32 = jnp.float32
BF16 = jnp.bfloat16
HIGHEST = lax.Precision.HIGHEST

D_MODEL = 2048
DEPTH = 2
HEAD_DIM = 64
EPS = 1e-6
HY_WIDTH = D_MODEL // 4
ATT_WIDTH = D_MODEL // 2
RET_WIDTH = D_MODEL // 4
HY_ORDER = 2
HY_EMB = 33
HY_FFN = 64
ATT_HEADS = ATT_WIDTH // HEAD_DIM
ATT_KV_HEADS = 4
ATT_GROUP = ATT_HEADS // ATT_KV_HEADS
KV_WIDTH = ATT_KV_HEADS * HEAD_DIM
WINDOW = 128
BLOCK = 128
REL_BUCKETS = 32
REL_MAX_DIST = 128
RET_HEADS = RET_WIDTH // HEAD_DIM
RET_CHUNK = 128
ROPE_BASE = 10000.0
N_GROUPS = 4
EXPERTS_PER_GROUP = 4
N_EXPERTS = N_GROUPS * EXPERTS_PER_GROUP
D_EXPERT = D_MODEL // 2
HY_IN = 3 * HY_WIDTH
ATT_IN = ATT_WIDTH + 2 * KV_WIDTH
RET_IN = 4 * RET_WIDTH

LANES = 128
SUBLANES = 8
MIB = 1024 * 1024


def _params(semantics, vmem_mib):
    return pltpu.CompilerParams(dimension_semantics=semantics, vmem_limit_bytes=vmem_mib * MIB)


def _dot(a, b):
    return jnp.dot(a, b, preferred_element_type=F32)


def _dot_nt(a, b):
    return lax.dot_general(a, b, (((1,), (1,)), ((), ())), preferred_element_type=F32)


def _norm_proj_kernel(x_ref, g_ref, *refs):
    w_refs, o_refs = refs[:len(refs) // 2], refs[len(refs) // 2:]
    x = x_ref[...]
    inv = lax.rsqrt(jnp.mean(x * x, axis=-1, keepdims=True) + EPS)
    h = (x * inv * g_ref[...]).astype(BF16)
    for w_ref, o_ref in zip(w_refs, o_refs):
        o_ref[...] = _dot(h, w_ref[...]).astype(o_ref.dtype)


def _norm_proj(x, g, weights, out_dtypes, tm=512):
    t, d = x.shape
    return pl.pallas_call(
        _norm_proj_kernel,
        grid=(t // tm,),
        in_specs=[pl.BlockSpec((tm, d), lambda i: (i, 0)), pl.BlockSpec((1, d), lambda i: (0, 0))]
        + [pl.BlockSpec(w.shape, lambda i: (0, 0), pipeline_mode=pl.Buffered(1)) for w in weights],
        out_specs=[pl.BlockSpec((tm, w.shape[1]), lambda i: (i, 0)) for w in weights],
        out_shape=[jax.ShapeDtypeStruct((t, w.shape[1]), dt) for w, dt in zip(weights, out_dtypes)],
        compiler_params=_params(("parallel",), 56),
        name="norm_proj",
    )(x, g.reshape(1, d), *weights)


def _norm_proj_stack_kernel(tiles_a, xa_ref, xb_ref, g_ref, w_ref, o_ref, x_ref):
    x = jnp.where(pl.program_id(0) < tiles_a, xa_ref[...], xb_ref[...])
    x_ref[...] = x
    inv = lax.rsqrt(jnp.mean(x * x, axis=-1, keepdims=True) + EPS)
    h = (x * inv * g_ref[...]).astype(BF16)
    o_ref[...] = _dot(h, w_ref[...]).astype(o_ref.dtype)


def _norm_proj_stack(xa, xb, g, w, out_dtype, tm=512):
    ta, d = xa.shape
    tb = xb.shape[0]
    n = w.shape[1]
    tiles_a = ta // tm
    return pl.pallas_call(
        functools.partial(_norm_proj_stack_kernel, tiles_a),
        grid=((ta + tb) // tm,),
        in_specs=[
            pl.BlockSpec((tm, d), lambda i: (jnp.minimum(i, tiles_a - 1), 0)),
            pl.BlockSpec((tm, d), lambda i: (jnp.maximum(i - tiles_a, 0), 0)),
            pl.BlockSpec((1, d), lambda i: (0, 0)),
            pl.BlockSpec((d, n), lambda i: (0, 0)),
        ],
        out_specs=[pl.BlockSpec((tm, n), lambda i: (i, 0)), pl.BlockSpec((tm, d), lambda i: (i, 0))],
        out_shape=[jax.ShapeDtypeStruct((ta + tb, n), out_dtype),
                   jax.ShapeDtypeStruct((ta + tb, d), xa.dtype)],
        compiler_params=_params(("parallel",), 56),
        name="norm_proj_stack",
    )(xa, xb, g.reshape(1, d), w)


def _short_conv(x, prev_row, next_row, w_ref, b_ref):
    n = x.shape[0]
    row = lax.broadcasted_iota(jnp.int32, x.shape, 0)
    prev = jnp.where(row == 0, prev_row, pltpu.roll(x, 1, 0))
    nxt = jnp.where(row == n - 1, next_row, pltpu.roll(x, n - 1, 0))
    return prev * w_ref[0:1, :] + x * w_ref[1:2, :] + nxt * w_ref[2:3, :] + b_ref[...]


def _split_rows(x, slab_ref):
    half = x.shape[0] // 2
    slabs = x.shape[1] // LANES
    for j in range(slabs):
        slab_ref[j] = x[:, j * LANES:(j + 1) * LANES]
    rows = lambda start: jnp.concatenate(
        [slab_ref[j, pl.ds(start, half, stride=2), :] for j in range(slabs)], axis=1)
    return rows(0), rows(1)


def _merge_rows(x_e, x_o, slab_ref):
    half = x_e.shape[0]
    slabs = x_e.shape[1] // LANES
    for j in range(slabs):
        slab_ref[j, pl.ds(0, half, stride=2), :] = x_e[:, j * LANES:(j + 1) * LANES]
        slab_ref[j, pl.ds(1, half, stride=2), :] = x_o[:, j * LANES:(j + 1) * LANES]
    return jnp.concatenate([slab_ref[j] for j in range(slabs)], axis=1)


def _hy_filter_kernel(z_ref, w1_ref, b1_ref, w2_ref, b2_ref, fr_ref, w3f_ref, w3b_ref,
                      decf_ref, decb_ref, hs_ref, hd_ref):
    fr = fr_ref[...]
    h = jnp.sin(fr * (jnp.dot(z_ref[...], w1_ref[...], precision=HIGHEST,
                              preferred_element_type=F32) + b1_ref[...]))
    h = jnp.sin(fr * (jnp.dot(h, w2_ref[...], precision=HIGHEST,
                              preferred_element_type=F32) + b2_ref[...]))
    n = h.shape[0]
    shape = (n, w3f_ref.shape[1])
    row = lax.broadcasted_iota(jnp.int32, shape, 0)
    t = row.astype(F32) * (1.0 / (n - 1))
    hf = jnp.dot(h, w3f_ref[...], precision=HIGHEST, preferred_element_type=F32)
    hf = hf * jnp.exp(-t * jnp.abs(decf_ref[...]))
    hb = jnp.dot(h, w3b_ref[...], precision=HIGHEST, preferred_element_type=F32)
    hb = hb * jnp.exp(-t * jnp.abs(decb_ref[...]))
    hb = jnp.where(row == 0, 0.0, hb)
    norm = (jnp.sum(jnp.abs(hf), axis=0, keepdims=True)
            + jnp.sum(jnp.abs(hb), axis=0, keepdims=True))
    hs_ref[...] = (hf + hb) / norm
    hd_ref[...] = (hb - hf) / norm


def _hy_filter(z, w1, b1, w2, b2, fr, w3, decay, tc=256):
    n = z.shape[0]
    cols = HY_ORDER * HY_WIDTH
    w3f, w3b = w3[:, :cols], w3[:, cols:]
    dec = decay.reshape(2, cols)
    zp = jnp.pad(z, ((0, 0), (0, LANES - HY_EMB)))
    w1p = jnp.pad(w1, ((0, LANES - HY_EMB), (0, 0)))
    full = lambda shape: pl.BlockSpec(shape, lambda j: (0, 0))
    col = lambda rows: pl.BlockSpec((rows, tc), lambda j: (0, j))
    return pl.pallas_call(
        _hy_filter_kernel,
        grid=(cols // tc,),
        in_specs=[full((n, LANES)), full((LANES, HY_FFN)), full((1, HY_FFN)), full((HY_FFN, HY_FFN)),
                  full((1, HY_FFN)), full((1, HY_FFN)), col(HY_FFN), col(HY_FFN), col(1), col(1)],
        out_specs=[col(n), col(n)],
        out_shape=[jax.ShapeDtypeStruct((n, cols), F32)] * 2,
        compiler_params=_params(("parallel",), 48),
        name="hy_filter",
    )(zp, w1p, b1.reshape(1, -1), w2, b2.reshape(1, -1), fr.reshape(1, -1), w3f, w3b,
      dec[0:1], dec[1:2])


def _alternating_sum(x):
    row = lax.broadcasted_iota(jnp.int32, x.shape, 0)
    return jnp.sum(x * (1 - 2 * (row & 1)).astype(F32), axis=0, keepdims=True)


def _half_spectra(fw_ref, rows, x_e, x_o, y_e, y_o):
    ec, oc = _dot(fw_ref[0, rows, :], x_e), _dot(fw_ref[1, rows, :], x_o)
    es, os_ = _dot(fw_ref[2, rows, :], y_e), _dot(fw_ref[3, rows, :], y_o)
    return ec + oc, ec - oc, es + os_, os_ - es


def _hy_kf_kernel(fw_ref, hs_ref, hd_ref, k_ref, slab_ref):
    hs_e, hs_o = _split_rows(hs_ref[...], slab_ref)
    hd_e, hd_o = _split_rows(hd_ref[...], slab_ref)
    k0, k1, k2, k3 = _half_spectra(fw_ref, slice(None), hs_e.astype(BF16), hs_o.astype(BF16),
                                   hd_e.astype(BF16), hd_o.astype(BF16))
    first = jnp.logical_and(lax.broadcasted_iota(jnp.int32, k0.shape, 0) == 0, pl.program_id(0) == 0)
    k_ref[0] = k0
    k_ref[1] = k1
    k_ref[2] = jnp.where(first, _alternating_sum(hs_e), k2)
    k_ref[3] = jnp.where(first, _alternating_sum(hd_o), k3)


def _hy_kf(fw, hs, hd, tf=256, tc=512):
    n, cols = hs.shape
    h = n // 2
    return pl.pallas_call(
        _hy_kf_kernel,
        grid=(h // tf, cols // tc),
        in_specs=[pl.BlockSpec((4, tf, h), lambda i, j: (0, i, 0)),
                  pl.BlockSpec((n, tc), lambda i, j: (0, j)),
                  pl.BlockSpec((n, tc), lambda i, j: (0, j))],
        out_specs=pl.BlockSpec((4, tf, tc), lambda i, j: (0, i, j)),
        out_shape=jax.ShapeDtypeStruct((4, h, cols), F32),
        scratch_shapes=[pltpu.VMEM((tc // LANES, n, LANES), F32)],
        compiler_params=_params(("parallel", "arbitrary"), 48),
        name="hy_filter_dft",
    )(fw, hs, hd)


HY_FREQ_TILE = 256


def _hy_fwd_kernel(conv, fw_ref, u_ref, k_ref, *rest):
    half = fw_ref.shape[1]
    u = u_ref[0]
    if conv:
        w_ref, b_ref, y_ref, slab_ref = rest
        u = _short_conv(u, 0.0, 0.0, w_ref, b_ref)
    else:
        y_ref, slab_ref = rest
    u_e, u_o = _split_rows(u, slab_ref)
    mid_a, mid_b = _alternating_sum(u_e), _alternating_sum(u_o)
    u_e, u_o = u_e.astype(BF16), u_o.astype(BF16)
    tf = HY_FREQ_TILE
    for i in range(half // tf):
        rows = slice(i * tf, (i + 1) * tf)
        p0, p1, p2, p3 = _half_spectra(fw_ref, rows, u_e, u_o, u_e, u_o)
        k0, k1, k2, k3 = (k_ref[j, rows, :] for j in range(4))
        y0, y2 = p0 * k0 + p2 * k2, p0 * k2 - p2 * k0
        y1, y3 = p1 * k1 + p3 * k3, p1 * k3 - p3 * k1
        he, ho = y2 - y3, y2 + y3
        if i == 0:
            first = lax.broadcasted_iota(jnp.int32, p0.shape, 0) == 0
            y0 = jnp.where(first, p0 * k0, y0)
            y1 = jnp.where(first, p1 * k1, y1)
            he = jnp.where(first, mid_a * k2 + mid_b * k3, he)
            ho = jnp.where(first, mid_a * k3 - mid_b * k2, ho)
        y_ref[0, 0, rows, :] = (y0 + y1).astype(BF16)
        y_ref[0, 1, rows, :] = he.astype(BF16)
        y_ref[0, 2, rows, :] = (y0 - y1).astype(BF16)
        y_ref[0, 3, rows, :] = ho.astype(BF16)


def _hy_fwd(fw, u3, ucol, k, kcol, conv=None):
    bsz, n, _ = u3.shape
    c = HY_WIDTH
    h = n // 2
    in_specs = [
        pl.BlockSpec((4, h, h), lambda b: (0, 0, 0), pipeline_mode=pl.Buffered(1)),
        pl.BlockSpec((1, n, c), lambda b: (b, 0, ucol)),
        pl.BlockSpec((4, h, c), lambda b: (0, 0, kcol), pipeline_mode=pl.Buffered(1)),
    ]
    args = [fw, u3, k]
    if conv is not None:
        in_specs += [pl.BlockSpec((3, c), lambda b: (0, ucol)), pl.BlockSpec((1, c), lambda b: (0, ucol))]
        args += list(conv)
    return pl.pallas_call(
        functools.partial(_hy_fwd_kernel, conv is not None),
        grid=(bsz,),
        in_specs=in_specs,
        out_specs=pl.BlockSpec((1, 4, h, c), lambda b: (b, 0, 0, 0)),
        out_shape=jax.ShapeDtypeStruct((bsz, 4, h, c), BF16),
        scratch_shapes=[pltpu.VMEM((c // LANES, n, LANES), F32)],
        compiler_params=_params(("arbitrary",), 52),
        name="hy_dft_fwd",
    )(*args)


def _hy_inv_kernel(s_conv, iv_ref, y_ref, d_ref, *rest):
    o_ref, slab_ref = rest[-2:]
    i = pl.program_id(0)
    first, last = i == 0, i == pl.num_programs(0) - 1

    def conv_tile(x_ref, lo_ref, hi_ref, w_ref, b_ref):
        prev_row = jnp.where(first, 0.0, lo_ref[0, SUBLANES - 1:SUBLANES, :])
        next_row = jnp.where(last, 0.0, hi_ref[0, 0:1, :])
        return _short_conv(x_ref[0], prev_row, next_row, w_ref, b_ref)

    if s_conv:
        skip = conv_tile(*rest[0:5])
        gate = conv_tile(*rest[5:10])
    else:
        skip = rest[0][0]
        gate = conv_tile(*rest[1:6])
    y_e = _dot(iv_ref[0], y_ref[0, 0]) + _dot(iv_ref[1], y_ref[0, 1])
    y_o = _dot(iv_ref[2], y_ref[0, 2]) + _dot(iv_ref[3], y_ref[0, 3])
    y = _merge_rows(y_e, y_o, slab_ref)
    o_ref[0] = (gate * (y + skip * d_ref[...])).astype(o_ref.dtype)


def _hy_inv(iv, y, d, raw3, conv, gcol, out_dtype, scol=None, s3=None, tt=1024):
    bsz, _, h, c = y.shape
    n = 2 * h
    conv_w, conv_b = conv
    halo = tt // SUBLANES
    tile = lambda col: pl.BlockSpec((1, tt, c), lambda i, b: (b, i, col))
    lo = lambda col: pl.BlockSpec((1, SUBLANES, c), lambda i, b: (b, jnp.maximum(i * halo - 1, 0), col))
    hi = lambda col: pl.BlockSpec((1, SUBLANES, c),
                                  lambda i, b: (b, jnp.minimum((i + 1) * halo, n // SUBLANES - 1), col))
    wspec = lambda col: pl.BlockSpec((3, c), lambda i, b: (0, col))
    bspec = lambda col: pl.BlockSpec((1, c), lambda i, b: (0, col))
    conv_in = lambda col: ([tile(col), lo(col), hi(col), wspec(col), bspec(col)],
                           [raw3, raw3, raw3, conv_w, conv_b])
    in_specs = [pl.BlockSpec((4, tt // 2, h), lambda i, b: (0, i, 0)),
                pl.BlockSpec((1, 4, h, c), lambda i, b: (b, 0, 0, 0)),
                pl.BlockSpec((1, c), lambda i, b: (0, 0))]
    args = [iv, y, d.reshape(1, c)]
    if s3 is None:
        specs, arrs = conv_in(scol)
    else:
        specs, arrs = [tile(0)], [s3]
    in_specs += specs
    args += arrs
    specs, arrs = conv_in(gcol)
    in_specs += specs
    args += arrs
    return pl.pallas_call(
        functools.partial(_hy_inv_kernel, s3 is None),
        grid=(n // tt, bsz),
        in_specs=in_specs,
        out_specs=pl.BlockSpec((1, tt, c), lambda i, b: (b, i, 0)),
        out_shape=jax.ShapeDtypeStruct((bsz, n, c), out_dtype),
        scratch_shapes=[pltpu.VMEM((c // LANES, tt, LANES), F32)],
        compiler_params=_params(("arbitrary", "arbitrary"), 48),
        name="hy_dft_inv",
    )(*args)


DFT_SPLIT = 32


def _dft_matrices(n):
    h = n // 2

    def cos_sin(offset):
        def table(rows, step):
            f = lax.broadcasted_iota(jnp.int32, (rows, h), 0) * step
            t = 2 * lax.broadcasted_iota(jnp.int32, (rows, h), 1) + offset
            ang = ((f * t) & (2 * n - 1)).astype(F32) * (math.pi / n)
            return jnp.cos(ang), jnp.sin(ang)
        cos_hi, sin_hi = (x[:, None, :] for x in table(h // DFT_SPLIT, DFT_SPLIT))
        cos_lo, sin_lo = (x[None, :, :] for x in table(DFT_SPLIT, 1))
        return ((cos_hi * cos_lo - sin_hi * sin_lo).reshape(h, h),
                (sin_hi * cos_lo + cos_hi * sin_lo).reshape(h, h))

    cos_e, sin_e = cos_sin(0)
    cos_o, sin_o = cos_sin(1)
    fwd = jnp.stack([cos_e, cos_o, sin_e, sin_o]).astype(BF16)
    m = lax.broadcasted_iota(jnp.int32, (h, h), 0)
    f = lax.broadcasted_iota(jnp.int32, (h, h), 1)
    weight = jnp.where(f == 0, 0.5 / n, 1.0 / n)
    mid = (1 - 2 * (m & 1)).astype(F32) * (1.0 / n)
    inv = jnp.stack([weight * cos_e.T, jnp.where(f == 0, mid, sin_e.T * (-1.0 / n)),
                     weight * cos_o.T, jnp.where(f == 0, -mid, sin_o.T * (-1.0 / n))]).astype(BF16)
    return fwd, inv


def _hyena_embedding(n):
    t = jnp.linspace(0.0, 1.0, n, dtype=F32)[:, None]
    bands = (HY_EMB - 1) // 2
    f = jnp.linspace(1e-4, bands - 1, bands, dtype=F32)[None]
    w = 2.0 * math.pi * jnp.arange(n, dtype=F32)[:, None] / n
    return jnp.concatenate([t, jnp.cos(f * w), -jnp.sin(f * w)], -1)


def _hyena(p3, dft, emb, conv_w, conv_b, w1, b1, w2, b2, w3, freq, decay, skip):
    ff, fi = dft
    conv = (conv_w, conv_b.reshape(1, -1))
    hs, hd = _hy_filter(emb, w1, b1, w2, b2, freq, w3, decay)
    k = _hy_kf(ff, hs, hd)
    y = _hy_fwd(ff, p3, 0, k, 0, conv)
    z = _hy_inv(fi, y, skip[0], p3, conv, 1, F32, scol=0)
    y = _hy_fwd(ff, z, 0, k, 1)
    return _hy_inv(fi, y, skip[1], p3, conv, 2, BF16, s3=z)


def _t5_bucket(rel):
    nb = REL_BUCKETS // 2
    max_exact = nb // 2
    n = np.abs(rel)
    large = max_exact + (np.log(np.maximum(n, 1) / max_exact) / np.log(REL_MAX_DIST / max_exact)
                         * (nb - max_exact)).astype(np.int32)
    large = np.minimum(large, nb - 1)
    return (rel > 0).astype(np.int32) * nb + np.where(n < max_exact, n, large)


def _attn_kernel(q_ref, kp_ref, kc_ref, kn_ref, vp_ref, vc_ref, vn_ref, bias_ref, sink_ref, o_ref):
    n = pl.program_id(1)
    q = q_ref[0] * (HEAD_DIM ** -0.5)
    k = jnp.concatenate([kp_ref[0], kc_ref[0], kn_ref[0]], axis=0)
    vt = jnp.concatenate([vp_ref[0], vc_ref[0], vn_ref[0]], axis=1)
    edge_prev = jnp.where(n == 0, -1e30, 0.0).astype(F32)
    edge_next = jnp.where(n == pl.num_programs(1) - 1, -1e30, 0.0).astype(F32)
    ones = jnp.ones((HEAD_DIM, 3 * BLOCK), BF16)
    for kh in range(ATT_KV_HEADS):
        k_h = k[:, kh * HEAD_DIM:(kh + 1) * HEAD_DIM]
        vt_h = jnp.concatenate([vt[kh * HEAD_DIM:(kh + 1) * HEAD_DIM], ones], axis=0)
        q_h = jnp.concatenate([q[:, (kh * ATT_GROUP + g) * HEAD_DIM:(kh * ATT_GROUP + g + 1) * HEAD_DIM]
                               for g in range(ATT_GROUP)], axis=0)
        s = _dot_nt(k_h, q_h) + bias_ref[kh]
        s = jnp.concatenate([s[:BLOCK] + edge_prev, s[BLOCK:2 * BLOCK], s[2 * BLOCK:] + edge_next],
                            axis=0)
        sink = sink_ref[kh]
        m = jnp.maximum(jnp.max(s, axis=0, keepdims=True), sink)
        p = jnp.exp(s - m).astype(BF16)
        o = _dot(vt_h, p)
        o = o[:HEAD_DIM] / (o[HEAD_DIM:HEAD_DIM + 1] + jnp.exp(sink - m))
        for g in range(0, ATT_GROUP, 2):
            pair = jnp.concatenate([o[:, g * BLOCK:(g + 1) * BLOCK],
                                    o[:, (g + 1) * BLOCK:(g + 2) * BLOCK]], axis=0)
            c0 = (kh * ATT_GROUP + g) * HEAD_DIM
            o_ref[0, :, c0:c0 + 2 * HEAD_DIM] = pair.T.astype(o_ref.dtype)


def _attention_bias(rel_bias):
    rel = (np.arange(3 * BLOCK)[None, :] - BLOCK) - np.arange(BLOCK)[:, None]
    bucket = jnp.asarray(_t5_bucket(rel).reshape(-1), jnp.int32)
    onehot = (bucket[:, None] == jnp.arange(REL_BUCKETS, dtype=jnp.int32)[None]).astype(F32)
    bias = jnp.dot(onehot, rel_bias.astype(F32), precision=HIGHEST)
    bias = bias.reshape(BLOCK, 3 * BLOCK, ATT_KV_HEADS, ATT_GROUP)
    bias = jnp.where(jnp.asarray(np.abs(rel) <= WINDOW)[:, :, None, None], bias, -1e30)
    return jnp.transpose(bias, (2, 1, 3, 0)).reshape(ATT_KV_HEADS, 3 * BLOCK, ATT_GROUP * BLOCK)


def _attention(att3, sink, rel_bias):
    bsz, n, _ = att3.shape
    nb = n // BLOCK
    bias = _attention_bias(rel_bias)
    sink_row = jnp.repeat(sink.astype(F32).reshape(ATT_KV_HEADS, 1, ATT_GROUP), BLOCK, axis=2)
    vt = jnp.swapaxes(att3[:, :, ATT_WIDTH + KV_WIDTH:], 1, 2)
    kcol = ATT_WIDTH // KV_WIDTH
    prev = lambda i: jnp.maximum(i - 1, 0)
    nxt = lambda i: jnp.minimum(i + 1, nb - 1)
    cur = lambda i: i
    k_spec = lambda blk: pl.BlockSpec((1, BLOCK, KV_WIDTH), lambda b, i: (b, blk(i), kcol))
    v_spec = lambda blk: pl.BlockSpec((1, KV_WIDTH, BLOCK), lambda b, i: (b, 0, blk(i)))
    return pl.pallas_call(
        _attn_kernel,
        grid=(bsz, nb),
        in_specs=[
            pl.BlockSpec((1, BLOCK, ATT_WIDTH), lambda b, i: (b, i, 0)),
            k_spec(prev), k_spec(cur), k_spec(nxt), v_spec(prev), v_spec(cur), v_spec(nxt),
            pl.BlockSpec((ATT_KV_HEADS, 3 * BLOCK, ATT_GROUP * BLOCK), lambda b, i: (0, 0, 0)),
            pl.BlockSpec((ATT_KV_HEADS, 1, ATT_GROUP * BLOCK), lambda b, i: (0, 0, 0)),
        ],
        out_specs=pl.BlockSpec((1, BLOCK, ATT_WIDTH), lambda b, i: (b, i, 0)),
        out_shape=jax.ShapeDtypeStruct((bsz, n, ATT_WIDTH), BF16),
        compiler_params=_params(("parallel", "parallel"), 32),
        name="window_attention",
    )(att3, att3, att3, att3, vt, vt, vt, bias, sink_row)


def _rotary_tables(n):
    inv = ROPE_BASE ** -jnp.linspace(0.0, 1.0, HEAD_DIM // 2, dtype=F32)
    ang = jnp.arange(n, dtype=F32)[:, None] * inv[None]
    cos, sin = jnp.cos(ang), jnp.sin(ang)
    cos_t = jnp.tile(jnp.concatenate([cos, cos], -1), (1, RET_HEADS))
    sin_t = jnp.tile(jnp.concatenate([-sin, sin], -1), (1, RET_HEADS))
    return cos_t, sin_t


def _rotate(x, cos_t, sin_t):
    width = x.shape[-1]
    half = HEAD_DIM // 2
    lane = lax.broadcasted_iota(jnp.int32, x.shape, 1)
    partner = jnp.where((lane & (HEAD_DIM - 1)) < half,
                        pltpu.roll(x, width - half, 1), pltpu.roll(x, half, 1))
    return x * cos_t + partner * sin_t


RET_PAIR = 2 * HEAD_DIM
RET_PAIRS = RET_WIDTH // RET_PAIR


def _same_head(shape):
    r = lax.broadcasted_iota(jnp.int32, shape, 0) // HEAD_DIM
    c = lax.broadcasted_iota(jnp.int32, shape, 1) // HEAD_DIM
    return r == c


RET_STATE_CHUNKS = 4
RET_OUT_CHUNKS = 2


def _ret_state_kernel(kf_ref, vf_ref, kb_ref, vb_ref, cf_ref, sf_ref, cb_ref, sb_ref, lg_ref,
                      of_ref, ob_ref, stf_ref, stb_ref):
    @pl.when(pl.program_id(1) == 0)
    def _():
        stf_ref[...] = jnp.zeros_like(stf_ref)
        stb_ref[...] = jnp.zeros_like(stb_ref)

    c = RET_CHUNK
    j = lax.broadcasted_iota(jnp.int32, (c, RET_WIDTH), 0).astype(F32)
    lgf = lg_ref[0:1, :]
    lgb = lg_ref[1:2, :]
    scale = HEAD_DIM ** -0.5
    same = _same_head((RET_PAIR, RET_PAIR))

    def scan(st_ref, o_ref, k_ref, v_ref, cos_ref, sin_ref, zeta, lg, order):
        grow = jnp.exp(c * lg)
        for sub in order:
            rows = slice(sub * c, (sub + 1) * c)
            o_ref[0, sub] = st_ref[...].astype(o_ref.dtype)
            k = (_rotate(k_ref[0, rows, :], cos_ref[rows, :], sin_ref[rows, :]) * scale * zeta).astype(BF16)
            v = v_ref[0, rows, :].astype(BF16)
            for p in range(RET_PAIRS):
                lanes = slice(p * RET_PAIR, (p + 1) * RET_PAIR)
                kv = lax.dot_general(k[:, lanes], v[:, lanes], (((0,), (0,)), ((), ())),
                                     preferred_element_type=F32)
                st_ref[p] = grow[:, lanes] * st_ref[p] + jnp.where(same, kv, 0.0)

    chunks = range(RET_STATE_CHUNKS)
    scan(stf_ref, of_ref, kf_ref, vf_ref, cf_ref, sf_ref, jnp.exp((c - 1.0 - j) * lgf), lgf, chunks)
    scan(stb_ref, ob_ref, kb_ref, vb_ref, cb_ref, sb_ref, jnp.exp(j * lgb), lgb, reversed(chunks))


def _ret_states(ret3, cos_t, sin_t, lg_lane):
    bsz, n, _ = ret3.shape
    nc = n // RET_CHUNK
    rows, w = RET_STATE_CHUNKS * RET_CHUNK, RET_WIDTH
    steps = n // rows
    fwd = lambda b, i: i
    bwd = lambda b, i: steps - 1 - i
    col = lambda pos, cb: pl.BlockSpec((1, rows, w), lambda b, i: (b, pos(b, i), cb))
    tab = lambda pos: pl.BlockSpec((rows, w), lambda b, i: (pos(b, i), 0))
    st_shape = (RET_PAIRS, RET_PAIR, RET_PAIR)
    out = lambda pos: pl.BlockSpec((1, RET_STATE_CHUNKS) + st_shape, lambda b, i: (b, pos(b, i), 0, 0, 0))
    return pl.pallas_call(
        _ret_state_kernel,
        grid=(bsz, steps),
        in_specs=[col(fwd, 1), col(fwd, 2), col(bwd, 1), col(bwd, 2),
                  tab(fwd), tab(fwd), tab(bwd), tab(bwd),
                  pl.BlockSpec((2, w), lambda b, i: (0, 0))],
        out_specs=[out(fwd), out(bwd)],
        out_shape=[jax.ShapeDtypeStruct((bsz, nc) + st_shape, BF16)] * 2,
        scratch_shapes=[pltpu.VMEM(st_shape, F32), pltpu.VMEM(st_shape, F32)],
        compiler_params=_params(("parallel", "arbitrary"), 32),
        name="retention_states",
    )(ret3, ret3, ret3, ret3, cos_t, sin_t, cos_t, sin_t, lg_lane)


def _ret_out_kernel(q_ref, k_ref, v_ref, g_ref, cos_ref, sin_ref, lg_ref, lgs_ref, stf_ref, stb_ref,
                    o_ref):
    c, w = RET_CHUNK, RET_WIDTH
    pos = lax.broadcasted_iota(jnp.int32, (c, w), 0).astype(F32)
    xi_f = jnp.exp((pos + 1.0) * lg_ref[0:1, :])
    xi_b = jnp.exp((c - pos) * lg_ref[1:2, :])
    diff = (lax.broadcasted_iota(jnp.int32, (c, c), 0)
            - lax.broadcasted_iota(jnp.int32, (c, c), 1)).astype(F32)
    first = lax.broadcasted_iota(jnp.int32, (c, RET_PAIR), 1) < HEAD_DIM
    avg = jnp.where(_same_head((RET_PAIR, RET_PAIR)), 1.0 / HEAD_DIM, 0.0).astype(BF16)
    decays = [jnp.concatenate(
        [jnp.exp(jnp.where(diff >= 0, diff * lgs_ref[0, h], -diff * lgs_ref[1, h]))
         for h in (2 * p, 2 * p + 1)], axis=0) for p in range(RET_PAIRS)]
    for sub in range(RET_OUT_CHUNKS):
        rows = slice(sub * c, (sub + 1) * c)
        cos_t, sin_t = cos_ref[rows, :], sin_ref[rows, :]
        q = _rotate(q_ref[0, rows, :], cos_t, sin_t)
        k = (_rotate(k_ref[0, rows, :], cos_t, sin_t) * (HEAD_DIM ** -0.5)).astype(BF16)
        v = v_ref[0, rows, :].astype(BF16)
        g = g_ref[0, rows, :]
        q_f = (q * xi_f).astype(BF16)
        q_b = (q * xi_b).astype(BF16)
        for p in range(RET_PAIRS):
            lanes = slice(p * RET_PAIR, (p + 1) * RET_PAIR)
            q_p = q[:, lanes]
            qs = jnp.concatenate([jnp.where(first, q_p, 0.0), jnp.where(first, 0.0, q_p)], axis=0)
            s = _dot_nt(qs.astype(BF16), k[:, lanes])
            o2 = _dot((s * decays[p]).astype(BF16), v[:, lanes])
            y = jnp.where(first, o2[:c], o2[c:])
            y = y + _dot(q_f[:, lanes], stf_ref[0, sub, p]) + _dot(q_b[:, lanes], stb_ref[0, sub, p])
            sq = y * y
            sq_hi = sq.astype(BF16)
            sq_lo = (sq - sq_hi.astype(F32)).astype(BF16)
            ms = _dot(sq_hi, avg) + _dot(sq_lo, avg)
            y = y * lax.rsqrt(ms + EPS)
            g_p = g[:, lanes]
            o_ref[0, rows, lanes] = (g_p / (1.0 + jnp.exp(-g_p)) * y).astype(o_ref.dtype)


def _retention(ret3, decay):
    bsz, n, _ = ret3.shape
    rows, w = RET_OUT_CHUNKS * RET_CHUNK, RET_WIDTH
    lg = -jnp.exp(decay.astype(F32))
    lg_lane = jnp.repeat(lg, HEAD_DIM, axis=1)
    cos_t, sin_t = _rotary_tables(n)
    stf, stb = _ret_states(ret3, cos_t, sin_t, lg_lane)
    col = lambda cb: pl.BlockSpec((1, rows, w), lambda b, i: (b, i, cb))
    tab = pl.BlockSpec((rows, w), lambda b, i: (i, 0))
    st = pl.BlockSpec((1, RET_OUT_CHUNKS, RET_PAIRS, RET_PAIR, RET_PAIR), lambda b, i: (b, i, 0, 0, 0))
    return pl.pallas_call(
        _ret_out_kernel,
        grid=(bsz, n // rows),
        in_specs=[col(0), col(1), col(2), col(3), tab, tab,
                  pl.BlockSpec((2, w), lambda b, i: (0, 0)),
                  pl.BlockSpec(memory_space=pltpu.SMEM), st, st],
        out_specs=pl.BlockSpec((1, rows, w), lambda b, i: (b, i, 0)),
        out_shape=jax.ShapeDtypeStruct((bsz, n, w), BF16),
        compiler_params=_params(("parallel", "parallel"), 32),
        name="retention_out",
    )(ret3, ret3, ret3, ret3, cos_t, sin_t, lg_lane, lg, stf, stb)


ROUTE_I0, ROUTE_I1, ROUTE_W0, ROUTE_W1, ROUTE_R0, ROUTE_R1 = 0, 1, 2, 3, 4, 5


def _mix_router_kernel(x_ref, yh_ref, ya_ref, yr_ref, w_ref, g_ref, wr_ref, br_ref,
                       xo_ref, hp_ref, route_ref, counts_ref, carry_ref, logits_ref):
    i = pl.program_id(0)

    @pl.when(i == 0)
    def _():
        carry_ref[...] = jnp.zeros_like(carry_ref)
        logits_ref[...] = jnp.zeros_like(logits_ref)

    logits = logits_ref[(i + 1) % 2]
    live = jnp.where(i >= 1, 1.0, 0.0).astype(F32)
    lane = lax.broadcasted_iota(jnp.int32, logits.shape, 1)
    big = jnp.int32(LANES)
    neg = -jnp.inf
    is_group = (lane >= N_EXPERTS) & (lane < N_EXPERTS + N_GROUPS)
    gl = jnp.where(is_group, logits, neg)
    gmax = jnp.max(gl, axis=-1, keepdims=True)
    g_w = 1.0 / jnp.sum(jnp.exp(gl - gmax), axis=-1, keepdims=True)
    g_sel = jnp.min(jnp.where(gl == gmax, lane, big), axis=-1, keepdims=True) - N_EXPERTS
    lo = g_sel * EXPERTS_PER_GROUP
    el = jnp.where((lane >= lo) & (lane < lo + EXPERTS_PER_GROUP), logits, neg)
    v0 = jnp.max(el, axis=-1, keepdims=True)
    i0 = jnp.min(jnp.where(el == v0, lane, big), axis=-1, keepdims=True)
    el = jnp.where(lane == i0, neg, el)
    v1 = jnp.max(el, axis=-1, keepdims=True)
    i1 = jnp.min(jnp.where(el == v1, lane, big), axis=-1, keepdims=True)
    e1 = jnp.exp(v1 - v0)
    w0 = g_w / (1.0 + e1)
    oh0 = jnp.where(lane == i0, 1.0, 0.0)
    oh1 = jnp.where(lane == i1, 1.0, 0.0)
    tm = logits.shape[0]
    earlier = jnp.where(lax.broadcasted_iota(jnp.int32, (tm, tm), 1)
                        < lax.broadcasted_iota(jnp.int32, (tm, tm), 0), 1.0, 0.0).astype(BF16)
    carry = carry_ref[...]
    tot0 = jnp.sum(oh0, axis=0, keepdims=True)
    before0 = _dot(earlier, oh0.astype(BF16)) + carry
    before1 = _dot(earlier, oh1.astype(BF16)) + (carry + tot0)
    r0 = jnp.sum(oh0 * before0, axis=-1, keepdims=True)
    r1 = jnp.sum(oh1 * before1, axis=-1, keepdims=True)
    carry = carry + live * (tot0 + jnp.sum(oh1, axis=0, keepdims=True))
    carry_ref[...] = carry
    counts_ref[...] = jnp.broadcast_to(carry, counts_ref.shape)
    route_ref[...] = (jnp.where(lane == ROUTE_I0, i0.astype(F32), 0.0)
                      + jnp.where(lane == ROUTE_I1, i1.astype(F32), 0.0)
                      + jnp.where(lane == ROUTE_W0, w0, 0.0)
                      + jnp.where(lane == ROUTE_W1, w0 * e1, 0.0)
                      + jnp.where(lane == ROUTE_R0, r0, 0.0)
                      + jnp.where(lane == ROUTE_R1, r1, 0.0))

    mix = jnp.concatenate([yh_ref[...], ya_ref[...], yr_ref[...]], axis=1)
    x = x_ref[...] + _dot(mix, w_ref[...])
    xo_ref[...] = x
    h = x * lax.rsqrt(jnp.mean(x * x, axis=-1, keepdims=True) + EPS) * g_ref[...]
    h_hi = h.astype(BF16)
    bits = lax.bitcast_convert_type(h_hi.astype(F32), jnp.uint32)
    half = bits.shape[1] // 2
    hp_ref[...] = bits[:, :half] | (bits[:, half:] >> 16)
    h_lo = (h - h_hi.astype(F32)).astype(BF16)
    both = _dot(h_hi, wr_ref[...])
    logits_ref[i % 2] = (both[:, :LANES] + (_dot(h_lo, wr_ref[:, :LANES]) + both[:, LANES:])
                         + br_ref[...])


def _mix_router(x, yh, ya, yr, w_out, g, wr, br, tm=512):
    t, d = x.shape
    steps = t // tm
    cur = lambda i: (jnp.minimum(i, steps - 1), 0)
    row = lambda width: pl.BlockSpec((tm, width), cur)
    full = lambda shape: pl.BlockSpec(shape, lambda i: (0, 0))
    return pl.pallas_call(
        _mix_router_kernel,
        grid=(steps + 1,),
        in_specs=[row(d), row(HY_WIDTH), row(ATT_WIDTH), row(RET_WIDTH),
                  pl.BlockSpec((d, d), lambda i: (0, 0), pipeline_mode=pl.Buffered(1)),
                  full((1, d)), full((d, 2 * LANES)), full((1, LANES))],
        out_specs=[row(d), row(d // 2),
                   pl.BlockSpec((tm, LANES), lambda i: (jnp.maximum(i - 1, 0), 0)),
                   full((SUBLANES, LANES))],
        out_shape=[jax.ShapeDtypeStruct((t, d), F32), jax.ShapeDtypeStruct((t, d // 2), jnp.uint32),
                   jax.ShapeDtypeStruct((t, LANES), F32), jax.ShapeDtypeStruct((SUBLANES, LANES), F32)],
        scratch_shapes=[pltpu.VMEM((1, LANES), F32), pltpu.VMEM((2, tm, LANES), F32)],
        compiler_params=_params(("arbitrary",), 48),
        name="mix_router",
    )(x, yh, ya, yr, w_out, g.reshape(1, d), wr, br)


MOE_TM = 512
MOE_TC = 512


def _dispatch_plan(route, counts, tm):
    experts = jnp.arange(N_EXPERTS, dtype=jnp.int32)
    ids = route[:, ROUTE_I0:ROUTE_I1 + 1].astype(jnp.int32)
    rank = route[:, ROUTE_R0:ROUTE_R1 + 1].astype(jnp.int32)
    n_tiles = 2 * route.shape[0] // tm + N_EXPERTS
    count = counts[0, :N_EXPERTS].astype(jnp.int32)
    tiles = (count + tm - 1) // tm
    tile_end = jnp.cumsum(tiles)
    offs = (tile_end - tiles) * tm
    slot = jnp.sum(jnp.where(ids[..., None] == experts, offs, 0), axis=-1) + rank
    tile_expert = jnp.minimum(
        jnp.sum(jnp.arange(n_tiles, dtype=jnp.int32)[:, None] >= tile_end[None, :], axis=1),
        N_EXPERTS - 1).astype(jnp.int32)
    later = (experts[None, :] > experts[:, None]) & (tiles[None, :] > 0)
    following = jnp.min(jnp.where(later, experts[None, :], N_EXPERTS), axis=1)
    following = jnp.where(following == N_EXPERTS, -1, following).astype(jnp.int32)
    pad = (offs + count, tiles * tm - count)
    return slot, n_tiles, tile_expert, tile_end[-1:].astype(jnp.int32), following[tile_expert], pad


def _moe_dispatch_kernel(tm, pad_start_ref, pad_rows_ref, nv_ref, idx_ref, hp_ref, xs_ref, zeros_ref,
                         sem, zsem):
    tokens = hp_ref.shape[0]
    n_tiles = xs_ref.shape[0] // tm

    @pl.when(pl.program_id(0) == 0)
    def _():
        zeros_ref[...] = jnp.zeros_like(zeros_ref)
        fills = []
        for e in range(N_EXPERTS):
            rows = pad_rows_ref[e]
            end = pad_start_ref[e] + rows
            size = tm // 2
            while size >= SUBLANES:
                above = rows & ~(2 * size - 1)
                fills.append(((rows & size) != 0, pl.multiple_of(end - above - size, SUBLANES), size))
                size //= 2
            for j in range(SUBLANES - 1):
                fills.append(((rows & (SUBLANES - 1)) > j, pad_start_ref[e] + j, 1))
        for j in range(N_EXPERTS):
            fills.append((nv_ref[0] + j < n_tiles, pl.multiple_of((nv_ref[0] + j) * tm, tm), tm))
        copies = [(cond, pltpu.make_async_copy(zeros_ref.at[pl.ds(0, size)],
                                               xs_ref.at[pl.ds(start, size)], zsem))
                  for cond, start, size in fills]
        for cond, cp in copies:
            pl.when(cond)(cp.start)
        for cond, cp in copies:
            pl.when(cond)(cp.wait)

    def issue(r, carry):
        for k in range(2):
            pltpu.make_async_copy(hp_ref.at[pl.ds(r, 1)],
                                  xs_ref.at[pl.ds(idx_ref[0, 0, 2 * r + k], 1)], sem).start(priority=k)
        return carry

    lax.fori_loop(0, tokens, issue, 0, unroll=4)
    for k in range(2):
        pltpu.make_async_copy(hp_ref, xs_ref.at[pl.ds(0, tokens)], sem).wait()


def _moe_dispatch(hp, slot, pad_start, pad_rows, n_valid, n_tiles, tm, tc=MOE_TC):
    t, half = hp.shape
    steps = t // tc
    return pl.pallas_call(
        functools.partial(_moe_dispatch_kernel, tm),
        grid_spec=pltpu.PrefetchScalarGridSpec(
            num_scalar_prefetch=3,
            grid=(steps,),
            in_specs=[pl.BlockSpec((1, 1, 2 * tc), lambda i, ps, pr, nv: (i, 0, 0),
                                   memory_space=pltpu.SMEM),
                      pl.BlockSpec((tc, half), lambda i, ps, pr, nv: (i, 0))],
            out_specs=pl.BlockSpec(memory_space=pl.ANY),
            scratch_shapes=[pltpu.VMEM((tm, half), hp.dtype), pltpu.SemaphoreType.DMA(()),
                            pltpu.SemaphoreType.DMA(())],
        ),
        out_shape=jax.ShapeDtypeStruct((n_tiles * tm, half), hp.dtype),
        compiler_params=_params(("arbitrary",), 16),
        name="moe_dispatch",
    )(pad_start, pad_rows, n_valid, slot.reshape(steps, 1, 2 * tc), hp)


MOE_CAST_ROWS = 256


def _moe_gemm_kernel(layer, te_ref, nv_ref, nxt_ref, xs_ref, wg_ref, wu_ref, wd_ref, y_ref,
                     stage_g, stage_u, stage_d, wgb_ref, wub_ref, wdb_ref, sem_ref):
    i = pl.program_id(0)
    last = nv_ref[0] - 1
    tile = jnp.minimum(i, last)
    expert = te_ref[tile]
    fresh = jnp.logical_or(i == 0, expert != te_ref[jnp.maximum(tile - 1, 0)])
    pairs = ((wg_ref, stage_g, wgb_ref), (wu_ref, stage_u, wub_ref), (wd_ref, stage_d, wdb_ref))

    def copies(e):
        return [pltpu.make_async_copy(w_ref.at[layer, e], stage, sem_ref.at[n])
                for n, (w_ref, stage, _) in enumerate(pairs)]

    @pl.when(jnp.logical_and(i <= last, fresh))
    def _():
        @pl.when(i == 0)
        def _():
            for cp in copies(expert):
                cp.start()

        for cp in copies(expert):
            cp.wait()
        for _, stage, dst_ref in pairs:
            def cast(j, carry):
                rows = pl.ds(pl.multiple_of(j * MOE_CAST_ROWS, MOE_CAST_ROWS), MOE_CAST_ROWS)
                dst_ref[rows, :] = stage[rows, :].astype(BF16)
                return carry
            lax.fori_loop(0, dst_ref.shape[0] // MOE_CAST_ROWS, cast, 0)

        @pl.when(nxt_ref[tile] >= 0)
        def _():
            for cp in copies(nxt_ref[tile]):
                cp.start()

    @pl.when(i <= last)
    def _():
        xp = xs_ref[...]
        x_a = lax.bitcast_convert_type(xp & jnp.uint32(0xFFFF0000), F32).astype(BF16)
        x_b = lax.bitcast_convert_type(xp << 16, F32).astype(BF16)
        x = jnp.concatenate([x_a, x_b], axis=1)
        a = _dot(x, wgb_ref[...])
        b = _dot(x, wub_ref[...])
        act = a / (1.0 + jnp.exp(-a)) * b
        y_ref[...] = _dot(act.astype(BF16), wdb_ref[...])

    @pl.when(i > last)
    def _():
        y_ref[...] = jnp.zeros_like(y_ref)


def _moe_gemm(xs, tile_expert, n_valid, next_expert, layer, wg, wu, wd, tm=MOE_TM):
    n_slots, half = xs.shape
    _, ne, d, de = wg.shape
    tile = lambda i, te, nv, nx: jnp.minimum(i, nv[0] - 1)
    hbm = pl.BlockSpec(memory_space=pl.ANY)
    return pl.pallas_call(
        functools.partial(_moe_gemm_kernel, layer),
        grid_spec=pltpu.PrefetchScalarGridSpec(
            num_scalar_prefetch=3,
            grid=(n_slots // tm,),
            in_specs=[pl.BlockSpec((tm, half), lambda i, te, nv, nx: (tile(i, te, nv, nx), 0)),
                      hbm, hbm, hbm],
            out_specs=pl.BlockSpec((tm, d), lambda i, te, nv, nx: (i, 0)),
            scratch_shapes=[pltpu.VMEM((d, de), F32), pltpu.VMEM((d, de), F32), pltpu.VMEM((de, d), F32),
                            pltpu.VMEM((d, de), BF16), pltpu.VMEM((d, de), BF16),
                            pltpu.VMEM((de, d), BF16), pltpu.SemaphoreType.DMA((3,))],
        ),
        out_shape=jax.ShapeDtypeStruct((n_slots, d), F32),
        compiler_params=_params(("arbitrary",), 58),
        name="moe_gemm",
    )(tile_expert, n_valid, next_expert, xs, wg, wu, wd)


def _moe_combine_kernel(final_norm, steps_a, idx_ref, idx_next_ref, x_ref, route_ref, gf_ref, y_ref,
                        *rest):
    o_refs, (buf_ref, sem_ref) = rest[:-2], rest[-2:]
    i = pl.program_id(0)
    n = pl.num_programs(0)
    tokens = x_ref.shape[0]

    def start_rows(idx, slot):
        def issue(r, carry):
            for k in range(2):
                pltpu.make_async_copy(y_ref.at[pl.ds(idx[0, 0, 2 * r + k], 1)],
                                      buf_ref.at[slot, k, pl.ds(r, 1)], sem_ref.at[slot]).start(priority=k)
            return carry
        lax.fori_loop(0, tokens, issue, 0, unroll=4)

    @pl.when(i == 0)
    def _():
        start_rows(idx_ref, 0)

    @pl.when(i + 1 < n)
    def _():
        start_rows(idx_next_ref, (i + 1) % 2)

    slot = i % 2
    for k in range(2):
        pltpu.make_async_copy(y_ref.at[pl.ds(0, tokens)], buf_ref.at[slot, k], sem_ref.at[slot]).wait()
    route = route_ref[...]
    y = x_ref[...] + route[:, ROUTE_W0:ROUTE_W0 + 1] * buf_ref[slot, 0] \
        + route[:, ROUTE_W1:ROUTE_W1 + 1] * buf_ref[slot, 1]
    if final_norm:
        y = y * lax.rsqrt(jnp.mean(y * y, axis=-1, keepdims=True) + EPS) * gf_ref[...]
    if len(o_refs) == 1:
        o_refs[0][...] = y
    else:
        @pl.when(i < steps_a)
        def _():
            o_refs[0][...] = y

        @pl.when(i >= steps_a)
        def _():
            o_refs[1][...] = y


def _moe_combine(x, route, slot, y, gf, final_norm, tokens_a=None, tc=MOE_TC):
    t, d = x.shape
    steps = t // tc
    idx = slot.reshape(steps, 1, 2 * tc)
    row = lambda width: pl.BlockSpec((tc, width), lambda i: (i, 0))
    smem = lambda pos: pl.BlockSpec((1, 1, 2 * tc), lambda i: (pos(i), 0, 0), memory_space=pltpu.SMEM)
    if tokens_a is None:
        steps_a = steps
        out_specs = [row(d)]
        out_shape = [jax.ShapeDtypeStruct((t, d), F32)]
    else:
        steps_a = tokens_a // tc
        out_specs = [pl.BlockSpec((tc, d), lambda i: (jnp.minimum(i, steps_a - 1), 0)),
                     pl.BlockSpec((tc, d), lambda i: (jnp.maximum(i - steps_a, 0), 0))]
        out_shape = [jax.ShapeDtypeStruct((tokens_a, d), F32),
                     jax.ShapeDtypeStruct((t - tokens_a, d), F32)]
    return pl.pallas_call(
        functools.partial(_moe_combine_kernel, final_norm, steps_a),
        grid=(steps,),
        in_specs=[smem(lambda i: i), smem(lambda i: jnp.minimum(i + 1, steps - 1)),
                  row(d), row(LANES), pl.BlockSpec((1, d), lambda i: (0, 0)),
                  pl.BlockSpec(memory_space=pl.ANY)],
        out_specs=out_specs,
        out_shape=out_shape,
        scratch_shapes=[pltpu.VMEM((2, 2, tc, d), F32), pltpu.SemaphoreType.DMA((2,))],
        compiler_params=_params(("arbitrary",), 52),
        name="moe_combine",
    )(idx, idx, x, route, gf.reshape(1, d), y)


def _moe(hp, route, counts, x, layer, wg, wu, wd, gf, final_norm, tokens_a=None):
    slot, n_tiles, tile_expert, n_valid, next_expert, pad = _dispatch_plan(route, counts, MOE_TM)
    xs = _moe_dispatch(hp, slot, pad[0], pad[1], n_valid, n_tiles, MOE_TM)
    y = _moe_gemm(xs, tile_expert, n_valid, next_expert, layer, wg, wu, wd)
    return _moe_combine(x, route, slot, y, gf, final_norm, tokens_a)


def _router_weights(wg, bg, we, be):
    d = wg.shape[0]
    pad = LANES - N_EXPERTS - N_GROUPS
    wr = jnp.concatenate([we, wg, jnp.zeros((d, pad), F32)], axis=1).astype(F32)
    br = jnp.concatenate([be, bg, jnp.zeros((pad,), F32)]).astype(F32).reshape(1, LANES)
    wr_hi = wr.astype(BF16)
    wr_lo = (wr - wr_hi.astype(F32)).astype(BF16)
    return jnp.concatenate([wr_hi, wr_lo], axis=1), br


def _layer(xs, bsz, p, l, consts, final_g, tokens_a):
    dft, emb = consts
    w_in = p['w_in'][l].astype(BF16)
    g1 = p['ln1_g'][l]
    w_hy, w_att, w_ret = w_in[:, :HY_IN], w_in[:, HY_IN:HY_IN + ATT_IN], w_in[:, HY_IN + ATT_IN:]
    if len(xs) == 2:
        hy, x = _norm_proj_stack(xs[0], xs[1], g1, w_hy, F32)
        att, ret = _norm_proj(x, g1, [w_att, w_ret], [BF16, F32])
    else:
        x, = xs
        hy, att, ret = _norm_proj(x, g1, [w_hy, w_att, w_ret], [F32, BF16, F32])
    t, d = x.shape
    n = t // bsz
    hy = hy.reshape(bsz, n, HY_IN)
    att = att.reshape(bsz, n, ATT_IN)
    ret = ret.reshape(bsz, n, RET_IN)
    y_h = _hyena(hy, dft, emb, p['hy_conv_w'][l], p['hy_conv_b'][l], p['hy_w1'][l], p['hy_b1'][l],
                 p['hy_w2'][l], p['hy_b2'][l], p['hy_w3'][l], p['hy_freq'][l], p['hy_decay'][l],
                 p['hy_skip'][l])
    y_a = _attention(att, p['attn_sink'][l], p['rel_bias'])
    y_r = _retention(ret, p['ret_decay'][l])
    wr, br = _router_weights(p['router_group_w'][l], p['router_group_b'][l],
                             p['router_expert_w'][l], p['router_expert_b'][l])
    x, hp, route, counts = _mix_router(x, y_h.reshape(t, -1), y_a.reshape(t, -1), y_r.reshape(t, -1),
                                       p['w_out'][l].astype(BF16), p['ln2_g'][l], wr, br)
    last = l == DEPTH - 1
    return _moe(hp, route, counts, x, l, p['moe_w_gate'], p['moe_w_up'], p['moe_w_down'], final_g,
                final_norm=last, tokens_a=tokens_a if last else None)


def kernel(x_prompt, x_sample, ln1_g, w_in, hy_conv_w, hy_conv_b, hy_w1, hy_b1, hy_w2, hy_b2, hy_w3,
           hy_freq, hy_decay, hy_skip, attn_sink, rel_bias, ret_decay, w_out, ln2_g, router_group_w,
           router_group_b, router_expert_w, router_expert_b, moe_w_gate, moe_w_up, moe_w_down, lnf_g):
    p = dict(ln1_g=ln1_g, w_in=w_in, hy_conv_w=hy_conv_w, hy_conv_b=hy_conv_b, hy_w1=hy_w1,
             hy_b1=hy_b1, hy_w2=hy_w2, hy_b2=hy_b2, hy_w3=hy_w3, hy_freq=hy_freq, hy_decay=hy_decay,
             hy_skip=hy_skip, attn_sink=attn_sink, rel_bias=rel_bias, ret_decay=ret_decay, w_out=w_out,
             ln2_g=ln2_g, router_group_w=router_group_w, router_group_b=router_group_b,
             router_expert_w=router_expert_w, router_expert_b=router_expert_b, moe_w_gate=moe_w_gate,
             moe_w_up=moe_w_up, moe_w_down=moe_w_down)
    bp, n, d = x_prompt.shape
    bs = x_sample.shape[0]
    assert x_sample.shape[1:] == (n, d)
    bsz = bp + bs
    consts = (_dft_matrices(n), _hyena_embedding(n))
    xs = (x_prompt.reshape(bp * n, d), x_sample.reshape(bs * n, d))
    for l in range(DEPTH):
        xs = _layer(xs, bsz, p, l, consts, lnf_g, bp * n)
    y_prompt, y_sample = xs
    return (y_prompt.reshape(bp, n, d), y_sample.reshape(bs, n, d))
```

```python
import functools
import math

import numpy as np
import jax
import jax.numpy as jnp
from jax import lax
from jax.experimental import pallas as pl
from jax.experimental.pallas import tpu as pltpu

F32 = jnp.float32
BF16 = jnp.bfloat16
HIGHEST = lax.Precision.HIGHEST

D_MODEL = 2048
DEPTH = 2
HEAD_DIM = 64
EPS = 1e-6
HY_WIDTH = D_MODEL // 4
ATT_WIDTH = D_MODEL // 2
RET_WIDTH = D_MODEL // 4
HY_ORDER = 2
HY_EMB = 33
HY_FFN = 64
ATT_HEADS = ATT_WIDTH // HEAD_DIM
ATT_KV_HEADS = 4
ATT_GROUP = ATT_HEADS // ATT_KV_HEADS
KV_WIDTH = ATT_KV_HEADS * HEAD_DIM
WINDOW = 128
BLOCK = 128
REL_BUCKETS = 32
REL_MAX_DIST = 128
RET_HEADS = RET_WIDTH // HEAD_DIM
RET_CHUNK = 128
ROPE_BASE = 10000.0
N_GROUPS = 4
EXPERTS_PER_GROUP = 4
N_EXPERTS = N_GROUPS * EXPERTS_PER_GROUP
D_EXPERT = D_MODEL // 2
HY_IN = 3 * HY_WIDTH
ATT_IN = ATT_WIDTH + 2 * KV_WIDTH
RET_IN = 4 * RET_WIDTH

LANES = 128
SUBLANES = 8
MIB = 1024 * 1024


def _params(semantics, vmem_mib):
    return pltpu.CompilerParams(dimension_semantics=semantics, vmem_limit_bytes=vmem_mib * MIB)


def _dot(a, b):
    return jnp.dot(a, b, preferred_element_type=F32)


def _dot_nt(a, b):
    return lax.dot_general(a, b, (((1,), (1,)), ((), ())), preferred_element_type=F32)


def _norm_proj_kernel(x_ref, g_ref, *refs):
    w_refs, o_refs = refs[:len(refs) // 2], refs[len(refs) // 2:]
    x = x_ref[...]
    inv = lax.rsqrt(jnp.mean(x * x, axis=-1, keepdims=True) + EPS)
    h = (x * inv * g_ref[...]).astype(BF16)
    for w_ref, o_ref in zip(w_refs, o_refs):
        o_ref[...] = _dot(h, w_ref[...]).astype(o_ref.dtype)


def _norm_proj(x, g, weights, out_dtypes, tm=512):
    t, d = x.shape
    return pl.pallas_call(
        _norm_proj_kernel,
        grid=(t // tm,),
        in_specs=[pl.BlockSpec((tm, d), lambda i: (i, 0)), pl.BlockSpec((1, d), lambda i: (0, 0))]
        + [pl.BlockSpec(w.shape, lambda i: (0, 0), pipeline_mode=pl.Buffered(1)) for w in weights],
        out_specs=[pl.BlockSpec((tm, w.shape[1]), lambda i: (i, 0)) for w in weights],
        out_shape=[jax.ShapeDtypeStruct((t, w.shape[1]), dt) for w, dt in zip(weights, out_dtypes)],
        compiler_params=_params(("parallel",), 56),
        name="norm_proj",
    )(x, g.reshape(1, d), *weights)


def _norm_proj_stack_kernel(tiles_a, xa_ref, xb_ref, g_ref, w_ref, o_ref, x_ref):
    x = jnp.where(pl.program_id(0) < tiles_a, xa_ref[...], xb_ref[...])
    x_ref[...] = x
    inv = lax.rsqrt(jnp.mean(x * x, axis=-1, keepdims=True) + EPS)
    h = (x * inv * g_ref[...]).astype(BF16)
    o_ref[...] = _dot(h, w_ref[...]).astype(o_ref.dtype)


def _norm_proj_stack(xa, xb, g, w, out_dtype, tm=512):
    ta, d = xa.shape
    tb = xb.shape[0]
    n = w.shape[1]
    tiles_a = ta // tm
    return pl.pallas_call(
        functools.partial(_norm_proj_stack_kernel, tiles_a),
        grid=((ta + tb) // tm,),
        in_specs=[
            pl.BlockSpec((tm, d), lambda i: (jnp.minimum(i, tiles_a - 1), 0)),
            pl.BlockSpec((tm, d), lambda i: (jnp.maximum(i - tiles_a, 0), 0)),
            pl.BlockSpec((1, d), lambda i: (0, 0)),
            pl.BlockSpec((d, n), lambda i: (0, 0)),
        ],
        out_specs=[pl.BlockSpec((tm, n), lambda i: (i, 0)), pl.BlockSpec((tm, d), lambda i: (i, 0))],
        out_shape=[jax.ShapeDtypeStruct((ta + tb, n), out_dtype),
                   jax.ShapeDtypeStruct((ta + tb, d), xa.dtype)],
        compiler_params=_params(("parallel",), 56),
        name="norm_proj_stack",
    )(xa, xb, g.reshape(1, d), w)


def _short_conv(x, prev_row, next_row, w_ref, b_ref):
    n = x.shape[0]
    row = lax.broadcasted_iota(jnp.int32, x.shape, 0)
    prev = jnp.where(row == 0, prev_row, pltpu.roll(x, 1, 0))
    nxt = jnp.where(row == n - 1, next_row, pltpu.roll(x, n - 1, 0))
    return prev * w_ref[0:1, :] + x * w_ref[1:2, :] + nxt * w_ref[2:3, :] + b_ref[...]


def _split_rows(x, slab_ref):
    half = x.shape[0] // 2
    slabs = x.shape[1] // LANES
    for j in range(slabs):
        slab_ref[j] = x[:, j * LANES:(j + 1) * LANES]
    rows = lambda start: jnp.concatenate(
        [slab_ref[j, pl.ds(start, half, stride=2), :] for j in range(slabs)], axis=1)
    return rows(0), rows(1)


def _merge_rows(x_e, x_o, slab_ref):
    half = x_e.shape[0]
    slabs = x_e.shape[1] // LANES
    for j in range(slabs):
        slab_ref[j, pl.ds(0, half, stride=2), :] = x_e[:, j * LANES:(j + 1) * LANES]
        slab_ref[j, pl.ds(1, half, stride=2), :] = x_o[:, j * LANES:(j + 1) * LANES]
    return jnp.concatenate([slab_ref[j] for j in range(slabs)], axis=1)


def _hy_filter_kernel(z_ref, w1_ref, b1_ref, w2_ref, b2_ref, fr_ref, w3f_ref, w3b_ref,
                      decf_ref, decb_ref, hs_ref, hd_ref):
    fr = fr_ref[...]
    h = jnp.sin(fr * (jnp.dot(z_ref[...], w1_ref[...], precision=HIGHEST,
                              preferred_element_type=F32) + b1_ref[...]))
    h = jnp.sin(fr * (jnp.dot(h, w2_ref[...], precision=HIGHEST,
                              preferred_element_type=F32) + b2_ref[...]))
    n = h.shape[0]
    shape = (n, w3f_ref.shape[1])
    row = lax.broadcasted_iota(jnp.int32, shape, 0)
    t = row.astype(F32) * (1.0 / (n - 1))
    hf = jnp.dot(h, w3f_ref[...], precision=HIGHEST, preferred_element_type=F32)
    hf = hf * jnp.exp(-t * jnp.abs(decf_ref[...]))
    hb = jnp.dot(h, w3b_ref[...], precision=HIGHEST, preferred_element_type=F32)
    hb = hb * jnp.exp(-t * jnp.abs(decb_ref[...]))
    hb = jnp.where(row == 0, 0.0, hb)
    norm = (jnp.sum(jnp.abs(hf), axis=0, keepdims=True)
            + jnp.sum(jnp.abs(hb), axis=0, keepdims=True))
    hs_ref[...] = (hf + hb) / norm
    hd_ref[...] = (hb - hf) / norm


def _hy_filter(z, w1, b1, w2, b2, fr, w3, decay, tc=256):
    n = z.shape[0]
    cols = HY_ORDER * HY_WIDTH
    w3f, w3b = w3[:, :cols], w3[:, cols:]
    dec = decay.reshape(2, cols)
    zp = jnp.pad(z, ((0, 0), (0, LANES - HY_EMB)))
    w1p = jnp.pad(w1, ((0, LANES - HY_EMB), (0, 0)))
    full = lambda shape: pl.BlockSpec(shape, lambda j: (0, 0))
    col = lambda rows: pl.BlockSpec((rows, tc), lambda j: (0, j))
    return pl.pallas_call(
        _hy_filter_kernel,
        grid=(cols // tc,),
        in_specs=[full((n, LANES)), full((LANES, HY_FFN)), full((1, HY_FFN)), full((HY_FFN, HY_FFN)),
                  full((1, HY_FFN)), full((1, HY_FFN)), col(HY_FFN), col(HY_FFN), col(1), col(1)],
        out_specs=[col(n), col(n)],
        out_shape=[jax.ShapeDtypeStruct((n, cols), F32)] * 2,
        compiler_params=_params(("parallel",), 48),
        name="hy_filter",
    )(zp, w1p, b1.reshape(1, -1), w2, b2.reshape(1, -1), fr.reshape(1, -1), w3f, w3b,
      dec[0:1], dec[1:2])


def _alternating_sum(x):
    row = lax.broadcasted_iota(jnp.int32, x.shape, 0)
    return jnp.sum(x * (1 - 2 * (row & 1)).astype(F32), axis=0, keepdims=True)


def _half_spectra(fw_ref, rows, x_e, x_o, y_e, y_o):
    ec, oc = _dot(fw_ref[0, rows, :], x_e), _dot(fw_ref[1, rows, :], x_o)
    es, os_ = _dot(fw_ref[2, rows, :], y_e), _dot(fw_ref[3, rows, :], y_o)
    return ec + oc, ec - oc, es + os_, os_ - es


def _hy_kf_kernel(fw_ref, hs_ref, hd_ref, k_ref, slab_ref):
    hs_e, hs_o = _split_rows(hs_ref[...], slab_ref)
    hd_e, hd_o = _split_rows(hd_ref[...], slab_ref)
    k0, k1, k2, k3 = _half_spectra(fw_ref, slice(None), hs_e.astype(BF16), hs_o.astype(BF16),
                                   hd_e.astype(BF16), hd_o.astype(BF16))
    first = jnp.logical_and(lax.broadcasted_iota(jnp.int32, k0.shape, 0) == 0, pl.program_id(0) == 0)
    k_ref[0] = k0
    k_ref[1] = k1
    k_ref[2] = jnp.where(first, _alternating_sum(hs_e), k2)
    k_ref[3] = jnp.where(first, _alternating_sum(hd_o), k3)


def _hy_kf(fw, hs, hd, tf=256, tc=512):
    n, cols = hs.shape
    h = n // 2
    return pl.pallas_call(
        _hy_kf_kernel,
        grid=(h // tf, cols // tc),
        in_specs=[pl.BlockSpec((4, tf, h), lambda i, j: (0, i, 0)),
                  pl.BlockSpec((n, tc), lambda i, j: (0, j)),
                  pl.BlockSpec((n, tc), lambda i, j: (0, j))],
        out_specs=pl.BlockSpec((4, tf, tc), lambda i, j: (0, i, j)),
        out_shape=jax.ShapeDtypeStruct((4, h, cols), F32),
        scratch_shapes=[pltpu.VMEM((tc // LANES, n, LANES), F32)],
        compiler_params=_params(("parallel", "arbitrary"), 48),
        name="hy_filter_dft",
    )(fw, hs, hd)


HY_FREQ_TILE = 512


def _hy_fwd_kernel(conv, fw_ref, u_ref, k_ref, *rest):
    half = fw_ref.shape[1]
    u = u_ref[0]
    if conv:
        w_ref, b_ref, y_ref, slab_ref = rest
        u = _short_conv(u, 0.0, 0.0, w_ref, b_ref)
    else:
        y_ref, slab_ref = rest
    u_e, u_o = _split_rows(u, slab_ref)
    mid_a, mid_b = _alternating_sum(u_e), _alternating_sum(u_o)
    u_e, u_o = u_e.astype(BF16), u_o.astype(BF16)
    tf = HY_FREQ_TILE
    for i in range(half // tf):
        rows = slice(i * tf, (i + 1) * tf)
        p0, p1, p2, p3 = _half_spectra(fw_ref, rows, u_e, u_o, u_e, u_o)
        k0, k1, k2, k3 = (k_ref[j, rows, :] for j in range(4))
        y0, y2 = p0 * k0 + p2 * k2, p0 * k2 - p2 * k0
        y1, y3 = p1 * k1 + p3 * k3, p1 * k3 - p3 * k1
        he, ho = y2 - y3, y2 + y3
        if i == 0:
            first = lax.broadcasted_iota(jnp.int32, p0.shape, 0) == 0
            y0 = jnp.where(first, p0 * k0, y0)
            y1 = jnp.where(first, p1 * k1, y1)
            he = jnp.where(first, mid_a * k2 + mid_b * k3, he)
            ho = jnp.where(first, mid_a * k3 - mid_b * k2, ho)
        y_ref[0, 0, rows, :] = (y0 + y1).astype(BF16)
        y_ref[0, 1, rows, :] = he.astype(BF16)
        y_ref[0, 2, rows, :] = (y0 - y1).astype(BF16)
        y_ref[0, 3, rows, :] = ho.astype(BF16)


def _hy_fwd(fw, u3, ucol, k, kcol, conv=None):
    bsz, n, _ = u3.shape
    c = HY_WIDTH
    h = n // 2
    in_specs = [
        pl.BlockSpec((4, h, h), lambda b: (0, 0, 0), pipeline_mode=pl.Buffered(1)),
        pl.BlockSpec((1, n, c), lambda b: (b, 0, ucol)),
        pl.BlockSpec((4, h, c), lambda b: (0, 0, kcol), pipeline_mode=pl.Buffered(1)),
    ]
    args = [fw, u3, k]
    if conv is not None:
        in_specs += [pl.BlockSpec((3, c), lambda b: (0, ucol)), pl.BlockSpec((1, c), lambda b: (0, ucol))]
        args += list(conv)
    return pl.pallas_call(
        functools.partial(_hy_fwd_kernel, conv is not None),
        grid=(bsz,),
        in_specs=in_specs,
        out_specs=pl.BlockSpec((1, 4, h, c), lambda b: (b, 0, 0, 0)),
        out_shape=jax.ShapeDtypeStruct((bsz, 4, h, c), BF16),
        scratch_shapes=[pltpu.VMEM((c // LANES, n, LANES), F32)],
        compiler_params=_params(("arbitrary",), 52),
        name="hy_dft_fwd",
    )(*args)


def _hy_inv_kernel(s_conv, iv_ref, y_ref, d_ref, *rest):
    o_ref, slab_ref = rest[-2:]
    i = pl.program_id(0)
    first, last = i == 0, i == pl.num_programs(0) - 1

    def conv_tile(x_ref, lo_ref, hi_ref, w_ref, b_ref):
        prev_row = jnp.where(first, 0.0, lo_ref[0, SUBLANES - 1:SUBLANES, :])
        next_row = jnp.where(last, 0.0, hi_ref[0, 0:1, :])
        return _short_conv(x_ref[0], prev_row, next_row, w_ref, b_ref)

    if s_conv:
        skip = conv_tile(*rest[0:5])
        gate = conv_tile(*rest[5:10])
    else:
        skip = rest[0][0]
        gate = conv_tile(*rest[1:6])
    y_e = _dot(iv_ref[0], y_ref[0, 0]) + _dot(iv_ref[1], y_ref[0, 1])
    y_o = _dot(iv_ref[2], y_ref[0, 2]) + _dot(iv_ref[3], y_ref[0, 3])
    y = _merge_rows(y_e, y_o, slab_ref)
    o_ref[0] = (gate * (y + skip * d_ref[...])).astype(o_ref.dtype)


def _hy_inv(iv, y, d, raw3, conv, gcol, out_dtype, scol=None, s3=None, tt=1024):
    bsz, _, h, c = y.shape
    n = 2 * h
    conv_w, conv_b = conv
    halo = tt // SUBLANES
    tile = lambda col: pl.BlockSpec((1, tt, c), lambda i, b: (b, i, col))
    lo = lambda col: pl.BlockSpec((1, SUBLANES, c), lambda i, b: (b, jnp.maximum(i * halo - 1, 0), col))
    hi = lambda col: pl.BlockSpec((1, SUBLANES, c),
                                  lambda i, b: (b, jnp.minimum((i + 1) * halo, n // SUBLANES - 1), col))
    wspec = lambda col: pl.BlockSpec((3, c), lambda i, b: (0, col))
    bspec = lambda col: pl.BlockSpec((1, c), lambda i, b: (0, col))
    conv_in = lambda col: ([tile(col), lo(col), hi(col), wspec(col), bspec(col)],
                           [raw3, raw3, raw3, conv_w, conv_b])
    in_specs = [pl.BlockSpec((4, tt // 2, h), lambda i, b: (0, i, 0)),
                pl.BlockSpec((1, 4, h, c), lambda i, b: (b, 0, 0, 0)),
                pl.BlockSpec((1, c), lambda i, b: (0, 0))]
    args = [iv, y, d.reshape(1, c)]
    if s3 is None:
        specs, arrs = conv_in(scol)
    else:
        specs, arrs = [tile(0)], [s3]
    in_specs += specs
    args += arrs
    specs, arrs = conv_in(gcol)
    in_specs += specs
    args += arrs
    return pl.pallas_call(
        functools.partial(_hy_inv_kernel, s3 is None),
        grid=(n // tt, bsz),
        in_specs=in_specs,
        out_specs=pl.BlockSpec((1, tt, c), lambda i, b: (b, i, 0)),
        out_shape=jax.ShapeDtypeStruct((bsz, n, c), out_dtype),
        scratch_shapes=[pltpu.VMEM((c // LANES, tt, LANES), F32)],
        compiler_params=_params(("arbitrary", "arbitrary"), 48),
        name="hy_dft_inv",
    )(*args)


DFT_SPLIT = 32


def _dft_matrices(n):
    h = n // 2

    def cos_sin(offset):
        def table(rows, step):
            f = lax.broadcasted_iota(jnp.int32, (rows, h), 0) * step
            t = 2 * lax.broadcasted_iota(jnp.int32, (rows, h), 1) + offset
            ang = ((f * t) & (2 * n - 1)).astype(F32) * (math.pi / n)
            return jnp.cos(ang), jnp.sin(ang)
        cos_hi, sin_hi = (x[:, None, :] for x in table(h // DFT_SPLIT, DFT_SPLIT))
        cos_lo, sin_lo = (x[None, :, :] for x in table(DFT_SPLIT, 1))
        return ((cos_hi * cos_lo - sin_hi * sin_lo).reshape(h, h),
                (sin_hi * cos_lo + cos_hi * sin_lo).reshape(h, h))

    cos_e, sin_e = cos_sin(0)
    cos_o, sin_o = cos_sin(1)
    fwd = jnp.stack([cos_e, cos_o, sin_e, sin_o]).astype(BF16)
    m = lax.broadcasted_iota(jnp.int32, (h, h), 0)
    f = lax.broadcasted_iota(jnp.int32, (h, h), 1)
    weight = jnp.where(f == 0, 0.5 / n, 1.0 / n)
    mid = (1 - 2 * (m & 1)).astype(F32) * (1.0 / n)
    inv = jnp.stack([weight * cos_e.T, jnp.where(f == 0, mid, sin_e.T * (-1.0 / n)),
                     weight * cos_o.T, jnp.where(f == 0, -mid, sin_o.T * (-1.0 / n))]).astype(BF16)
    return fwd, inv


def _hyena_embedding(n):
    t = jnp.linspace(0.0, 1.0, n, dtype=F32)[:, None]
    bands = (HY_EMB - 1) // 2
    f = jnp.linspace(1e-4, bands - 1, bands, dtype=F32)[None]
    w = 2.0 * math.pi * jnp.arange(n, dtype=F32)[:, None] / n
    return jnp.concatenate([t, jnp.cos(f * w), -jnp.sin(f * w)], -1)


def _hyena(p3, dft, emb, conv_w, conv_b, w1, b1, w2, b2, w3, freq, decay, skip):
    ff, fi = dft
    conv = (conv_w, conv_b.reshape(1, -1))
    hs, hd = _hy_filter(emb, w1, b1, w2, b2, freq, w3, decay)
    k = _hy_kf(ff, hs, hd)
    y = _hy_fwd(ff, p3, 0, k, 0, conv)
    z = _hy_inv(fi, y, skip[0], p3, conv, 1, F32, scol=0)
    y = _hy_fwd(ff, z, 0, k, 1)
    return _hy_inv(fi, y, skip[1], p3, conv, 2, BF16, s3=z)


def _t5_bucket(rel):
    nb = REL_BUCKETS // 2
    max_exact = nb // 2
    n = np.abs(rel)
    large = max_exact + (np.log(np.maximum(n, 1) / max_exact) / np.log(REL_MAX_DIST / max_exact)
                         * (nb - max_exact)).astype(np.int32)
    large = np.minimum(large, nb - 1)
    return (rel > 0).astype(np.int32) * nb + np.where(n < max_exact, n, large)


def _attn_kernel(q_ref, kp_ref, kc_ref, kn_ref, vp_ref, vc_ref, vn_ref, bias_ref, sink_ref, o_ref):
    n = pl.program_id(1)
    q = q_ref[0] * (HEAD_DIM ** -0.5)
    k = jnp.concatenate([kp_ref[0], kc_ref[0], kn_ref[0]], axis=0)
    vt = jnp.concatenate([vp_ref[0], vc_ref[0], vn_ref[0]], axis=1)
    edge_prev = jnp.where(n == 0, -1e30, 0.0).astype(F32)
    edge_next = jnp.where(n == pl.num_programs(1) - 1, -1e30, 0.0).astype(F32)
    ones = jnp.ones((HEAD_DIM, 3 * BLOCK), BF16)
    for kh in range(ATT_KV_HEADS):
        k_h = k[:, kh * HEAD_DIM:(kh + 1) * HEAD_DIM]
        vt_h = jnp.concatenate([vt[kh * HEAD_DIM:(kh + 1) * HEAD_DIM], ones], axis=0)
        q_h = jnp.concatenate([q[:, (kh * ATT_GROUP + g) * HEAD_DIM:(kh * ATT_GROUP + g + 1) * HEAD_DIM]
                               for g in range(ATT_GROUP)], axis=0)
        s = _dot_nt(k_h, q_h) + bias_ref[kh]
        s = jnp.concatenate([s[:BLOCK] + edge_prev, s[BLOCK:2 * BLOCK], s[2 * BLOCK:] + edge_next],
                            axis=0)
        sink = sink_ref[kh]
        m = jnp.maximum(jnp.max(s, axis=0, keepdims=True), sink)
        p = jnp.exp(s - m).astype(BF16)
        o = _dot(vt_h, p)
        o = o[:HEAD_DIM] / (o[HEAD_DIM:HEAD_DIM + 1] + jnp.exp(sink - m))
        for g in range(0, ATT_GROUP, 2):
            pair = jnp.concatenate([o[:, g * BLOCK:(g + 1) * BLOCK],
                                    o[:, (g + 1) * BLOCK:(g + 2) * BLOCK]], axis=0)
            c0 = (kh * ATT_GROUP + g) * HEAD_DIM
            o_ref[0, :, c0:c0 + 2 * HEAD_DIM] = pair.T.astype(o_ref.dtype)


def _attention_bias(rel_bias):
    rel = (np.arange(3 * BLOCK)[None, :] - BLOCK) - np.arange(BLOCK)[:, None]
    bucket = jnp.asarray(_t5_bucket(rel).reshape(-1), jnp.int32)
    onehot = (bucket[:, None] == jnp.arange(REL_BUCKETS, dtype=jnp.int32)[None]).astype(F32)
    bias = jnp.dot(onehot, rel_bias.astype(F32), precision=HIGHEST)
    bias = bias.reshape(BLOCK, 3 * BLOCK, ATT_KV_HEADS, ATT_GROUP)
    bias = jnp.where(jnp.asarray(np.abs(rel) <= WINDOW)[:, :, None, None], bias, -1e30)
    return jnp.transpose(bias, (2, 1, 3, 0)).reshape(ATT_KV_HEADS, 3 * BLOCK, ATT_GROUP * BLOCK)


def _attention(att3, sink, rel_bias):
    bsz, n, _ = att3.shape
    nb = n // BLOCK
    bias = _attention_bias(rel_bias)
    sink_row = jnp.repeat(sink.astype(F32).reshape(ATT_KV_HEADS, 1, ATT_GROUP), BLOCK, axis=2)
    vt = jnp.swapaxes(att3[:, :, ATT_WIDTH + KV_WIDTH:], 1, 2)
    kcol = ATT_WIDTH // KV_WIDTH
    prev = lambda i: jnp.maximum(i - 1, 0)
    nxt = lambda i: jnp.minimum(i + 1, nb - 1)
    cur = lambda i: i
    k_spec = lambda blk: pl.BlockSpec((1, BLOCK, KV_WIDTH), lambda b, i: (b, blk(i), kcol))
    v_spec = lambda blk: pl.BlockSpec((1, KV_WIDTH, BLOCK), lambda b, i: (b, 0, blk(i)))
    return pl.pallas_call(
        _attn_kernel,
        grid=(bsz, nb),
        in_specs=[
            pl.BlockSpec((1, BLOCK, ATT_WIDTH), lambda b, i: (b, i, 0)),
            k_spec(prev), k_spec(cur), k_spec(nxt), v_spec(prev), v_spec(cur), v_spec(nxt),
            pl.BlockSpec((ATT_KV_HEADS, 3 * BLOCK, ATT_GROUP * BLOCK), lambda b, i: (0, 0, 0)),
            pl.BlockSpec((ATT_KV_HEADS, 1, ATT_GROUP * BLOCK), lambda b, i: (0, 0, 0)),
        ],
        out_specs=pl.BlockSpec((1, BLOCK, ATT_WIDTH), lambda b, i: (b, i, 0)),
        out_shape=jax.ShapeDtypeStruct((bsz, n, ATT_WIDTH), BF16),
        compiler_params=_params(("parallel", "parallel"), 32),
        name="window_attention",
    )(att3, att3, att3, att3, vt, vt, vt, bias, sink_row)


def _rotary_tables(n):
    inv = ROPE_BASE ** -jnp.linspace(0.0, 1.0, HEAD_DIM // 2, dtype=F32)
    ang = jnp.arange(n, dtype=F32)[:, None] * inv[None]
    cos, sin = jnp.cos(ang), jnp.sin(ang)
    cos_t = jnp.tile(jnp.concatenate([cos, cos], -1), (1, RET_HEADS))
    sin_t = jnp.tile(jnp.concatenate([-sin, sin], -1), (1, RET_HEADS))
    return cos_t, sin_t


def _rotate(x, cos_t, sin_t):
    width = x.shape[-1]
    half = HEAD_DIM // 2
    lane = lax.broadcasted_iota(jnp.int32, x.shape, 1)
    partner = jnp.where((lane & (HEAD_DIM - 1)) < half,
                        pltpu.roll(x, width - half, 1), pltpu.roll(x, half, 1))
    return x * cos_t + partner * sin_t


RET_PAIR = 2 * HEAD_DIM
RET_PAIRS = RET_WIDTH // RET_PAIR


def _same_head(shape):
    r = lax.broadcasted_iota(jnp.int32, shape, 0) // HEAD_DIM
    c = lax.broadcasted_iota(jnp.int32, shape, 1) // HEAD_DIM
    return r == c


RET_STATE_CHUNKS = 8
RET_OUT_CHUNKS = 4


def _ret_state_kernel(kf_ref, vf_ref, kb_ref, vb_ref, cf_ref, sf_ref, cb_ref, sb_ref, lg_ref,
                      of_ref, ob_ref, stf_ref, stb_ref):
    @pl.when(pl.program_id(1) == 0)
    def _():
        stf_ref[...] = jnp.zeros_like(stf_ref)
        stb_ref[...] = jnp.zeros_like(stb_ref)

    c = RET_CHUNK
    j = lax.broadcasted_iota(jnp.int32, (c, RET_WIDTH), 0).astype(F32)
    lgf = lg_ref[0:1, :]
    lgb = lg_ref[1:2, :]
    scale = HEAD_DIM ** -0.5
    same = _same_head((RET_PAIR, RET_PAIR))

    def scan(st_ref, o_ref, k_ref, v_ref, cos_ref, sin_ref, zeta, lg, order):
        grow = jnp.exp(c * lg)
        for sub in order:
            rows = slice(sub * c, (sub + 1) * c)
            o_ref[0, sub] = st_ref[...].astype(o_ref.dtype)
            k = (_rotate(k_ref[0, rows, :], cos_ref[rows, :], sin_ref[rows, :]) * scale * zeta).astype(BF16)
            v = v_ref[0, rows, :].astype(BF16)
            for p in range(RET_PAIRS):
                lanes = slice(p * RET_PAIR, (p + 1) * RET_PAIR)
                kv = lax.dot_general(k[:, lanes], v[:, lanes], (((0,), (0,)), ((), ())),
                                     preferred_element_type=F32)
                st_ref[p] = grow[:, lanes] * st_ref[p] + jnp.where(same, kv, 0.0)

    chunks = range(RET_STATE_CHUNKS)
    scan(stf_ref, of_ref, kf_ref, vf_ref, cf_ref, sf_ref, jnp.exp((c - 1.0 - j) * lgf), lgf, chunks)
    scan(stb_ref, ob_ref, kb_ref, vb_ref, cb_ref, sb_ref, jnp.exp(j * lgb), lgb, reversed(chunks))


def _ret_states(ret3, cos_t, sin_t, lg_lane):
    bsz, n, _ = ret3.shape
    nc = n // RET_CHUNK
    rows, w = RET_STATE_CHUNKS * RET_CHUNK, RET_WIDTH
    steps = n // rows
    fwd = lambda b, i: i
    bwd = lambda b, i: steps - 1 - i
    col = lambda pos, cb: pl.BlockSpec((1, rows, w), lambda b, i: (b, pos(b, i), cb))
    tab = lambda pos: pl.BlockSpec((rows, w), lambda b, i: (pos(b, i), 0))
    st_shape = (RET_PAIRS, RET_PAIR, RET_PAIR)
    out = lambda pos: pl.BlockSpec((1, RET_STATE_CHUNKS) + st_shape, lambda b, i: (b, pos(b, i), 0, 0, 0))
    return pl.pallas_call(
        _ret_state_kernel,
        grid=(bsz, steps),
        in_specs=[col(fwd, 1), col(fwd, 2), col(bwd, 1), col(bwd, 2),
                  tab(fwd), tab(fwd), tab(bwd), tab(bwd),
                  pl.BlockSpec((2, w), lambda b, i: (0, 0))],
        out_specs=[out(fwd), out(bwd)],
        out_shape=[jax.ShapeDtypeStruct((bsz, nc) + st_shape, BF16)] * 2,
        scratch_shapes=[pltpu.VMEM(st_shape, F32), pltpu.VMEM(st_shape, F32)],
        compiler_params=_params(("parallel", "arbitrary"), 32),
        name="retention_states",
    )(ret3, ret3, ret3, ret3, cos_t, sin_t, cos_t, sin_t, lg_lane)


def _ret_out_kernel(q_ref, k_ref, v_ref, g_ref, cos_ref, sin_ref, lg_ref, lgs_ref, stf_ref, stb_ref,
                    o_ref):
    c, w = RET_CHUNK, RET_WIDTH
    pos = lax.broadcasted_iota(jnp.int32, (c, w), 0).astype(F32)
    xi_f = jnp.exp((pos + 1.0) * lg_ref[0:1, :])
    xi_b = jnp.exp((c - pos) * lg_ref[1:2, :])
    diff = (lax.broadcasted_iota(jnp.int32, (c, c), 0)
            - lax.broadcasted_iota(jnp.int32, (c, c), 1)).astype(F32)
    first = lax.broadcasted_iota(jnp.int32, (c, RET_PAIR), 1) < HEAD_DIM
    avg = jnp.where(_same_head((RET_PAIR, RET_PAIR)), 1.0 / HEAD_DIM, 0.0).astype(BF16)
    decays = [jnp.concatenate(
        [jnp.exp(jnp.where(diff >= 0, diff * lgs_ref[0, h], -diff * lgs_ref[1, h]))
         for h in (2 * p, 2 * p + 1)], axis=0) for p in range(RET_PAIRS)]
    for sub in range(RET_OUT_CHUNKS):
        rows = slice(sub * c, (sub + 1) * c)
        cos_t, sin_t = cos_ref[rows, :], sin_ref[rows, :]
        q = _rotate(q_ref[0, rows, :], cos_t, sin_t)
        k = (_rotate(k_ref[0, rows, :], cos_t, sin_t) * (HEAD_DIM ** -0.5)).astype(BF16)
        v = v_ref[0, rows, :].astype(BF16)
        g = g_ref[0, rows, :]
        q_f = (q * xi_f).astype(BF16)
        q_b = (q * xi_b).astype(BF16)
        for p in range(RET_PAIRS):
            lanes = slice(p * RET_PAIR, (p + 1) * RET_PAIR)
            q_p = q[:, lanes]
            qs = jnp.concatenate([jnp.where(first, q_p, 0.0), jnp.where(first, 0.0, q_p)], axis=0)
            s = _dot_nt(qs.astype(BF16), k[:, lanes])
            o2 = _dot((s * decays[p]).astype(BF16), v[:, lanes])
            y = jnp.where(first, o2[:c], o2[c:])
            y = y + _dot(q_f[:, lanes], stf_ref[0, sub, p]) + _dot(q_b[:, lanes], stb_ref[0, sub, p])
            sq = y * y
            sq_hi = sq.astype(BF16)
            sq_lo = (sq - sq_hi.astype(F32)).astype(BF16)
            ms = _dot(sq_hi, avg) + _dot(sq_lo, avg)
            y = y * lax.rsqrt(ms + EPS)
            g_p = g[:, lanes]
            o_ref[0, rows, lanes] = (g_p / (1.0 + jnp.exp(-g_p)) * y).astype(o_ref.dtype)


def _retention(ret3, decay):
    bsz, n, _ = ret3.shape
    rows, w = RET_OUT_CHUNKS * RET_CHUNK, RET_WIDTH
    lg = -jnp.exp(decay.astype(F32))
    lg_lane = jnp.repeat(lg, HEAD_DIM, axis=1)
    cos_t, sin_t = _rotary_tables(n)
    stf, stb = _ret_states(ret3, cos_t, sin_t, lg_lane)
    col = lambda cb: pl.BlockSpec((1, rows, w), lambda b, i: (b, i, cb))
    tab = pl.BlockSpec((rows, w), lambda b, i: (i, 0))
    st = pl.BlockSpec((1, RET_OUT_CHUNKS, RET_PAIRS, RET_PAIR, RET_PAIR), lambda b, i: (b, i, 0, 0, 0))
    return pl.pallas_call(
        _ret_out_kernel,
        grid=(bsz, n // rows),
        in_specs=[col(0), col(1), col(2), col(3), tab, tab,
                  pl.BlockSpec((2, w), lambda b, i: (0, 0)),
                  pl.BlockSpec(memory_space=pltpu.SMEM), st, st],
        out_specs=pl.BlockSpec((1, rows, w), lambda b, i: (b, i, 0)),
        out_shape=jax.ShapeDtypeStruct((bsz, n, w), BF16),
        compiler_params=_params(("parallel", "parallel"), 32),
        name="retention_out",
    )(ret3, ret3, ret3, ret3, cos_t, sin_t, lg_lane, lg, stf, stb)


ROUTE_I0, ROUTE_I1, ROUTE_W0, ROUTE_W1, ROUTE_R0, ROUTE_R1 = 0, 1, 2, 3, 4, 5


def _mix_router_kernel(x_ref, yh_ref, ya_ref, yr_ref, w_ref, g_ref, wr_ref, br_ref,
                       xo_ref, hp_ref, route_ref, counts_ref, carry_ref, logits_ref):
    i = pl.program_id(0)

    @pl.when(i == 0)
    def _():
        carry_ref[...] = jnp.zeros_like(carry_ref)
        logits_ref[...] = jnp.zeros_like(logits_ref)

    logits = logits_ref[(i + 1) % 2]
    live = jnp.where(i >= 1, 1.0, 0.0).astype(F32)
    lane = lax.broadcasted_iota(jnp.int32, logits.shape, 1)
    big = jnp.int32(LANES)
    neg = -jnp.inf
    is_group = (lane >= N_EXPERTS) & (lane < N_EXPERTS + N_GROUPS)
    gl = jnp.where(is_group, logits, neg)
    gmax = jnp.max(gl, axis=-1, keepdims=True)
    g_w = 1.0 / jnp.sum(jnp.exp(gl - gmax), axis=-1, keepdims=True)
    g_sel = jnp.min(jnp.where(gl == gmax, lane, big), axis=-1, keepdims=True) - N_EXPERTS
    lo = g_sel * EXPERTS_PER_GROUP
    el = jnp.where((lane >= lo) & (lane < lo + EXPERTS_PER_GROUP), logits, neg)
    v0 = jnp.max(el, axis=-1, keepdims=True)
    i0 = jnp.min(jnp.where(el == v0, lane, big), axis=-1, keepdims=True)
    el = jnp.where(lane == i0, neg, el)
    v1 = jnp.max(el, axis=-1, keepdims=True)
    i1 = jnp.min(jnp.where(el == v1, lane, big), axis=-1, keepdims=True)
    e1 = jnp.exp(v1 - v0)
    w0 = g_w / (1.0 + e1)
    oh0 = jnp.where(lane == i0, 1.0, 0.0)
    oh1 = jnp.where(lane == i1, 1.0, 0.0)
    tm = logits.shape[0]
    earlier = jnp.where(lax.broadcasted_iota(jnp.int32, (tm, tm), 1)
                        < lax.broadcasted_iota(jnp.int32, (tm, tm), 0), 1.0, 0.0).astype(BF16)
    carry = carry_ref[...]
    tot0 = jnp.sum(oh0, axis=0, keepdims=True)
    before0 = _dot(earlier, oh0.astype(BF16)) + carry
    before1 = _dot(earlier, oh1.astype(BF16)) + (carry + tot0)
    r0 = jnp.sum(oh0 * before0, axis=-1, keepdims=True)
    r1 = jnp.sum(oh1 * before1, axis=-1, keepdims=True)
    carry = carry + live * (tot0 + jnp.sum(oh1, axis=0, keepdims=True))
    carry_ref[...] = carry
    counts_ref[...] = jnp.broadcast_to(carry, counts_ref.shape)
    route_ref[...] = (jnp.where(lane == ROUTE_I0, i0.astype(F32), 0.0)
                      + jnp.where(lane == ROUTE_I1, i1.astype(F32), 0.0)
                      + jnp.where(lane == ROUTE_W0, w0, 0.0)
                      + jnp.where(lane == ROUTE_W1, w0 * e1, 0.0)
                      + jnp.where(lane == ROUTE_R0, r0, 0.0)
                      + jnp.where(lane == ROUTE_R1, r1, 0.0))

    mix = jnp.concatenate([yh_ref[...], ya_ref[...], yr_ref[...]], axis=1)
    x = x_ref[...] + _dot(mix, w_ref[...])
    xo_ref[...] = x
    h = x * lax.rsqrt(jnp.mean(x * x, axis=-1, keepdims=True) + EPS) * g_ref[...]
    h_hi = h.astype(BF16)
    bits = lax.bitcast_convert_type(h_hi.astype(F32), jnp.uint32)
    half = bits.shape[1] // 2
    hp_ref[...] = bits[:, :half] | (bits[:, half:] >> 16)
    h_lo = (h - h_hi.astype(F32)).astype(BF16)
    both = _dot(h_hi, wr_ref[...])
    logits_ref[i % 2] = (both[:, :LANES] + (_dot(h_lo, wr_ref[:, :LANES]) + both[:, LANES:])
                         + br_ref[...])


def _mix_router(x, yh, ya, yr, w_out, g, wr, br, tm=512):
    t, d = x.shape
    steps = t // tm
    cur = lambda i: (jnp.minimum(i, steps - 1), 0)
    row = lambda width: pl.BlockSpec((tm, width), cur)
    full = lambda shape: pl.BlockSpec(shape, lambda i: (0, 0))
    return pl.pallas_call(
        _mix_router_kernel,
        grid=(steps + 1,),
        in_specs=[row(d), row(HY_WIDTH), row(ATT_WIDTH), row(RET_WIDTH),
                  pl.BlockSpec((d, d), lambda i: (0, 0), pipeline_mode=pl.Buffered(1)),
                  full((1, d)), full((d, 2 * LANES)), full((1, LANES))],
        out_specs=[row(d), row(d // 2),
                   pl.BlockSpec((tm, LANES), lambda i: (jnp.maximum(i - 1, 0), 0)),
                   full((SUBLANES, LANES))],
        out_shape=[jax.ShapeDtypeStruct((t, d), F32), jax.ShapeDtypeStruct((t, d // 2), jnp.uint32),
                   jax.ShapeDtypeStruct((t, LANES), F32), jax.ShapeDtypeStruct((SUBLANES, LANES), F32)],
        scratch_shapes=[pltpu.VMEM((1, LANES), F32), pltpu.VMEM((2, tm, LANES), F32)],
        compiler_params=_params(("arbitrary",), 48),
        name="mix_router",
    )(x, yh, ya, yr, w_out, g.reshape(1, d), wr, br)


MOE_TM = 512
MOE_TC = 512


def _dispatch_plan(route, counts, tm):
    experts = jnp.arange(N_EXPERTS, dtype=jnp.int32)
    ids = route[:, ROUTE_I0:ROUTE_I1 + 1].astype(jnp.int32)
    rank = route[:, ROUTE_R0:ROUTE_R1 + 1].astype(jnp.int32)
    n_tiles = 2 * route.shape[0] // tm + N_EXPERTS
    count = counts[0, :N_EXPERTS].astype(jnp.int32)
    tiles = (count + tm - 1) // tm
    tile_end = jnp.cumsum(tiles)
    offs = (tile_end - tiles) * tm
    slot = jnp.sum(jnp.where(ids[..., None] == experts, offs, 0), axis=-1) + rank
    tile_expert = jnp.minimum(
        jnp.sum(jnp.arange(n_tiles, dtype=jnp.int32)[:, None] >= tile_end[None, :], axis=1),
        N_EXPERTS - 1).astype(jnp.int32)
    later = (experts[None, :] > experts[:, None]) & (tiles[None, :] > 0)
    following = jnp.min(jnp.where(later, experts[None, :], N_EXPERTS), axis=1)
    following = jnp.where(following == N_EXPERTS, -1, following).astype(jnp.int32)
    pad = (offs + count, tiles * tm - count)
    return slot, n_tiles, tile_expert, tile_end[-1:].astype(jnp.int32), following[tile_expert], pad


def _moe_dispatch_kernel(tm, pad_start_ref, pad_rows_ref, nv_ref, idx_ref, hp_ref, xs_ref, zeros_ref,
                         sem, zsem):
    tokens = hp_ref.shape[0]
    n_tiles = xs_ref.shape[0] // tm

    @pl.when(pl.program_id(0) == 0)
    def _():
        zeros_ref[...] = jnp.zeros_like(zeros_ref)
        fills = []
        for e in range(N_EXPERTS):
            rows = pad_rows_ref[e]
            end = pad_start_ref[e] + rows
            size = tm // 2
            while size >= SUBLANES:
                above = rows & ~(2 * size - 1)
                fills.append(((rows & size) != 0, pl.multiple_of(end - above - size, SUBLANES), size))
                size //= 2
            for j in range(SUBLANES - 1):
                fills.append(((rows & (SUBLANES - 1)) > j, pad_start_ref[e] + j, 1))
        for j in range(N_EXPERTS):
            fills.append((nv_ref[0] + j < n_tiles, pl.multiple_of((nv_ref[0] + j) * tm, tm), tm))
        copies = [(cond, pltpu.make_async_copy(zeros_ref.at[pl.ds(0, size)],
                                               xs_ref.at[pl.ds(start, size)], zsem))
                  for cond, start, size in fills]
        for cond, cp in copies:
            pl.when(cond)(cp.start)
        for cond, cp in copies:
            pl.when(cond)(cp.wait)

    def issue(r, carry):
        for k in range(2):
            pltpu.make_async_copy(hp_ref.at[pl.ds(r, 1)],
                                  xs_ref.at[pl.ds(idx_ref[0, 0, 2 * r + k], 1)], sem).start(priority=k)
        return carry

    lax.fori_loop(0, tokens, issue, 0, unroll=4)
    for k in range(2):
        pltpu.make_async_copy(hp_ref, xs_ref.at[pl.ds(0, tokens)], sem).wait()


def _moe_dispatch(hp, slot, pad_start, pad_rows, n_valid, n_tiles, tm, tc=MOE_TC):
    t, half = hp.shape
    steps = t // tc
    return pl.pallas_call(
        functools.partial(_moe_dispatch_kernel, tm),
        grid_spec=pltpu.PrefetchScalarGridSpec(
            num_scalar_prefetch=3,
            grid=(steps,),
            in_specs=[pl.BlockSpec((1, 1, 2 * tc), lambda i, ps, pr, nv: (i, 0, 0),
                                   memory_space=pltpu.SMEM),
                      pl.BlockSpec((tc, half), lambda i, ps, pr, nv: (i, 0))],
            out_specs=pl.BlockSpec(memory_space=pl.ANY),
            scratch_shapes=[pltpu.VMEM((tm, half), hp.dtype), pltpu.SemaphoreType.DMA(()),
                            pltpu.SemaphoreType.DMA(())],
        ),
        out_shape=jax.ShapeDtypeStruct((n_tiles * tm, half), hp.dtype),
        compiler_params=_params(("arbitrary",), 16),
        name="moe_dispatch",
    )(pad_start, pad_rows, n_valid, slot.reshape(steps, 1, 2 * tc), hp)


MOE_CAST_ROWS = 256


def _moe_gemm_kernel(layer, te_ref, nv_ref, nxt_ref, xs_ref, wg_ref, wu_ref, wd_ref, y_ref,
                     stage_g, stage_u, stage_d, wgb_ref, wub_ref, wdb_ref, sem_ref):
    i = pl.program_id(0)
    last = nv_ref[0] - 1
    tile = jnp.minimum(i, last)
    expert = te_ref[tile]
    fresh = jnp.logical_or(i == 0, expert != te_ref[jnp.maximum(tile - 1, 0)])
    pairs = ((wg_ref, stage_g, wgb_ref), (wu_ref, stage_u, wub_ref), (wd_ref, stage_d, wdb_ref))

    def copies(e):
        return [pltpu.make_async_copy(w_ref.at[layer, e], stage, sem_ref.at[n])
                for n, (w_ref, stage, _) in enumerate(pairs)]

    @pl.when(jnp.logical_and(i <= last, fresh))
    def _():
        @pl.when(i == 0)
        def _():
            for cp in copies(expert):
                cp.start()

        for cp in copies(expert):
            cp.wait()
        for _, stage, dst_ref in pairs:
            def cast(j, carry):
                rows = pl.ds(pl.multiple_of(j * MOE_CAST_ROWS, MOE_CAST_ROWS), MOE_CAST_ROWS)
                dst_ref[rows, :] = stage[rows, :].astype(BF16)
                return carry
            lax.fori_loop(0, dst_ref.shape[0] // MOE_CAST_ROWS, cast, 0)

        @pl.when(nxt_ref[tile] >= 0)
        def _():
            for cp in copies(nxt_ref[tile]):
                cp.start()

    @pl.when(i <= last)
    def _():
        xp = xs_ref[...]
        x_a = lax.bitcast_convert_type(xp & jnp.uint32(0xFFFF0000), F32).astype(BF16)
        x_b = lax.bitcast_convert_type(xp << 16, F32).astype(BF16)
        x = jnp.concatenate([x_a, x_b], axis=1)
        a = _dot(x, wgb_ref[...])
        b = _dot(x, wub_ref[...])
        act = a / (1.0 + jnp.exp(-a)) * b
        y_ref[...] = _dot(act.astype(BF16), wdb_ref[...])

    @pl.when(i > last)
    def _():
        y_ref[...] = jnp.zeros_like(y_ref)


def _moe_gemm(xs, tile_expert, n_valid, next_expert, layer, wg, wu, wd, tm=MOE_TM):
    n_slots, half = xs.shape
    _, ne, d, de = wg.shape
    tile = lambda i, te, nv, nx: jnp.minimum(i, nv[0] - 1)
    hbm = pl.BlockSpec(memory_space=pl.ANY)
    return pl.pallas_call(
        functools.partial(_moe_gemm_kernel, layer),
        grid_spec=pltpu.PrefetchScalarGridSpec(
            num_scalar_prefetch=3,
            grid=(n_slots // tm,),
            in_specs=[pl.BlockSpec((tm, half), lambda i, te, nv, nx: (tile(i, te, nv, nx), 0)),
                      hbm, hbm, hbm],
            out_specs=pl.BlockSpec((tm, d), lambda i, te, nv, nx: (i, 0)),
            scratch_shapes=[pltpu.VMEM((d, de), F32), pltpu.VMEM((d, de), F32), pltpu.VMEM((de, d), F32),
                            pltpu.VMEM((d, de), BF16), pltpu.VMEM((d, de), BF16),
                            pltpu.VMEM((de, d), BF16), pltpu.SemaphoreType.DMA((3,))],
        ),
        out_shape=jax.ShapeDtypeStruct((n_slots, d), F32),
        compiler_params=_params(("arbitrary",), 58),
        name="moe_gemm",
    )(tile_expert, n_valid, next_expert, xs, wg, wu, wd)


def _moe_combine_kernel(final_norm, steps_a, idx_ref, idx_next_ref, x_ref, route_ref, gf_ref, y_ref,
                        *rest):
    o_refs, (buf_ref, sem_ref) = rest[:-2], rest[-2:]
    i = pl.program_id(0)
    n = pl.num_programs(0)
    tokens = x_ref.shape[0]

    def start_rows(idx, slot):
        def issue(r, carry):
            for k in range(2):
                pltpu.make_async_copy(y_ref.at[pl.ds(idx[0, 0, 2 * r + k], 1)],
                                      buf_ref.at[slot, k, pl.ds(r, 1)], sem_ref.at[slot]).start(priority=k)
            return carry
        lax.fori_loop(0, tokens, issue, 0, unroll=4)

    @pl.when(i == 0)
    def _():
        start_rows(idx_ref, 0)

    @pl.when(i + 1 < n)
    def _():
        start_rows(idx_next_ref, (i + 1) % 2)

    slot = i % 2
    for k in range(2):
        pltpu.make_async_copy(y_ref.at[pl.ds(0, tokens)], buf_ref.at[slot, k], sem_ref.at[slot]).wait()
    route = route_ref[...]
    y = x_ref[...] + route[:, ROUTE_W0:ROUTE_W0 + 1] * buf_ref[slot, 0] \
        + route[:, ROUTE_W1:ROUTE_W1 + 1] * buf_ref[slot, 1]
    if final_norm:
        y = y * lax.rsqrt(jnp.mean(y * y, axis=-1, keepdims=True) + EPS) * gf_ref[...]
    if len(o_refs) == 1:
        o_refs[0][...] = y
    else:
        @pl.when(i < steps_a)
        def _():
            o_refs[0][...] = y

        @pl.when(i >= steps_a)
        def _():
            o_refs[1][...] = y


def _moe_combine(x, route, slot, y, gf, final_norm, tokens_a=None, tc=MOE_TC):
    t, d = x.shape
    steps = t // tc
    idx = slot.reshape(steps, 1, 2 * tc)
    row = lambda width: pl.BlockSpec((tc, width), lambda i: (i, 0))
    smem = lambda pos: pl.BlockSpec((1, 1, 2 * tc), lambda i: (pos(i), 0, 0), memory_space=pltpu.SMEM)
    if tokens_a is None:
        steps_a = steps
        out_specs = [row(d)]
        out_shape = [jax.ShapeDtypeStruct((t, d), F32)]
    else:
        steps_a = tokens_a // tc
        out_specs = [pl.BlockSpec((tc, d), lambda i: (jnp.minimum(i, steps_a - 1), 0)),
                     pl.BlockSpec((tc, d), lambda i: (jnp.maximum(i - steps_a, 0), 0))]
        out_shape = [jax.ShapeDtypeStruct((tokens_a, d), F32),
                     jax.ShapeDtypeStruct((t - tokens_a, d), F32)]
    return pl.pallas_call(
        functools.partial(_moe_combine_kernel, final_norm, steps_a),
        grid=(steps,),
        in_specs=[smem(lambda i: i), smem(lambda i: jnp.minimum(i + 1, steps - 1)),
                  row(d), row(LANES), pl.BlockSpec((1, d), lambda i: (0, 0)),
                  pl.BlockSpec(memory_space=pl.ANY)],
        out_specs=out_specs,
        out_shape=out_shape,
        scratch_shapes=[pltpu.VMEM((2, 2, tc, d), F32), pltpu.SemaphoreType.DMA((2,))],
        compiler_params=_params(("arbitrary",), 52),
        name="moe_combine",
    )(idx, idx, x, route, gf.reshape(1, d), y)


def _moe(hp, route, counts, x, layer, wg, wu, wd, gf, final_norm, tokens_a=None):
    slot, n_tiles, tile_expert, n_valid, next_expert, pad = _dispatch_plan(route, counts, MOE_TM)
    xs = _moe_dispatch(hp, slot, pad[0], pad[1], n_valid, n_tiles, MOE_TM)
    y = _moe_gemm(xs, tile_expert, n_valid, next_expert, layer, wg, wu, wd)
    return _moe_combine(x, route, slot, y, gf, final_norm, tokens_a)


def _router_weights(wg, bg, we, be):
    d = wg.shape[0]
    pad = LANES - N_EXPERTS - N_GROUPS
    wr = jnp.concatenate([we, wg, jnp.zeros((d, pad), F32)], axis=1).astype(F32)
    br = jnp.concatenate([be, bg, jnp.zeros((pad,), F32)]).astype(F32).reshape(1, LANES)
    wr_hi = wr.astype(BF16)
    wr_lo = (wr - wr_hi.astype(F32)).astype(BF16)
    return jnp.concatenate([wr_hi, wr_lo], axis=1), br


def _layer(xs, bsz, p, l, consts, final_g, tokens_a):
    dft, emb = consts
    w_in = p['w_in'][l].astype(BF16)
    g1 = p['ln1_g'][l]
    w_hy, w_att, w_ret = w_in[:, :HY_IN], w_in[:, HY_IN:HY_IN + ATT_IN], w_in[:, HY_IN + ATT_IN:]
    if len(xs) == 2:
        hy, x = _norm_proj_stack(xs[0], xs[1], g1, w_hy, F32)
        att, ret = _norm_proj(x, g1, [w_att, w_ret], [BF16, F32])
    else:
        x, = xs
        hy, att, ret = _norm_proj(x, g1, [w_hy, w_att, w_ret], [F32, BF16, F32])
    t, d = x.shape
    n = t // bsz
    hy = hy.reshape(bsz, n, HY_IN)
    att = att.reshape(bsz, n, ATT_IN)
    ret = ret.reshape(bsz, n, RET_IN)
    y_h = _hyena(hy, dft, emb, p['hy_conv_w'][l], p['hy_conv_b'][l], p['hy_w1'][l], p['hy_b1'][l],
                 p['hy_w2'][l], p['hy_b2'][l], p['hy_w3'][l], p['hy_freq'][l], p['hy_decay'][l],
                 p['hy_skip'][l])
    y_a = _attention(att, p['attn_sink'][l], p['rel_bias'])
    y_r = _retention(ret, p['ret_decay'][l])
    wr, br = _router_weights(p['router_group_w'][l], p['router_group_b'][l],
                             p['router_expert_w'][l], p['router_expert_b'][l])
    x, hp, route, counts = _mix_router(x, y_h.reshape(t, -1), y_a.reshape(t, -1), y_r.reshape(t, -1),
                                       p['w_out'][l].astype(BF16), p['ln2_g'][l], wr, br)
    last = l == DEPTH - 1
    return _moe(hp, route, counts, x, l, p['moe_w_gate'], p['moe_w_up'], p['moe_w_down'], final_g,
                final_norm=last, tokens_a=tokens_a if last else None)


def kernel(x_prompt, x_sample, ln1_g, w_in, hy_conv_w, hy_conv_b, hy_w1, hy_b1, hy_w2, hy_b2, hy_w3,
           hy_freq, hy_decay, hy_skip, attn_sink, rel_bias, ret_decay, w_out, ln2_g, router_group_w,
           router_group_b, router_expert_w, router_expert_b, moe_w_gate, moe_w_up, moe_w_down, lnf_g):
    p = dict(ln1_g=ln1_g, w_in=w_in, hy_conv_w=hy_conv_w, hy_conv_b=hy_conv_b, hy_w1=hy_w1,
             hy_b1=hy_b1, hy_w2=hy_w2, hy_b2=hy_b2, hy_w3=hy_w3, hy_freq=hy_freq, hy_decay=hy_decay,
             hy_skip=hy_skip, attn_sink=attn_sink, rel_bias=rel_bias, ret_decay=ret_decay, w_out=w_out,
             ln2_g=ln2_g, router_group_w=router_group_w, router_group_b=router_group_b,
             router_expert_w=router_expert_w, router_expert_b=router_expert_b, moe_w_gate=moe_w_gate,
             moe_w_up=moe_w_up, moe_w_down=moe_w_down)
    bp, n, d = x_prompt.shape
    bs = x_sample.shape[0]
    assert x_sample.shape[1:] == (n, d)
    bsz = bp + bs
    consts = (_dft_matrices(n), _hyena_embedding(n))
    xs = (x_prompt.reshape(bp * n, d), x_sample.reshape(bs * n, d))
    for l in range(DEPTH):
        xs = _layer(xs, bsz, p, l, consts, lnf_g, bp * n)
    y_prompt, y_sample = xs
    return (y_prompt.reshape(bp, n, d), y_sample.reshape(bs, n, d))
```

```python
import functools
import math

import numpy as np
import jax
import jax.numpy as jnp
from jax import lax
from jax.experimental import pallas as pl
from jax.experimental.pallas import tpu as pltpu

F32 = jnp.float32
BF16 = jnp.bfloat16
HIGHEST = lax.Precision.HIGHEST

D_MODEL = 2048
DEPTH = 2
HEAD_DIM = 64
EPS = 1e-6
HY_WIDTH = D_MODEL // 4
ATT_WIDTH = D_MODEL // 2
RET_WIDTH = D_MODEL // 4
HY_ORDER = 2
HY_EMB = 33
HY_FFN = 64
ATT_HEADS = ATT_WIDTH // HEAD_DIM
ATT_KV_HEADS = 4
ATT_GROUP = ATT_HEADS // ATT_KV_HEADS
KV_WIDTH = ATT_KV_HEADS * HEAD_DIM
WINDOW = 128
BLOCK = 128
REL_BUCKETS = 32
REL_MAX_DIST = 128
RET_HEADS = RET_WIDTH // HEAD_DIM
RET_CHUNK = 128
ROPE_BASE = 10000.0
N_GROUPS = 4
EXPERTS_PER_GROUP = 4
N_EXPERTS = N_GROUPS * EXPERTS_PER_GROUP
D_EXPERT = D_MODEL // 2
HY_IN = 3 * HY_WIDTH
ATT_IN = ATT_WIDTH + 2 * KV_WIDTH
RET_IN = 4 * RET_WIDTH

LANES = 128
SUBLANES = 8
MIB = 1024 * 1024


def _params(semantics, vmem_mib):
    return pltpu.CompilerParams(dimension_semantics=semantics, vmem_limit_bytes=vmem_mib * MIB)


def _dot(a, b):
    return jnp.dot(a, b, preferred_element_type=F32)


def _dot_nt(a, b):
    return lax.dot_general(a, b, (((1,), (1,)), ((), ())), preferred_element_type=F32)


def _norm_proj_kernel(x_ref, g_ref, *refs):
    w_refs, o_refs = refs[:len(refs) // 2], refs[len(refs) // 2:]
    x = x_ref[...]
    inv = lax.rsqrt(jnp.mean(x * x, axis=-1, keepdims=True) + EPS)
    h = (x * inv * g_ref[...]).astype(BF16)
    for w_ref, o_ref in zip(w_refs, o_refs):
        o_ref[...] = _dot(h, w_ref[...]).astype(o_ref.dtype)


def _norm_proj(x, g, weights, out_dtypes, tm=512):
    t, d = x.shape
    return pl.pallas_call(
        _norm_proj_kernel,
        grid=(t // tm,),
        in_specs=[pl.BlockSpec((tm, d), lambda i: (i, 0)), pl.BlockSpec((1, d), lambda i: (0, 0))]
        + [pl.BlockSpec(w.shape, lambda i: (0, 0), pipeline_mode=pl.Buffered(1)) for w in weights],
        out_specs=[pl.BlockSpec((tm, w.shape[1]), lambda i: (i, 0)) for w in weights],
        out_shape=[jax.ShapeDtypeStruct((t, w.shape[1]), dt) for w, dt in zip(weights, out_dtypes)],
        compiler_params=_params(("parallel",), 56),
        name="norm_proj",
    )(x, g.reshape(1, d), *weights)


def _norm_proj_stack_kernel(tiles_a, xa_ref, xb_ref, g_ref, w_ref, o_ref, x_ref):
    x = jnp.where(pl.program_id(0) < tiles_a, xa_ref[...], xb_ref[...])
    x_ref[...] = x
    inv = lax.rsqrt(jnp.mean(x * x, axis=-1, keepdims=True) + EPS)
    h = (x * inv * g_ref[...]).astype(BF16)
    o_ref[...] = _dot(h, w_ref[...]).astype(o_ref.dtype)


def _norm_proj_stack(xa, xb, g, w, out_dtype, tm=512):
    ta, d = xa.shape
    tb = xb.shape[0]
    n = w.shape[1]
    tiles_a = ta // tm
    return pl.pallas_call(
        functools.partial(_norm_proj_stack_kernel, tiles_a),
        grid=((ta + tb) // tm,),
        in_specs=[
            pl.BlockSpec((tm, d), lambda i: (jnp.minimum(i, tiles_a - 1), 0)),
            pl.BlockSpec((tm, d), lambda i: (jnp.maximum(i - tiles_a, 0), 0)),
            pl.BlockSpec((1, d), lambda i: (0, 0)),
            pl.BlockSpec((d, n), lambda i: (0, 0)),
        ],
        out_specs=[pl.BlockSpec((tm, n), lambda i: (i, 0)), pl.BlockSpec((tm, d), lambda i: (i, 0))],
        out_shape=[jax.ShapeDtypeStruct((ta + tb, n), out_dtype),
                   jax.ShapeDtypeStruct((ta + tb, d), xa.dtype)],
        compiler_params=_params(("parallel",), 56),
        name="norm_proj_stack",
    )(xa, xb, g.reshape(1, d), w)


def _short_conv(x, prev_row, next_row, w_ref, b_ref):
    n = x.shape[0]
    row = lax.broadcasted_iota(jnp.int32, x.shape, 0)
    prev = jnp.where(row == 0, prev_row, pltpu.roll(x, 1, 0))
    nxt = jnp.where(row == n - 1, next_row, pltpu.roll(x, n - 1, 0))
    return prev * w_ref[0:1, :] + x * w_ref[1:2, :] + nxt * w_ref[2:3, :] + b_ref[...]


def _split_rows(x, slab_ref):
    half = x.shape[0] // 2
    slabs = x.shape[1] // LANES
    for j in range(slabs):
        slab_ref[j] = x[:, j * LANES:(j + 1) * LANES]
    rows = lambda start: jnp.concatenate(
        [slab_ref[j, pl.ds(start, half, stride=2), :] for j in range(slabs)], axis=1)
    return rows(0), rows(1)


def _merge_rows(x_e, x_o, slab_ref):
    half = x_e.shape[0]
    slabs = x_e.shape[1] // LANES
    for j in range(slabs):
        slab_ref[j, pl.ds(0, half, stride=2), :] = x_e[:, j * LANES:(j + 1) * LANES]
        slab_ref[j, pl.ds(1, half, stride=2), :] = x_o[:, j * LANES:(j + 1) * LANES]
    return jnp.concatenate([slab_ref[j] for j in range(slabs)], axis=1)


def _hy_filter_kernel(z_ref, w1_ref, b1_ref, w2_ref, b2_ref, fr_ref, w3f_ref, w3b_ref,
                      decf_ref, decb_ref, hs_ref, hd_ref):
    fr = fr_ref[...]
    h = jnp.sin(fr * (jnp.dot(z_ref[...], w1_ref[...], precision=HIGHEST,
                              preferred_element_type=F32) + b1_ref[...]))
    h = jnp.sin(fr * (jnp.dot(h, w2_ref[...], precision=HIGHEST,
                              preferred_element_type=F32) + b2_ref[...]))
    n = h.shape[0]
    shape = (n, w3f_ref.shape[1])
    row = lax.broadcasted_iota(jnp.int32, shape, 0)
    t = row.astype(F32) * (1.0 / (n - 1))
    hf = jnp.dot(h, w3f_ref[...], precision=HIGHEST, preferred_element_type=F32)
    hf = hf * jnp.exp(-t * jnp.abs(decf_ref[...]))
    hb = jnp.dot(h, w3b_ref[...], precision=HIGHEST, preferred_element_type=F32)
    hb = hb * jnp.exp(-t * jnp.abs(decb_ref[...]))
    hb = jnp.where(row == 0, 0.0, hb)
    norm = (jnp.sum(jnp.abs(hf), axis=0, keepdims=True)
            + jnp.sum(jnp.abs(hb), axis=0, keepdims=True))
    hs_ref[...] = (hf + hb) / norm
    hd_ref[...] = (hb - hf) / norm


def _hy_filter(z, w1, b1, w2, b2, fr, w3, decay, tc=256):
    n = z.shape[0]
    cols = HY_ORDER * HY_WIDTH
    w3f, w3b = w3[:, :cols], w3[:, cols:]
    dec = decay.reshape(2, cols)
    zp = jnp.pad(z, ((0, 0), (0, LANES - HY_EMB)))
    w1p = jnp.pad(w1, ((0, LANES - HY_EMB), (0, 0)))
    full = lambda shape: pl.BlockSpec(shape, lambda j: (0, 0))
    col = lambda rows: pl.BlockSpec((rows, tc), lambda j: (0, j))
    return pl.pallas_call(
        _hy_filter_kernel,
        grid=(cols // tc,),
        in_specs=[full((n, LANES)), full((LANES, HY_FFN)), full((1, HY_FFN)), full((HY_FFN, HY_FFN)),
                  full((1, HY_FFN)), full((1, HY_FFN)), col(HY_FFN), col(HY_FFN), col(1), col(1)],
        out_specs=[col(n), col(n)],
        out_shape=[jax.ShapeDtypeStruct((n, cols), F32)] * 2,
        compiler_params=_params(("parallel",), 48),
        name="hy_filter",
    )(zp, w1p, b1.reshape(1, -1), w2, b2.reshape(1, -1), fr.reshape(1, -1), w3f, w3b,
      dec[0:1], dec[1:2])


def _alternating_sum(x):
    row = lax.broadcasted_iota(jnp.int32, x.shape, 0)
    return jnp.sum(x * (1 - 2 * (row & 1)).astype(F32), axis=0, keepdims=True)


def _half_spectra(fw_ref, rows, x_e, x_o, y_e, y_o):
    ec, oc = _dot(fw_ref[0, rows, :], x_e), _dot(fw_ref[1, rows, :], x_o)
    es, os_ = _dot(fw_ref[2, rows, :], y_e), _dot(fw_ref[3, rows, :], y_o)
    return ec + oc, ec - oc, es + os_, os_ - es


def _hy_kf_kernel(fw_ref, hs_ref, hd_ref, k_ref, slab_ref):
    hs_e, hs_o = _split_rows(hs_ref[...], slab_ref)
    hd_e, hd_o = _split_rows(hd_ref[...], slab_ref)
    k0, k1, k2, k3 = _half_spectra(fw_ref, slice(None), hs_e.astype(BF16), hs_o.astype(BF16),
                                   hd_e.astype(BF16), hd_o.astype(BF16))
    first = jnp.logical_and(lax.broadcasted_iota(jnp.int32, k0.shape, 0) == 0, pl.program_id(0) == 0)
    k_ref[0] = k0
    k_ref[1] = k1
    k_ref[2] = jnp.where(first, _alternating_sum(hs_e), k2)
    k_ref[3] = jnp.where(first, _alternating_sum(hd_o), k3)


def _hy_kf(fw, hs, hd, tf=256, tc=512):
    n, cols = hs.shape
    h = n // 2
    return pl.pallas_call(
        _hy_kf_kernel,
        grid=(h // tf, cols // tc),
        in_specs=[pl.BlockSpec((4, tf, h), lambda i, j: (0, i, 0)),
                  pl.BlockSpec((n, tc), lambda i, j: (0, j)),
                  pl.BlockSpec((n, tc), lambda i, j: (0, j))],
        out_specs=pl.BlockSpec((4, tf, tc), lambda i, j: (0, i, j)),
        out_shape=jax.ShapeDtypeStruct((4, h, cols), F32),
        scratch_shapes=[pltpu.VMEM((tc // LANES, n, LANES), F32)],
        compiler_params=_params(("parallel", "arbitrary"), 48),
        name="hy_filter_dft",
    )(fw, hs, hd)


HY_FREQ_TILE = 512


def _hy_fwd_kernel(conv, fw_ref, u_ref, k_ref, *rest):
    half = fw_ref.shape[1]
    u = u_ref[0]
    if conv:
        w_ref, b_ref, y_ref, slab_ref = rest
        u = _short_conv(u, 0.0, 0.0, w_ref, b_ref)
    else:
        y_ref, slab_ref = rest
    u_e, u_o = _split_rows(u, slab_ref)
    mid_a, mid_b = _alternating_sum(u_e), _alternating_sum(u_o)
    u_e, u_o = u_e.astype(BF16), u_o.astype(BF16)
    tf = HY_FREQ_TILE
    for i in range(half // tf):
        rows = slice(i * tf, (i + 1) * tf)
        p0, p1, p2, p3 = _half_spectra(fw_ref, rows, u_e, u_o, u_e, u_o)
        k0, k1, k2, k3 = (k_ref[j, rows, :] for j in range(4))
        y0, y2 = p0 * k0 + p2 * k2, p0 * k2 - p2 * k0
        y1, y3 = p1 * k1 + p3 * k3, p1 * k3 - p3 * k1
        he, ho = y2 - y3, y2 + y3
        if i == 0:
            first = lax.broadcasted_iota(jnp.int32, p0.shape, 0) == 0
            y0 = jnp.where(first, p0 * k0, y0)
            y1 = jnp.where(first, p1 * k1, y1)
            he = jnp.where(first, mid_a * k2 + mid_b * k3, he)
            ho = jnp.where(first, mid_a * k3 - mid_b * k2, ho)
        y_ref[0, 0, rows, :] = (y0 + y1).astype(BF16)
        y_ref[0, 1, rows, :] = he.astype(BF16)
        y_ref[0, 2, rows, :] = (y0 - y1).astype(BF16)
        y_ref[0, 3, rows, :] = ho.astype(BF16)


def _hy_fwd(fw, u3, ucol, k, kcol, conv=None):
    bsz, n, _ = u3.shape
    c = HY_WIDTH
    h = n // 2
    in_specs = [
        pl.BlockSpec((4, h, h), lambda b: (0, 0, 0), pipeline_mode=pl.Buffered(1)),
        pl.BlockSpec((1, n, c), lambda b: (b, 0, ucol)),
        pl.BlockSpec((4, h, c), lambda b: (0, 0, kcol), pipeline_mode=pl.Buffered(1)),
    ]
    args = [fw, u3, k]
    if conv is not None:
        in_specs += [pl.BlockSpec((3, c), lambda b: (0, ucol)), pl.BlockSpec((1, c), lambda b: (0, ucol))]
        args += list(conv)
    return pl.pallas_call(
        functools.partial(_hy_fwd_kernel, conv is not None),
        grid=(bsz,),
        in_specs=in_specs,
        out_specs=pl.BlockSpec((1, 4, h, c), lambda b: (b, 0, 0, 0)),
        out_shape=jax.ShapeDtypeStruct((bsz, 4, h, c), BF16),
        scratch_shapes=[pltpu.VMEM((c // LANES, n, LANES), F32)],
        compiler_params=_params(("arbitrary",), 52),
        name="hy_dft_fwd",
    )(*args)


def _hy_inv_kernel(s_conv, iv_ref, y_ref, d_ref, *rest):
    o_ref, slab_ref = rest[-2:]
    i = pl.program_id(0)
    first, last = i == 0, i == pl.num_programs(0) - 1

    def conv_tile(x_ref, lo_ref, hi_ref, w_ref, b_ref):
        prev_row = jnp.where(first, 0.0, lo_ref[0, SUBLANES - 1:SUBLANES, :])
        next_row = jnp.where(last, 0.0, hi_ref[0, 0:1, :])
        return _short_conv(x_ref[0], prev_row, next_row, w_ref, b_ref)

    if s_conv:
        skip = conv_tile(*rest[0:5])
        gate = conv_tile(*rest[5:10])
    else:
        skip = rest[0][0]
        gate = conv_tile(*rest[1:6])
    y_e = _dot(iv_ref[0], y_ref[0, 0]) + _dot(iv_ref[1], y_ref[0, 1])
    y_o = _dot(iv_ref[2], y_ref[0, 2]) + _dot(iv_ref[3], y_ref[0, 3])
    y = _merge_rows(y_e, y_o, slab_ref)
    o_ref[0] = (gate * (y + skip * d_ref[...])).astype(o_ref.dtype)


def _hy_inv(iv, y, d, raw3, conv, gcol, out_dtype, scol=None, s3=None, tt=1024):
    bsz, _, h, c = y.shape
    n = 2 * h
    conv_w, conv_b = conv
    halo = tt // SUBLANES
    tile = lambda col: pl.BlockSpec((1, tt, c), lambda i, b: (b, i, col))
    lo = lambda col: pl.BlockSpec((1, SUBLANES, c), lambda i, b: (b, jnp.maximum(i * halo - 1, 0), col))
    hi = lambda col: pl.BlockSpec((1, SUBLANES, c),
                                  lambda i, b: (b, jnp.minimum((i + 1) * halo, n // SUBLANES - 1), col))
    wspec = lambda col: pl.BlockSpec((3, c), lambda i, b: (0, col))
    bspec = lambda col: pl.BlockSpec((1, c), lambda i, b: (0, col))
    conv_in = lambda col: ([tile(col), lo(col), hi(col), wspec(col), bspec(col)],
                           [raw3, raw3, raw3, conv_w, conv_b])
    in_specs = [pl.BlockSpec((4, tt // 2, h), lambda i, b: (0, i, 0)),
                pl.BlockSpec((1, 4, h, c), lambda i, b: (b, 0, 0, 0)),
                pl.BlockSpec((1, c), lambda i, b: (0, 0))]
    args = [iv, y, d.reshape(1, c)]
    if s3 is None:
        specs, arrs = conv_in(scol)
    else:
        specs, arrs = [tile(0)], [s3]
    in_specs += specs
    args += arrs
    specs, arrs = conv_in(gcol)
    in_specs += specs
    args += arrs
    return pl.pallas_call(
        functools.partial(_hy_inv_kernel, s3 is None),
        grid=(n // tt, bsz),
        in_specs=in_specs,
        out_specs=pl.BlockSpec((1, tt, c), lambda i, b: (b, i, 0)),
        out_shape=jax.ShapeDtypeStruct((bsz, n, c), out_dtype),
        scratch_shapes=[pltpu.VMEM((c // LANES, tt, LANES), F32)],
        compiler_params=_params(("arbitrary", "arbitrary"), 48),
        name="hy_dft_inv",
    )(*args)


DFT_SPLIT = 32


def _dft_matrices(n):
    h = n // 2

    def cos_sin(offset):
        def table(rows, step):
            f = lax.broadcasted_iota(jnp.int32, (rows, h), 0) * step
            t = 2 * lax.broadcasted_iota(jnp.int32, (rows, h), 1) + offset
            ang = ((f * t) & (2 * n - 1)).astype(F32) * (math.pi / n)
            return jnp.cos(ang), jnp.sin(ang)
        cos_hi, sin_hi = (x[:, None, :] for x in table(h // DFT_SPLIT, DFT_SPLIT))
        cos_lo, sin_lo = (x[None, :, :] for x in table(DFT_SPLIT, 1))
        return ((cos_hi * cos_lo - sin_hi * sin_lo).reshape(h, h),
                (sin_hi * cos_lo + cos_hi * sin_lo).reshape(h, h))

    cos_e, sin_e = cos_sin(0)
    cos_o, sin_o = cos_sin(1)
    fwd = jnp.stack([cos_e, cos_o, sin_e, sin_o]).astype(BF16)
    m = lax.broadcasted_iota(jnp.int32, (h, h), 0)
    f = lax.broadcasted_iota(jnp.int32, (h, h), 1)
    weight = jnp.where(f == 0, 0.5 / n, 1.0 / n)
    mid = (1 - 2 * (m & 1)).astype(F32) * (1.0 / n)
    inv = jnp.stack([weight * cos_e.T, jnp.where(f == 0, mid, sin_e.T * (-1.0 / n)),
                     weight * cos_o.T, jnp.where(f == 0, -mid, sin_o.T * (-1.0 / n))]).astype(BF16)
    return fwd, inv


def _hyena_embedding(n):
    t = jnp.linspace(0.0, 1.0, n, dtype=F32)[:, None]
    bands = (HY_EMB - 1) // 2
    f = jnp.linspace(1e-4, bands - 1, bands, dtype=F32)[None]
    w = 2.0 * math.pi * jnp.arange(n, dtype=F32)[:, None] / n
    return jnp.concatenate([t, jnp.cos(f * w), -jnp.sin(f * w)], -1)


def _hyena(p3, dft, emb, conv_w, conv_b, w1, b1, w2, b2, w3, freq, decay, skip):
    ff, fi = dft
    conv = (conv_w, conv_b.reshape(1, -1))
    hs, hd = _hy_filter(emb, w1, b1, w2, b2, freq, w3, decay)
    k = _hy_kf(ff, hs, hd)
    y = _hy_fwd(ff, p3, 0, k, 0, conv)
    z = _hy_inv(fi, y, skip[0], p3, conv, 1, F32, scol=0)
    y = _hy_fwd(ff, z, 0, k, 1)
    return _hy_inv(fi, y, skip[1], p3, conv, 2, BF16, s3=z)


def _t5_bucket(rel):
    nb = REL_BUCKETS // 2
    max_exact = nb // 2
    n = np.abs(rel)
    large = max_exact + (np.log(np.maximum(n, 1) / max_exact) / np.log(REL_MAX_DIST / max_exact)
                         * (nb - max_exact)).astype(np.int32)
    large = np.minimum(large, nb - 1)
    return (rel > 0).astype(np.int32) * nb + np.where(n < max_exact, n, large)


def _attn_kernel(q_ref, kp_ref, kc_ref, kn_ref, vp_ref, vc_ref, vn_ref, bias_ref, sink_ref, o_ref):
    n = pl.program_id(1)
    q = q_ref[0] * (HEAD_DIM ** -0.5)
    k = jnp.concatenate([kp_ref[0], kc_ref[0], kn_ref[0]], axis=0)
    vt = jnp.concatenate([vp_ref[0], vc_ref[0], vn_ref[0]], axis=1)
    edge_prev = jnp.where(n == 0, -1e30, 0.0).astype(F32)
    edge_next = jnp.where(n == pl.num_programs(1) - 1, -1e30, 0.0).astype(F32)
    ones = jnp.ones((HEAD_DIM, 3 * BLOCK), BF16)
    for kh in range(ATT_KV_HEADS):
        k_h = k[:, kh * HEAD_DIM:(kh + 1) * HEAD_DIM]
        vt_h = jnp.concatenate([vt[kh * HEAD_DIM:(kh + 1) * HEAD_DIM], ones], axis=0)
        q_h = jnp.concatenate([q[:, (kh * ATT_GROUP + g) * HEAD_DIM:(kh * ATT_GROUP + g + 1) * HEAD_DIM]
                               for g in range(ATT_GROUP)], axis=0)
        s = _dot_nt(k_h, q_h) + bias_ref[kh]
        s = jnp.concatenate([s[:BLOCK] + edge_prev, s[BLOCK:2 * BLOCK], s[2 * BLOCK:] + edge_next],
                            axis=0)
        sink = sink_ref[kh]
        m = jnp.maximum(jnp.max(s, axis=0, keepdims=True), sink)
        p = jnp.exp(s - m).astype(BF16)
        o = _dot(vt_h, p)
        o = o[:HEAD_DIM] / (o[HEAD_DIM:HEAD_DIM + 1] + jnp.exp(sink - m))
        for g in range(0, ATT_GROUP, 2):
            pair = jnp.concatenate([o[:, g * BLOCK:(g + 1) * BLOCK],
                                    o[:, (g + 1) * BLOCK:(g + 2) * BLOCK]], axis=0)
            c0 = (kh * ATT_GROUP + g) * HEAD_DIM
            o_ref[0, :, c0:c0 + 2 * HEAD_DIM] = pair.T.astype(o_ref.dtype)


def _attention_bias(rel_bias):
    rel = (np.arange(3 * BLOCK)[None, :] - BLOCK) - np.arange(BLOCK)[:, None]
    bucket = jnp.asarray(_t5_bucket(rel).reshape(-1), jnp.int32)
    onehot = (bucket[:, None] == jnp.arange(REL_BUCKETS, dtype=jnp.int32)[None]).astype(F32)
    bias = jnp.dot(onehot, rel_bias.astype(F32), precision=HIGHEST)
    bias = bias.reshape(BLOCK, 3 * BLOCK, ATT_KV_HEADS, ATT_GROUP)
    bias = jnp.where(jnp.asarray(np.abs(rel) <= WINDOW)[:, :, None, None], bias, -1e30)
    return jnp.transpose(bias, (2, 1, 3, 0)).reshape(ATT_KV_HEADS, 3 * BLOCK, ATT_GROUP * BLOCK)


def _attention(att3, sink, rel_bias):
    bsz, n, _ = att3.shape
    nb = n // BLOCK
    bias = _attention_bias(rel_bias)
    sink_row = jnp.repeat(sink.astype(F32).reshape(ATT_KV_HEADS, 1, ATT_GROUP), BLOCK, axis=2)
    vt = jnp.swapaxes(att3[:, :, ATT_WIDTH + KV_WIDTH:], 1, 2)
    kcol = ATT_WIDTH // KV_WIDTH
    prev = lambda i: jnp.maximum(i - 1, 0)
    nxt = lambda i: jnp.minimum(i + 1, nb - 1)
    cur = lambda i: i
    k_spec = lambda blk: pl.BlockSpec((1, BLOCK, KV_WIDTH), lambda b, i: (b, blk(i), kcol))
    v_spec = lambda blk: pl.BlockSpec((1, KV_WIDTH, BLOCK), lambda b, i: (b, 0, blk(i)))
    return pl.pallas_call(
        _attn_kernel,
        grid=(bsz, nb),
        in_specs=[
            pl.BlockSpec((1, BLOCK, ATT_WIDTH), lambda b, i: (b, i, 0)),
            k_spec(prev), k_spec(cur), k_spec(nxt), v_spec(prev), v_spec(cur), v_spec(nxt),
            pl.BlockSpec((ATT_KV_HEADS, 3 * BLOCK, ATT_GROUP * BLOCK), lambda b, i: (0, 0, 0)),
            pl.BlockSpec((ATT_KV_HEADS, 1, ATT_GROUP * BLOCK), lambda b, i: (0, 0, 0)),
        ],
        out_specs=pl.BlockSpec((1, BLOCK, ATT_WIDTH), lambda b, i: (b, i, 0)),
        out_shape=jax.ShapeDtypeStruct((bsz, n, ATT_WIDTH), BF16),
        compiler_params=_params(("parallel", "parallel"), 32),
        name="window_attention",
    )(att3, att3, att3, att3, vt, vt, vt, bias, sink_row)


def _rotary_tables(n):
    inv = ROPE_BASE ** -jnp.linspace(0.0, 1.0, HEAD_DIM // 2, dtype=F32)
    ang = jnp.arange(n, dtype=F32)[:, None] * inv[None]
    cos, sin = jnp.cos(ang), jnp.sin(ang)
    cos_t = jnp.tile(jnp.concatenate([cos, cos], -1), (1, RET_HEADS))
    sin_t = jnp.tile(jnp.concatenate([-sin, sin], -1), (1, RET_HEADS))
    return cos_t, sin_t


def _rotate(x, cos_t, sin_t):
    width = x.shape[-1]
    half = HEAD_DIM // 2
    lane = lax.broadcasted_iota(jnp.int32, x.shape, 1)
    partner = jnp.where((lane & (HEAD_DIM - 1)) < half,
                        pltpu.roll(x, width - half, 1), pltpu.roll(x, half, 1))
    return x * cos_t + partner * sin_t


RET_PAIR = 2 * HEAD_DIM
RET_PAIRS = RET_WIDTH // RET_PAIR


def _same_head(shape):
    r = lax.broadcasted_iota(jnp.int32, shape, 0) // HEAD_DIM
    c = lax.broadcasted_iota(jnp.int32, shape, 1) // HEAD_DIM
    return r == c


RET_STATE_CHUNKS = 8
RET_OUT_CHUNKS = 4


def _ret_state_kernel(kf_ref, vf_ref, kb_ref, vb_ref, cf_ref, sf_ref, cb_ref, sb_ref, lg_ref,
                      of_ref, ob_ref, stf_ref, stb_ref):
    @pl.when(pl.program_id(1) == 0)
    def _():
        stf_ref[...] = jnp.zeros_like(stf_ref)
        stb_ref[...] = jnp.zeros_like(stb_ref)

    c = RET_CHUNK
    j = lax.broadcasted_iota(jnp.int32, (c, RET_WIDTH), 0).astype(F32)
    lgf = lg_ref[0:1, :]
    lgb = lg_ref[1:2, :]
    scale = HEAD_DIM ** -0.5
    same = _same_head((RET_PAIR, RET_PAIR))

    def scan(st_ref, o_ref, k_ref, v_ref, cos_ref, sin_ref, zeta, lg, order):
        grow = jnp.exp(c * lg)
        for sub in order:
            rows = slice(sub * c, (sub + 1) * c)
            o_ref[0, sub] = st_ref[...].astype(o_ref.dtype)
            k = (_rotate(k_ref[0, rows, :], cos_ref[rows, :], sin_ref[rows, :]) * scale * zeta).astype(BF16)
            v = v_ref[0, rows, :].astype(BF16)
            for p in range(RET_PAIRS):
                lanes = slice(p * RET_PAIR, (p + 1) * RET_PAIR)
                kv = lax.dot_general(k[:, lanes], v[:, lanes], (((0,), (0,)), ((), ())),
                                     preferred_element_type=F32)
                st_ref[p] = grow[:, lanes] * st_ref[p] + jnp.where(same, kv, 0.0)

    chunks = range(RET_STATE_CHUNKS)
    scan(stf_ref, of_ref, kf_ref, vf_ref, cf_ref, sf_ref, jnp.exp((c - 1.0 - j) * lgf), lgf, chunks)
    scan(stb_ref, ob_ref, kb_ref, vb_ref, cb_ref, sb_ref, jnp.exp(j * lgb), lgb, reversed(chunks))


def _ret_states(ret3, cos_t, sin_t, lg_lane):
    bsz, n, _ = ret3.shape
    nc = n // RET_CHUNK
    rows, w = RET_STATE_CHUNKS * RET_CHUNK, RET_WIDTH
    steps = n // rows
    fwd = lambda b, i: i
    bwd = lambda b, i: steps - 1 - i
    col = lambda pos, cb: pl.BlockSpec((1, rows, w), lambda b, i: (b, pos(b, i), cb))
    tab = lambda pos: pl.BlockSpec((rows, w), lambda b, i: (pos(b, i), 0))
    st_shape = (RET_PAIRS, RET_PAIR, RET_PAIR)
    out = lambda pos: pl.BlockSpec((1, RET_STATE_CHUNKS) + st_shape, lambda b, i: (b, pos(b, i), 0, 0, 0))
    return pl.pallas_call(
        _ret_state_kernel,
        grid=(bsz, steps),
        in_specs=[col(fwd, 1), col(fwd, 2), col(bwd, 1), col(bwd, 2),
                  tab(fwd), tab(fwd), tab(bwd), tab(bwd),
                  pl.BlockSpec((2, w), lambda b, i: (0, 0))],
        out_specs=[out(fwd), out(bwd)],
        out_shape=[jax.ShapeDtypeStruct((bsz, nc) + st_shape, BF16)] * 2,
        scratch_shapes=[pltpu.VMEM(st_shape, F32), pltpu.VMEM(st_shape, F32)],
        compiler_params=_params(("parallel", "arbitrary"), 32),
        name="retention_states",
    )(ret3, ret3, ret3, ret3, cos_t, sin_t, cos_t, sin_t, lg_lane)


def _ret_out_kernel(q_ref, k_ref, v_ref, g_ref, cos_ref, sin_ref, lg_ref, lgs_ref, stf_ref, stb_ref,
                    o_ref):
    c, w = RET_CHUNK, RET_WIDTH
    pos = lax.broadcasted_iota(jnp.int32, (c, w), 0).astype(F32)
    xi_f = jnp.exp((pos + 1.0) * lg_ref[0:1, :])
    xi_b = jnp.exp((c - pos) * lg_ref[1:2, :])
    diff = (lax.broadcasted_iota(jnp.int32, (c, c), 0)
            - lax.broadcasted_iota(jnp.int32, (c, c), 1)).astype(F32)
    first = lax.broadcasted_iota(jnp.int32, (c, RET_PAIR), 1) < HEAD_DIM
    avg = jnp.where(_same_head((RET_PAIR, RET_PAIR)), 1.0 / HEAD_DIM, 0.0).astype(BF16)
    decays = [jnp.concatenate(
        [jnp.exp(jnp.where(diff >= 0, diff * lgs_ref[0, h], -diff * lgs_ref[1, h]))
         for h in (2 * p, 2 * p + 1)], axis=0) for p in range(RET_PAIRS)]
    for sub in range(RET_OUT_CHUNKS):
        rows = slice(sub * c, (sub + 1) * c)
        cos_t, sin_t = cos_ref[rows, :], sin_ref[rows, :]
        q = _rotate(q_ref[0, rows, :], cos_t, sin_t)
        k = (_rotate(k_ref[0, rows, :], cos_t, sin_t) * (HEAD_DIM ** -0.5)).astype(BF16)
        v = v_ref[0, rows, :].astype(BF16)
        g = g_ref[0, rows, :]
        q_f = (q * xi_f).astype(BF16)
        q_b = (q * xi_b).astype(BF16)
        for p in range(RET_PAIRS):
            lanes = slice(p * RET_PAIR, (p + 1) * RET_PAIR)
            q_p = q[:, lanes]
            qs = jnp.concatenate([jnp.where(first, q_p, 0.0), jnp.where(first, 0.0, q_p)], axis=0)
            s = _dot_nt(qs.astype(BF16), k[:, lanes])
            o2 = _dot((s * decays[p]).astype(BF16), v[:, lanes])
            y = jnp.where(first, o2[:c], o2[c:])
            y = y + _dot(q_f[:, lanes], stf_ref[0, sub, p]) + _dot(q_b[:, lanes], stb_ref[0, sub, p])
            sq = y * y
            sq_hi = sq.astype(BF16)
            sq_lo = (sq - sq_hi.astype(F32)).astype(BF16)
            ms = _dot(sq_hi, avg) + _dot(sq_lo, avg)
            y = y * lax.rsqrt(ms + EPS)
            g_p = g[:, lanes]
            o_ref[0, rows, lanes] = (g_p / (1.0 + jnp.exp(-g_p)) * y).astype(o_ref.dtype)


def _retention(ret3, decay):
    bsz, n, _ = ret3.shape
    rows, w = RET_OUT_CHUNKS * RET_CHUNK, RET_WIDTH
    lg = -jnp.exp(decay.astype(F32))
    lg_lane = jnp.repeat(lg, HEAD_DIM, axis=1)
    cos_t, sin_t = _rotary_tables(n)
    stf, stb = _ret_states(ret3, cos_t, sin_t, lg_lane)
    col = lambda cb: pl.BlockSpec((1, rows, w), lambda b, i: (b, i, cb))
    tab = pl.BlockSpec((rows, w), lambda b, i: (i, 0))
    st = pl.BlockSpec((1, RET_OUT_CHUNKS, RET_PAIRS, RET_PAIR, RET_PAIR), lambda b, i: (b, i, 0, 0, 0))
    return pl.pallas_call(
        _ret_out_kernel,
        grid=(bsz, n // rows),
        in_specs=[col(0), col(1), col(2), col(3), tab, tab,
                  pl.BlockSpec((2, w), lambda b, i: (0, 0)),
                  pl.BlockSpec(memory_space=pltpu.SMEM), st, st],
        out_specs=pl.BlockSpec((1, rows, w), lambda b, i: (b, i, 0)),
        out_shape=jax.ShapeDtypeStruct((bsz, n, w), BF16),
        compiler_params=_params(("parallel", "parallel"), 32),
        name="retention_out",
    )(ret3, ret3, ret3, ret3, cos_t, sin_t, lg_lane, lg, stf, stb)


ROUTE_I0, ROUTE_I1, ROUTE_W0, ROUTE_W1, ROUTE_R0, ROUTE_R1 = 0, 1, 2, 3, 4, 5


def _mix_router_kernel(x_ref, yh_ref, ya_ref, yr_ref, w_ref, g_ref, wr_ref, br_ref,
                       xo_ref, hp_ref, route_ref, counts_ref, carry_ref, logits_ref):
    i = pl.program_id(0)

    @pl.when(i == 0)
    def _():
        carry_ref[...] = jnp.zeros_like(carry_ref)
        logits_ref[...] = jnp.zeros_like(logits_ref)

    logits = logits_ref[(i + 1) % 2]
    live = jnp.where(i >= 1, 1.0, 0.0).astype(F32)
    lane = lax.broadcasted_iota(jnp.int32, logits.shape, 1)
    big = jnp.int32(LANES)
    neg = -jnp.inf
    is_group = (lane >= N_EXPERTS) & (lane < N_EXPERTS + N_GROUPS)
    gl = jnp.where(is_group, logits, neg)
    gmax = jnp.max(gl, axis=-1, keepdims=True)
    g_w = 1.0 / jnp.sum(jnp.exp(gl - gmax), axis=-1, keepdims=True)
    g_sel = jnp.min(jnp.where(gl == gmax, lane, big), axis=-1, keepdims=True) - N_EXPERTS
    lo = g_sel * EXPERTS_PER_GROUP
    el = jnp.where((lane >= lo) & (lane < lo + EXPERTS_PER_GROUP), logits, neg)
    v0 = jnp.max(el, axis=-1, keepdims=True)
    i0 = jnp.min(jnp.where(el == v0, lane, big), axis=-1, keepdims=True)
    el = jnp.where(lane == i0, neg, el)
    v1 = jnp.max(el, axis=-1, keepdims=True)
    i1 = jnp.min(jnp.where(el == v1, lane, big), axis=-1, keepdims=True)
    e1 = jnp.exp(v1 - v0)
    w0 = g_w / (1.0 + e1)
    oh0 = jnp.where(lane == i0, 1.0, 0.0)
    oh1 = jnp.where(lane == i1, 1.0, 0.0)
    tm = logits.shape[0]
    earlier = jnp.where(lax.broadcasted_iota(jnp.int32, (tm, tm), 1)
                        < lax.broadcasted_iota(jnp.int32, (tm, tm), 0), 1.0, 0.0).astype(BF16)
    carry = carry_ref[...]
    tot0 = jnp.sum(oh0, axis=0, keepdims=True)
    before0 = _dot(earlier, oh0.astype(BF16)) + carry
    before1 = _dot(earlier, oh1.astype(BF16)) + (carry + tot0)
    r0 = jnp.sum(oh0 * before0, axis=-1, keepdims=True)
    r1 = jnp.sum(oh1 * before1, axis=-1, keepdims=True)
    carry = carry + live * (tot0 + jnp.sum(oh1, axis=0, keepdims=True))
    carry_ref[...] = carry
    counts_ref[...] = jnp.broadcast_to(carry, counts_ref.shape)
    route_ref[...] = (jnp.where(lane == ROUTE_I0, i0.astype(F32), 0.0)
                      + jnp.where(lane == ROUTE_I1, i1.astype(F32), 0.0)
                      + jnp.where(lane == ROUTE_W0, w0, 0.0)
                      + jnp.where(lane == ROUTE_W1, w0 * e1, 0.0)
                      + jnp.where(lane == ROUTE_R0, r0, 0.0)
                      + jnp.where(lane == ROUTE_R1, r1, 0.0))

    mix = jnp.concatenate([yh_ref[...], ya_ref[...], yr_ref[...]], axis=1)
    x = x_ref[...] + _dot(mix, w_ref[...])
    xo_ref[...] = x
    h = x * lax.rsqrt(jnp.mean(x * x, axis=-1, keepdims=True) + EPS) * g_ref[...]
    h_hi = h.astype(BF16)
    bits = lax.bitcast_convert_type(h_hi.astype(F32), jnp.uint32)
    half = bits.shape[1] // 2
    hp_ref[...] = bits[:, :half] | (bits[:, half:] >> 16)
    h_lo = (h - h_hi.astype(F32)).astype(BF16)
    both = _dot(h_hi, wr_ref[...])
    logits_ref[i % 2] = (both[:, :LANES] + (_dot(h_lo, wr_ref[:, :LANES]) + both[:, LANES:])
                         + br_ref[...])


def _mix_router(x, yh, ya, yr, w_out, g, wr, br, tm=512):
    t, d = x.shape
    steps = t // tm
    cur = lambda i: (jnp.minimum(i, steps - 1), 0)
    row = lambda width: pl.BlockSpec((tm, width), cur)
    full = lambda shape: pl.BlockSpec(shape, lambda i: (0, 0))
    return pl.pallas_call(
        _mix_router_kernel,
        grid=(steps + 1,),
        in_specs=[row(d), row(HY_WIDTH), row(ATT_WIDTH), row(RET_WIDTH),
                  pl.BlockSpec((d, d), lambda i: (0, 0), pipeline_mode=pl.Buffered(1)),
                  full((1, d)), full((d, 2 * LANES)), full((1, LANES))],
        out_specs=[row(d), row(d // 2),
                   pl.BlockSpec((tm, LANES), lambda i: (jnp.maximum(i - 1, 0), 0)),
                   full((SUBLANES, LANES))],
        out_shape=[jax.ShapeDtypeStruct((t, d), F32), jax.ShapeDtypeStruct((t, d // 2), jnp.uint32),
                   jax.ShapeDtypeStruct((t, LANES), F32), jax.ShapeDtypeStruct((SUBLANES, LANES), F32)],
        scratch_shapes=[pltpu.VMEM((1, LANES), F32), pltpu.VMEM((2, tm, LANES), F32)],
        compiler_params=_params(("arbitrary",), 48),
        name="mix_router",
    )(x, yh, ya, yr, w_out, g.reshape(1, d), wr, br)


MOE_TM = 512
MOE_TC = 512


def _dispatch_plan(route, counts, tm):
    experts = jnp.arange(N_EXPERTS, dtype=jnp.int32)
    ids = route[:, ROUTE_I0:ROUTE_I1 + 1].astype(jnp.int32)
    rank = route[:, ROUTE_R0:ROUTE_R1 + 1].astype(jnp.int32)
    n_tiles = 2 * route.shape[0] // tm + N_EXPERTS
    count = counts[0, :N_EXPERTS].astype(jnp.int32)
    tiles = (count + tm - 1) // tm
    tile_end = jnp.cumsum(tiles)
    offs = (tile_end - tiles) * tm
    slot = jnp.sum(jnp.where(ids[..., None] == experts, offs, 0), axis=-1) + rank
    tile_expert = jnp.minimum(
        jnp.sum(jnp.arange(n_tiles, dtype=jnp.int32)[:, None] >= tile_end[None, :], axis=1),
        N_EXPERTS - 1).astype(jnp.int32)
    later = (experts[None, :] > experts[:, None]) & (tiles[None, :] > 0)
    following = jnp.min(jnp.where(later, experts[None, :], N_EXPERTS), axis=1)
    following = jnp.where(following == N_EXPERTS, -1, following).astype(jnp.int32)
    pad = (offs + count, tiles * tm - count)
    return slot, n_tiles, tile_expert, tile_end[-1:].astype(jnp.int32), following[tile_expert], pad


def _moe_dispatch_kernel(tm, pad_start_ref, pad_rows_ref, nv_ref, idx_ref, hp_ref, xs_ref, zeros_ref,
                         sem, zsem):
    tokens = hp_ref.shape[0]
    n_tiles = xs_ref.shape[0] // tm

    @pl.when(pl.program_id(0) == 0)
    def _():
        zeros_ref[...] = jnp.zeros_like(zeros_ref)
        fills = []
        for e in range(N_EXPERTS):
            rows = pad_rows_ref[e]
            end = pad_start_ref[e] + rows
            size = tm // 2
            while size >= SUBLANES:
                above = rows & ~(2 * size - 1)
                fills.append(((rows & size) != 0, pl.multiple_of(end - above - size, SUBLANES), size))
                size //= 2
            for j in range(SUBLANES - 1):
                fills.append(((rows & (SUBLANES - 1)) > j, pad_start_ref[e] + j, 1))
        for j in range(N_EXPERTS):
            fills.append((nv_ref[0] + j < n_tiles, pl.multiple_of((nv_ref[0] + j) * tm, tm), tm))
        copies = [(cond, pltpu.make_async_copy(zeros_ref.at[pl.ds(0, size)],
                                               xs_ref.at[pl.ds(start, size)], zsem))
                  for cond, start, size in fills]
        for cond, cp in copies:
            pl.when(cond)(cp.start)
        for cond, cp in copies:
            pl.when(cond)(cp.wait)

    def issue(r, carry):
        for k in range(2):
            pltpu.make_async_copy(hp_ref.at[pl.ds(r, 1)],
                                  xs_ref.at[pl.ds(idx_ref[0, 0, 2 * r + k], 1)], sem).start(priority=k)
        return carry

    lax.fori_loop(0, tokens, issue, 0, unroll=4)
    for k in range(2):
        pltpu.make_async_copy(hp_ref, xs_ref.at[pl.ds(0, tokens)], sem).wait()


def _moe_dispatch(hp, slot, pad_start, pad_rows, n_valid, n_tiles, tm, tc=MOE_TC):
    t, half = hp.shape
    steps = t // tc
    return pl.pallas_call(
        functools.partial(_moe_dispatch_kernel, tm),
        grid_spec=pltpu.PrefetchScalarGridSpec(
            num_scalar_prefetch=3,
            grid=(steps,),
            in_specs=[pl.BlockSpec((1, 1, 2 * tc), lambda i, ps, pr, nv: (i, 0, 0),
                                   memory_space=pltpu.SMEM),
                      pl.BlockSpec((tc, half), lambda i, ps, pr, nv: (i, 0))],
            out_specs=pl.BlockSpec(memory_space=pl.ANY),
            scratch_shapes=[pltpu.VMEM((tm, half), hp.dtype), pltpu.SemaphoreType.DMA(()),
                            pltpu.SemaphoreType.DMA(())],
        ),
        out_shape=jax.ShapeDtypeStruct((n_tiles * tm, half), hp.dtype),
        compiler_params=_params(("arbitrary",), 16),
        name="moe_dispatch",
    )(pad_start, pad_rows, n_valid, slot.reshape(steps, 1, 2 * tc), hp)


MOE_CAST_ROWS = 256


def _moe_gemm_kernel(layer, te_ref, nv_ref, nxt_ref, xs_ref, wg_ref, wu_ref, wd_ref, y_ref,
                     stage_g, stage_u, stage_d, wgb_ref, wub_ref, wdb_ref, sem_ref):
    i = pl.program_id(0)
    last = nv_ref[0] - 1
    tile = jnp.minimum(i, last)
    expert = te_ref[tile]
    fresh = jnp.logical_or(i == 0, expert != te_ref[jnp.maximum(tile - 1, 0)])
    pairs = ((wg_ref, stage_g, wgb_ref), (wu_ref, stage_u, wub_ref), (wd_ref, stage_d, wdb_ref))

    def copies(e):
        return [pltpu.make_async_copy(w_ref.at[layer, e], stage, sem_ref.at[n])
                for n, (w_ref, stage, _) in enumerate(pairs)]

    @pl.when(jnp.logical_and(i <= last, fresh))
    def _():
        @pl.when(i == 0)
        def _():
            for cp in copies(expert):
                cp.start()

        for cp in copies(expert):
            cp.wait()
        for _, stage, dst_ref in pairs:
            def cast(j, carry):
                rows = pl.ds(pl.multiple_of(j * MOE_CAST_ROWS, MOE_CAST_ROWS), MOE_CAST_ROWS)
                dst_ref[rows, :] = stage[rows, :].astype(BF16)
                return carry
            lax.fori_loop(0, dst_ref.shape[0] // MOE_CAST_ROWS, cast, 0)

        @pl.when(nxt_ref[tile] >= 0)
        def _():
            for cp in copies(nxt_ref[tile]):
                cp.start()

    @pl.when(i <= last)
    def _():
        xp = xs_ref[...]
        x_a = lax.bitcast_convert_type(xp & jnp.uint32(0xFFFF0000), F32).astype(BF16)
        x_b = lax.bitcast_convert_type(xp << 16, F32).astype(BF16)
        x = jnp.concatenate([x_a, x_b], axis=1)
        a = _dot(x, wgb_ref[...])
        b = _dot(x, wub_ref[...])
        act = a / (1.0 + jnp.exp(-a)) * b
        y = _dot(act.astype(BF16), wdb_ref[...])
        bits = lax.bitcast_convert_type(y.astype(BF16).astype(F32), jnp.uint32)
        cols = bits.shape[1] // 2
        y_ref[...] = bits[:, :cols] | (bits[:, cols:] >> 16)

    @pl.when(i > last)
    def _():
        y_ref[...] = jnp.zeros_like(y_ref)


def _moe_gemm(xs, tile_expert, n_valid, next_expert, layer, wg, wu, wd, tm=MOE_TM):
    n_slots, half = xs.shape
    _, ne, d, de = wg.shape
    tile = lambda i, te, nv, nx: jnp.minimum(i, nv[0] - 1)
    hbm = pl.BlockSpec(memory_space=pl.ANY)
    return pl.pallas_call(
        functools.partial(_moe_gemm_kernel, layer),
        grid_spec=pltpu.PrefetchScalarGridSpec(
            num_scalar_prefetch=3,
            grid=(n_slots // tm,),
            in_specs=[pl.BlockSpec((tm, half), lambda i, te, nv, nx: (tile(i, te, nv, nx), 0)),
                      hbm, hbm, hbm],
            out_specs=pl.BlockSpec((tm, d // 2), lambda i, te, nv, nx: (i, 0)),
            scratch_shapes=[pltpu.VMEM((d, de), F32), pltpu.VMEM((d, de), F32), pltpu.VMEM((de, d), F32),
                            pltpu.VMEM((d, de), BF16), pltpu.VMEM((d, de), BF16),
                            pltpu.VMEM((de, d), BF16), pltpu.SemaphoreType.DMA((3,))],
        ),
        out_shape=jax.ShapeDtypeStruct((n_slots, d // 2), jnp.uint32),
        compiler_params=_params(("arbitrary",), 58),
        name="moe_gemm",
    )(tile_expert, n_valid, next_expert, xs, wg, wu, wd)


def _moe_combine_kernel(final_norm, steps_a, idx_ref, idx_next_ref, x_ref, route_ref, gf_ref, y_ref,
                        *rest):
    o_refs, (buf_ref, sem_ref) = rest[:-2], rest[-2:]
    i = pl.program_id(0)
    n = pl.num_programs(0)
    tokens = x_ref.shape[0]

    def start_rows(idx, slot):
        def issue(r, carry):
            for k in range(2):
                pltpu.make_async_copy(y_ref.at[pl.ds(idx[0, 0, 2 * r + k], 1)],
                                      buf_ref.at[slot, k, pl.ds(r, 1)], sem_ref.at[slot]).start(priority=k)
            return carry
        lax.fori_loop(0, tokens, issue, 0, unroll=4)

    @pl.when(i == 0)
    def _():
        start_rows(idx_ref, 0)

    @pl.when(i + 1 < n)
    def _():
        start_rows(idx_next_ref, (i + 1) % 2)

    slot = i % 2
    for k in range(2):
        pltpu.make_async_copy(y_ref.at[pl.ds(0, tokens)], buf_ref.at[slot, k], sem_ref.at[slot]).wait()
    route = route_ref[...]
    def rows(k):
        packed = buf_ref[slot, k]
        return jnp.concatenate([lax.bitcast_convert_type(packed & jnp.uint32(0xFFFF0000), F32),
                                lax.bitcast_convert_type(packed << 16, F32)], axis=1)

    y = x_ref[...] + route[:, ROUTE_W0:ROUTE_W0 + 1] * rows(0) \
        + route[:, ROUTE_W1:ROUTE_W1 + 1] * rows(1)
    if final_norm:
        y = y * lax.rsqrt(jnp.mean(y * y, axis=-1, keepdims=True) + EPS) * gf_ref[...]
    if len(o_refs) == 1:
        o_refs[0][...] = y
    else:
        @pl.when(i < steps_a)
        def _():
            o_refs[0][...] = y

        @pl.when(i >= steps_a)
        def _():
            o_refs[1][...] = y


def _moe_combine(x, route, slot, y, gf, final_norm, tokens_a=None, tc=MOE_TC):
    t, d = x.shape
    steps = t // tc
    idx = slot.reshape(steps, 1, 2 * tc)
    row = lambda width: pl.BlockSpec((tc, width), lambda i: (i, 0))
    smem = lambda pos: pl.BlockSpec((1, 1, 2 * tc), lambda i: (pos(i), 0, 0), memory_space=pltpu.SMEM)
    if tokens_a is None:
        steps_a = steps
        out_specs = [row(d)]
        out_shape = [jax.ShapeDtypeStruct((t, d), F32)]
    else:
        steps_a = tokens_a // tc
        out_specs = [pl.BlockSpec((tc, d), lambda i: (jnp.minimum(i, steps_a - 1), 0)),
                     pl.BlockSpec((tc, d), lambda i: (jnp.maximum(i - steps_a, 0), 0))]
        out_shape = [jax.ShapeDtypeStruct((tokens_a, d), F32),
                     jax.ShapeDtypeStruct((t - tokens_a, d), F32)]
    return pl.pallas_call(
        functools.partial(_moe_combine_kernel, final_norm, steps_a),
        grid=(steps,),
        in_specs=[smem(lambda i: i), smem(lambda i: jnp.minimum(i + 1, steps - 1)),
                  row(d), row(LANES), pl.BlockSpec((1, d), lambda i: (0, 0)),
                  pl.BlockSpec(memory_space=pl.ANY)],
        out_specs=out_specs,
        out_shape=out_shape,
        scratch_shapes=[pltpu.VMEM((2, 2, tc, d // 2), jnp.uint32), pltpu.SemaphoreType.DMA((2,))],
        compiler_params=_params(("arbitrary",), 52),
        name="moe_combine",
    )(idx, idx, x, route, gf.reshape(1, d), y)


def _moe(hp, route, counts, x, layer, wg, wu, wd, gf, final_norm, tokens_a=None):
    slot, n_tiles, tile_expert, n_valid, next_expert, pad = _dispatch_plan(route, counts, MOE_TM)
    xs = _moe_dispatch(hp, slot, pad[0], pad[1], n_valid, n_tiles, MOE_TM)
    y = _moe_gemm(xs, tile_expert, n_valid, next_expert, layer, wg, wu, wd)
    return _moe_combine(x, route, slot, y, gf, final_norm, tokens_a)


def _router_weights(wg, bg, we, be):
    d = wg.shape[0]
    pad = LANES - N_EXPERTS - N_GROUPS
    wr = jnp.concatenate([we, wg, jnp.zeros((d, pad), F32)], axis=1).astype(F32)
    br = jnp.concatenate([be, bg, jnp.zeros((pad,), F32)]).astype(F32).reshape(1, LANES)
    wr_hi = wr.astype(BF16)
    wr_lo = (wr - wr_hi.astype(F32)).astype(BF16)
    return jnp.concatenate([wr_hi, wr_lo], axis=1), br


def _layer(xs, bsz, p, l, consts, final_g, tokens_a):
    dft, emb = consts
    w_in = p['w_in'][l].astype(BF16)
    g1 = p['ln1_g'][l]
    w_hy, w_att, w_ret = w_in[:, :HY_IN], w_in[:, HY_IN:HY_IN + ATT_IN], w_in[:, HY_IN + ATT_IN:]
    if len(xs) == 2:
        hy, x = _norm_proj_stack(xs[0], xs[1], g1, w_hy, F32)
        att, ret = _norm_proj(x, g1, [w_att, w_ret], [BF16, F32])
    else:
        x, = xs
        hy, att, ret = _norm_proj(x, g1, [w_hy, w_att, w_ret], [F32, BF16, F32])
    t, d = x.shape
    n = t // bsz
    hy = hy.reshape(bsz, n, HY_IN)
    att = att.reshape(bsz, n, ATT_IN)
    ret = ret.reshape(bsz, n, RET_IN)
    y_h = _hyena(hy, dft, emb, p['hy_conv_w'][l], p['hy_conv_b'][l], p['hy_w1'][l], p['hy_b1'][l],
                 p['hy_w2'][l], p['hy_b2'][l], p['hy_w3'][l], p['hy_freq'][l], p['hy_decay'][l],
                 p['hy_skip'][l])
    y_a = _attention(att, p['attn_sink'][l], p['rel_bias'])
    y_r = _retention(ret, p['ret_decay'][l])
    wr, br = _router_weights(p['router_group_w'][l], p['router_group_b'][l],
                             p['router_expert_w'][l], p['router_expert_b'][l])
    x, hp, route, counts = _mix_router(x, y_h.reshape(t, -1), y_a.reshape(t, -1), y_r.reshape(t, -1),
                                       p['w_out'][l].astype(BF16), p['ln2_g'][l], wr, br)
    last = l == DEPTH - 1
    return _moe(hp, route, counts, x, l, p['moe_w_gate'], p['moe_w_up'], p['moe_w_down'], final_g,
                final_norm=last, tokens_a=tokens_a if last else None)


def kernel(x_prompt, x_sample, ln1_g, w_in, hy_conv_w, hy_conv_b, hy_w1, hy_b1, hy_w2, hy_b2, hy_w3,
           hy_freq, hy_decay, hy_skip, attn_sink, rel_bias, ret_decay, w_out, ln2_g, router_group_w,
           router_group_b, router_expert_w, router_expert_b, moe_w_gate, moe_w_up, moe_w_down, lnf_g):
    p = dict(ln1_g=ln1_g, w_in=w_in, hy_conv_w=hy_conv_w, hy_conv_b=hy_conv_b, hy_w1=hy_w1,
             hy_b1=hy_b1, hy_w2=hy_w2, hy_b2=hy_b2, hy_w3=hy_w3, hy_freq=hy_freq, hy_decay=hy_decay,
             hy_skip=hy_skip, attn_sink=attn_sink, rel_bias=rel_bias, ret_decay=ret_decay, w_out=w_out,
             ln2_g=ln2_g, router_group_w=router_group_w, router_group_b=router_group_b,
             router_expert_w=router_expert_w, router_expert_b=router_expert_b, moe_w_gate=moe_w_gate,
             moe_w_up=moe_w_up, moe_w_down=moe_w_down)
    bp, n, d = x_prompt.shape
    bs = x_sample.shape[0]
    assert x_sample.shape[1:] == (n, d)
    bsz = bp + bs
    consts = (_dft_matrices(n), _hyena_embedding(n))
    xs = (x_prompt.reshape(bp * n, d), x_sample.reshape(bs * n, d))
    for l in range(DEPTH):
        xs = _layer(xs, bsz, p, l, consts, lnf_g, bp * n)
    y_prompt, y_sample = xs
    return (y_prompt.reshape(bp, n, d), y_sample.reshape(bs, n, d))
```
